```python
import jax, jax.numpy as jnp
from jax import lax
import numpy as np

D_MODEL = 1024
BATCH = 8
SEQ = 2048
DEPTH = 1

PLE_DIM = 256
EPS = 1e-6
GLA_HEADS = 4
GLA_DK = 64
GLA_DV = 128
GLA_LOWRANK = 16
GLA_TAU = 16.0
GLA_CHUNK = 64
FOX_HEADS = 8
FOX_DH = 64
FOX_BLOCK = 128
D_FF = 2816
CONV_W = 3

GLA_QK_W = GLA_HEADS * GLA_DK
GLA_V_W = GLA_HEADS * GLA_DV
FOX_W = FOX_HEADS * FOX_DH
MIX_W = GLA_V_W + FOX_W
IN_SPLIT_SIZES = (GLA_QK_W, GLA_QK_W, GLA_V_W, GLA_V_W, GLA_LOWRANK, FOX_W, FOX_W, FOX_W, FOX_HEADS)
IN_W = 2 * GLA_QK_W + 2 * GLA_V_W + GLA_LOWRANK + 3 * FOX_W + FOX_HEADS

kernel_name = "hymba_gla_fox_convffn_ple"


def rms_norm(x, gain):
    xf = x.astype(jnp.float32)
    y = xf * lax.rsqrt(jnp.mean(xf * xf, axis=-1, keepdims=True) + EPS)
    return (y * gain.astype(jnp.float32)).astype(x.dtype)


def split_cols(t, sizes):
    offs = np.cumsum(sizes)[:-1].tolist()
    return jnp.split(t, offs, axis=-1)


def gla_mixer(q, k, v, lr, og, lr_w, lr_b, onorm_g):
    B, S, _ = q.shape
    N = S // GLA_CHUNK
    f32 = jnp.float32
    log_a = jax.nn.log_sigmoid((lr @ lr_w + lr_b).astype(f32)) / GLA_TAU

    def chunks(t, d):
        return t.astype(f32).reshape(B, N, GLA_CHUNK, GLA_HEADS, d).transpose(0, 3, 1, 2, 4)

    qc = chunks(q, GLA_DK) * (GLA_DK ** -0.5)
    kc = chunks(k, GLA_DK)
    vc = chunks(v, GLA_DV)
    b = jnp.cumsum(chunks(log_a, GLA_DK), axis=3)
    b_last = b[:, :, :, -1:, :]
    qe = qc * jnp.exp(b)
    ke = kc * jnp.exp(-b)
    kd = kc * jnp.exp(b_last - b)
    causal = jnp.tril(jnp.ones((GLA_CHUNK, GLA_CHUNK), dtype=bool))
    A = jnp.where(causal, jnp.einsum('bhnck,bhnsk->bhncs', qe, ke), 0.0)
    o_intra = jnp.einsum('bhncs,bhnsv->bhncv', A, vc)
    dS = jnp.einsum('bhnck,bhncv->nbhkv', kd, vc)
    decay = jnp.exp(b_last[:, :, :, 0, :]).transpose(2, 0, 1, 3)

    def step(state, inp):
        d, ds = inp
        return state * d[..., None] + ds, state

    s0 = jnp.zeros((B, GLA_HEADS, GLA_DK, GLA_DV), f32)
    _, s_prev = lax.scan(step, s0, (decay, dS))
    o_inter = jnp.einsum('bhnck,nbhkv->bhncv', qe, s_prev)
    o = (o_intra + o_inter).transpose(0, 2, 3, 1, 4).reshape(B, S, GLA_HEADS, GLA_DV)
    o = rms_norm(o, onorm_g) * jax.nn.silu(og.astype(f32).reshape(B, S, GLA_HEADS, GLA_DV))
    return o.reshape(B, S, GLA_V_W).astype(q.dtype)


def fox_mixer(q, k, v, f_logit, b_f, qn_g, kn_g):
    B, S, _ = q.shape
    f32 = jnp.float32
    qh = rms_norm(q.reshape(B, S, FOX_HEADS, FOX_DH), qn_g).transpose(0, 2, 1, 3)
    kh = rms_norm(k.reshape(B, S, FOX_HEADS, FOX_DH), kn_g).transpose(0, 2, 1, 3)
    vh = v.reshape(B, S, FOX_HEADS, FOX_DH).transpose(0, 2, 1, 3)
    c = jnp.cumsum(jax.nn.log_sigmoid((f_logit + b_f).astype(f32)), axis=1).transpose(0, 2, 1)
    scale = FOX_DH ** -0.5
    outs = []
    for i in range(S // FOX_BLOCK):
        q0, end = i * FOX_BLOCK, (i + 1) * FOX_BLOCK
        qs = qh[:, :, q0:end]
        ks = kh[:, :, :end]
        vs = vh[:, :, :end]
        logits = jnp.einsum('bhqd,bhkd->bhqk', qs, ks).astype(f32) * scale
        logits = logits + c[:, :, q0:end, None] - c[:, :, None, :end]
        qpos = q0 + jnp.arange(FOX_BLOCK)
        kpos = jnp.arange(end)
        logits = jnp.where(kpos[None, :] <= qpos[:, None], logits, -jnp.inf)
        probs = jax.nn.softmax(logits, axis=-1)
        outs.append(jnp.einsum('bhqk,bhkd->bhqd', probs.astype(vs.dtype), vs))
    o = jnp.concatenate(outs, axis=2)
    return o.transpose(0, 2, 1, 3).reshape(B, S, FOX_W)


def conv_ffn(h, w_up, conv_w, conv_b, w_down):
    S = h.shape[1]
    u = h @ w_up
    u_pad = jnp.pad(u, ((0, 0), (CONV_W - 1, 0), (0, 0)))
    u_c = conv_b + sum(u_pad[:, j:j + S] * conv_w[j] for j in range(CONV_W))
    gate, val = jnp.split(u_c, 2, axis=-1)
    return (jax.nn.silu(gate) * val) @ w_down


def setup_inputs(seed: int = 0) -> dict:
    key = jax.random.key(seed)
    ks = jax.random.split(key, 24)
    f32 = jnp.float32

    def nrm(k, shape, scale):
        return jax.random.normal(k, shape, f32) * scale

    L = DEPTH
    return {
        "x": nrm(ks[0], (BATCH, SEQ, D_MODEL), 1.0),
        "p": nrm(ks[1], (DEPTH, BATCH, SEQ, PLE_DIM), 1.0),
        "norm1_g": 1.0 + nrm(ks[2], (L, D_MODEL), 0.02),
        "w_in": nrm(ks[3], (L, D_MODEL, IN_W), D_MODEL ** -0.5),
        "gla_lr_w": nrm(ks[4], (L, GLA_LOWRANK, GLA_QK_W), GLA_LOWRANK ** -0.5),
        "gla_lr_b": nrm(ks[5], (L, GLA_QK_W), 0.1),
        "gla_onorm_g": 1.0 + nrm(ks[6], (L, GLA_DV), 0.02),
        "fox_b_f": nrm(ks[7], (L, FOX_HEADS), 0.1),
        "fox_qnorm_g": 1.0 + nrm(ks[8], (L, FOX_DH), 0.02),
        "fox_knorm_g": 1.0 + nrm(ks[9], (L, FOX_DH), 0.02),
        "w_o": nrm(ks[10], (L, MIX_W, D_MODEL), MIX_W ** -0.5),
        "norm2_g": 1.0 + nrm(ks[11], (L, D_MODEL), 0.02),
        "w_up": nrm(ks[12], (L, D_MODEL, 2 * D_FF), D_MODEL ** -0.5),
        "conv_w": nrm(ks[13], (L, CONV_W, 2 * D_FF), CONV_W ** -0.5),
        "conv_b": nrm(ks[14], (L, 2 * D_FF), 0.02),
        "w_down": nrm(ks[15], (L, D_FF, D_MODEL), D_FF ** -0.5),
        "norm3_g": 1.0 + nrm(ks[16], (L, D_MODEL), 0.02),
        "w_pe": nrm(ks[17], (L, PLE_DIM, D_MODEL), PLE_DIM ** -0.5),
        "pe_norm_g": 1.0 + nrm(ks[18], (L, D_MODEL), 0.02),
        "w_pg": nrm(ks[19], (L, D_MODEL, D_MODEL), D_MODEL ** -0.5),
        "b_pg": nrm(ks[20], (L, D_MODEL), 0.02),
    }


def reference(x, p, norm1_g, w_in, gla_lr_w, gla_lr_b, gla_onorm_g, fox_b_f, fox_qnorm_g,
              fox_knorm_g, w_o, norm2_g, w_up, conv_w, conv_b, w_down, norm3_g, w_pe,
              pe_norm_g, w_pg, b_pg):
    for i in range(DEPTH):
        h = rms_norm(x, norm1_g[i])
        proj = h @ w_in[i]
        g_q, g_k, g_v, g_og, g_lr, f_q, f_k, f_v, f_f = split_cols(proj, IN_SPLIT_SIZES)
        y_gla = gla_mixer(g_q, g_k, g_v, g_lr, g_og, gla_lr_w[i], gla_lr_b[i], gla_onorm_g[i])
        y_fox = fox_mixer(f_q, f_k, f_v, f_f, fox_b_f[i], fox_qnorm_g[i], fox_knorm_g[i])
        x = x + jnp.concatenate([y_gla, y_fox], axis=-1) @ w_o[i]
        x = x + conv_ffn(rms_norm(x, norm2_g[i]), w_up[i], conv_w[i], conv_b[i], w_down[i])
        e = rms_norm(p[i] @ w_pe[i], pe_norm_g[i])
        gate = jax.nn.sigmoid(rms_norm(x, norm3_g[i]) @ w_pg[i] + b_pg[i])
        x = x + gate * e
    return x
```

```python
import functools

import jax
import jax.numpy as jnp
from jax import lax
from jax.experimental import pallas as pl
from jax.experimental.pallas import tpu as pltpu

F32 = jnp.float32
BF16 = jnp.bfloat16

EPS = 1e-6
LANES = 128
GLA_HEADS, GLA_DK, GLA_DV = 4, 64, 128
GLA_LOWRANK = 16
GLA_INV_TAU = 1.0 / 16.0
GLA_CHUNK = 64
GLA_SUPER = 256
FOX_HEADS, FOX_DH = 8, 64
FOX_TILE = 256
CONV_W = 3
FFN_CHUNK = 256
CARRY_ROWS = 8
MISC_F0 = 0
MISC_LR0 = 8
VMEM_LIMIT = 56 * 1024 * 1024


def _dot(a, b):
    return jnp.dot(a, b, preferred_element_type=F32)


def _dot_nt(a, b):
    return lax.dot_general(a, b, (((1,), (1,)), ((), ())), preferred_element_type=F32)


def _dot_tn(a, b):
    return lax.dot_general(a, b, (((0,), (0,)), ((), ())), preferred_element_type=F32)


def _log_sigmoid(z):
    return jnp.minimum(z, 0.0) - jnp.log1p(jnp.exp(-jnp.abs(z)))


def _split_bf16(x, pieces):
    out = []
    for _ in range(pieces):
        p = x.astype(BF16)
        out.append(p)
        x = x - p.astype(F32)
    return out


def _rms(x, gain):
    ms = jnp.mean(x * x, axis=-1, keepdims=True)
    return x * lax.rsqrt(ms + EPS) * gain


def _inproj_kernel(x_ref, g_ref, w_ref, gq_ref, gk_ref, gv_ref, gog_ref, fq_ref, fk_ref, fv_ref, misc_ref):
    h = _rms(x_ref[...], g_ref[...]).astype(BF16)
    off = 0
    for ref in (gq_ref, gk_ref, gv_ref, gog_ref, fq_ref, fk_ref, fv_ref, misc_ref):
        n = ref.shape[-1]
        ref[...] = _dot(h, w_ref[:, off:off + n]).astype(ref.dtype)
        off += n


def _inproj(x2d, g, w_all, tm):
    t, d = x2d.shape
    widths = (256, 256, 512, 512, 512, 512, 512, LANES)
    dtypes = (BF16,) * 7 + (F32,)
    return pl.pallas_call(
        _inproj_kernel,
        grid=(t // tm,),
        in_specs=[
            pl.BlockSpec((tm, d), lambda i: (i, 0)),
            pl.BlockSpec((1, d), lambda i: (0, 0)),
            pl.BlockSpec(w_all.shape, lambda i: (0, 0)),
        ],
        out_specs=[pl.BlockSpec((tm, n), lambda i: (i, 0)) for n in widths],
        out_shape=[jax.ShapeDtypeStruct((t, n), dt) for n, dt in zip(widths, dtypes)],
        compiler_params=pltpu.CompilerParams(dimension_semantics=("arbitrary",), vmem_limit_bytes=VMEM_LIMIT),
        name="inproj",
    )(x2d, g, w_all)


def _gla_kernel(q_ref, k_ref, v_ref, og_ref, misc_ref, lrw_ref, lrb_ref, ong_ref, y_ref, st_ref):
    s_len = q_ref.shape[0]
    r = GLA_SUPER
    dkw = GLA_HEADS * GLA_DK
    dvw = GLA_HEADS * GLA_DV
    row = lax.broadcasted_iota(jnp.int32, (r, r), 0)
    col = lax.broadcasted_iota(jnp.int32, (r, r), 1)
    same_chunk = (row // GLA_CHUNK) == (col // GLA_CHUNK)
    causal_bd = same_chunk & (col <= row)
    tri_bd = causal_bd.astype(BF16)
    ones_bd = same_chunk.astype(BF16)
    st_row = lax.broadcasted_iota(jnp.int32, (dvw, dkw), 0)
    st_col = lax.broadcasted_iota(jnp.int32, (dvw, dkw), 1)
    st_mask = (st_row // GLA_DV) == (st_col // GLA_DK)
    head_of_col = lax.broadcasted_iota(jnp.int32, (r, dkw), 1) // GLA_DK

    st_ref[...] = jnp.zeros_like(st_ref)

    @pl.loop(0, s_len // r)
    def _(sb):
        rows = pl.ds(pl.multiple_of(sb * r, r), r)
        z = _dot(misc_ref[rows, :].astype(BF16), lrw_ref[...]) + lrb_ref[...]
        la = _log_sigmoid(z) * GLA_INV_TAU
        la_hi, la_lo = _split_bf16(la, 2)
        b = _dot(tri_bd, la_hi) + _dot(tri_bd, la_lo)
        b_tot = _dot(ones_bd, la_hi) + _dot(ones_bd, la_lo)
        q = q_ref[rows, :].astype(F32)
        k = k_ref[rows, :].astype(F32)
        qe = (q * (GLA_DK ** -0.5)) * jnp.exp(b)
        ke = (k * jnp.exp(-b)).astype(BF16)
        kd = (k * jnp.exp(b_tot - b)).astype(BF16)
        decay = jnp.exp(b_tot)
        v = v_ref[rows, :]
        qe_bf = qe.astype(BF16)

        o_parts = []
        for h in range(GLA_HEADS):
            qh = jnp.where(head_of_col == h, qe, 0.0).astype(BF16)
            a = jnp.where(causal_bd, _dot_nt(qh, ke), 0.0).astype(BF16)
            o_parts.append(_dot(a, v[:, h * GLA_DV:(h + 1) * GLA_DV]))
        o = jnp.concatenate(o_parts, axis=-1)

        o_inter = []
        for c in range(r // GLA_CHUNK):
            cr = slice(c * GLA_CHUNK, (c + 1) * GLA_CHUNK)
            st = st_ref[...]
            o_inter.append(_dot_nt(qe_bf[cr], st.astype(BF16)))
            ds_t = _dot_tn(v[cr], kd[cr])
            st_ref[...] = st * decay[c * GLA_CHUNK:c * GLA_CHUNK + 1] + jnp.where(st_mask, ds_t, 0.0)
        o = o + jnp.concatenate(o_inter, axis=0)

        og = og_ref[rows, :].astype(F32)
        gate = og * jax.nn.sigmoid(og)
        for h in range(GLA_HEADS):
            hs = slice(h * GLA_DV, (h + 1) * GLA_DV)
            y_ref[rows, hs] = (_rms(o[:, hs], ong_ref[...]) * gate[:, hs]).astype(y_ref.dtype)


def _gla(gq, gk, gv, gog, misc, lrw_pad, lrb, ong, b, s):
    dkw, dvw = GLA_HEADS * GLA_DK, GLA_HEADS * GLA_DV
    seq = lambda w: pl.BlockSpec((s, w), lambda i: (i, 0))
    whole = lambda a: pl.BlockSpec(a.shape, lambda i: (0, 0))
    return pl.pallas_call(
        _gla_kernel,
        grid=(b,),
        in_specs=[seq(dkw), seq(dkw), seq(dvw), seq(dvw), seq(LANES), whole(lrw_pad), whole(lrb), whole(ong)],
        out_specs=seq(dvw),
        out_shape=jax.ShapeDtypeStruct((b * s, dvw), BF16),
        scratch_shapes=[pltpu.VMEM((dvw, dkw), F32)],
        compiler_params=pltpu.CompilerParams(dimension_semantics=("arbitrary",), vmem_limit_bytes=VMEM_LIMIT),
        name="gla",
    )(gq, gk, gv, gog, misc, lrw_pad, lrb, ong)


def _fox_kernel(q_ref, k_ref, v_ref, misc_ref, bf_ref, qg_ref, kg_ref, y_ref,
                c_scr, qa_scr, ka_scr, s_scr, m_scr, l_scr, acc_scr):
    s_len = q_ref.shape[0]
    t = FOX_TILE
    n_tiles = s_len // t
    hp = pl.program_id(1)
    lane = lax.broadcasted_iota(jnp.int32, (s_len, LANES), 1)

    trow = lax.broadcasted_iota(jnp.int32, (t, t), 0)
    tcol = lax.broadcasted_iota(jnp.int32, (t, t), 1)
    tri = (tcol <= trow).astype(BF16)
    carry = jnp.zeros((1, LANES), F32)
    for blk in range(n_tiles):
        rows = slice(blk * t, (blk + 1) * t)
        logf = _log_sigmoid(misc_ref[rows, :] + bf_ref[...])
        cb = carry
        for piece in _split_bf16(logf, 3):
            cb = cb + _dot(tri, piece)
        c_scr[rows, :] = cb
        carry = cb[t - 1:t]

    q = q_ref[...].astype(F32)
    k = k_ref[...].astype(F32)
    for e in range(2):
        in_head = (lane >= FOX_DH) if e else (lane < FOX_DH)
        piece0 = 0 if e else FOX_DH
        c_col = jnp.sum(jnp.where(lane == MISC_F0 + 2 * hp + e, c_scr[...], 0.0), axis=-1, keepdims=True)
        nc_pieces = _split_bf16(-c_col, 3)

        def head_norm(x, g):
            ms = jnp.sum(jnp.where(in_head, x * x, 0.0), axis=-1, keepdims=True) * (1.0 / FOX_DH)
            return x * lax.rsqrt(ms + EPS) * g

        qn = head_norm(q, qg_ref[...]) * (FOX_DH ** -0.5)
        kn = head_norm(k, kg_ref[...])
        is_piece = (lane >= piece0) & (lane < piece0 + 3)
        qa_scr[...] = jnp.where(in_head, qn, jnp.where(is_piece, 1.0, 0.0)).astype(BF16)
        ka = jnp.where(in_head, kn, 0.0).astype(BF16)
        for j, p in enumerate(nc_pieces):
            ka = jnp.where(lane == piece0 + j, p, ka)
        ka_scr[...] = ka

        @pl.loop(0, n_tiles)
        def _(i):
            qrows = pl.ds(pl.multiple_of(i * t, t), t)
            qa = qa_scr[qrows, :]
            m_scr[...] = jnp.full(m_scr.shape, -jnp.inf, F32)

            def tile_max(sc):
                return jnp.maximum(sc[:, :LANES], sc[:, LANES:])

            @pl.loop(0, i)
            def _(kb):
                krows = pl.ds(pl.multiple_of(kb * t, t), t)
                sc = _dot_nt(qa, ka_scr[krows, :])
                s_scr[kb] = sc
                m_scr[...] = jnp.maximum(m_scr[...], tile_max(sc))

            sc = jnp.where(tcol <= trow, _dot_nt(qa, ka_scr[qrows, :]), -jnp.inf)
            s_scr[i] = sc
            m = jnp.max(jnp.maximum(m_scr[...], tile_max(sc)), axis=-1, keepdims=True)

            l_scr[...] = jnp.zeros_like(l_scr)
            acc_scr[...] = jnp.zeros_like(acc_scr)

            @pl.loop(0, i + 1)
            def _(kb):
                krows = pl.ds(pl.multiple_of(kb * t, t), t)
                p = jnp.exp(s_scr[kb] - m)
                l_scr[...] += p[:, :LANES] + p[:, LANES:]
                acc_scr[...] += _dot(p.astype(BF16), v_ref[krows, :])

            o = acc_scr[...] / jnp.sum(l_scr[...], axis=-1, keepdims=True)
            if e == 0:
                y_ref[qrows, :] = o.astype(y_ref.dtype)
            else:
                tl = lax.broadcasted_iota(jnp.int32, (t, LANES), 1)
                y_ref[qrows, :] = jnp.where(tl >= FOX_DH, o, y_ref[qrows, :].astype(F32)).astype(y_ref.dtype)


def _fox(fq, fk, fv, misc, bf_pad, qg2, kg2, b, s):
    n_pairs = FOX_HEADS * FOX_DH // LANES
    pair = pl.BlockSpec((s, LANES), lambda i, j: (i, j))
    whole = lambda a: pl.BlockSpec(a.shape, lambda i, j: (0, 0))
    t = FOX_TILE
    return pl.pallas_call(
        _fox_kernel,
        grid=(b, n_pairs),
        in_specs=[pair, pair, pair, pl.BlockSpec((s, LANES), lambda i, j: (i, 0)),
                  whole(bf_pad), whole(qg2), whole(kg2)],
        out_specs=pair,
        out_shape=jax.ShapeDtypeStruct((b * s, FOX_HEADS * FOX_DH), BF16),
        scratch_shapes=[
            pltpu.VMEM((s, LANES), F32),
            pltpu.VMEM((s, LANES), BF16),
            pltpu.VMEM((s, LANES), BF16),
            pltpu.VMEM((s // t, t, t), F32),
            pltpu.VMEM((t, LANES), F32),
            pltpu.VMEM((t, LANES), F32),
            pltpu.VMEM((t, LANES), F32),
        ],
        compiler_params=pltpu.CompilerParams(dimension_semantics=("arbitrary", "arbitrary"),
                                             vmem_limit_bytes=VMEM_LIMIT),
        name="fox",
    )(fq, fk, fv, misc, bf_pad, qg2, kg2)


def _post_kernel(x_ref, yg_ref, yf_ref, p_ref, wo_ref, g2_ref, wup_ref, cw_ref, cb_ref, wdn_ref,
                 g3_ref, wpg_ref, bpg_ref, wpe_ref, gpe_ref, o_ref,
                 x1_scr, h2_scr, u_scr, carry_scr, acc_scr):
    tm = x_ref.shape[0]
    n_chunks = wdn_ref.shape[0] // FFN_CHUNK
    dvw = yg_ref.shape[1]
    first_tile = pl.program_id(1) == 0

    x1 = x_ref[...] + _dot(yg_ref[...], wo_ref[:dvw, :]) + _dot(yf_ref[...], wo_ref[dvw:, :])
    x1_scr[...] = x1
    h2_scr[...] = _rms(x1, g2_ref[...]).astype(BF16)
    acc_scr[...] = jnp.zeros_like(acc_scr)

    @pl.when(first_tile)
    def _():
        carry_scr[...] = jnp.zeros_like(carry_scr)

    for j in range(n_chunks):
        cols = slice(j * 2 * FFN_CHUNK, (j + 1) * 2 * FFN_CHUNK)
        u = _dot(h2_scr[...], wup_ref[:, cols])
        u_scr[:CARRY_ROWS, :] = carry_scr[j]
        u_scr[CARRY_ROWS:, :] = u
        carry_scr[j] = u[tm - CARRY_ROWS:, :]
        uc = (cb_ref[:, cols]
              + u_scr[CARRY_ROWS - 2:CARRY_ROWS - 2 + tm, :] * cw_ref[0:1, cols]
              + u_scr[CARRY_ROWS - 1:CARRY_ROWS - 1 + tm, :] * cw_ref[1:2, cols]
              + u * cw_ref[2:3, cols])
        gate = uc[:, :FFN_CHUNK]
        act = (gate * jax.nn.sigmoid(gate) * uc[:, FFN_CHUNK:]).astype(BF16)
        acc_scr[...] += _dot(act, wdn_ref[j * FFN_CHUNK:(j + 1) * FFN_CHUNK, :])

    x2 = x1_scr[...] + acc_scr[...]
    gate = jax.nn.sigmoid(_dot(_rms(x2, g3_ref[...]).astype(BF16), wpg_ref[...]) + bpg_ref[...])
    emb = _rms(_dot(p_ref[...].astype(BF16), wpe_ref[...]), gpe_ref[...])
    o_ref[...] = x2 + gate * emb


def _post(x2d, yg, yf, p2d, wo, g2, wup, cw, cb, wdn, g3, wpg, bpg, wpe, gpe, b, s, tm):
    d = x2d.shape[1]
    nt = s // tm
    tile = lambda w: pl.BlockSpec((tm, w), lambda i, j: (i * nt + j, 0))
    resident = lambda a: pl.BlockSpec(a.shape, lambda i, j: (0,) * a.ndim, pipeline_mode=pl.Buffered(1))
    n_chunks = wdn.shape[0] // FFN_CHUNK
    return pl.pallas_call(
        _post_kernel,
        grid=(b, nt),
        in_specs=[tile(d), tile(yg.shape[1]), tile(yf.shape[1]), tile(p2d.shape[1])]
                 + [resident(a) for a in (wo, g2, wup, cw, cb, wdn, g3, wpg, bpg, wpe, gpe)],
        out_specs=tile(d),
        out_shape=jax.ShapeDtypeStruct(x2d.shape, F32),
        scratch_shapes=[
            pltpu.VMEM((tm, d), F32),
            pltpu.VMEM((tm, d), BF16),
            pltpu.VMEM((CARRY_ROWS + tm, 2 * FFN_CHUNK), F32),
            pltpu.VMEM((n_chunks, CARRY_ROWS, 2 * FFN_CHUNK), F32),
            pltpu.VMEM((tm, d), F32),
        ],
        compiler_params=pltpu.CompilerParams(dimension_semantics=("arbitrary", "arbitrary"),
                                             vmem_limit_bytes=VMEM_LIMIT),
        name="post",
    )(x2d, yg, yf, p2d, wo, g2, wup, cw, cb, wdn, g3, wpg, bpg, wpe, gpe)


def _interleave_chunks(a, d_ff):
    lead = a.shape[:-1]
    n = d_ff // FFN_CHUNK
    g = a[..., :d_ff].reshape(lead + (n, FFN_CHUNK))
    v = a[..., d_ff:].reshape(lead + (n, FFN_CHUNK))
    return jnp.concatenate([g, v], axis=-1).reshape(lead + (2 * d_ff,))


def _layer(x2d, p2d, b, s, norm1_g, w_in, lr_w, lr_b, onorm_g, b_f, qn_g, kn_g, w_o, norm2_g, w_up,
           conv_w, conv_b, w_down, norm3_g, w_pe, pe_norm_g, w_pg, b_pg):
    d = x2d.shape[1]
    d_ff = w_down.shape[0]
    gqk, gvw, fw = GLA_HEADS * GLA_DK, GLA_HEADS * GLA_DV, FOX_HEADS * FOX_DH
    sizes = (gqk, gqk, gvw, gvw, GLA_LOWRANK, fw, fw, fw, FOX_HEADS)
    offs = [0]
    for n in sizes:
        offs.append(offs[-1] + n)
    w_gq, w_gk, w_gv, w_gog, w_lr, w_fq, w_fk, w_fv, w_ff = (w_in[:, offs[i]:offs[i + 1]] for i in range(9))
    w_misc = jnp.zeros((d, LANES), w_in.dtype)
    w_misc = w_misc.at[:, MISC_F0:MISC_F0 + FOX_HEADS].set(w_ff).at[:, MISC_LR0:MISC_LR0 + GLA_LOWRANK].set(w_lr)
    w_all = jnp.concatenate([w_gq, w_gk, w_gv, w_gog, w_fq, w_fk, w_fv, w_misc], axis=1).astype(BF16)

    row = lambda a: a.reshape(1, -1).astype(F32)
    gq, gk, gv, gog, fq, fk, fv, misc = _inproj(x2d, row(norm1_g), w_all, tm=512)

    lrw_pad = jnp.zeros((LANES, gqk), F32).at[MISC_LR0:MISC_LR0 + GLA_LOWRANK].set(lr_w).astype(BF16)
    y_gla = _gla(gq, gk, gv, gog, misc, lrw_pad, row(lr_b), row(onorm_g), b, s)

    bf_pad = jnp.zeros((1, LANES), F32).at[0, MISC_F0:MISC_F0 + FOX_HEADS].set(b_f)
    y_fox = _fox(fq, fk, fv, misc, bf_pad, row(jnp.tile(qn_g, 2)), row(jnp.tile(kn_g, 2)), b, s)

    return _post(
        x2d, y_gla, y_fox, p2d, w_o.astype(BF16), row(norm2_g),
        _interleave_chunks(w_up, d_ff).astype(BF16), _interleave_chunks(conv_w, d_ff).astype(F32),
        _interleave_chunks(conv_b.reshape(1, -1), d_ff).astype(F32), w_down.astype(BF16),
        row(norm3_g), w_pg.astype(BF16), row(b_pg), w_pe.astype(BF16), row(pe_norm_g), b, s, tm=512)


def kernel(x, p, norm1_g, w_in, gla_lr_w, gla_lr_b, gla_onorm_g, fox_b_f, fox_qnorm_g, fox_knorm_g, w_o, norm2_g, w_up, conv_w, conv_b, w_down, norm3_g, w_pe, pe_norm_g, w_pg, b_pg):
    b, s, d = x.shape
    x2d = x.reshape(b * s, d)
    for i in range(p.shape[0]):
        x2d = _layer(x2d, p[i].reshape(b * s, -1), b, s, norm1_g[i], w_in[i], gla_lr_w[i], gla_lr_b[i],
                     gla_onorm_g[i], fox_b_f[i], fox_qnorm_g[i], fox_knorm_g[i], w_o[i], norm2_g[i], w_up[i],
                     conv_w[i], conv_b[i], w_down[i], norm3_g[i], w_pe[i], pe_norm_g[i], w_pg[i], b_pg[i])
    return x2d.reshape(b, s, d)
```

```python
import functools

import jax
import jax.numpy as jnp
from jax import lax
from jax.experimental import pallas as pl
from jax.experimental.pallas import tpu as pltpu

F32 = jnp.float32
BF16 = jnp.bfloat16

EPS = 1e-6
LANES = 128
GLA_HEADS, GLA_DK, GLA_DV = 4, 64, 128
GLA_LOWRANK = 16
GLA_INV_TAU = 1.0 / 16.0
GLA_CHUNK = 64
GLA_SUPER = 256
FOX_HEADS, FOX_DH = 8, 64
FOX_TILE = 256
CONV_W = 3
FFN_CHUNK = 256
CARRY_ROWS = 8
MISC_F0 = 0
MISC_LR0 = 8
VMEM_LIMIT = 56 * 1024 * 1024


def _dot(a, b):
    return jnp.dot(a, b, preferred_element_type=F32)


def _dot_nt(a, b):
    return lax.dot_general(a, b, (((1,), (1,)), ((), ())), preferred_element_type=F32)


def _dot_tn(a, b):
    return lax.dot_general(a, b, (((0,), (0,)), ((), ())), preferred_element_type=F32)


def _log_sigmoid(z):
    return jnp.minimum(z, 0.0) - jnp.log1p(jnp.exp(-jnp.abs(z)))


def _split_bf16(x, pieces):
    out = []
    for _ in range(pieces):
        p = x.astype(BF16)
        out.append(p)
        x = x - p.astype(F32)
    return out


def _rms(x, gain):
    ms = jnp.mean(x * x, axis=-1, keepdims=True)
    return x * lax.rsqrt(ms + EPS) * gain


def _inproj_kernel(x_ref, g_ref, w_ref, gq_ref, gk_ref, gv_ref, gog_ref, fq_ref, fk_ref, fv_ref, misc_ref):
    h = _rms(x_ref[...], g_ref[...]).astype(BF16)
    off = 0
    for ref in (gq_ref, gk_ref, gv_ref, gog_ref, fq_ref, fk_ref, fv_ref, misc_ref):
        n = ref.shape[-1]
        ref[...] = _dot(h, w_ref[:, off:off + n]).astype(ref.dtype)
        off += n


def _inproj(x2d, g, w_all, tm):
    t, d = x2d.shape
    widths = (256, 256, 512, 512, 512, 512, 512, LANES)
    dtypes = (BF16,) * 7 + (F32,)
    return pl.pallas_call(
        _inproj_kernel,
        grid=(t // tm,),
        in_specs=[
            pl.BlockSpec((tm, d), lambda i: (i, 0)),
            pl.BlockSpec((1, d), lambda i: (0, 0)),
            pl.BlockSpec(w_all.shape, lambda i: (0, 0)),
        ],
        out_specs=[pl.BlockSpec((tm, n), lambda i: (i, 0)) for n in widths],
        out_shape=[jax.ShapeDtypeStruct((t, n), dt) for n, dt in zip(widths, dtypes)],
        compiler_params=pltpu.CompilerParams(dimension_semantics=("arbitrary",), vmem_limit_bytes=VMEM_LIMIT),
        name="inproj",
    )(x2d, g, w_all)


def _gla_kernel(q_ref, k_ref, v_ref, og_ref, misc_ref, lrw_ref, lrb_ref, ong_ref, y_ref, st_ref):
    s_len = q_ref.shape[0]
    r = GLA_SUPER
    dkw = GLA_HEADS * GLA_DK
    dvw = GLA_HEADS * GLA_DV
    row = lax.broadcasted_iota(jnp.int32, (r, r), 0)
    col = lax.broadcasted_iota(jnp.int32, (r, r), 1)
    same_chunk = (row // GLA_CHUNK) == (col // GLA_CHUNK)
    causal_bd = same_chunk & (col <= row)
    tri_bd = causal_bd.astype(BF16)
    ones_bd = same_chunk.astype(BF16)
    st_row = lax.broadcasted_iota(jnp.int32, (dvw, dkw), 0)
    st_col = lax.broadcasted_iota(jnp.int32, (dvw, dkw), 1)
    st_mask = (st_row // GLA_DV) == (st_col // GLA_DK)
    head_of_col = lax.broadcasted_iota(jnp.int32, (r, dkw), 1) // GLA_DK

    st_ref[...] = jnp.zeros_like(st_ref)

    @pl.loop(0, s_len // r)
    def _(sb):
        rows = pl.ds(pl.multiple_of(sb * r, r), r)
        z = _dot(misc_ref[rows, :].astype(BF16), lrw_ref[...]) + lrb_ref[...]
        la = _log_sigmoid(z) * GLA_INV_TAU
        la_hi, la_lo = _split_bf16(la, 2)
        b = _dot(tri_bd, la_hi) + _dot(tri_bd, la_lo)
        b_tot = _dot(ones_bd, la_hi) + _dot(ones_bd, la_lo)
        q = q_ref[rows, :].astype(F32)
        k = k_ref[rows, :].astype(F32)
        qe = (q * (GLA_DK ** -0.5)) * jnp.exp(b)
        ke = (k * jnp.exp(-b)).astype(BF16)
        kd = (k * jnp.exp(b_tot - b)).astype(BF16)
        decay = jnp.exp(b_tot)
        v = v_ref[rows, :]
        qe_bf = qe.astype(BF16)

        o_parts = []
        for h in range(GLA_HEADS):
            qh = jnp.where(head_of_col == h, qe, 0.0).astype(BF16)
            a = jnp.where(causal_bd, _dot_nt(qh, ke), 0.0).astype(BF16)
            o_parts.append(_dot(a, v[:, h * GLA_DV:(h + 1) * GLA_DV]))
        o = jnp.concatenate(o_parts, axis=-1)

        o_inter = []
        for c in range(r // GLA_CHUNK):
            cr = slice(c * GLA_CHUNK, (c + 1) * GLA_CHUNK)
            st = st_ref[...]
            o_inter.append(_dot_nt(qe_bf[cr], st.astype(BF16)))
            ds_t = _dot_tn(v[cr], kd[cr])
            st_ref[...] = st * decay[c * GLA_CHUNK:c * GLA_CHUNK + 1] + jnp.where(st_mask, ds_t, 0.0)
        o = o + jnp.concatenate(o_inter, axis=0)

        og = og_ref[rows, :].astype(F32)
        gate = og * jax.nn.sigmoid(og)
        for h in range(GLA_HEADS):
            hs = slice(h * GLA_DV, (h + 1) * GLA_DV)
            y_ref[rows, hs] = (_rms(o[:, hs], ong_ref[...]) * gate[:, hs]).astype(y_ref.dtype)


def _gla(gq, gk, gv, gog, misc, lrw_pad, lrb, ong, b, s):
    dkw, dvw = GLA_HEADS * GLA_DK, GLA_HEADS * GLA_DV
    seq = lambda w: pl.BlockSpec((s, w), lambda i: (i, 0))
    whole = lambda a: pl.BlockSpec(a.shape, lambda i: (0, 0))
    return pl.pallas_call(
        _gla_kernel,
        grid=(b,),
        in_specs=[seq(dkw), seq(dkw), seq(dvw), seq(dvw), seq(LANES), whole(lrw_pad), whole(lrb), whole(ong)],
        out_specs=seq(dvw),
        out_shape=jax.ShapeDtypeStruct((b * s, dvw), BF16),
        scratch_shapes=[pltpu.VMEM((dvw, dkw), F32)],
        compiler_params=pltpu.CompilerParams(dimension_semantics=("arbitrary",), vmem_limit_bytes=VMEM_LIMIT),
        name="gla",
    )(gq, gk, gv, gog, misc, lrw_pad, lrb, ong)


def _fox_kernel(q_ref, k_ref, v_ref, misc_ref, bf_ref, qg_ref, kg_ref, y_ref,
                c_scr, qa_scr, ka_scr, vt_scr, s_scr):
    s_len = q_ref.shape[0]
    t = FOX_TILE
    n_tiles = s_len // t
    hp = pl.program_id(1)
    lane = lax.broadcasted_iota(jnp.int32, (s_len, LANES), 1)
    krow = lax.broadcasted_iota(jnp.int32, (t, t), 0)
    qcol = lax.broadcasted_iota(jnp.int32, (t, t), 1)

    @pl.when(hp == 0)
    def _():
        tri = (qcol <= krow).astype(BF16)
        carry = jnp.zeros((1, LANES), F32)
        for blk in range(n_tiles):
            rows = slice(blk * t, (blk + 1) * t)
            logf = _log_sigmoid(misc_ref[rows, :] + bf_ref[...])
            cb = carry
            for piece in _split_bf16(logf, 3):
                cb = cb + _dot(tri, piece)
            c_scr[rows, :] = cb
            carry = cb[t - 1:t]

    lower = lane < FOX_DH

    def pair_norm(x, g):
        sq = x * x
        ms_lo = jnp.sum(jnp.where(lower, sq, 0.0), axis=-1, keepdims=True)
        ms_hi = jnp.sum(jnp.where(lower, 0.0, sq), axis=-1, keepdims=True)
        ms = jnp.where(lower, ms_lo, ms_hi) * (1.0 / FOX_DH)
        return x * lax.rsqrt(ms + EPS) * g

    qn = pair_norm(q_ref[...].astype(F32), qg_ref[...]) * (FOX_DH ** -0.5)
    kn = pair_norm(k_ref[...].astype(F32), kg_ref[...])
    c_all = c_scr[...]
    for e in range(2):
        in_head = (lane >= FOX_DH) if e else lower
        piece0 = 0 if e else FOX_DH
        c_col = jnp.sum(jnp.where(lane == MISC_F0 + 2 * hp + e, c_all, 0.0), axis=-1, keepdims=True)
        nc_pieces = _split_bf16(-c_col, 3)
        is_piece = (lane >= piece0) & (lane < piece0 + 3)
        qa_scr[e] = jnp.where(in_head, qn, jnp.where(is_piece, 1.0, 0.0)).astype(BF16)
        ka = jnp.where(in_head, kn, 0.0).astype(BF16)
        for j, p in enumerate(nc_pieces):
            ka = jnp.where(lane == piece0 + j, p, ka)
        ka_scr[e] = ka
    vt_scr[...] = v_ref[...].astype(F32).T.astype(BF16)

    orow = lax.broadcasted_iota(jnp.int32, (LANES, t), 0)
    for i in range(n_tiles):
        o_t = []
        for e in range(2):
            qa = qa_scr[e, i * t:(i + 1) * t, :]
            mp = None
            for kb in range(i + 1):
                sc = _dot_nt(ka_scr[e, kb * t:(kb + 1) * t, :], qa)
                if kb == i:
                    sc = jnp.where(krow <= qcol, sc, -jnp.inf)
                s_scr[e, kb] = sc
                bm = jnp.max(sc.reshape(t // 8, 8, t), axis=0)
                mp = bm if mp is None else jnp.maximum(mp, bm)
            m = jnp.max(mp, axis=0, keepdims=True)
            lp = jnp.zeros((8, t), F32)
            acc = jnp.zeros((LANES, t), F32)
            for kb in range(i + 1):
                p = jnp.exp(s_scr[e, kb] - m)
                lp = lp + jnp.sum(p.reshape(t // 8, 8, t), axis=0)
                acc = acc + _dot(vt_scr[:, kb * t:(kb + 1) * t], p.astype(BF16))
            o_t.append(acc / jnp.sum(lp, axis=0, keepdims=True))
        o_pair = jnp.where(orow < FOX_DH, o_t[0], o_t[1])
        y_ref[i * t:(i + 1) * t, :] = o_pair.T.astype(y_ref.dtype)


def _fox(fq, fk, fv, misc, bf_pad, qg2, kg2, b, s):
    n_pairs = FOX_HEADS * FOX_DH // LANES
    pair = pl.BlockSpec((s, LANES), lambda i, j: (i, j))
    whole = lambda a: pl.BlockSpec(a.shape, lambda i, j: (0, 0))
    t = FOX_TILE
    return pl.pallas_call(
        _fox_kernel,
        grid=(b, n_pairs),
        in_specs=[pair, pair, pair, pl.BlockSpec((s, LANES), lambda i, j: (i, 0)),
                  whole(bf_pad), whole(qg2), whole(kg2)],
        out_specs=pair,
        out_shape=jax.ShapeDtypeStruct((b * s, FOX_HEADS * FOX_DH), BF16),
        scratch_shapes=[
            pltpu.VMEM((s, LANES), F32),
            pltpu.VMEM((2, s, LANES), BF16),
            pltpu.VMEM((2, s, LANES), BF16),
            pltpu.VMEM((LANES, s), BF16),
            pltpu.VMEM((2, s // t, t, t), F32),
        ],
        compiler_params=pltpu.CompilerParams(dimension_semantics=("arbitrary", "arbitrary"),
                                             vmem_limit_bytes=VMEM_LIMIT),
        name="fox",
    )(fq, fk, fv, misc, bf_pad, qg2, kg2)


def _post_kernel(x_ref, yg_ref, yf_ref, p_ref, wo_ref, g2_ref, wup_ref, cw_ref, cb_ref, wdn_ref,
                 g3_ref, wpg_ref, bpg_ref, wpe_ref, gpe_ref, o_ref,
                 x1_scr, h2_scr, u_scr, carry_scr, acc_scr):
    tm = x_ref.shape[0]
    n_chunks = wdn_ref.shape[0] // FFN_CHUNK
    dvw = yg_ref.shape[1]
    first_tile = pl.program_id(1) == 0

    x1 = x_ref[...] + _dot(yg_ref[...], wo_ref[:dvw, :]) + _dot(yf_ref[...], wo_ref[dvw:, :])
    x1_scr[...] = x1
    h2_scr[...] = _rms(x1, g2_ref[...]).astype(BF16)
    acc_scr[...] = jnp.zeros_like(acc_scr)

    @pl.when(first_tile)
    def _():
        carry_scr[...] = jnp.zeros_like(carry_scr)

    for j in range(n_chunks):
        cols = slice(j * 2 * FFN_CHUNK, (j + 1) * 2 * FFN_CHUNK)
        u = _dot(h2_scr[...], wup_ref[:, cols])
        u_scr[:CARRY_ROWS, :] = carry_scr[j]
        u_scr[CARRY_ROWS:, :] = u
        carry_scr[j] = u[tm - CARRY_ROWS:, :]
        uc = (cb_ref[:, cols]
              + u_scr[CARRY_ROWS - 2:CARRY_ROWS - 2 + tm, :] * cw_ref[0:1, cols]
              + u_scr[CARRY_ROWS - 1:CARRY_ROWS - 1 + tm, :] * cw_ref[1:2, cols]
              + u * cw_ref[2:3, cols])
        gate = uc[:, :FFN_CHUNK]
        act = (gate * jax.nn.sigmoid(gate) * uc[:, FFN_CHUNK:]).astype(BF16)
        acc_scr[...] += _dot(act, wdn_ref[j * FFN_CHUNK:(j + 1) * FFN_CHUNK, :])

    x2 = x1_scr[...] + acc_scr[...]
    gate = jax.nn.sigmoid(_dot(_rms(x2, g3_ref[...]).astype(BF16), wpg_ref[...]) + bpg_ref[...])
    emb = _rms(_dot(p_ref[...].astype(BF16), wpe_ref[...]), gpe_ref[...])
    o_ref[...] = x2 + gate * emb


def _post(x2d, yg, yf, p2d, wo, g2, wup, cw, cb, wdn, g3, wpg, bpg, wpe, gpe, b, s, tm):
    d = x2d.shape[1]
    nt = s // tm
    tile = lambda w: pl.BlockSpec((tm, w), lambda i, j: (i * nt + j, 0))
    resident = lambda a: pl.BlockSpec(a.shape, lambda i, j: (0,) * a.ndim, pipeline_mode=pl.Buffered(1))
    n_chunks = wdn.shape[0] // FFN_CHUNK
    return pl.pallas_call(
        _post_kernel,
        grid=(b, nt),
        in_specs=[tile(d), tile(yg.shape[1]), tile(yf.shape[1]), tile(p2d.shape[1])]
                 + [resident(a) for a in (wo, g2, wup, cw, cb, wdn, g3, wpg, bpg, wpe, gpe)],
        out_specs=tile(d),
        out_shape=jax.ShapeDtypeStruct(x2d.shape, F32),
        scratch_shapes=[
            pltpu.VMEM((tm, d), F32),
            pltpu.VMEM((tm, d), BF16),
            pltpu.VMEM((CARRY_ROWS + tm, 2 * FFN_CHUNK), F32),
            pltpu.VMEM((n_chunks, CARRY_ROWS, 2 * FFN_CHUNK), F32),
            pltpu.VMEM((tm, d), F32),
        ],
        compiler_params=pltpu.CompilerParams(dimension_semantics=("arbitrary", "arbitrary"),
                                             vmem_limit_bytes=VMEM_LIMIT),
        name="post",
    )(x2d, yg, yf, p2d, wo, g2, wup, cw, cb, wdn, g3, wpg, bpg, wpe, gpe)


def _interleave_chunks(a, d_ff):
    lead = a.shape[:-1]
    n = d_ff // FFN_CHUNK
    g = a[..., :d_ff].reshape(lead + (n, FFN_CHUNK))
    v = a[..., d_ff:].reshape(lead + (n, FFN_CHUNK))
    return jnp.concatenate([g, v], axis=-1).reshape(lead + (2 * d_ff,))


def _layer(x2d, p2d, b, s, norm1_g, w_in, lr_w, lr_b, onorm_g, b_f, qn_g, kn_g, w_o, norm2_g, w_up,
           conv_w, conv_b, w_down, norm3_g, w_pe, pe_norm_g, w_pg, b_pg):
    d = x2d.shape[1]
    d_ff = w_down.shape[0]
    gqk, gvw, fw = GLA_HEADS * GLA_DK, GLA_HEADS * GLA_DV, FOX_HEADS * FOX_DH
    sizes = (gqk, gqk, gvw, gvw, GLA_LOWRANK, fw, fw, fw, FOX_HEADS)
    offs = [0]
    for n in sizes:
        offs.append(offs[-1] + n)
    w_gq, w_gk, w_gv, w_gog, w_lr, w_fq, w_fk, w_fv, w_ff = (w_in[:, offs[i]:offs[i + 1]] for i in range(9))
    w_misc = jnp.zeros((d, LANES), w_in.dtype)
    w_misc = w_misc.at[:, MISC_F0:MISC_F0 + FOX_HEADS].set(w_ff).at[:, MISC_LR0:MISC_LR0 + GLA_LOWRANK].set(w_lr)
    w_all = jnp.concatenate([w_gq, w_gk, w_gv, w_gog, w_fq, w_fk, w_fv, w_misc], axis=1).astype(BF16)

    row = lambda a: a.reshape(1, -1).astype(F32)
    gq, gk, gv, gog, fq, fk, fv, misc = _inproj(x2d, row(norm1_g), w_all, tm=512)

    lrw_pad = jnp.zeros((LANES, gqk), F32).at[MISC_LR0:MISC_LR0 + GLA_LOWRANK].set(lr_w).astype(BF16)
    y_gla = _gla(gq, gk, gv, gog, misc, lrw_pad, row(lr_b), row(onorm_g), b, s)

    bf_pad = jnp.zeros((1, LANES), F32).at[0, MISC_F0:MISC_F0 + FOX_HEADS].set(b_f)
    y_fox = _fox(fq, fk, fv, misc, bf_pad, row(jnp.tile(qn_g, 2)), row(jnp.tile(kn_g, 2)), b, s)

    return _post(
        x2d, y_gla, y_fox, p2d, w_o.astype(BF16), row(norm2_g),
        _interleave_chunks(w_up, d_ff).astype(BF16), _interleave_chunks(conv_w, d_ff).astype(F32),
        _interleave_chunks(conv_b.reshape(1, -1), d_ff).astype(F32), w_down.astype(BF16),
        row(norm3_g), w_pg.astype(BF16), row(b_pg), w_pe.astype(BF16), row(pe_norm_g), b, s, tm=512)


def kernel(x, p, norm1_g, w_in, gla_lr_w, gla_lr_b, gla_onorm_g, fox_b_f, fox_qnorm_g, fox_knorm_g, w_o, norm2_g, w_up, conv_w, conv_b, w_down, norm3_g, w_pe, pe_norm_g, w_pg, b_pg):
    b, s, d = x.shape
    x2d = x.reshape(b * s, d)
    for i in range(p.shape[0]):
        x2d = _layer(x2d, p[i].reshape(b * s, -1), b, s, norm1_g[i], w_in[i], gla_lr_w[i], gla_lr_b[i],
                     gla_onorm_g[i], fox_b_f[i], fox_qnorm_g[i], fox_knorm_g[i], w_o[i], norm2_g[i], w_up[i],
                     conv_w[i], conv_b[i], w_down[i], norm3_g[i], w_pe[i], pe_norm_g[i], w_pg[i], b_pg[i])
    return x2d.reshape(b, s, d)
```

```python
import functools

import jax
import jax.numpy as jnp
from jax import lax
from jax.experimental import pallas as pl
from jax.experimental.pallas import tpu as pltpu

F32 = jnp.float32
BF16 = jnp.bfloat16

EPS = 1e-6
LANES = 128
GLA_HEADS, GLA_DK, GLA_DV = 4, 64, 128
GLA_LOWRANK = 16
GLA_INV_TAU = 1.0 / 16.0
GLA_CHUNK = 64
GLA_SUPER = 256
FOX_HEADS, FOX_DH = 8, 64
FOX_TILE = 256
CONV_W = 3
FFN_CHUNK = 256
CARRY_ROWS = 8
MISC_F0 = 0
MISC_LR0 = 8
VMEM_LIMIT = 56 * 1024 * 1024


def _dot(a, b):
    return jnp.dot(a, b, preferred_element_type=F32)


def _dot_nt(a, b):
    return lax.dot_general(a, b, (((1,), (1,)), ((), ())), preferred_element_type=F32)


def _dot_tn(a, b):
    return lax.dot_general(a, b, (((0,), (0,)), ((), ())), preferred_element_type=F32)


def _log_sigmoid(z):
    return jnp.minimum(z, 0.0) - jnp.log1p(jnp.exp(-jnp.abs(z)))


def _split_bf16(x, pieces):
    out = []
    for _ in range(pieces):
        p = x.astype(BF16)
        out.append(p)
        x = x - p.astype(F32)
    return out


def _rms(x, gain):
    ms = jnp.mean(x * x, axis=-1, keepdims=True)
    return x * lax.rsqrt(ms + EPS) * gain


def _inproj_kernel(x_ref, g_ref, w_ref, gq_ref, gk_ref, gv_ref, gog_ref, fq_ref, fk_ref, fv_ref, misc_ref):
    h = _rms(x_ref[...], g_ref[...]).astype(BF16)
    off = 0
    for ref in (gq_ref, gk_ref, gv_ref, gog_ref, fq_ref, fk_ref, fv_ref, misc_ref):
        n = ref.shape[-1]
        ref[...] = _dot(h, w_ref[:, off:off + n]).astype(ref.dtype)
        off += n


def _inproj(x2d, g, w_all, tm):
    t, d = x2d.shape
    widths = (256, 256, 512, 512, 512, 512, 512, LANES)
    dtypes = (BF16,) * 7 + (F32,)
    return pl.pallas_call(
        _inproj_kernel,
        grid=(t // tm,),
        in_specs=[
            pl.BlockSpec((tm, d), lambda i: (i, 0)),
            pl.BlockSpec((1, d), lambda i: (0, 0)),
            pl.BlockSpec(w_all.shape, lambda i: (0, 0)),
        ],
        out_specs=[pl.BlockSpec((tm, n), lambda i: (i, 0)) for n in widths],
        out_shape=[jax.ShapeDtypeStruct((t, n), dt) for n, dt in zip(widths, dtypes)],
        compiler_params=pltpu.CompilerParams(dimension_semantics=("arbitrary",), vmem_limit_bytes=VMEM_LIMIT),
        name="inproj",
    )(x2d, g, w_all)


def _gla_kernel(q_ref, k_ref, v_ref, og_ref, misc_ref, lrw_ref, lrb_ref, ong_ref, y_ref, st_ref):
    s_len = q_ref.shape[0]
    r = GLA_SUPER
    dkw = GLA_HEADS * GLA_DK
    dvw = GLA_HEADS * GLA_DV
    row = lax.broadcasted_iota(jnp.int32, (r, r), 0)
    col = lax.broadcasted_iota(jnp.int32, (r, r), 1)
    same_chunk = (row // GLA_CHUNK) == (col // GLA_CHUNK)
    causal_bd = same_chunk & (col <= row)
    tri_bd = causal_bd.astype(BF16)
    ones_bd = same_chunk.astype(BF16)
    st_row = lax.broadcasted_iota(jnp.int32, (dvw, dkw), 0)
    st_col = lax.broadcasted_iota(jnp.int32, (dvw, dkw), 1)
    st_mask = (st_row // GLA_DV) == (st_col // GLA_DK)
    head_of_col = lax.broadcasted_iota(jnp.int32, (r, dkw), 1) // GLA_DK

    st_ref[...] = jnp.zeros_like(st_ref)

    @pl.loop(0, s_len // r)
    def _(sb):
        rows = pl.ds(pl.multiple_of(sb * r, r), r)
        z = _dot(misc_ref[rows, :].astype(BF16), lrw_ref[...]) + lrb_ref[...]
        la = _log_sigmoid(z) * GLA_INV_TAU
        la_hi, la_lo = _split_bf16(la, 2)
        b = _dot(tri_bd, la_hi) + _dot(tri_bd, la_lo)
        b_tot = _dot(ones_bd, la_hi) + _dot(ones_bd, la_lo)
        q = q_ref[rows, :].astype(F32)
        k = k_ref[rows, :].astype(F32)
        qe = (q * (GLA_DK ** -0.5)) * jnp.exp(b)
        ke = (k * jnp.exp(-b)).astype(BF16)
        kd = (k * jnp.exp(b_tot - b)).astype(BF16)
        decay = jnp.exp(b_tot)
        v = v_ref[rows, :]
        qe_bf = qe.astype(BF16)

        o_parts = []
        for h in range(GLA_HEADS):
            qh = jnp.where(head_of_col == h, qe, 0.0).astype(BF16)
            a = jnp.where(causal_bd, _dot_nt(qh, ke), 0.0).astype(BF16)
            o_parts.append(_dot(a, v[:, h * GLA_DV:(h + 1) * GLA_DV]))
        o = jnp.concatenate(o_parts, axis=-1)

        o_inter = []
        for c in range(r // GLA_CHUNK):
            cr = slice(c * GLA_CHUNK, (c + 1) * GLA_CHUNK)
            st = st_ref[...]
            o_inter.append(_dot_nt(qe_bf[cr], st.astype(BF16)))
            ds_t = _dot_tn(v[cr], kd[cr])
            st_ref[...] = st * decay[c * GLA_CHUNK:c * GLA_CHUNK + 1] + jnp.where(st_mask, ds_t, 0.0)
        o = o + jnp.concatenate(o_inter, axis=0)

        og = og_ref[rows, :].astype(F32)
        gate = og * jax.nn.sigmoid(og)
        for h in range(GLA_HEADS):
            hs = slice(h * GLA_DV, (h + 1) * GLA_DV)
            y_ref[rows, hs] = (_rms(o[:, hs], ong_ref[...]) * gate[:, hs]).astype(y_ref.dtype)


def _gla(gq, gk, gv, gog, misc, lrw_pad, lrb, ong, b, s):
    dkw, dvw = GLA_HEADS * GLA_DK, GLA_HEADS * GLA_DV
    seq = lambda w: pl.BlockSpec((s, w), lambda i: (i, 0))
    whole = lambda a: pl.BlockSpec(a.shape, lambda i: (0, 0))
    return pl.pallas_call(
        _gla_kernel,
        grid=(b,),
        in_specs=[seq(dkw), seq(dkw), seq(dvw), seq(dvw), seq(LANES), whole(lrw_pad), whole(lrb), whole(ong)],
        out_specs=seq(dvw),
        out_shape=jax.ShapeDtypeStruct((b * s, dvw), BF16),
        scratch_shapes=[pltpu.VMEM((dvw, dkw), F32)],
        compiler_params=pltpu.CompilerParams(dimension_semantics=("arbitrary",), vmem_limit_bytes=VMEM_LIMIT),
        name="gla",
    )(gq, gk, gv, gog, misc, lrw_pad, lrb, ong)


def _fox_kernel(q_ref, k_ref, v_ref, misc_ref, bf_ref, qg_ref, kg_ref, y_ref,
                c_scr, qa_scr, ka_scr, vt_scr, s_scr):
    s_len = q_ref.shape[0]
    t = FOX_TILE
    n_tiles = s_len // t
    hp = pl.program_id(1)
    lane = lax.broadcasted_iota(jnp.int32, (s_len, LANES), 1)
    krow = lax.broadcasted_iota(jnp.int32, (t, t), 0)
    qcol = lax.broadcasted_iota(jnp.int32, (t, t), 1)

    @pl.when(hp == 0)
    def _():
        tri = (qcol <= krow).astype(BF16)
        carry = jnp.zeros((1, LANES), F32)
        for blk in range(n_tiles):
            rows = slice(blk * t, (blk + 1) * t)
            logf = _log_sigmoid(misc_ref[rows, :] + bf_ref[...])
            cb = carry
            for piece in _split_bf16(logf, 3):
                cb = cb + _dot(tri, piece)
            c_scr[rows, :] = cb
            carry = cb[t - 1:t]

    lower = lane < FOX_DH

    def pair_norm(x, g):
        sq = x * x
        ms_lo = jnp.sum(jnp.where(lower, sq, 0.0), axis=-1, keepdims=True)
        ms_hi = jnp.sum(jnp.where(lower, 0.0, sq), axis=-1, keepdims=True)
        ms = jnp.where(lower, ms_lo, ms_hi) * (1.0 / FOX_DH)
        return x * lax.rsqrt(ms + EPS) * g

    qn = pair_norm(q_ref[...].astype(F32), qg_ref[...]) * (FOX_DH ** -0.5)
    kn = pair_norm(k_ref[...].astype(F32), kg_ref[...])
    c_all = c_scr[...]
    for e in range(2):
        in_head = (lane >= FOX_DH) if e else lower
        piece0 = 0 if e else FOX_DH
        c_col = jnp.sum(jnp.where(lane == MISC_F0 + 2 * hp + e, c_all, 0.0), axis=-1, keepdims=True)
        nc_pieces = _split_bf16(-c_col, 3)
        is_piece = (lane >= piece0) & (lane < piece0 + 3)
        qa_scr[e] = jnp.where(in_head, qn, jnp.where(is_piece, 1.0, 0.0)).astype(BF16)
        ka = jnp.where(in_head, kn, 0.0).astype(BF16)
        for j, p in enumerate(nc_pieces):
            ka = jnp.where(lane == piece0 + j, p, ka)
        ka_scr[e] = ka
    vt_scr[...] = v_ref[...].astype(F32).T.astype(BF16)

    orow = lax.broadcasted_iota(jnp.int32, (LANES, t), 0)
    for i in range(n_tiles):
        o_t = []
        for e in range(2):
            qa = qa_scr[e, i * t:(i + 1) * t, :]
            mp = None
            for kb in range(i + 1):
                sc = _dot_nt(ka_scr[e, kb * t:(kb + 1) * t, :], qa)
                if kb == i:
                    sc = jnp.where(krow <= qcol, sc, -jnp.inf)
                s_scr[e, kb] = sc
                bm = jnp.max(sc.reshape(t // 8, 8, t), axis=0)
                mp = bm if mp is None else jnp.maximum(mp, bm)
            m = jnp.max(mp, axis=0, keepdims=True)
            lp = jnp.zeros((8, t), F32)
            acc = jnp.zeros((LANES, t), F32)
            for kb in range(i + 1):
                p = jnp.exp(s_scr[e, kb] - m)
                lp = lp + jnp.sum(p.reshape(t // 8, 8, t), axis=0)
                acc = acc + _dot(vt_scr[:, kb * t:(kb + 1) * t], p.astype(BF16))
            o_t.append(acc / jnp.sum(lp, axis=0, keepdims=True))
        o_pair = jnp.where(orow < FOX_DH, o_t[0], o_t[1])
        y_ref[i * t:(i + 1) * t, :] = o_pair.T.astype(y_ref.dtype)


def _fox(fq, fk, fv, misc, bf_pad, qg2, kg2, b, s):
    n_pairs = FOX_HEADS * FOX_DH // LANES
    pair = pl.BlockSpec((s, LANES), lambda i, j: (i, j))
    whole = lambda a: pl.BlockSpec(a.shape, lambda i, j: (0, 0))
    t = FOX_TILE
    return pl.pallas_call(
        _fox_kernel,
        grid=(b, n_pairs),
        in_specs=[pair, pair, pair, pl.BlockSpec((s, LANES), lambda i, j: (i, 0)),
                  whole(bf_pad), whole(qg2), whole(kg2)],
        out_specs=pair,
        out_shape=jax.ShapeDtypeStruct((b * s, FOX_HEADS * FOX_DH), BF16),
        scratch_shapes=[
            pltpu.VMEM((s, LANES), F32),
            pltpu.VMEM((2, s, LANES), BF16),
            pltpu.VMEM((2, s, LANES), BF16),
            pltpu.VMEM((LANES, s), BF16),
            pltpu.VMEM((2, s // t, t, t), F32),
        ],
        compiler_params=pltpu.CompilerParams(dimension_semantics=("arbitrary", "arbitrary"),
                                             vmem_limit_bytes=VMEM_LIMIT),
        name="fox",
    )(fq, fk, fv, misc, bf_pad, qg2, kg2)


def _post_kernel(x_ref, yg_ref, yf_ref, p_ref, wo_ref, g2_ref, wup_ref, cw_ref, cb_ref, wdn_ref,
                 g3_ref, wpg_ref, bpg_ref, wpe_ref, gpe_ref, o_ref,
                 x1_scr, h2_scr, u_scr, act_scr, carry_scr, acc_scr):
    tm = x_ref.shape[0]
    n_chunks = wdn_ref.shape[0] // FFN_CHUNK
    n_slabs = 2 * FFN_CHUNK // LANES
    dvw = yg_ref.shape[1]
    first_tile = pl.program_id(1) == 0

    x1 = x_ref[...] + _dot(yg_ref[...], wo_ref[:dvw, :]) + _dot(yf_ref[...], wo_ref[dvw:, :])
    x1_scr[...] = x1
    h2_scr[...] = _rms(x1, g2_ref[...]).astype(BF16)
    acc_scr[...] = jnp.zeros_like(acc_scr)

    @pl.when(first_tile)
    def _():
        carry_scr[...] = jnp.zeros_like(carry_scr)

    for j in range(n_chunks + 1):
        if j < n_chunks:
            c0 = j * 2 * FFN_CHUNK
            ub = u_scr.at[j % 2]
            u = _dot(h2_scr[...], wup_ref[:, c0:c0 + 2 * FFN_CHUNK])
            for c in range(n_slabs):
                ub[c, :CARRY_ROWS, :] = carry_scr[j, c]
                ub[c, CARRY_ROWS:, :] = u[:, c * LANES:(c + 1) * LANES]
                carry_scr[j, c] = ub[c, tm:, :]
        if j > 0:
            acc_scr[...] += _dot(act_scr[(j - 1) % 2], wdn_ref[(j - 1) * FFN_CHUNK:j * FFN_CHUNK, :])
        if j < n_chunks:
            def conv(c):
                cs = slice(c0 + c * LANES, c0 + (c + 1) * LANES)
                return (cb_ref[:, cs]
                        + ub[c, CARRY_ROWS - 2:CARRY_ROWS - 2 + tm, :] * cw_ref[0:1, cs]
                        + ub[c, CARRY_ROWS - 1:CARRY_ROWS - 1 + tm, :] * cw_ref[1:2, cs]
                        + ub[c, CARRY_ROWS:, :] * cw_ref[2:3, cs])
            for c in range(n_slabs // 2):
                gate = conv(c)
                act = gate * jax.nn.sigmoid(gate) * conv(c + n_slabs // 2)
                act_scr[j % 2, :, c * LANES:(c + 1) * LANES] = act.astype(BF16)

    x2 = x1_scr[...] + acc_scr[...]
    gate = jax.nn.sigmoid(_dot(_rms(x2, g3_ref[...]).astype(BF16), wpg_ref[...]) + bpg_ref[...])
    emb = _rms(_dot(p_ref[...].astype(BF16), wpe_ref[...]), gpe_ref[...])
    o_ref[...] = x2 + gate * emb


def _post(x2d, yg, yf, p2d, wo, g2, wup, cw, cb, wdn, g3, wpg, bpg, wpe, gpe, b, s, tm):
    d = x2d.shape[1]
    nt = s // tm
    tile = lambda w: pl.BlockSpec((tm, w), lambda i, j: (i * nt + j, 0))
    resident = lambda a: pl.BlockSpec(a.shape, lambda i, j: (0,) * a.ndim, pipeline_mode=pl.Buffered(1))
    n_chunks = wdn.shape[0] // FFN_CHUNK
    n_slabs = 2 * FFN_CHUNK // LANES
    return pl.pallas_call(
        _post_kernel,
        grid=(b, nt),
        in_specs=[tile(d), tile(yg.shape[1]), tile(yf.shape[1]), tile(p2d.shape[1])]
                 + [resident(a) for a in (wo, g2, wup, cw, cb, wdn, g3, wpg, bpg, wpe, gpe)],
        out_specs=tile(d),
        out_shape=jax.ShapeDtypeStruct(x2d.shape, F32),
        scratch_shapes=[
            pltpu.VMEM((tm, d), F32),
            pltpu.VMEM((tm, d), BF16),
            pltpu.VMEM((2, n_slabs, CARRY_ROWS + tm, LANES), F32),
            pltpu.VMEM((2, tm, FFN_CHUNK), BF16),
            pltpu.VMEM((n_chunks, n_slabs, CARRY_ROWS, LANES), F32),
            pltpu.VMEM((tm, d), F32),
        ],
        compiler_params=pltpu.CompilerParams(dimension_semantics=("arbitrary", "arbitrary"),
                                             vmem_limit_bytes=VMEM_LIMIT),
        name="post",
    )(x2d, yg, yf, p2d, wo, g2, wup, cw, cb, wdn, g3, wpg, bpg, wpe, gpe)


def _interleave_chunks(a, d_ff):
    lead = a.shape[:-1]
    n = d_ff // FFN_CHUNK
    g = a[..., :d_ff].reshape(lead + (n, FFN_CHUNK))
    v = a[..., d_ff:].reshape(lead + (n, FFN_CHUNK))
    return jnp.concatenate([g, v], axis=-1).reshape(lead + (2 * d_ff,))


def _layer(x2d, p2d, b, s, norm1_g, w_in, lr_w, lr_b, onorm_g, b_f, qn_g, kn_g, w_o, norm2_g, w_up,
           conv_w, conv_b, w_down, norm3_g, w_pe, pe_norm_g, w_pg, b_pg):
    d = x2d.shape[1]
    d_ff = w_down.shape[0]
    gqk, gvw, fw = GLA_HEADS * GLA_DK, GLA_HEADS * GLA_DV, FOX_HEADS * FOX_DH
    sizes = (gqk, gqk, gvw, gvw, GLA_LOWRANK, fw, fw, fw, FOX_HEADS)
    offs = [0]
    for n in sizes:
        offs.append(offs[-1] + n)
    w_gq, w_gk, w_gv, w_gog, w_lr, w_fq, w_fk, w_fv, w_ff = (w_in[:, offs[i]:offs[i + 1]] for i in range(9))
    w_misc = jnp.zeros((d, LANES), w_in.dtype)
    w_misc = w_misc.at[:, MISC_F0:MISC_F0 + FOX_HEADS].set(w_ff).at[:, MISC_LR0:MISC_LR0 + GLA_LOWRANK].set(w_lr)
    w_all = jnp.concatenate([w_gq, w_gk, w_gv, w_gog, w_fq, w_fk, w_fv, w_misc], axis=1).astype(BF16)

    row = lambda a: a.reshape(1, -1).astype(F32)
    gq, gk, gv, gog, fq, fk, fv, misc = _inproj(x2d, row(norm1_g), w_all, tm=512)

    lrw_pad = jnp.zeros((LANES, gqk), F32).at[MISC_LR0:MISC_LR0 + GLA_LOWRANK].set(lr_w).astype(BF16)
    y_gla = _gla(gq, gk, gv, gog, misc, lrw_pad, row(lr_b), row(onorm_g), b, s)

    bf_pad = jnp.zeros((1, LANES), F32).at[0, MISC_F0:MISC_F0 + FOX_HEADS].set(b_f)
    y_fox = _fox(fq, fk, fv, misc, bf_pad, row(jnp.tile(qn_g, 2)), row(jnp.tile(kn_g, 2)), b, s)

    return _post(
        x2d, y_gla, y_fox, p2d, w_o.astype(BF16), row(norm2_g),
        _interleave_chunks(w_up, d_ff).astype(BF16), _interleave_chunks(conv_w, d_ff).astype(F32),
        _interleave_chunks(conv_b.reshape(1, -1), d_ff).astype(F32), w_down.astype(BF16),
        row(norm3_g), w_pg.astype(BF16), row(b_pg), w_pe.astype(BF16), row(pe_norm_g), b, s, tm=512)


def kernel(x, p, norm1_g, w_in, gla_lr_w, gla_lr_b, gla_onorm_g, fox_b_f, fox_qnorm_g, fox_knorm_g, w_o, norm2_g, w_up, conv_w, conv_b, w_down, norm3_g, w_pe, pe_norm_g, w_pg, b_pg):
    b, s, d = x.shape
    x2d = x.reshape(b * s, d)
    for i in range(p.shape[0]):
        x2d = _layer(x2d, p[i].reshape(b * s, -1), b, s, norm1_g[i], w_in[i], gla_lr_w[i], gla_lr_b[i],
                     gla_onorm_g[i], fox_b_f[i], fox_qnorm_g[i], fox_knorm_g[i], w_o[i], norm2_g[i], w_up[i],
                     conv_w[i], conv_b[i], w_down[i], norm3_g[i], w_pe[i], pe_norm_g[i], w_pg[i], b_pg[i])
    return x2d.reshape(b, s, d)
```

```python
import functools

import jax
import jax.numpy as jnp
from jax import lax
from jax.experimental import pallas as pl
from jax.experimental.pallas import tpu as pltpu

F32 = jnp.float32
BF16 = jnp.bfloat16

EPS = 1e-6
LANES = 128
GLA_HEADS, GLA_DK, GLA_DV = 4, 64, 128
GLA_LOWRANK = 16
GLA_INV_TAU = 1.0 / 16.0
GLA_CHUNK = 64
GLA_SUPER = 256
FOX_HEADS, FOX_DH = 8, 64
FOX_TILE = 256
CONV_W = 3
FFN_CHUNK = 256
CARRY_ROWS = 8
MISC_F0 = 0
MISC_LR0 = 8
VMEM_LIMIT = 56 * 1024 * 1024


def _dot(a, b):
    return jnp.dot(a, b, preferred_element_type=F32)


def _dot_nt(a, b):
    return lax.dot_general(a, b, (((1,), (1,)), ((), ())), preferred_element_type=F32)


def _dot_tn(a, b):
    return lax.dot_general(a, b, (((0,), (0,)), ((), ())), preferred_element_type=F32)


def _log_sigmoid(z):
    return jnp.minimum(z, 0.0) - jnp.log1p(jnp.exp(-jnp.abs(z)))


def _split_bf16(x, pieces):
    out = []
    for _ in range(pieces):
        p = x.astype(BF16)
        out.append(p)
        x = x - p.astype(F32)
    return out


def _rms(x, gain):
    ms = jnp.mean(x * x, axis=-1, keepdims=True)
    return x * lax.rsqrt(ms + EPS) * gain


def _inproj_kernel(x_ref, g_ref, wg_ref, wf_ref, wm_ref,
                   gq_ref, gk_ref, gv_ref, gog_ref, fq_ref, fk_ref, fv_ref, misc_ref):
    h = _rms(x_ref[...], g_ref[...]).astype(BF16)
    for w_ref, outs in ((wg_ref, (gq_ref, gk_ref, gv_ref, gog_ref)), (wf_ref, (fq_ref, fk_ref, fv_ref)),
                        (wm_ref, (misc_ref,))):
        off = 0
        for ref in outs:
            n = ref.shape[-1]
            ref[...] = _dot(h, w_ref[:, off:off + n]).astype(ref.dtype)
            off += n


def _inproj(x2d, g, w_gla, w_fox, w_misc, tm):
    t, d = x2d.shape
    widths = (256, 256, 512, 512, 512, 512, 512, LANES)
    dtypes = (BF16,) * 7 + (F32,)
    return pl.pallas_call(
        _inproj_kernel,
        grid=(t // tm,),
        in_specs=[
            pl.BlockSpec((tm, d), lambda i: (i, 0)),
            pl.BlockSpec((1, d), lambda i: (0, 0)),
        ] + [pl.BlockSpec(w.shape, lambda i: (0, 0)) for w in (w_gla, w_fox, w_misc)],
        out_specs=[pl.BlockSpec((tm, n), lambda i: (i, 0)) for n in widths],
        out_shape=[jax.ShapeDtypeStruct((t, n), dt) for n, dt in zip(widths, dtypes)],
        compiler_params=pltpu.CompilerParams(dimension_semantics=("arbitrary",), vmem_limit_bytes=VMEM_LIMIT),
        name="inproj",
    )(x2d, g, w_gla, w_fox, w_misc)


def _gla_kernel(q_ref, k_ref, v_ref, og_ref, misc_ref, lrw_ref, lrb_ref, ong_ref, y_ref, st_ref):
    s_len = q_ref.shape[0]
    r = GLA_SUPER
    dkw = GLA_HEADS * GLA_DK
    dvw = GLA_HEADS * GLA_DV
    row = lax.broadcasted_iota(jnp.int32, (r, r), 0)
    col = lax.broadcasted_iota(jnp.int32, (r, r), 1)
    same_chunk = (row // GLA_CHUNK) == (col // GLA_CHUNK)
    causal_bd = same_chunk & (col <= row)
    tri_bd = causal_bd.astype(BF16)
    ones_bd = same_chunk.astype(BF16)
    st_row = lax.broadcasted_iota(jnp.int32, (dvw, dkw), 0)
    st_col = lax.broadcasted_iota(jnp.int32, (dvw, dkw), 1)
    st_mask = (st_row // GLA_DV) == (st_col // GLA_DK)
    head_of_col = lax.broadcasted_iota(jnp.int32, (r, dkw), 1) // GLA_DK

    st_ref[...] = jnp.zeros_like(st_ref)

    @pl.loop(0, s_len // r, unroll=2)
    def _(sb):
        rows = pl.ds(pl.multiple_of(sb * r, r), r)
        z = _dot(misc_ref[rows, :].astype(BF16), lrw_ref[...]) + lrb_ref[...]
        la = _log_sigmoid(z) * GLA_INV_TAU
        la_hi, la_lo = _split_bf16(la, 2)
        b = _dot(tri_bd, la_hi) + _dot(tri_bd, la_lo)
        b_tot = _dot(ones_bd, la_hi) + _dot(ones_bd, la_lo)
        q = q_ref[rows, :].astype(F32)
        k = k_ref[rows, :].astype(F32)
        qe = (q * (GLA_DK ** -0.5)) * jnp.exp(b)
        ke = (k * jnp.exp(-b)).astype(BF16)
        kd = (k * jnp.exp(b_tot - b)).astype(BF16)
        decay = jnp.exp(b_tot)
        v = v_ref[rows, :]
        qe_bf = qe.astype(BF16)

        o_parts = []
        for h in range(GLA_HEADS):
            qh = jnp.where(head_of_col == h, qe, 0.0).astype(BF16)
            a = jnp.where(causal_bd, _dot_nt(qh, ke), 0.0).astype(BF16)
            o_parts.append(_dot(a, v[:, h * GLA_DV:(h + 1) * GLA_DV]))
        o = jnp.concatenate(o_parts, axis=-1)

        o_inter = []
        for c in range(r // GLA_CHUNK):
            cr = slice(c * GLA_CHUNK, (c + 1) * GLA_CHUNK)
            st = st_ref[...]
            o_inter.append(_dot_nt(qe_bf[cr], st.astype(BF16)))
            ds_t = _dot_tn(v[cr], kd[cr])
            st_ref[...] = st * decay[c * GLA_CHUNK:c * GLA_CHUNK + 1] + jnp.where(st_mask, ds_t, 0.0)
        o = o + jnp.concatenate(o_inter, axis=0)

        og = og_ref[rows, :].astype(F32)
        gate = og * jax.nn.sigmoid(og)
        for h in range(GLA_HEADS):
            hs = slice(h * GLA_DV, (h + 1) * GLA_DV)
            y_ref[rows, hs] = (_rms(o[:, hs], ong_ref[...]) * gate[:, hs]).astype(y_ref.dtype)


def _gla(gq, gk, gv, gog, misc, lrw_pad, lrb, ong, b, s):
    dkw, dvw = GLA_HEADS * GLA_DK, GLA_HEADS * GLA_DV
    seq = lambda w: pl.BlockSpec((s, w), lambda i: (i, 0))
    whole = lambda a: pl.BlockSpec(a.shape, lambda i: (0, 0))
    return pl.pallas_call(
        _gla_kernel,
        grid=(b,),
        in_specs=[seq(dkw), seq(dkw), seq(dvw), seq(dvw), seq(LANES), whole(lrw_pad), whole(lrb), whole(ong)],
        out_specs=seq(dvw),
        out_shape=jax.ShapeDtypeStruct((b * s, dvw), BF16),
        scratch_shapes=[pltpu.VMEM((dvw, dkw), F32)],
        compiler_params=pltpu.CompilerParams(dimension_semantics=("arbitrary",), vmem_limit_bytes=VMEM_LIMIT),
        name="gla",
    )(gq, gk, gv, gog, misc, lrw_pad, lrb, ong)


def _fox_kernel(q_ref, k_ref, v_ref, misc_ref, bf_ref, qg_ref, kg_ref, y_ref,
                c_scr, qa_scr, ka_scr, vt_scr, s_scr):
    s_len = q_ref.shape[0]
    t = FOX_TILE
    n_tiles = s_len // t
    hp = pl.program_id(1)
    lane = lax.broadcasted_iota(jnp.int32, (s_len, LANES), 1)
    krow = lax.broadcasted_iota(jnp.int32, (t, t), 0)
    qcol = lax.broadcasted_iota(jnp.int32, (t, t), 1)

    @pl.when(hp == 0)
    def _():
        tri = (qcol <= krow).astype(BF16)
        carry = jnp.zeros((1, LANES), F32)
        for blk in range(n_tiles):
            rows = slice(blk * t, (blk + 1) * t)
            logf = _log_sigmoid(misc_ref[rows, :] + bf_ref[...])
            cb = carry
            for piece in _split_bf16(logf, 3):
                cb = cb + _dot(tri, piece)
            c_scr[rows, :] = cb
            carry = cb[t - 1:t]

    lower = lane < FOX_DH

    def pair_norm(x, g):
        sq = x * x
        ms_lo = jnp.sum(jnp.where(lower, sq, 0.0), axis=-1, keepdims=True)
        ms_hi = jnp.sum(jnp.where(lower, 0.0, sq), axis=-1, keepdims=True)
        ms = jnp.where(lower, ms_lo, ms_hi) * (1.0 / FOX_DH)
        return x * lax.rsqrt(ms + EPS) * g

    qn = pair_norm(q_ref[...].astype(F32), qg_ref[...]) * (FOX_DH ** -0.5)
    kn = pair_norm(k_ref[...].astype(F32), kg_ref[...])
    c_all = c_scr[...]
    for e in range(2):
        in_head = (lane >= FOX_DH) if e else lower
        piece0 = 0 if e else FOX_DH
        c_col = jnp.sum(jnp.where(lane == MISC_F0 + 2 * hp + e, c_all, 0.0), axis=-1, keepdims=True)
        nc_pieces = _split_bf16(-c_col, 3)
        is_piece = (lane >= piece0) & (lane < piece0 + 3)
        qa_scr[e] = jnp.where(in_head, qn, jnp.where(is_piece, 1.0, 0.0)).astype(BF16)
        ka = jnp.where(in_head, kn, 0.0).astype(BF16)
        for j, p in enumerate(nc_pieces):
            ka = jnp.where(lane == piece0 + j, p, ka)
        ka_scr[e] = ka
    vt_scr[...] = v_ref[...].astype(F32).T.astype(BF16)

    orow = lax.broadcasted_iota(jnp.int32, (LANES, t), 0)
    for i in range(n_tiles):
        o_t = []
        for e in range(2):
            qa = qa_scr[e, i * t:(i + 1) * t, :]
            mp = None
            for kb in range(i + 1):
                sc = _dot_nt(ka_scr[e, kb * t:(kb + 1) * t, :], qa)
                if kb == i:
                    sc = jnp.where(krow <= qcol, sc, -jnp.inf)
                s_scr[e, kb] = sc
                bm = jnp.max(sc.reshape(t // 8, 8, t), axis=0)
                mp = bm if mp is None else jnp.maximum(mp, bm)
            m = jnp.max(mp, axis=0, keepdims=True)
            lp = jnp.zeros((8, t), F32)
            acc = jnp.zeros((LANES, t), F32)
            for kb in range(i + 1):
                p = jnp.exp(s_scr[e, kb] - m)
                lp = lp + jnp.sum(p.reshape(t // 8, 8, t), axis=0)
                acc = acc + _dot(vt_scr[:, kb * t:(kb + 1) * t], p.astype(BF16))
            o_t.append(acc / jnp.sum(lp, axis=0, keepdims=True))
        o_pair = jnp.where(orow < FOX_DH, o_t[0], o_t[1])
        y_ref[i * t:(i + 1) * t, :] = o_pair.T.astype(y_ref.dtype)


def _fox(fq, fk, fv, misc, bf_pad, qg2, kg2, b, s):
    n_pairs = FOX_HEADS * FOX_DH // LANES
    pair = pl.BlockSpec((s, LANES), lambda i, j: (i, j))
    whole = lambda a: pl.BlockSpec(a.shape, lambda i, j: (0, 0))
    t = FOX_TILE
    return pl.pallas_call(
        _fox_kernel,
        grid=(b, n_pairs),
        in_specs=[pair, pair, pair, pl.BlockSpec((s, LANES), lambda i, j: (i, 0)),
                  whole(bf_pad), whole(qg2), whole(kg2)],
        out_specs=pair,
        out_shape=jax.ShapeDtypeStruct((b * s, FOX_HEADS * FOX_DH), BF16),
        scratch_shapes=[
            pltpu.VMEM((s, LANES), F32),
            pltpu.VMEM((2, s, LANES), BF16),
            pltpu.VMEM((2, s, LANES), BF16),
            pltpu.VMEM((LANES, s), BF16),
            pltpu.VMEM((2, s // t, t, t), F32),
        ],
        compiler_params=pltpu.CompilerParams(dimension_semantics=("arbitrary", "arbitrary"),
                                             vmem_limit_bytes=VMEM_LIMIT),
        name="fox",
    )(fq, fk, fv, misc, bf_pad, qg2, kg2)


def _post_kernel(x_ref, yg_ref, yf_ref, p_ref, wo_ref, g2_ref, wup_ref, cw_ref, cb_ref, wdn_ref,
                 g3_ref, wpg_ref, bpg_ref, wpe_ref, gpe_ref, o_ref,
                 x1_scr, h2_scr, u_scr, act_scr, carry_scr, acc_scr):
    tm = x_ref.shape[0]
    d_ff = wdn_ref.shape[0]
    n_chunks = d_ff // FFN_CHUNK
    n_slabs = 2 * FFN_CHUNK // LANES
    half = n_slabs // 2
    dvw = yg_ref.shape[1]
    first_tile = pl.program_id(1) == 0

    x1 = x_ref[...] + _dot(yg_ref[...], wo_ref[:dvw, :]) + _dot(yf_ref[...], wo_ref[dvw:, :])
    x1_scr[...] = x1
    h2_scr[...] = _rms(x1, g2_ref[...]).astype(BF16)
    acc_scr[...] = jnp.zeros_like(acc_scr)

    @pl.when(first_tile)
    def _():
        carry_scr[...] = jnp.zeros_like(carry_scr)

    for j in range(n_chunks + 1):
        if j < n_chunks:
            ub = u_scr.at[j % 2]
            col0 = [(c // half) * d_ff + j * FFN_CHUNK + (c % half) * LANES for c in range(n_slabs)]
            h2 = h2_scr[...]
            for part in range(2):
                u = _dot(h2, wup_ref[:, col0[part * half]:col0[part * half] + FFN_CHUNK])
                for c in range(part * half, (part + 1) * half):
                    ub[c, :CARRY_ROWS, :] = carry_scr[j, c]
                    ub[c, CARRY_ROWS:, :] = u[:, (c % half) * LANES:(c % half + 1) * LANES]
                    carry_scr[j, c] = ub[c, tm:, :]
        if j > 0:
            acc_scr[...] += _dot(act_scr[(j - 1) % 2], wdn_ref[(j - 1) * FFN_CHUNK:j * FFN_CHUNK, :])
        if j < n_chunks:
            def conv(c):
                cs = slice(col0[c], col0[c] + LANES)
                return (cb_ref[:, cs]
                        + ub[c, CARRY_ROWS - 2:CARRY_ROWS - 2 + tm, :] * cw_ref[0:1, cs]
                        + ub[c, CARRY_ROWS - 1:CARRY_ROWS - 1 + tm, :] * cw_ref[1:2, cs]
                        + ub[c, CARRY_ROWS:, :] * cw_ref[2:3, cs])
            for c in range(half):
                gate = conv(c)
                act = gate * jax.nn.sigmoid(gate) * conv(c + half)
                act_scr[j % 2, :, c * LANES:(c + 1) * LANES] = act.astype(BF16)

    x2 = x1_scr[...] + acc_scr[...]
    gate = jax.nn.sigmoid(_dot(_rms(x2, g3_ref[...]).astype(BF16), wpg_ref[...]) + bpg_ref[...])
    emb = _rms(_dot(p_ref[...].astype(BF16), wpe_ref[...]), gpe_ref[...])
    o_ref[...] = x2 + gate * emb


def _post(x2d, yg, yf, p2d, wo, g2, wup, cw, cb, wdn, g3, wpg, bpg, wpe, gpe, b, s, tm):
    d = x2d.shape[1]
    nt = s // tm
    tile = lambda w: pl.BlockSpec((tm, w), lambda i, j: (i * nt + j, 0))
    resident = lambda a: pl.BlockSpec(a.shape, lambda i, j: (0,) * a.ndim, pipeline_mode=pl.Buffered(1))
    n_chunks = wdn.shape[0] // FFN_CHUNK
    n_slabs = 2 * FFN_CHUNK // LANES
    return pl.pallas_call(
        _post_kernel,
        grid=(b, nt),
        in_specs=[tile(d), tile(yg.shape[1]), tile(yf.shape[1]), tile(p2d.shape[1])]
                 + [resident(a) for a in (wo, g2, wup, cw, cb, wdn, g3, wpg, bpg, wpe, gpe)],
        out_specs=tile(d),
        out_shape=jax.ShapeDtypeStruct(x2d.shape, F32),
        scratch_shapes=[
            pltpu.VMEM((tm, d), F32),
            pltpu.VMEM((tm, d), BF16),
            pltpu.VMEM((2, n_slabs, CARRY_ROWS + tm, LANES), F32),
            pltpu.VMEM((2, tm, FFN_CHUNK), BF16),
            pltpu.VMEM((n_chunks, n_slabs, CARRY_ROWS, LANES), F32),
            pltpu.VMEM((tm, d), F32),
        ],
        compiler_params=pltpu.CompilerParams(dimension_semantics=("arbitrary", "arbitrary"),
                                             vmem_limit_bytes=VMEM_LIMIT),
        name="post",
    )(x2d, yg, yf, p2d, wo, g2, wup, cw, cb, wdn, g3, wpg, bpg, wpe, gpe)


def _layer(x2d, p2d, b, s, norm1_g, w_in, lr_w, lr_b, onorm_g, b_f, qn_g, kn_g, w_o, norm2_g, w_up,
           conv_w, conv_b, w_down, norm3_g, w_pe, pe_norm_g, w_pg, b_pg):
    d = x2d.shape[1]
    d_ff = w_down.shape[0]
    gqk, gvw, fw = GLA_HEADS * GLA_DK, GLA_HEADS * GLA_DV, FOX_HEADS * FOX_DH
    sizes = (gqk, gqk, gvw, gvw, GLA_LOWRANK, fw, fw, fw, FOX_HEADS)
    offs = [0]
    for n in sizes:
        offs.append(offs[-1] + n)
    w_gla = w_in[:, offs[0]:offs[4]].astype(BF16)
    w_fox = w_in[:, offs[5]:offs[8]].astype(BF16)
    w_misc = jnp.zeros((d, LANES), BF16)
    w_misc = w_misc.at[:, MISC_F0:MISC_F0 + FOX_HEADS].set(w_in[:, offs[8]:offs[9]].astype(BF16))
    w_misc = w_misc.at[:, MISC_LR0:MISC_LR0 + GLA_LOWRANK].set(w_in[:, offs[4]:offs[5]].astype(BF16))

    row = lambda a: a.reshape(1, -1).astype(F32)
    gq, gk, gv, gog, fq, fk, fv, misc = _inproj(x2d, row(norm1_g), w_gla, w_fox, w_misc, tm=512)

    lrw_pad = jnp.zeros((LANES, gqk), F32).at[MISC_LR0:MISC_LR0 + GLA_LOWRANK].set(lr_w).astype(BF16)
    y_gla = _gla(gq, gk, gv, gog, misc, lrw_pad, row(lr_b), row(onorm_g), b, s)

    bf_pad = jnp.zeros((1, LANES), F32).at[0, MISC_F0:MISC_F0 + FOX_HEADS].set(b_f)
    y_fox = _fox(fq, fk, fv, misc, bf_pad, row(jnp.tile(qn_g, 2)), row(jnp.tile(kn_g, 2)), b, s)

    return _post(
        x2d, y_gla, y_fox, p2d, w_o.astype(BF16), row(norm2_g),
        w_up.astype(BF16), conv_w.astype(F32), row(conv_b), w_down.astype(BF16),
        row(norm3_g), w_pg.astype(BF16), row(b_pg), w_pe.astype(BF16), row(pe_norm_g), b, s, tm=512)


def kernel(x, p, norm1_g, w_in, gla_lr_w, gla_lr_b, gla_onorm_g, fox_b_f, fox_qnorm_g, fox_knorm_g, w_o, norm2_g, w_up, conv_w, conv_b, w_down, norm3_g, w_pe, pe_norm_g, w_pg, b_pg):
    b, s, d = x.shape
    x2d = x.reshape(b * s, d)
    for i in range(p.shape[0]):
        x2d = _layer(x2d, p[i].reshape(b * s, -1), b, s, norm1_g[i], w_in[i], gla_lr_w[i], gla_lr_b[i],
                     gla_onorm_g[i], fox_b_f[i], fox_qnorm_g[i], fox_knorm_g[i], w_o[i], norm2_g[i], w_up[i],
                     conv_w[i], conv_b[i], w_down[i], norm3_g[i], w_pe[i], pe_norm_g[i], w_pg[i], b_pg[i])
    return x2d.reshape(b, s, d)
```

```python
import functools

import jax
import jax.numpy as jnp
from jax import lax
from jax.experimental import pallas as pl
from jax.experimental.pallas import tpu as pltpu

F32 = jnp.float32
BF16 = jnp.bfloat16

EPS = 1e-6
LOG2E = 1.4426950408889634
LANES = 128
GLA_HEADS, GLA_DK, GLA_DV = 4, 64, 128
GLA_LOWRANK = 16
GLA_INV_TAU = 1.0 / 16.0
GLA_CHUNK = 64
GLA_SUPER = 256
FOX_HEADS, FOX_DH = 8, 64
FOX_TILE = 256
CONV_W = 3
FFN_CHUNK = 256
CARRY_ROWS = 8
MISC_F0 = 0
MISC_LR0 = 8
VMEM_LIMIT = 56 * 1024 * 1024


def _dot(a, b):
    return jnp.dot(a, b, preferred_element_type=F32)


def _dot_nt(a, b):
    return lax.dot_general(a, b, (((1,), (1,)), ((), ())), preferred_element_type=F32)


def _dot_tn(a, b):
    return lax.dot_general(a, b, (((0,), (0,)), ((), ())), preferred_element_type=F32)


def _log_sigmoid(z):
    return jnp.minimum(z, 0.0) - jnp.log1p(jnp.exp(-jnp.abs(z)))


def _split_bf16(x, pieces):
    out = []
    for _ in range(pieces):
        p = x.astype(BF16)
        out.append(p)
        x = x - p.astype(F32)
    return out


def _rms(x, gain):
    ms = jnp.mean(x * x, axis=-1, keepdims=True)
    return x * lax.rsqrt(ms + EPS) * gain


def _inproj_kernel(x_ref, g_ref, wg_ref, wf_ref, wm_ref, bf_ref, qg_ref, kg_ref,
                   gq_ref, gk_ref, gv_ref, gog_ref, fqa_ref, fka_ref, fv_ref, misc_ref, carry_scr, h_scr, *,
                   tiles_per_seq):
    tm = x_ref.shape[0]
    fw = FOX_HEADS * FOX_DH

    @pl.when(pl.program_id(0) % tiles_per_seq == 0)
    def _():
        carry_scr[...] = jnp.zeros_like(carry_scr)

    h_scr[...] = _rms(x_ref[...], g_ref[...]).astype(BF16)
    h = h_scr[...]
    misc = _dot(h, wm_ref[...])
    misc_ref[...] = misc
    q_all = _dot(h, wf_ref[:, :fw])
    k_all = _dot(h, wf_ref[:, fw:2 * fw])

    blk = FOX_TILE
    trow = lax.broadcasted_iota(jnp.int32, (blk, blk), 0)
    tcol = lax.broadcasted_iota(jnp.int32, (blk, blk), 1)
    tri = (tcol <= trow).astype(BF16)
    carry = carry_scr[...]
    c_blocks = []
    for r0 in range(0, tm, blk):
        cb = carry
        for piece in _split_bf16(_log_sigmoid(misc[r0:r0 + blk] + bf_ref[...]), 3):
            cb = cb + _dot(tri, piece)
        c_blocks.append(cb)
        carry = cb[blk - 1:blk]
    carry_scr[...] = carry
    c_all = jnp.concatenate(c_blocks, axis=0)

    lane = lax.broadcasted_iota(jnp.int32, (tm, LANES), 1)
    lower = lane < FOX_DH

    def pair_norm(x, g):
        sq = x * x
        ms_lo = jnp.sum(jnp.where(lower, sq, 0.0), axis=-1, keepdims=True)
        ms_hi = jnp.sum(jnp.where(lower, 0.0, sq), axis=-1, keepdims=True)
        ms = jnp.where(lower, ms_lo, ms_hi) * (1.0 / FOX_DH)
        return x * lax.rsqrt(ms + EPS) * g

    fv_ref[...] = _dot(h, wf_ref[:, 2 * fw:]).astype(fv_ref.dtype)
    off = 0
    for ref in (gq_ref, gk_ref, gv_ref, gog_ref):
        n = ref.shape[-1]
        ref[...] = _dot(h, wg_ref[:, off:off + n]).astype(ref.dtype)
        off += n

    for pair in range(fw // LANES):
        ps = slice(pair * LANES, (pair + 1) * LANES)
        qn = pair_norm(q_all[:, ps], qg_ref[...])
        kn = pair_norm(k_all[:, ps], kg_ref[...])
        for e in range(2):
            head = 2 * pair + e
            in_head = (lane >= FOX_DH) if e else lower
            piece0 = 0 if e else FOX_DH
            c_col = jnp.sum(jnp.where(lane == MISC_F0 + head, c_all, 0.0), axis=-1, keepdims=True)
            is_piece = (lane >= piece0) & (lane < piece0 + 3)
            hs = slice(head * LANES, (head + 1) * LANES)
            fqa_ref[:, hs] = jnp.where(in_head, qn, jnp.where(is_piece, 1.0, 0.0)).astype(BF16)
            ka = jnp.where(in_head, kn, 0.0).astype(BF16)
            for j, p in enumerate(_split_bf16(c_col * (-LOG2E), 3)):
                ka = jnp.where(lane == piece0 + j, p, ka)
            fka_ref[:, hs] = ka


def _inproj(x2d, g, w_gla, w_fox, w_misc, bf_pad, qg2, kg2, s, tm):
    t, d = x2d.shape
    widths = (256, 256, 512, 512, FOX_HEADS * LANES, FOX_HEADS * LANES, 512, LANES)
    dtypes = (BF16,) * 7 + (F32,)
    whole = lambda a: pl.BlockSpec(a.shape, lambda i: (0, 0))
    return pl.pallas_call(
        functools.partial(_inproj_kernel, tiles_per_seq=s // tm),
        grid=(t // tm,),
        in_specs=[pl.BlockSpec((tm, d), lambda i: (i, 0))]
                 + [whole(a) for a in (g, w_gla, w_fox, w_misc, bf_pad, qg2, kg2)],
        out_specs=[pl.BlockSpec((tm, n), lambda i: (i, 0)) for n in widths],
        out_shape=[jax.ShapeDtypeStruct((t, n), dt) for n, dt in zip(widths, dtypes)],
        scratch_shapes=[pltpu.VMEM((1, LANES), F32),
                        pltpu.VMEM((tm, d), BF16)],
        compiler_params=pltpu.CompilerParams(dimension_semantics=("arbitrary",), vmem_limit_bytes=VMEM_LIMIT),
        name="inproj",
    )(x2d, g, w_gla, w_fox, w_misc, bf_pad, qg2, kg2)


def _gla_kernel(q_ref, k_ref, v_ref, og_ref, misc_ref, lrw_ref, lrb_ref, ong_ref, y_ref, st_ref):
    s_len = q_ref.shape[0]
    r = GLA_SUPER
    dkw = GLA_HEADS * GLA_DK
    dvw = GLA_HEADS * GLA_DV
    row = lax.broadcasted_iota(jnp.int32, (r, r), 0)
    col = lax.broadcasted_iota(jnp.int32, (r, r), 1)
    same_chunk = (row // GLA_CHUNK) == (col // GLA_CHUNK)
    causal_bd = same_chunk & (col <= row)
    tri_bd = causal_bd.astype(BF16)
    ones_bd = same_chunk.astype(BF16)
    st_row = lax.broadcasted_iota(jnp.int32, (dvw, dkw), 0)
    st_col = lax.broadcasted_iota(jnp.int32, (dvw, dkw), 1)
    st_mask = (st_row // GLA_DV) == (st_col // GLA_DK)
    head_of_col = lax.broadcasted_iota(jnp.int32, (r, dkw), 1) // GLA_DK

    st_ref[...] = jnp.zeros_like(st_ref)

    @pl.loop(0, s_len // r, unroll=2)
    def _(sb):
        rows = pl.ds(pl.multiple_of(sb * r, r), r)
        z = _dot(misc_ref[rows, :].astype(BF16), lrw_ref[...]) + lrb_ref[...]
        la = _log_sigmoid(z) * GLA_INV_TAU
        la_hi, la_lo = _split_bf16(la, 2)
        b = _dot(tri_bd, la_hi) + _dot(tri_bd, la_lo)
        b_tot = _dot(ones_bd, la_hi) + _dot(ones_bd, la_lo)
        q = q_ref[rows, :].astype(F32)
        k = k_ref[rows, :].astype(F32)
        qe = (q * (GLA_DK ** -0.5)) * jnp.exp(b)
        ke = (k * jnp.exp(-b)).astype(BF16)
        kd = (k * jnp.exp(b_tot - b)).astype(BF16)
        decay = jnp.exp(b_tot)
        v = v_ref[rows, :]
        qe_bf = qe.astype(BF16)

        o_parts = []
        for h in range(GLA_HEADS):
            qh = jnp.where(head_of_col == h, qe, 0.0).astype(BF16)
            a = jnp.where(causal_bd, _dot_nt(qh, ke), 0.0).astype(BF16)
            o_parts.append(_dot(a, v[:, h * GLA_DV:(h + 1) * GLA_DV]))
        o = jnp.concatenate(o_parts, axis=-1)

        o_inter = []
        for c in range(r // GLA_CHUNK):
            cr = slice(c * GLA_CHUNK, (c + 1) * GLA_CHUNK)
            st = st_ref[...]
            o_inter.append(_dot_nt(qe_bf[cr], st.astype(BF16)))
            ds_t = _dot_tn(v[cr], kd[cr])
            st_ref[...] = st * decay[c * GLA_CHUNK:c * GLA_CHUNK + 1] + jnp.where(st_mask, ds_t, 0.0)
        o = o + jnp.concatenate(o_inter, axis=0)

        og = og_ref[rows, :].astype(F32)
        gate = og * jax.nn.sigmoid(og)
        for h in range(GLA_HEADS):
            hs = slice(h * GLA_DV, (h + 1) * GLA_DV)
            y_ref[rows, hs] = (_rms(o[:, hs], ong_ref[...]) * gate[:, hs]).astype(y_ref.dtype)


def _gla(gq, gk, gv, gog, misc, lrw_pad, lrb, ong, b, s):
    dkw, dvw = GLA_HEADS * GLA_DK, GLA_HEADS * GLA_DV
    seq = lambda w: pl.BlockSpec((s, w), lambda i: (i, 0))
    whole = lambda a: pl.BlockSpec(a.shape, lambda i: (0, 0))
    return pl.pallas_call(
        _gla_kernel,
        grid=(b,),
        in_specs=[seq(dkw), seq(dkw), seq(dvw), seq(dvw), seq(LANES), whole(lrw_pad), whole(lrb), whole(ong)],
        out_specs=seq(dvw),
        out_shape=jax.ShapeDtypeStruct((b * s, dvw), BF16),
        scratch_shapes=[pltpu.VMEM((dvw, dkw), F32)],
        compiler_params=pltpu.CompilerParams(dimension_semantics=("arbitrary",), vmem_limit_bytes=VMEM_LIMIT),
        name="gla",
    )(gq, gk, gv, gog, misc, lrw_pad, lrb, ong)


def _fox_kernel(qa_ref, ka_ref, v_ref, y_ref, vt_scr, s_scr):
    s_len = v_ref.shape[0]
    t = FOX_TILE
    n_tiles = s_len // t
    krow = lax.broadcasted_iota(jnp.int32, (t, t), 0)
    qcol = lax.broadcasted_iota(jnp.int32, (t, t), 1)
    orow = lax.broadcasted_iota(jnp.int32, (LANES, t), 0)
    vt_scr[...] = v_ref[...].astype(F32).T.astype(BF16)

    def scores(i, e):
        hs = slice(e * LANES, (e + 1) * LANES)
        qa = qa_ref[i * t:(i + 1) * t, hs]
        mp = None
        for kb in range(i + 1):
            sc = _dot_nt(ka_ref[kb * t:(kb + 1) * t, hs], qa)
            if kb == i:
                sc = jnp.where(krow <= qcol, sc, -jnp.inf)
            s_scr[e, kb] = sc
            bm = jnp.max(sc.reshape(t // 8, 8, t), axis=0)
            mp = bm if mp is None else jnp.maximum(mp, bm)
        return jnp.max(mp, axis=0, keepdims=True)

    def weighted_values(i, e, m):
        lp = jnp.zeros((8, t), F32)
        acc = jnp.zeros((LANES, t), F32)
        for kb in range(i + 1):
            p = jnp.exp2(s_scr[e, kb] - m)
            lp = lp + jnp.sum(p.reshape(t // 8, 8, t), axis=0)
            acc = acc + _dot(vt_scr[:, kb * t:(kb + 1) * t], p.astype(BF16))
        return acc / jnp.sum(lp, axis=0, keepdims=True)

    units = [(i, e) for i in range(n_tiles) for e in range(2)]
    m_next = scores(*units[0])
    o_even = None
    for n, (i, e) in enumerate(units):
        m_cur = m_next
        if n + 1 < len(units):
            m_next = scores(*units[n + 1])
        o_t = weighted_values(i, e, m_cur)
        if e == 0:
            o_even = o_t
        else:
            o_pair = jnp.where(orow < FOX_DH, o_even, o_t)
            y_ref[i * t:(i + 1) * t, :] = o_pair.T.astype(y_ref.dtype)


def _fox(fqa, fka, fv, b, s):
    n_pairs = FOX_HEADS * FOX_DH // LANES
    t = FOX_TILE
    heads2 = pl.BlockSpec((s, 2 * LANES), lambda i, j: (i, j))
    pair = pl.BlockSpec((s, LANES), lambda i, j: (i, j))
    return pl.pallas_call(
        _fox_kernel,
        grid=(b, n_pairs),
        in_specs=[heads2, heads2, pair],
        out_specs=pair,
        out_shape=jax.ShapeDtypeStruct((b * s, FOX_HEADS * FOX_DH), BF16),
        scratch_shapes=[
            pltpu.VMEM((LANES, s), BF16),
            pltpu.VMEM((2, s // t, t, t), F32),
        ],
        compiler_params=pltpu.CompilerParams(dimension_semantics=("arbitrary", "arbitrary"),
                                             vmem_limit_bytes=VMEM_LIMIT),
        name="fox",
    )(fqa, fka, fv)


def _post_kernel(x_ref, yg_ref, yf_ref, p_ref, wo_ref, g2_ref, wup_ref, cw_ref, cb_ref, wdn_ref,
                 g3_ref, wpg_ref, bpg_ref, wpe_ref, gpe_ref, o_ref,
                 x1_scr, h2_scr, u_scr, act_scr, carry_scr, acc_scr):
    tm = x_ref.shape[0]
    d_ff = wdn_ref.shape[0]
    n_chunks = d_ff // FFN_CHUNK
    n_slabs = 2 * FFN_CHUNK // LANES
    half = n_slabs // 2
    dvw = yg_ref.shape[1]
    first_tile = pl.program_id(1) == 0

    x1 = x_ref[...] + _dot(yg_ref[...], wo_ref[:dvw, :]) + _dot(yf_ref[...], wo_ref[dvw:, :])
    x1_scr[...] = x1
    h2_scr[...] = _rms(x1, g2_ref[...]).astype(BF16)
    acc_scr[...] = jnp.zeros_like(acc_scr)

    @pl.when(first_tile)
    def _():
        carry_scr[...] = jnp.zeros_like(carry_scr)

    for j in range(n_chunks + 1):
        if j < n_chunks:
            ub = u_scr.at[j % 2]
            col0 = [(c // half) * d_ff + j * FFN_CHUNK + (c % half) * LANES for c in range(n_slabs)]
            h2 = h2_scr[...]
            for part in range(2):
                u = _dot(h2, wup_ref[:, col0[part * half]:col0[part * half] + FFN_CHUNK])
                for c in range(part * half, (part + 1) * half):
                    ub[c, :CARRY_ROWS, :] = carry_scr[j, c]
                    ub[c, CARRY_ROWS:, :] = u[:, (c % half) * LANES:(c % half + 1) * LANES]
                    carry_scr[j, c] = ub[c, tm:, :]
        if j > 0:
            acc_scr[...] += _dot(act_scr[(j - 1) % 2], wdn_ref[(j - 1) * FFN_CHUNK:j * FFN_CHUNK, :])
        if j < n_chunks:
            def conv(c):
                cs = slice(col0[c], col0[c] + LANES)
                return (cb_ref[:, cs]
                        + ub[c, CARRY_ROWS - 2:CARRY_ROWS - 2 + tm, :] * cw_ref[0:1, cs]
                        + ub[c, CARRY_ROWS - 1:CARRY_ROWS - 1 + tm, :] * cw_ref[1:2, cs]
                        + ub[c, CARRY_ROWS:, :] * cw_ref[2:3, cs])
            for c in range(half):
                gate = conv(c)
                act = gate * jax.nn.sigmoid(gate) * conv(c + half)
                act_scr[j % 2, :, c * LANES:(c + 1) * LANES] = act.astype(BF16)

    x2 = x1_scr[...] + acc_scr[...]
    gate = jax.nn.sigmoid(_dot(_rms(x2, g3_ref[...]).astype(BF16), wpg_ref[...]) + bpg_ref[...])
    emb = _rms(_dot(p_ref[...].astype(BF16), wpe_ref[...]), gpe_ref[...])
    o_ref[...] = x2 + gate * emb


def _post(x2d, yg, yf, p2d, wo, g2, wup, cw, cb, wdn, g3, wpg, bpg, wpe, gpe, b, s, tm):
    d = x2d.shape[1]
    nt = s // tm
    tile = lambda w: pl.BlockSpec((tm, w), lambda i, j: (i * nt + j, 0))
    resident = lambda a: pl.BlockSpec(a.shape, lambda i, j: (0,) * a.ndim, pipeline_mode=pl.Buffered(1))
    n_chunks = wdn.shape[0] // FFN_CHUNK
    n_slabs = 2 * FFN_CHUNK // LANES
    return pl.pallas_call(
        _post_kernel,
        grid=(b, nt),
        in_specs=[tile(d), tile(yg.shape[1]), tile(yf.shape[1]), tile(p2d.shape[1])]
                 + [resident(a) for a in (wo, g2, wup, cw, cb, wdn, g3, wpg, bpg, wpe, gpe)],
        out_specs=tile(d),
        out_shape=jax.ShapeDtypeStruct(x2d.shape, F32),
        scratch_shapes=[
            pltpu.VMEM((tm, d), F32),
            pltpu.VMEM((tm, d), BF16),
            pltpu.VMEM((2, n_slabs, CARRY_ROWS + tm, LANES), F32),
            pltpu.VMEM((2, tm, FFN_CHUNK), BF16),
            pltpu.VMEM((n_chunks, n_slabs, CARRY_ROWS, LANES), F32),
            pltpu.VMEM((tm, d), F32),
        ],
        compiler_params=pltpu.CompilerParams(dimension_semantics=("arbitrary", "arbitrary"),
                                             vmem_limit_bytes=VMEM_LIMIT),
        name="post",
    )(x2d, yg, yf, p2d, wo, g2, wup, cw, cb, wdn, g3, wpg, bpg, wpe, gpe)


def _layer(x2d, p2d, b, s, norm1_g, w_in, lr_w, lr_b, onorm_g, b_f, qn_g, kn_g, w_o, norm2_g, w_up,
           conv_w, conv_b, w_down, norm3_g, w_pe, pe_norm_g, w_pg, b_pg):
    d = x2d.shape[1]
    d_ff = w_down.shape[0]
    gqk, gvw, fw = GLA_HEADS * GLA_DK, GLA_HEADS * GLA_DV, FOX_HEADS * FOX_DH
    sizes = (gqk, gqk, gvw, gvw, GLA_LOWRANK, fw, fw, fw, FOX_HEADS)
    offs = [0]
    for n in sizes:
        offs.append(offs[-1] + n)
    w_gla = w_in[:, offs[0]:offs[4]].astype(BF16)
    w_fox = w_in[:, offs[5]:offs[8]].astype(BF16)
    w_misc = jnp.zeros((d, LANES), BF16)
    w_misc = w_misc.at[:, MISC_F0:MISC_F0 + FOX_HEADS].set(w_in[:, offs[8]:offs[9]].astype(BF16))
    w_misc = w_misc.at[:, MISC_LR0:MISC_LR0 + GLA_LOWRANK].set(w_in[:, offs[4]:offs[5]].astype(BF16))

    row = lambda a: a.reshape(1, -1).astype(F32)
    bf_pad = jnp.zeros((1, LANES), F32).at[0, MISC_F0:MISC_F0 + FOX_HEADS].set(b_f)
    qg2 = row(jnp.tile(qn_g, 2)) * (FOX_DH ** -0.5 * LOG2E)
    gq, gk, gv, gog, fqa, fka, fv, misc = _inproj(x2d, row(norm1_g), w_gla, w_fox, w_misc, bf_pad, qg2,
                                                  row(jnp.tile(kn_g, 2)), s, tm=512)

    lrw_pad = jnp.zeros((LANES, gqk), F32).at[MISC_LR0:MISC_LR0 + GLA_LOWRANK].set(lr_w).astype(BF16)
    y_gla = _gla(gq, gk, gv, gog, misc, lrw_pad, row(lr_b), row(onorm_g), b, s)
    y_fox = _fox(fqa, fka, fv, b, s)

    return _post(
        x2d, y_gla, y_fox, p2d, w_o.astype(BF16), row(norm2_g),
        w_up.astype(BF16), conv_w.astype(F32), row(conv_b), w_down.astype(BF16),
        row(norm3_g), w_pg.astype(BF16), row(b_pg), w_pe.astype(BF16), row(pe_norm_g), b, s, tm=512)


def kernel(x, p, norm1_g, w_in, gla_lr_w, gla_lr_b, gla_onorm_g, fox_b_f, fox_qnorm_g, fox_knorm_g, w_o, norm2_g, w_up, conv_w, conv_b, w_down, norm3_g, w_pe, pe_norm_g, w_pg, b_pg):
    b, s, d = x.shape
    x2d = x.reshape(b * s, d)
    for i in range(p.shape[0]):
        x2d = _layer(x2d, p[i].reshape(b * s, -1), b, s, norm1_g[i], w_in[i], gla_lr_w[i], gla_lr_b[i],
                     gla_onorm_g[i], fox_b_f[i], fox_qnorm_g[i], fox_knorm_g[i], w_o[i], norm2_g[i], w_up[i],
                     conv_w[i], conv_b[i], w_down[i], norm3_g[i], w_pe[i], pe_norm_g[i], w_pg[i], b_pg[i])
    return x2d.reshape(b, s, d)
```

```python
import functools

import jax
import jax.numpy as jnp
from jax import lax
from jax.experimental import pallas as pl
from jax.experimental.pallas import tpu as pltpu

F32 = jnp.float32
BF16 = jnp.bfloat16

EPS = 1e-6
LOG2E = 1.4426950408889634
LANES = 128
GLA_HEADS, GLA_DK, GLA_DV = 4, 64, 128
GLA_LOWRANK = 16
GLA_INV_TAU = 1.0 / 16.0
GLA_CHUNK = 64
GLA_SUPER = 256
FOX_HEADS, FOX_DH = 8, 64
FOX_TILE = 256
CONV_W = 3
FFN_CHUNK = 256
CARRY_ROWS = 8
MISC_F0 = 0
MISC_LR0 = 8
VMEM_LIMIT = 56 * 1024 * 1024


def _dot(a, b):
    return jnp.dot(a, b, preferred_element_type=F32)


def _dot_nt(a, b):
    return lax.dot_general(a, b, (((1,), (1,)), ((), ())), preferred_element_type=F32)


def _dot_tn(a, b):
    return lax.dot_general(a, b, (((0,), (0,)), ((), ())), preferred_element_type=F32)


def _log_sigmoid(z):
    return jnp.minimum(z, 0.0) - jnp.log(1.0 + jnp.exp(-jnp.abs(z)))


def _split_bf16(x, pieces):
    out = []
    for _ in range(pieces):
        p = x.astype(BF16)
        out.append(p)
        x = x - p.astype(F32)
    return out


def _rms(x, gain):
    ms = jnp.mean(x * x, axis=-1, keepdims=True)
    return x * lax.rsqrt(ms + EPS) * gain


def _inproj_kernel(x_ref, g_ref, wg_ref, wf_ref, wm_ref, bf_ref, qg_ref, kg_ref,
                   gq_ref, gk_ref, gv_ref, gog_ref, fqa_ref, fka_ref, fv_ref, misc_ref, carry_scr, h_scr, *,
                   tiles_per_seq):
    tm = x_ref.shape[0]
    fw = FOX_HEADS * FOX_DH

    @pl.when(pl.program_id(0) % tiles_per_seq == 0)
    def _():
        carry_scr[...] = jnp.zeros_like(carry_scr)

    h_scr[...] = _rms(x_ref[...], g_ref[...]).astype(BF16)
    h = h_scr[...]
    misc = _dot(h, wm_ref[...])
    misc_ref[...] = misc
    q_all = _dot(h, wf_ref[:, :fw])
    k_all = _dot(h, wf_ref[:, fw:2 * fw])

    blk = FOX_TILE
    trow = lax.broadcasted_iota(jnp.int32, (blk, blk), 0)
    tcol = lax.broadcasted_iota(jnp.int32, (blk, blk), 1)
    tri = (tcol <= trow).astype(BF16)
    carry = carry_scr[...]
    c_blocks = []
    for r0 in range(0, tm, blk):
        cb = carry
        for piece in _split_bf16(_log_sigmoid(misc[r0:r0 + blk] + bf_ref[...]), 3):
            cb = cb + _dot(tri, piece)
        c_blocks.append(cb)
        carry = cb[blk - 1:blk]
    carry_scr[...] = carry
    c_all = jnp.concatenate(c_blocks, axis=0)

    lane = lax.broadcasted_iota(jnp.int32, (tm, LANES), 1)
    lower = lane < FOX_DH

    def pair_norm(x, g):
        sq = x * x
        ms_lo = jnp.sum(jnp.where(lower, sq, 0.0), axis=-1, keepdims=True)
        ms_hi = jnp.sum(jnp.where(lower, 0.0, sq), axis=-1, keepdims=True)
        ms = jnp.where(lower, ms_lo, ms_hi) * (1.0 / FOX_DH)
        return x * lax.rsqrt(ms + EPS) * g

    fv_ref[...] = _dot(h, wf_ref[:, 2 * fw:]).astype(fv_ref.dtype)
    off = 0
    for ref in (gq_ref, gk_ref, gv_ref, gog_ref):
        n = ref.shape[-1]
        ref[...] = _dot(h, wg_ref[:, off:off + n]).astype(ref.dtype)
        off += n

    for pair in range(fw // LANES):
        ps = slice(pair * LANES, (pair + 1) * LANES)
        qn = pair_norm(q_all[:, ps], qg_ref[...])
        kn = pair_norm(k_all[:, ps], kg_ref[...])
        for e in range(2):
            head = 2 * pair + e
            in_head = (lane >= FOX_DH) if e else lower
            piece0 = 0 if e else FOX_DH
            c_col = jnp.sum(jnp.where(lane == MISC_F0 + head, c_all, 0.0), axis=-1, keepdims=True)
            is_piece = (lane >= piece0) & (lane < piece0 + 3)
            hs = slice(head * LANES, (head + 1) * LANES)
            fqa_ref[:, hs] = jnp.where(in_head, qn, jnp.where(is_piece, 1.0, 0.0)).astype(BF16)
            ka = jnp.where(in_head, kn, 0.0).astype(BF16)
            for j, p in enumerate(_split_bf16(c_col * (-LOG2E), 3)):
                ka = jnp.where(lane == piece0 + j, p, ka)
            fka_ref[:, hs] = ka


def _inproj(x2d, g, w_gla, w_fox, w_misc, bf_pad, qg2, kg2, s, tm):
    t, d = x2d.shape
    widths = (256, 256, 512, 512, FOX_HEADS * LANES, FOX_HEADS * LANES, 512, LANES)
    dtypes = (BF16,) * 7 + (F32,)
    whole = lambda a: pl.BlockSpec(a.shape, lambda i: (0, 0))
    return pl.pallas_call(
        functools.partial(_inproj_kernel, tiles_per_seq=s // tm),
        grid=(t // tm,),
        in_specs=[pl.BlockSpec((tm, d), lambda i: (i, 0))]
                 + [whole(a) for a in (g, w_gla, w_fox, w_misc, bf_pad, qg2, kg2)],
        out_specs=[pl.BlockSpec((tm, n), lambda i: (i, 0)) for n in widths],
        out_shape=[jax.ShapeDtypeStruct((t, n), dt) for n, dt in zip(widths, dtypes)],
        scratch_shapes=[pltpu.VMEM((1, LANES), F32),
                        pltpu.VMEM((tm, d), BF16)],
        compiler_params=pltpu.CompilerParams(dimension_semantics=("arbitrary",), vmem_limit_bytes=VMEM_LIMIT),
        name="inproj",
    )(x2d, g, w_gla, w_fox, w_misc, bf_pad, qg2, kg2)


def _gla_kernel(q_ref, k_ref, v_ref, og_ref, misc_ref, lrw_ref, lrb_ref, ong_ref, y_ref):
    s_len = q_ref.shape[0]
    r = GLA_SUPER
    n_sb = s_len // r
    n_ch = r // GLA_CHUNK
    dkw = GLA_HEADS * GLA_DK
    row = lax.broadcasted_iota(jnp.int32, (r, r), 0)
    col = lax.broadcasted_iota(jnp.int32, (r, r), 1)
    causal_bd = ((row // GLA_CHUNK) == (col // GLA_CHUNK)) & (col <= row)
    tri_bd = causal_bd.astype(BF16)
    head_of_col = lax.broadcasted_iota(jnp.int32, (r, dkw), 1) // GLA_DK
    heads = range(GLA_HEADS)
    sb_rows = lambda n: slice(n * r, (n + 1) * r)
    ch_rows = lambda c: slice(c * GLA_CHUNK, (c + 1) * GLA_CHUNK)
    v_cols = lambda h: slice(h * GLA_DV, (h + 1) * GLA_DV)

    la_pieces = []
    for n in range(n_sb):
        z = _dot(misc_ref[sb_rows(n), :].astype(BF16), lrw_ref[...]) + lrb_ref[...]
        la_pieces.append(_split_bf16(_log_sigmoid(z) * GLA_INV_TAU, 2))

    qh, ke, kd, decay = [], [], [], []
    for n in range(n_sb):
        b = _dot(tri_bd, la_pieces[n][0]) + _dot(tri_bd, la_pieces[n][1])
        b_last = [b[(c + 1) * GLA_CHUNK - 1:(c + 1) * GLA_CHUNK, :] for c in range(n_ch)]
        b_tot = jnp.concatenate([jnp.broadcast_to(bl, (GLA_CHUNK, dkw)) for bl in b_last], axis=0)
        q = q_ref[sb_rows(n), :].astype(F32)
        k = k_ref[sb_rows(n), :].astype(F32)
        qe = (q * (GLA_DK ** -0.5)) * jnp.exp(b)
        qh.append([jnp.where(head_of_col == h, qe, 0.0).astype(BF16) for h in heads])
        ke.append((k * jnp.exp(-b)).astype(BF16))
        kd_n = k * jnp.exp(b_tot - b)
        kd.append([jnp.where(head_of_col == h, kd_n, 0.0).astype(BF16) for h in heads])
        decay.append([jnp.exp(bl) for bl in b_last])

    a = [[jnp.where(causal_bd, _dot_nt(qh[n][h], ke[n]), 0.0).astype(BF16) for h in heads] for n in range(n_sb)]

    o_intra = [[_dot(a[n][h], v_ref[sb_rows(n), v_cols(h)]) for h in heads] for n in range(n_sb)]

    ds = []
    for n in range(n_sb):
        for c in range(n_ch):
            rows = slice(n * r + c * GLA_CHUNK, n * r + (c + 1) * GLA_CHUNK)
            v_stack = jnp.concatenate([v_ref[rows, v_cols(h)] for h in heads], axis=0)
            kd_stack = jnp.concatenate([kd[n][h][ch_rows(c)] for h in heads], axis=0)
            ds.append(_dot_tn(v_stack, kd_stack))

    st = jnp.zeros((GLA_DV, dkw), F32)
    o_inter = []
    for n in range(n_sb):
        for c in range(n_ch):
            qe_stack = jnp.concatenate([qh[n][h][ch_rows(c)] for h in heads], axis=0)
            o_inter.append(_dot_nt(qe_stack, st.astype(BF16)))
            st = st * decay[n][c] + ds[n * n_ch + c]

    for n in range(n_sb):
        og = og_ref[sb_rows(n), :].astype(F32)
        gate = og * jax.nn.sigmoid(og)
        for h in heads:
            inter = jnp.concatenate([o_inter[n * n_ch + c][ch_rows(h)] for c in range(n_ch)], axis=0)
            o = o_intra[n][h] + inter
            y_ref[sb_rows(n), v_cols(h)] = (_rms(o, ong_ref[...]) * gate[:, v_cols(h)]).astype(y_ref.dtype)


def _gla(gq, gk, gv, gog, misc, lrw_pad, lrb, ong, b, s):
    dkw, dvw = GLA_HEADS * GLA_DK, GLA_HEADS * GLA_DV
    seq = lambda w: pl.BlockSpec((s, w), lambda i: (i, 0))
    whole = lambda a: pl.BlockSpec(a.shape, lambda i: (0, 0))
    return pl.pallas_call(
        _gla_kernel,
        grid=(b,),
        in_specs=[seq(dkw), seq(dkw), seq(dvw), seq(dvw), seq(LANES), whole(lrw_pad), whole(lrb), whole(ong)],
        out_specs=seq(dvw),
        out_shape=jax.ShapeDtypeStruct((b * s, dvw), BF16),
        compiler_params=pltpu.CompilerParams(dimension_semantics=("arbitrary",), vmem_limit_bytes=VMEM_LIMIT),
        name="gla",
    )(gq, gk, gv, gog, misc, lrw_pad, lrb, ong)


def _fox_kernel(qa_ref, ka_ref, v_ref, y_ref, vt_scr, s_scr):
    s_len = v_ref.shape[0]
    t = FOX_TILE
    n_tiles = s_len // t
    krow = lax.broadcasted_iota(jnp.int32, (t, t), 0)
    qcol = lax.broadcasted_iota(jnp.int32, (t, t), 1)
    orow = lax.broadcasted_iota(jnp.int32, (LANES, t), 0)
    vt_scr[...] = v_ref[...].astype(F32).T.astype(BF16)

    def scores(i, e):
        hs = slice(e * LANES, (e + 1) * LANES)
        qa = qa_ref[i * t:(i + 1) * t, hs]
        mp = None
        for kb in range(i + 1):
            sc = _dot_nt(ka_ref[kb * t:(kb + 1) * t, hs], qa)
            if kb == i:
                sc = jnp.where(krow <= qcol, sc, -jnp.inf)
            s_scr[e, kb] = sc
            bm = jnp.max(sc.reshape(t // 8, 8, t), axis=0)
            mp = bm if mp is None else jnp.maximum(mp, bm)
        return jnp.max(mp, axis=0, keepdims=True)

    def weighted_values(i, e, m):
        lp = jnp.zeros((8, t), F32)
        acc = jnp.zeros((LANES, t), F32)
        for kb in range(i + 1):
            p = jnp.exp2(s_scr[e, kb] - m)
            lp = lp + jnp.sum(p.reshape(t // 8, 8, t), axis=0)
            acc = acc + _dot(vt_scr[:, kb * t:(kb + 1) * t], p.astype(BF16))
        return acc / jnp.sum(lp, axis=0, keepdims=True)

    units = [(i, e) for i in range(n_tiles) for e in range(2)]
    m_next = scores(*units[0])
    o_even = None
    for n, (i, e) in enumerate(units):
        m_cur = m_next
        if n + 1 < len(units):
            m_next = scores(*units[n + 1])
        o_t = weighted_values(i, e, m_cur)
        if e == 0:
            o_even = o_t
        else:
            o_pair = jnp.where(orow < FOX_DH, o_even, o_t)
            y_ref[i * t:(i + 1) * t, :] = o_pair.T.astype(y_ref.dtype)


def _fox(fqa, fka, fv, b, s):
    n_pairs = FOX_HEADS * FOX_DH // LANES
    t = FOX_TILE
    heads2 = pl.BlockSpec((s, 2 * LANES), lambda i, j: (i, j))
    pair = pl.BlockSpec((s, LANES), lambda i, j: (i, j))
    return pl.pallas_call(
        _fox_kernel,
        grid=(b, n_pairs),
        in_specs=[heads2, heads2, pair],
        out_specs=pair,
        out_shape=jax.ShapeDtypeStruct((b * s, FOX_HEADS * FOX_DH), BF16),
        scratch_shapes=[
            pltpu.VMEM((LANES, s), BF16),
            pltpu.VMEM((2, s // t, t, t), F32),
        ],
        compiler_params=pltpu.CompilerParams(dimension_semantics=("arbitrary", "arbitrary"),
                                             vmem_limit_bytes=VMEM_LIMIT),
        name="fox",
    )(fqa, fka, fv)


def _post_kernel(x_ref, yg_ref, yf_ref, p_ref, wo_ref, g2_ref, wup_ref, cw_ref, cb_ref, wdn_ref,
                 g3_ref, wpg_ref, bpg_ref, wpe_ref, gpe_ref, o_ref,
                 x1_scr, h2_scr, u_scr, act_scr, carry_scr, acc_scr):
    tm = x_ref.shape[0]
    d_ff = wdn_ref.shape[0]
    n_chunks = d_ff // FFN_CHUNK
    n_slabs = 2 * FFN_CHUNK // LANES
    half = n_slabs // 2
    dvw = yg_ref.shape[1]
    first_tile = pl.program_id(1) == 0

    x1 = x_ref[...] + _dot(yg_ref[...], wo_ref[:dvw, :]) + _dot(yf_ref[...], wo_ref[dvw:, :])
    x1_scr[...] = x1
    h2_scr[...] = _rms(x1, g2_ref[...]).astype(BF16)
    acc_scr[...] = jnp.zeros_like(acc_scr)

    @pl.when(first_tile)
    def _():
        carry_scr[...] = jnp.zeros_like(carry_scr)

    for j in range(n_chunks + 1):
        if j < n_chunks:
            ub = u_scr.at[j % 2]
            col0 = [(c // half) * d_ff + j * FFN_CHUNK + (c % half) * LANES for c in range(n_slabs)]
            h2 = h2_scr[...]
            for part in range(2):
                u = _dot(h2, wup_ref[:, col0[part * half]:col0[part * half] + FFN_CHUNK])
                for c in range(part * half, (part + 1) * half):
                    ub[c, :CARRY_ROWS, :] = carry_scr[j, c]
                    ub[c, CARRY_ROWS:, :] = u[:, (c % half) * LANES:(c % half + 1) * LANES]
                    carry_scr[j, c] = ub[c, tm:, :]
        if j > 0:
            acc_scr[...] += _dot(act_scr[(j - 1) % 2], wdn_ref[(j - 1) * FFN_CHUNK:j * FFN_CHUNK, :])
        if j < n_chunks:
            def conv(c):
                cs = slice(col0[c], col0[c] + LANES)
                return (cb_ref[:, cs]
                        + ub[c, CARRY_ROWS - 2:CARRY_ROWS - 2 + tm, :] * cw_ref[0:1, cs]
                        + ub[c, CARRY_ROWS - 1:CARRY_ROWS - 1 + tm, :] * cw_ref[1:2, cs]
                        + ub[c, CARRY_ROWS:, :] * cw_ref[2:3, cs])
            for c in range(half):
                gate = conv(c)
                act = gate * jax.nn.sigmoid(gate) * conv(c + half)
                act_scr[j % 2, :, c * LANES:(c + 1) * LANES] = act.astype(BF16)

    x2 = x1_scr[...] + acc_scr[...]
    gate = jax.nn.sigmoid(_dot(_rms(x2, g3_ref[...]).astype(BF16), wpg_ref[...]) + bpg_ref[...])
    emb = _rms(_dot(p_ref[...].astype(BF16), wpe_ref[...]), gpe_ref[...])
    o_ref[...] = x2 + gate * emb


def _post(x2d, yg, yf, p2d, wo, g2, wup, cw, cb, wdn, g3, wpg, bpg, wpe, gpe, b, s, tm):
    d = x2d.shape[1]
    nt = s // tm
    tile = lambda w: pl.BlockSpec((tm, w), lambda i, j: (i * nt + j, 0))
    resident = lambda a: pl.BlockSpec(a.shape, lambda i, j: (0,) * a.ndim, pipeline_mode=pl.Buffered(1))
    n_chunks = wdn.shape[0] // FFN_CHUNK
    n_slabs = 2 * FFN_CHUNK // LANES
    return pl.pallas_call(
        _post_kernel,
        grid=(b, nt),
        in_specs=[tile(d), tile(yg.shape[1]), tile(yf.shape[1]), tile(p2d.shape[1])]
                 + [resident(a) for a in (wo, g2, wup, cw, cb, wdn, g3, wpg, bpg, wpe, gpe)],
        out_specs=tile(d),
        out_shape=jax.ShapeDtypeStruct(x2d.shape, F32),
        scratch_shapes=[
            pltpu.VMEM((tm, d), F32),
            pltpu.VMEM((tm, d), BF16),
            pltpu.VMEM((2, n_slabs, CARRY_ROWS + tm, LANES), F32),
            pltpu.VMEM((2, tm, FFN_CHUNK), BF16),
            pltpu.VMEM((n_chunks, n_slabs, CARRY_ROWS, LANES), F32),
            pltpu.VMEM((tm, d), F32),
        ],
        compiler_params=pltpu.CompilerParams(dimension_semantics=("arbitrary", "arbitrary"),
                                             vmem_limit_bytes=VMEM_LIMIT),
        name="post",
    )(x2d, yg, yf, p2d, wo, g2, wup, cw, cb, wdn, g3, wpg, bpg, wpe, gpe)


def _layer(x2d, p2d, b, s, norm1_g, w_in, lr_w, lr_b, onorm_g, b_f, qn_g, kn_g, w_o, norm2_g, w_up,
           conv_w, conv_b, w_down, norm3_g, w_pe, pe_norm_g, w_pg, b_pg):
    d = x2d.shape[1]
    d_ff = w_down.shape[0]
    gqk, gvw, fw = GLA_HEADS * GLA_DK, GLA_HEADS * GLA_DV, FOX_HEADS * FOX_DH
    sizes = (gqk, gqk, gvw, gvw, GLA_LOWRANK, fw, fw, fw, FOX_HEADS)
    offs = [0]
    for n in sizes:
        offs.append(offs[-1] + n)
    w_gla = w_in[:, offs[0]:offs[4]].astype(BF16)
    w_fox = w_in[:, offs[5]:offs[8]].astype(BF16)
    w_misc = jnp.zeros((d, LANES), BF16)
    w_misc = w_misc.at[:, MISC_F0:MISC_F0 + FOX_HEADS].set(w_in[:, offs[8]:offs[9]].astype(BF16))
    w_misc = w_misc.at[:, MISC_LR0:MISC_LR0 + GLA_LOWRANK].set(w_in[:, offs[4]:offs[5]].astype(BF16))

    row = lambda a: a.reshape(1, -1).astype(F32)
    bf_pad = jnp.zeros((1, LANES), F32).at[0, MISC_F0:MISC_F0 + FOX_HEADS].set(b_f)
    qg2 = row(jnp.tile(qn_g, 2)) * (FOX_DH ** -0.5 * LOG2E)
    gq, gk, gv, gog, fqa, fka, fv, misc = _inproj(x2d, row(norm1_g), w_gla, w_fox, w_misc, bf_pad, qg2,
                                                  row(jnp.tile(kn_g, 2)), s, tm=512)

    lrw_pad = jnp.zeros((LANES, gqk), F32).at[MISC_LR0:MISC_LR0 + GLA_LOWRANK].set(lr_w).astype(BF16)
    y_gla = _gla(gq, gk, gv, gog, misc, lrw_pad, row(lr_b), row(onorm_g), b, s)
    y_fox = _fox(fqa, fka, fv, b, s)

    return _post(
        x2d, y_gla, y_fox, p2d, w_o.astype(BF16), row(norm2_g),
        w_up.astype(BF16), conv_w.astype(F32), row(conv_b), w_down.astype(BF16),
        row(norm3_g), w_pg.astype(BF16), row(b_pg), w_pe.astype(BF16), row(pe_norm_g), b, s, tm=512)


def kernel(x, p, norm1_g, w_in, gla_lr_w, gla_lr_b, gla_onorm_g, fox_b_f, fox_qnorm_g, fox_knorm_g, w_o, norm2_g, w_up, conv_w, conv_b, w_down, norm3_g, w_pe, pe_norm_g, w_pg, b_pg):
    b, s, d = x.shape
    x2d = x.reshape(b * s, d)
    for i in range(p.shape[0]):
        x2d = _layer(x2d, p[i].reshape(b * s, -1), b, s, norm1_g[i], w_in[i], gla_lr_w[i], gla_lr_b[i],
                     gla_onorm_g[i], fox_b_f[i], fox_qnorm_g[i], fox_knorm_g[i], w_o[i], norm2_g[i], w_up[i],
                     conv_w[i], conv_b[i], w_down[i], norm3_g[i], w_pe[i], pe_norm_g[i], w_pg[i], b_pg[i])
    return x2d.reshape(b, s, d)
```

```python
import functools

import jax
import jax.numpy as jnp
from jax import lax
from jax.experimental import pallas as pl
from jax.experimental.pallas import tpu as pltpu

F32 = jnp.float32
BF16 = jnp.bfloat16

EPS = 1e-6
LOG2E = 1.4426950408889634
LANES = 128
GLA_HEADS, GLA_DK, GLA_DV = 4, 64, 128
GLA_LOWRANK = 16
GLA_INV_TAU = 1.0 / 16.0
GLA_CHUNK = 64
GLA_SUPER = 256
FOX_HEADS, FOX_DH = 8, 64
FOX_TILE = 256
CONV_W = 3
FFN_CHUNK = 256
CARRY_ROWS = 8
DOWN_LAG = 2
MISC_F0 = 0
MISC_LR0 = 8
VMEM_LIMIT = 56 * 1024 * 1024


def _dot(a, b):
    return jnp.dot(a, b, preferred_element_type=F32)


def _dot_nt(a, b):
    return lax.dot_general(a, b, (((1,), (1,)), ((), ())), preferred_element_type=F32)


def _dot_tn(a, b):
    return lax.dot_general(a, b, (((0,), (0,)), ((), ())), preferred_element_type=F32)


def _log_sigmoid(z):
    return jnp.minimum(z, 0.0) - jnp.log(1.0 + jnp.exp(-jnp.abs(z)))


def _split_bf16(x, pieces):
    out = []
    for _ in range(pieces):
        p = x.astype(BF16)
        out.append(p)
        x = x - p.astype(F32)
    return out


def _rms(x, gain):
    ms = jnp.mean(x * x, axis=-1, keepdims=True)
    return x * lax.rsqrt(ms + EPS) * gain


def _inproj_kernel(x_ref, g_ref, wg_ref, wf_ref, wm_ref, bf_ref, qg_ref, kg_ref,
                   gq_ref, gk_ref, gv_ref, gog_ref, fqa_ref, fka_ref, fv_ref, misc_ref, carry_scr, h_scr, *,
                   tiles_per_seq):
    tm = x_ref.shape[0]
    fw = FOX_HEADS * FOX_DH

    @pl.when(pl.program_id(0) % tiles_per_seq == 0)
    def _():
        carry_scr[...] = jnp.zeros_like(carry_scr)

    h_scr[...] = _rms(x_ref[...], g_ref[...]).astype(BF16)
    h = h_scr[...]
    misc = _dot(h, wm_ref[...])
    misc_ref[...] = misc
    q_all = _dot(h, wf_ref[:, :fw])
    k_all = _dot(h, wf_ref[:, fw:2 * fw])

    blk = FOX_TILE
    trow = lax.broadcasted_iota(jnp.int32, (blk, blk), 0)
    tcol = lax.broadcasted_iota(jnp.int32, (blk, blk), 1)
    tri = (tcol <= trow).astype(BF16)
    carry = carry_scr[...]
    c_blocks = []
    for r0 in range(0, tm, blk):
        cb = carry
        for piece in _split_bf16(_log_sigmoid(misc[r0:r0 + blk] + bf_ref[...]), 3):
            cb = cb + _dot(tri, piece)
        c_blocks.append(cb)
        carry = cb[blk - 1:blk]
    carry_scr[...] = carry
    c_all = jnp.concatenate(c_blocks, axis=0)

    lane = lax.broadcasted_iota(jnp.int32, (tm, LANES), 1)
    lower = lane < FOX_DH

    def pair_norm(x, g):
        sq = x * x
        ms_lo = jnp.sum(jnp.where(lower, sq, 0.0), axis=-1, keepdims=True)
        ms_hi = jnp.sum(jnp.where(lower, 0.0, sq), axis=-1, keepdims=True)
        ms = jnp.where(lower, ms_lo, ms_hi) * (1.0 / FOX_DH)
        return x * lax.rsqrt(ms + EPS) * g

    fv_ref[...] = _dot(h, wf_ref[:, 2 * fw:]).astype(fv_ref.dtype)
    off = 0
    for ref in (gq_ref, gk_ref, gv_ref, gog_ref):
        n = ref.shape[-1]
        ref[...] = _dot(h, wg_ref[:, off:off + n]).astype(ref.dtype)
        off += n

    for pair in range(fw // LANES):
        ps = slice(pair * LANES, (pair + 1) * LANES)
        qn = pair_norm(q_all[:, ps], qg_ref[...])
        kn = pair_norm(k_all[:, ps], kg_ref[...])
        for e in range(2):
            head = 2 * pair + e
            in_head = (lane >= FOX_DH) if e else lower
            piece0 = 0 if e else FOX_DH
            c_col = jnp.sum(jnp.where(lane == MISC_F0 + head, c_all, 0.0), axis=-1, keepdims=True)
            is_piece = (lane >= piece0) & (lane < piece0 + 3)
            hs = slice(head * LANES, (head + 1) * LANES)
            fqa_ref[:, hs] = jnp.where(in_head, qn, jnp.where(is_piece, 1.0, 0.0)).astype(BF16)
            ka = jnp.where(in_head, kn, 0.0).astype(BF16)
            for j, p in enumerate(_split_bf16(c_col * (-LOG2E), 3)):
                ka = jnp.where(lane == piece0 + j, p, ka)
            fka_ref[:, hs] = ka


def _inproj(x2d, g, w_gla, w_fox, w_misc, bf_pad, qg2, kg2, s, tm):
    t, d = x2d.shape
    widths = (256, 256, 512, 512, FOX_HEADS * LANES, FOX_HEADS * LANES, 512, LANES)
    dtypes = (BF16,) * 7 + (F32,)
    whole = lambda a: pl.BlockSpec(a.shape, lambda i: (0, 0))
    return pl.pallas_call(
        functools.partial(_inproj_kernel, tiles_per_seq=s // tm),
        grid=(t // tm,),
        in_specs=[pl.BlockSpec((tm, d), lambda i: (i, 0))]
                 + [whole(a) for a in (g, w_gla, w_fox, w_misc, bf_pad, qg2, kg2)],
        out_specs=[pl.BlockSpec((tm, n), lambda i: (i, 0)) for n in widths],
        out_shape=[jax.ShapeDtypeStruct((t, n), dt) for n, dt in zip(widths, dtypes)],
        scratch_shapes=[pltpu.VMEM((1, LANES), F32),
                        pltpu.VMEM((tm, d), BF16)],
        compiler_params=pltpu.CompilerParams(dimension_semantics=("arbitrary",), vmem_limit_bytes=VMEM_LIMIT),
        name="inproj",
    )(x2d, g, w_gla, w_fox, w_misc, bf_pad, qg2, kg2)


def _gla_kernel(q_ref, k_ref, v_ref, og_ref, misc_ref, lrw_ref, lrb_ref, ong_ref, y_ref):
    s_len = q_ref.shape[0]
    r = GLA_SUPER
    n_sb = s_len // r
    n_ch = r // GLA_CHUNK
    dkw = GLA_HEADS * GLA_DK
    row = lax.broadcasted_iota(jnp.int32, (r, r), 0)
    col = lax.broadcasted_iota(jnp.int32, (r, r), 1)
    causal_bd = ((row // GLA_CHUNK) == (col // GLA_CHUNK)) & (col <= row)
    tri_bd = causal_bd.astype(BF16)
    head_of_col = lax.broadcasted_iota(jnp.int32, (r, dkw), 1) // GLA_DK
    heads = range(GLA_HEADS)
    sb_rows = lambda n: slice(n * r, (n + 1) * r)
    ch_rows = lambda c: slice(c * GLA_CHUNK, (c + 1) * GLA_CHUNK)
    v_cols = lambda h: slice(h * GLA_DV, (h + 1) * GLA_DV)

    la_pieces = []
    for n in range(n_sb):
        z = _dot(misc_ref[sb_rows(n), :].astype(BF16), lrw_ref[...]) + lrb_ref[...]
        la_pieces.append(_split_bf16(_log_sigmoid(z) * GLA_INV_TAU, 2))

    qh, ke, kd, decay = [], [], [], []
    for n in range(n_sb):
        b = _dot(tri_bd, la_pieces[n][0]) + _dot(tri_bd, la_pieces[n][1])
        b_last = [b[(c + 1) * GLA_CHUNK - 1:(c + 1) * GLA_CHUNK, :] for c in range(n_ch)]
        b_tot = jnp.concatenate([jnp.broadcast_to(bl, (GLA_CHUNK, dkw)) for bl in b_last], axis=0)
        q = q_ref[sb_rows(n), :].astype(F32)
        k = k_ref[sb_rows(n), :].astype(F32)
        qe = (q * (GLA_DK ** -0.5)) * jnp.exp(b)
        qh.append([jnp.where(head_of_col == h, qe, 0.0).astype(BF16) for h in heads])
        ke.append((k * jnp.exp(-b)).astype(BF16))
        kd_n = k * jnp.exp(b_tot - b)
        kd.append([jnp.where(head_of_col == h, kd_n, 0.0).astype(BF16) for h in heads])
        decay.append([jnp.exp(bl) for bl in b_last])

    a = [[jnp.where(causal_bd, _dot_nt(qh[n][h], ke[n]), 0.0).astype(BF16) for h in heads] for n in range(n_sb)]

    o_intra = [[_dot(a[n][h], v_ref[sb_rows(n), v_cols(h)]) for h in heads] for n in range(n_sb)]

    ds = []
    for n in range(n_sb):
        for c in range(n_ch):
            rows = slice(n * r + c * GLA_CHUNK, n * r + (c + 1) * GLA_CHUNK)
            v_stack = jnp.concatenate([v_ref[rows, v_cols(h)] for h in heads], axis=0)
            kd_stack = jnp.concatenate([kd[n][h][ch_rows(c)] for h in heads], axis=0)
            ds.append(_dot_tn(v_stack, kd_stack))

    st = jnp.zeros((GLA_DV, dkw), F32)
    o_inter = []
    for n in range(n_sb):
        for c in range(n_ch):
            qe_stack = jnp.concatenate([qh[n][h][ch_rows(c)] for h in heads], axis=0)
            o_inter.append(_dot_nt(qe_stack, st.astype(BF16)))
            st = st * decay[n][c] + ds[n * n_ch + c]

    for n in range(n_sb):
        og = og_ref[sb_rows(n), :].astype(F32)
        gate = og * jax.nn.sigmoid(og)
        for h in heads:
            inter = jnp.concatenate([o_inter[n * n_ch + c][ch_rows(h)] for c in range(n_ch)], axis=0)
            o = o_intra[n][h] + inter
            y_ref[sb_rows(n), v_cols(h)] = (_rms(o, ong_ref[...]) * gate[:, v_cols(h)]).astype(y_ref.dtype)


def _gla(gq, gk, gv, gog, misc, lrw_pad, lrb, ong, b, s):
    dkw, dvw = GLA_HEADS * GLA_DK, GLA_HEADS * GLA_DV
    seq = lambda w: pl.BlockSpec((s, w), lambda i: (i, 0))
    whole = lambda a: pl.BlockSpec(a.shape, lambda i: (0, 0))
    return pl.pallas_call(
        _gla_kernel,
        grid=(b,),
        in_specs=[seq(dkw), seq(dkw), seq(dvw), seq(dvw), seq(LANES), whole(lrw_pad), whole(lrb), whole(ong)],
        out_specs=seq(dvw),
        out_shape=jax.ShapeDtypeStruct((b * s, dvw), BF16),
        compiler_params=pltpu.CompilerParams(dimension_semantics=("arbitrary",), vmem_limit_bytes=VMEM_LIMIT),
        name="gla",
    )(gq, gk, gv, gog, misc, lrw_pad, lrb, ong)


def _fox_kernel(qa_ref, ka_ref, v_ref, y_ref, vt_scr, s_scr):
    s_len = v_ref.shape[0]
    t = FOX_TILE
    n_tiles = s_len // t
    krow = lax.broadcasted_iota(jnp.int32, (t, t), 0)
    qcol = lax.broadcasted_iota(jnp.int32, (t, t), 1)
    orow = lax.broadcasted_iota(jnp.int32, (LANES, t), 0)
    vt_scr[...] = v_ref[...].astype(F32).T.astype(BF16)

    def scores(i, e):
        hs = slice(e * LANES, (e + 1) * LANES)
        qa = qa_ref[i * t:(i + 1) * t, hs]
        mp = None
        for kb in range(i + 1):
            sc = _dot_nt(ka_ref[kb * t:(kb + 1) * t, hs], qa)
            if kb == i:
                sc = jnp.where(krow <= qcol, sc, -jnp.inf)
            s_scr[e, kb] = sc
            bm = jnp.max(sc.reshape(t // 8, 8, t), axis=0)
            mp = bm if mp is None else jnp.maximum(mp, bm)
        return jnp.max(mp, axis=0, keepdims=True)

    def weighted_values(i, e, m):
        lp = jnp.zeros((8, t), F32)
        acc = jnp.zeros((LANES, t), F32)
        for kb in range(i + 1):
            p = jnp.exp2(s_scr[e, kb] - m)
            lp = lp + jnp.sum(p.reshape(t // 8, 8, t), axis=0)
            acc = acc + _dot(vt_scr[:, kb * t:(kb + 1) * t], p.astype(BF16))
        return acc / jnp.sum(lp, axis=0, keepdims=True)

    units = [(i, e) for i in range(n_tiles) for e in range(2)]
    m_next = scores(*units[0])
    o_even = None
    for n, (i, e) in enumerate(units):
        m_cur = m_next
        if n + 1 < len(units):
            m_next = scores(*units[n + 1])
        o_t = weighted_values(i, e, m_cur)
        if e == 0:
            o_even = o_t
        else:
            o_pair = jnp.where(orow < FOX_DH, o_even, o_t)
            y_ref[i * t:(i + 1) * t, :] = o_pair.T.astype(y_ref.dtype)


def _fox(fqa, fka, fv, b, s):
    n_pairs = FOX_HEADS * FOX_DH // LANES
    t = FOX_TILE
    heads2 = pl.BlockSpec((s, 2 * LANES), lambda i, j: (i, j))
    pair = pl.BlockSpec((s, LANES), lambda i, j: (i, j))
    return pl.pallas_call(
        _fox_kernel,
        grid=(b, n_pairs),
        in_specs=[heads2, heads2, pair],
        out_specs=pair,
        out_shape=jax.ShapeDtypeStruct((b * s, FOX_HEADS * FOX_DH), BF16),
        scratch_shapes=[
            pltpu.VMEM((LANES, s), BF16),
            pltpu.VMEM((2, s // t, t, t), F32),
        ],
        compiler_params=pltpu.CompilerParams(dimension_semantics=("arbitrary", "arbitrary"),
                                             vmem_limit_bytes=VMEM_LIMIT),
        name="fox",
    )(fqa, fka, fv)


def _post_kernel(x_ref, yg_ref, yf_ref, p_ref, wo_ref, g2_ref, wup_ref, cw_ref, cb_ref, wdn_ref,
                 g3_ref, wpg_ref, bpg_ref, wpe_ref, gpe_ref, o_ref,
                 x1_scr, h2_scr, u_scr, act_scr, carry_scr, acc_scr):
    tm = x_ref.shape[0]
    d_ff = wdn_ref.shape[0]
    n_chunks = d_ff // FFN_CHUNK
    n_slabs = 2 * FFN_CHUNK // LANES
    half = n_slabs // 2
    dvw = yg_ref.shape[1]
    first_tile = pl.program_id(1) == 0

    @pl.when(first_tile)
    def _():
        carry_scr[...] = jnp.zeros_like(carry_scr)

    o_ref[...] = _rms(_dot(p_ref[...].astype(BF16), wpe_ref[...]), gpe_ref[...])

    x1 = x_ref[...] + _dot(yg_ref[...], wo_ref[:dvw, :]) + _dot(yf_ref[...], wo_ref[dvw:, :])
    x1_scr[...] = x1
    h2_scr[...] = _rms(x1, g2_ref[...]).astype(BF16)
    acc_scr[...] = jnp.zeros_like(acc_scr)

    for j in range(n_chunks + DOWN_LAG):
        if j < n_chunks:
            ub = u_scr.at[j % 2]
            col0 = [(c // half) * d_ff + j * FFN_CHUNK + (c % half) * LANES for c in range(n_slabs)]
            h2 = h2_scr[...]
            for part in range(2):
                u = _dot(h2, wup_ref[:, col0[part * half]:col0[part * half] + FFN_CHUNK])
                for c in range(part * half, (part + 1) * half):
                    ub[c, :CARRY_ROWS, :] = carry_scr[j, c]
                    ub[c, CARRY_ROWS:, :] = u[:, (c % half) * LANES:(c % half + 1) * LANES]
                    carry_scr[j, c] = ub[c, tm:, :]
        if j >= DOWN_LAG:
            jd = j - DOWN_LAG
            acc_scr[...] += _dot(act_scr[jd % (DOWN_LAG + 1)], wdn_ref[jd * FFN_CHUNK:(jd + 1) * FFN_CHUNK, :])
        if j < n_chunks:
            def conv(c):
                cs = slice(col0[c], col0[c] + LANES)
                return (cb_ref[:, cs]
                        + ub[c, CARRY_ROWS - 2:CARRY_ROWS - 2 + tm, :] * cw_ref[0:1, cs]
                        + ub[c, CARRY_ROWS - 1:CARRY_ROWS - 1 + tm, :] * cw_ref[1:2, cs]
                        + ub[c, CARRY_ROWS:, :] * cw_ref[2:3, cs])
            for c in range(half):
                gate = conv(c)
                act = gate * jax.nn.sigmoid(gate) * conv(c + half)
                act_scr[j % (DOWN_LAG + 1), :, c * LANES:(c + 1) * LANES] = act.astype(BF16)

    x2 = x1_scr[...] + acc_scr[...]
    gate = jax.nn.sigmoid(_dot(_rms(x2, g3_ref[...]).astype(BF16), wpg_ref[...]) + bpg_ref[...])
    o_ref[...] = x2 + gate * o_ref[...]


def _post(x2d, yg, yf, p2d, wo, g2, wup, cw, cb, wdn, g3, wpg, bpg, wpe, gpe, b, s, tm):
    d = x2d.shape[1]
    nt = s // tm
    tile = lambda w: pl.BlockSpec((tm, w), lambda i, j: (i * nt + j, 0))
    resident = lambda a: pl.BlockSpec(a.shape, lambda i, j: (0,) * a.ndim, pipeline_mode=pl.Buffered(1))
    n_chunks = wdn.shape[0] // FFN_CHUNK
    n_slabs = 2 * FFN_CHUNK // LANES
    return pl.pallas_call(
        _post_kernel,
        grid=(b, nt),
        in_specs=[tile(d), tile(yg.shape[1]), tile(yf.shape[1]), tile(p2d.shape[1])]
                 + [resident(a) for a in (wo, g2, wup, cw, cb, wdn, g3, wpg, bpg, wpe, gpe)],
        out_specs=tile(d),
        out_shape=jax.ShapeDtypeStruct(x2d.shape, F32),
        scratch_shapes=[
            pltpu.VMEM((tm, d), F32),
            pltpu.VMEM((tm, d), BF16),
            pltpu.VMEM((2, n_slabs, CARRY_ROWS + tm, LANES), F32),
            pltpu.VMEM((DOWN_LAG + 1, tm, FFN_CHUNK), BF16),
            pltpu.VMEM((n_chunks, n_slabs, CARRY_ROWS, LANES), F32),
            pltpu.VMEM((tm, d), F32),
        ],
        compiler_params=pltpu.CompilerParams(dimension_semantics=("arbitrary", "arbitrary"),
                                             vmem_limit_bytes=VMEM_LIMIT),
        name="post",
    )(x2d, yg, yf, p2d, wo, g2, wup, cw, cb, wdn, g3, wpg, bpg, wpe, gpe)


def _layer(x2d, p2d, b, s, norm1_g, w_in, lr_w, lr_b, onorm_g, b_f, qn_g, kn_g, w_o, norm2_g, w_up,
           conv_w, conv_b, w_down, norm3_g, w_pe, pe_norm_g, w_pg, b_pg):
    d = x2d.shape[1]
    d_ff = w_down.shape[0]
    gqk, gvw, fw = GLA_HEADS * GLA_DK, GLA_HEADS * GLA_DV, FOX_HEADS * FOX_DH
    sizes = (gqk, gqk, gvw, gvw, GLA_LOWRANK, fw, fw, fw, FOX_HEADS)
    offs = [0]
    for n in sizes:
        offs.append(offs[-1] + n)
    w_gla = w_in[:, offs[0]:offs[4]].astype(BF16)
    w_fox = w_in[:, offs[5]:offs[8]].astype(BF16)
    w_misc = jnp.zeros((d, LANES), BF16)
    w_misc = w_misc.at[:, MISC_F0:MISC_F0 + FOX_HEADS].set(w_in[:, offs[8]:offs[9]].astype(BF16))
    w_misc = w_misc.at[:, MISC_LR0:MISC_LR0 + GLA_LOWRANK].set(w_in[:, offs[4]:offs[5]].astype(BF16))

    row = lambda a: a.reshape(1, -1).astype(F32)
    bf_pad = jnp.zeros((1, LANES), F32).at[0, MISC_F0:MISC_F0 + FOX_HEADS].set(b_f)
    qg2 = row(jnp.tile(qn_g, 2)) * (FOX_DH ** -0.5 * LOG2E)
    gq, gk, gv, gog, fqa, fka, fv, misc = _inproj(x2d, row(norm1_g), w_gla, w_fox, w_misc, bf_pad, qg2,
                                                  row(jnp.tile(kn_g, 2)), s, tm=512)

    lrw_pad = jnp.zeros((LANES, gqk), F32).at[MISC_LR0:MISC_LR0 + GLA_LOWRANK].set(lr_w).astype(BF16)
    y_gla = _gla(gq, gk, gv, gog, misc, lrw_pad, row(lr_b), row(onorm_g), b, s)
    y_fox = _fox(fqa, fka, fv, b, s)

    return _post(
        x2d, y_gla, y_fox, p2d, w_o.astype(BF16), row(norm2_g),
        w_up.astype(BF16), conv_w.astype(F32), row(conv_b), w_down.astype(BF16),
        row(norm3_g), w_pg.astype(BF16), row(b_pg), w_pe.astype(BF16), row(pe_norm_g), b, s, tm=512)


def kernel(x, p, norm1_g, w_in, gla_lr_w, gla_lr_b, gla_onorm_g, fox_b_f, fox_qnorm_g, fox_knorm_g, w_o, norm2_g, w_up, conv_w, conv_b, w_down, norm3_g, w_pe, pe_norm_g, w_pg, b_pg):
    b, s, d = x.shape
    x2d = x.reshape(b * s, d)
    for i in range(p.shape[0]):
        x2d = _layer(x2d, p[i].reshape(b * s, -1), b, s, norm1_g[i], w_in[i], gla_lr_w[i], gla_lr_b[i],
                     gla_onorm_g[i], fox_b_f[i], fox_qnorm_g[i], fox_knorm_g[i], w_o[i], norm2_g[i], w_up[i],
                     conv_w[i], conv_b[i], w_down[i], norm3_g[i], w_pe[i], pe_norm_g[i], w_pg[i], b_pg[i])
    return x2d.reshape(b, s, d)
```

```python
import functools

import jax
import jax.numpy as jnp
from jax import lax
from jax.experimental import pallas as pl
from jax.experimental.pallas import tpu as pltpu

F32 = jnp.float32
BF16 = jnp.bfloat16

EPS = 1e-6
LOG2E = 1.4426950408889634
LANES = 128
GLA_HEADS, GLA_DK, GLA_DV = 4, 64, 128
GLA_LOWRANK = 16
GLA_INV_TAU = 1.0 / 16.0
GLA_CHUNK = 64
GLA_SUPER = 256
FOX_HEADS, FOX_DH = 8, 64
FOX_TILE = 256
CONV_W = 3
FFN_CHUNK = 256
CARRY_ROWS = 8
DOWN_LAG = 2
MISC_F0 = 0
MISC_LR0 = 8
VMEM_LIMIT = 56 * 1024 * 1024


def _dot(a, b):
    return jnp.dot(a, b, preferred_element_type=F32)


def _dot_nt(a, b):
    return lax.dot_general(a, b, (((1,), (1,)), ((), ())), preferred_element_type=F32)


def _dot_tn(a, b):
    return lax.dot_general(a, b, (((0,), (0,)), ((), ())), preferred_element_type=F32)


def _log_sigmoid(z):
    return jnp.minimum(z, 0.0) - jnp.log(1.0 + jnp.exp(-jnp.abs(z)))


def _split_bf16(x, pieces):
    out = []
    for _ in range(pieces):
        p = x.astype(BF16)
        out.append(p)
        x = x - p.astype(F32)
    return out


def _rms(x, gain):
    ms = jnp.mean(x * x, axis=-1, keepdims=True)
    return x * lax.rsqrt(ms + EPS) * gain


def _inproj_kernel(x_ref, g_ref, wg_ref, wf_ref, wm_ref, bf_ref, qg_ref, kg_ref,
                   gq_ref, gk_ref, gv_ref, gog_ref, fqa_ref, fka_ref, fv_ref, misc_ref, carry_scr, h_scr, *,
                   tiles_per_seq):
    tm = x_ref.shape[0]
    fw = FOX_HEADS * FOX_DH

    @pl.when(pl.program_id(0) % tiles_per_seq == 0)
    def _():
        carry_scr[...] = jnp.zeros_like(carry_scr)

    h_scr[...] = _rms(x_ref[...], g_ref[...]).astype(BF16)
    h = h_scr[...]
    misc = _dot(h, wm_ref[...])
    misc_ref[...] = misc
    q_all = _dot(h, wf_ref[:, :fw])
    k_all = _dot(h, wf_ref[:, fw:2 * fw])

    blk = FOX_TILE
    trow = lax.broadcasted_iota(jnp.int32, (blk, blk), 0)
    tcol = lax.broadcasted_iota(jnp.int32, (blk, blk), 1)
    tri = (tcol <= trow).astype(BF16)
    carry = carry_scr[...]
    c_blocks = []
    for r0 in range(0, tm, blk):
        cb = carry
        for piece in _split_bf16(_log_sigmoid(misc[r0:r0 + blk] + bf_ref[...]), 3):
            cb = cb + _dot(tri, piece)
        c_blocks.append(cb)
        carry = cb[blk - 1:blk]
    carry_scr[...] = carry
    c_all = jnp.concatenate(c_blocks, axis=0)

    lane = lax.broadcasted_iota(jnp.int32, (tm, LANES), 1)
    lower = lane < FOX_DH

    def pair_norm(x, g):
        sq = x * x
        ms_lo = jnp.sum(jnp.where(lower, sq, 0.0), axis=-1, keepdims=True)
        ms_hi = jnp.sum(jnp.where(lower, 0.0, sq), axis=-1, keepdims=True)
        ms = jnp.where(lower, ms_lo, ms_hi) * (1.0 / FOX_DH)
        return x * lax.rsqrt(ms + EPS) * g

    fv_ref[...] = _dot(h, wf_ref[:, 2 * fw:]).astype(fv_ref.dtype)
    off = 0
    for ref in (gq_ref, gk_ref, gv_ref, gog_ref):
        n = ref.shape[-1]
        ref[...] = _dot(h, wg_ref[:, off:off + n]).astype(ref.dtype)
        off += n

    for pair in range(fw // LANES):
        ps = slice(pair * LANES, (pair + 1) * LANES)
        qn = pair_norm(q_all[:, ps], qg_ref[...])
        kn = pair_norm(k_all[:, ps], kg_ref[...])
        for e in range(2):
            head = 2 * pair + e
            in_head = (lane >= FOX_DH) if e else lower
            piece0 = 0 if e else FOX_DH
            c_col = jnp.sum(jnp.where(lane == MISC_F0 + head, c_all, 0.0), axis=-1, keepdims=True)
            is_piece = (lane >= piece0) & (lane < piece0 + 3)
            hs = slice(head * LANES, (head + 1) * LANES)
            fqa_ref[:, hs] = jnp.where(in_head, qn, jnp.where(is_piece, 1.0, 0.0)).astype(BF16)
            ka = jnp.where(in_head, kn, 0.0).astype(BF16)
            for j, p in enumerate(_split_bf16(c_col * (-LOG2E), 3)):
                ka = jnp.where(lane == piece0 + j, p, ka)
            fka_ref[:, hs] = ka


def _inproj(x2d, g, w_gla, w_fox, w_misc, bf_pad, qg2, kg2, s, tm):
    t, d = x2d.shape
    widths = (256, 256, 512, 512, FOX_HEADS * LANES, FOX_HEADS * LANES, 512, LANES)
    dtypes = (BF16,) * 7 + (F32,)
    whole = lambda a: pl.BlockSpec(a.shape, lambda i: (0, 0))
    return pl.pallas_call(
        functools.partial(_inproj_kernel, tiles_per_seq=s // tm),
        grid=(t // tm,),
        in_specs=[pl.BlockSpec((tm, d), lambda i: (i, 0))]
                 + [whole(a) for a in (g, w_gla, w_fox, w_misc, bf_pad, qg2, kg2)],
        out_specs=[pl.BlockSpec((tm, n), lambda i: (i, 0)) for n in widths],
        out_shape=[jax.ShapeDtypeStruct((t, n), dt) for n, dt in zip(widths, dtypes)],
        scratch_shapes=[pltpu.VMEM((1, LANES), F32),
                        pltpu.VMEM((tm, d), BF16)],
        compiler_params=pltpu.CompilerParams(dimension_semantics=("arbitrary",), vmem_limit_bytes=VMEM_LIMIT),
        name="inproj",
    )(x2d, g, w_gla, w_fox, w_misc, bf_pad, qg2, kg2)


def _gla_stages(q_ref, k_ref, v_ref, og_ref, misc_ref, lrw_ref, lrb_ref, ong_ref, y_ref, st_scr):
    s_len = q_ref.shape[0]
    r = GLA_SUPER
    n_sb = s_len // r
    n_ch = r // GLA_CHUNK
    dkw = GLA_HEADS * GLA_DK
    row = lax.broadcasted_iota(jnp.int32, (r, r), 0)
    col = lax.broadcasted_iota(jnp.int32, (r, r), 1)
    causal_bd = ((row // GLA_CHUNK) == (col // GLA_CHUNK)) & (col <= row)
    tri_bd = causal_bd.astype(BF16)
    head_of_col = lax.broadcasted_iota(jnp.int32, (r, dkw), 1) // GLA_DK
    heads = range(GLA_HEADS)
    sb_rows = lambda n: slice(n * r, (n + 1) * r)
    ch_rows = lambda c: slice(c * GLA_CHUNK, (c + 1) * GLA_CHUNK)
    v_cols = lambda h: slice(h * GLA_DV, (h + 1) * GLA_DV)

    la_pieces = []
    for n in range(n_sb):
        z = _dot(misc_ref[sb_rows(n), :].astype(BF16), lrw_ref[...]) + lrb_ref[...]
        la_pieces.append(_split_bf16(_log_sigmoid(z) * GLA_INV_TAU, 2))
        yield

    qh, ke, kd, decay = [], [], [], []
    for n in range(n_sb):
        b = _dot(tri_bd, la_pieces[n][0]) + _dot(tri_bd, la_pieces[n][1])
        b_last = [b[(c + 1) * GLA_CHUNK - 1:(c + 1) * GLA_CHUNK, :] for c in range(n_ch)]
        b_tot = jnp.concatenate([jnp.broadcast_to(bl, (GLA_CHUNK, dkw)) for bl in b_last], axis=0)
        q = q_ref[sb_rows(n), :].astype(F32)
        k = k_ref[sb_rows(n), :].astype(F32)
        qe = (q * (GLA_DK ** -0.5)) * jnp.exp(b)
        qh.append([jnp.where(head_of_col == h, qe, 0.0).astype(BF16) for h in heads])
        ke.append((k * jnp.exp(-b)).astype(BF16))
        kd_n = k * jnp.exp(b_tot - b)
        kd.append([jnp.where(head_of_col == h, kd_n, 0.0).astype(BF16) for h in heads])
        decay.append([jnp.exp(bl) for bl in b_last])
        yield

    a = []
    for n in range(n_sb):
        a.append([jnp.where(causal_bd, _dot_nt(qh[n][h], ke[n]), 0.0).astype(BF16) for h in heads])
        yield

    o_intra = []
    for n in range(n_sb):
        o_intra.append([_dot(a[n][h], v_ref[sb_rows(n), v_cols(h)]) for h in heads])
        yield

    ds = []
    for n in range(n_sb):
        for c in range(n_ch):
            rows = slice(n * r + c * GLA_CHUNK, n * r + (c + 1) * GLA_CHUNK)
            v_stack = jnp.concatenate([v_ref[rows, v_cols(h)] for h in heads], axis=0)
            kd_stack = jnp.concatenate([kd[n][h][ch_rows(c)] for h in heads], axis=0)
            ds.append(_dot_tn(v_stack, kd_stack))
        yield

    st = st_scr[...]
    o_inter = []
    for n in range(n_sb):
        for c in range(n_ch):
            qe_stack = jnp.concatenate([qh[n][h][ch_rows(c)] for h in heads], axis=0)
            o_inter.append(_dot_nt(qe_stack, st.astype(BF16)))
            st = st * decay[n][c] + ds[n * n_ch + c]
        yield
    st_scr[...] = st

    for n in range(n_sb):
        og = og_ref[sb_rows(n), :].astype(F32)
        gate = og * jax.nn.sigmoid(og)
        for h in heads:
            inter = jnp.concatenate([o_inter[n * n_ch + c][ch_rows(h)] for c in range(n_ch)], axis=0)
            o = o_intra[n][h] + inter
            y_ref[sb_rows(n), v_cols(h)] = (_rms(o, ong_ref[...]) * gate[:, v_cols(h)]).astype(y_ref.dtype)
        yield


def _fox_stages(qa_ref, ka_ref, v_ref, y_ref, vt_scr, s_scr):
    s_len = v_ref.shape[0]
    t = FOX_TILE
    n_tiles = s_len // t
    krow = lax.broadcasted_iota(jnp.int32, (t, t), 0)
    qcol = lax.broadcasted_iota(jnp.int32, (t, t), 1)
    orow = lax.broadcasted_iota(jnp.int32, (LANES, t), 0)
    vt_scr[...] = v_ref[...].astype(F32).T.astype(BF16)

    def scores(i, e):
        hs = slice(e * LANES, (e + 1) * LANES)
        qa = qa_ref[i * t:(i + 1) * t, hs]
        mp = None
        for kb in range(i + 1):
            sc = _dot_nt(ka_ref[kb * t:(kb + 1) * t, hs], qa)
            if kb == i:
                sc = jnp.where(krow <= qcol, sc, -jnp.inf)
            s_scr[e, kb] = sc
            bm = jnp.max(sc.reshape(t // 8, 8, t), axis=0)
            mp = bm if mp is None else jnp.maximum(mp, bm)
        return jnp.max(mp, axis=0, keepdims=True)

    def weighted_values(i, e, m):
        lp = jnp.zeros((8, t), F32)
        acc = jnp.zeros((LANES, t), F32)
        for kb in range(i + 1):
            p = jnp.exp2(s_scr[e, kb] - m)
            lp = lp + jnp.sum(p.reshape(t // 8, 8, t), axis=0)
            acc = acc + _dot(vt_scr[:, kb * t:(kb + 1) * t], p.astype(BF16))
        return acc / jnp.sum(lp, axis=0, keepdims=True)

    units = [(i, e) for i in range(n_tiles) for e in range(2)]
    m_next = scores(*units[0])
    yield
    o_even = None
    for n, (i, e) in enumerate(units):
        m_cur = m_next
        if n + 1 < len(units):
            m_next = scores(*units[n + 1])
            yield
        o_t = weighted_values(i, e, m_cur)
        if e == 0:
            o_even = o_t
        else:
            o_pair = jnp.where(orow < FOX_DH, o_even, o_t)
            y_ref[i * t:(i + 1) * t, :] = o_pair.T.astype(y_ref.dtype)
        yield


def _mixers_kernel(qa_ref, ka_ref, fv_ref, gq_ref, gk_ref, gv_ref, gog_ref, misc_ref, lrw_ref, lrb_ref, ong_ref,
                   yf_ref, yg_ref, vt_scr, s_scr, st_scr):
    @pl.when(pl.program_id(1) == 0)
    def _():
        st_scr[...] = jnp.zeros_like(st_scr)

    fox = _fox_stages(qa_ref, ka_ref, fv_ref, yf_ref, vt_scr, s_scr)
    gla = _gla_stages(gq_ref, gk_ref, gv_ref, gog_ref, misc_ref, lrw_ref, lrb_ref, ong_ref, yg_ref, st_scr)
    rota = (fox, fox, gla)
    finished = set()
    turn = 0
    while len(finished) < 2:
        gen = rota[turn % len(rota)]
        turn += 1
        if gen not in finished:
            try:
                next(gen)
            except StopIteration:
                finished.add(gen)


def _mixers(fqa, fka, fv, gq, gk, gv, gog, misc, lrw_pad, lrb, ong, b, s):
    n_pairs = FOX_HEADS * FOX_DH // LANES
    t = FOX_TILE
    dkw, dvw = GLA_HEADS * GLA_DK, GLA_HEADS * GLA_DV
    sq = s // n_pairs
    heads2 = pl.BlockSpec((s, 2 * LANES), lambda i, j: (i, j))
    pair = pl.BlockSpec((s, LANES), lambda i, j: (i, j))
    quarter = lambda w: pl.BlockSpec((sq, w), lambda i, j: (i * n_pairs + j, 0))
    whole = lambda a: pl.BlockSpec(a.shape, lambda i, j: (0, 0))
    return pl.pallas_call(
        _mixers_kernel,
        grid=(b, n_pairs),
        in_specs=[heads2, heads2, pair, quarter(dkw), quarter(dkw), quarter(dvw), quarter(dvw), quarter(LANES),
                  whole(lrw_pad), whole(lrb), whole(ong)],
        out_specs=[pair, quarter(dvw)],
        out_shape=[jax.ShapeDtypeStruct((b * s, FOX_HEADS * FOX_DH), BF16),
                   jax.ShapeDtypeStruct((b * s, dvw), BF16)],
        scratch_shapes=[
            pltpu.VMEM((LANES, s), BF16),
            pltpu.VMEM((2, s // t, t, t), F32),
            pltpu.VMEM((GLA_DV, dkw), F32),
        ],
        compiler_params=pltpu.CompilerParams(dimension_semantics=("arbitrary", "arbitrary"),
                                             vmem_limit_bytes=VMEM_LIMIT),
        name="mixers",
    )(fqa, fka, fv, gq, gk, gv, gog, misc, lrw_pad, lrb, ong)


def _post_kernel(x_ref, yg_ref, yf_ref, p_ref, wo_ref, g2_ref, wup_ref, cw_ref, cb_ref, wdn_ref,
                 g3_ref, wpg_ref, bpg_ref, wpe_ref, gpe_ref, o_ref,
                 x1_scr, h2_scr, u_scr, act_scr, carry_scr, acc_scr):
    tm = x_ref.shape[0]
    d_ff = wdn_ref.shape[0]
    n_chunks = d_ff // FFN_CHUNK
    n_slabs = 2 * FFN_CHUNK // LANES
    half = n_slabs // 2
    dvw = yg_ref.shape[1]
    first_tile = pl.program_id(1) == 0

    @pl.when(first_tile)
    def _():
        carry_scr[...] = jnp.zeros_like(carry_scr)

    o_ref[...] = _rms(_dot(p_ref[...].astype(BF16), wpe_ref[...]), gpe_ref[...])

    x1 = x_ref[...] + _dot(yg_ref[...], wo_ref[:dvw, :]) + _dot(yf_ref[...], wo_ref[dvw:, :])
    x1_scr[...] = x1
    h2_scr[...] = _rms(x1, g2_ref[...]).astype(BF16)
    acc_scr[...] = jnp.zeros_like(acc_scr)

    for j in range(n_chunks + DOWN_LAG):
        if j < n_chunks:
            ub = u_scr.at[j % 2]
            col0 = [(c // half) * d_ff + j * FFN_CHUNK + (c % half) * LANES for c in range(n_slabs)]
            h2 = h2_scr[...]
            for part in range(2):
                u = _dot(h2, wup_ref[:, col0[part * half]:col0[part * half] + FFN_CHUNK])
                for c in range(part * half, (part + 1) * half):
                    ub[c, :CARRY_ROWS, :] = carry_scr[j, c]
                    ub[c, CARRY_ROWS:, :] = u[:, (c % half) * LANES:(c % half + 1) * LANES]
                    carry_scr[j, c] = ub[c, tm:, :]
        if j >= DOWN_LAG:
            jd = j - DOWN_LAG
            acc_scr[...] += _dot(act_scr[jd % (DOWN_LAG + 1)], wdn_ref[jd * FFN_CHUNK:(jd + 1) * FFN_CHUNK, :])
        if j < n_chunks:
            def conv(c):
                cs = slice(col0[c], col0[c] + LANES)
                return (cb_ref[:, cs]
                        + ub[c, CARRY_ROWS - 2:CARRY_ROWS - 2 + tm, :] * cw_ref[0:1, cs]
                        + ub[c, CARRY_ROWS - 1:CARRY_ROWS - 1 + tm, :] * cw_ref[1:2, cs]
                        + ub[c, CARRY_ROWS:, :] * cw_ref[2:3, cs])
            for c in range(half):
                gate = conv(c)
                act = gate * jax.nn.sigmoid(gate) * conv(c + half)
                act_scr[j % (DOWN_LAG + 1), :, c * LANES:(c + 1) * LANES] = act.astype(BF16)

    x2 = x1_scr[...] + acc_scr[...]
    gate = jax.nn.sigmoid(_dot(_rms(x2, g3_ref[...]).astype(BF16), wpg_ref[...]) + bpg_ref[...])
    o_ref[...] = x2 + gate * o_ref[...]


def _post(x2d, yg, yf, p2d, wo, g2, wup, cw, cb, wdn, g3, wpg, bpg, wpe, gpe, b, s, tm):
    d = x2d.shape[1]
    nt = s // tm
    tile = lambda w: pl.BlockSpec((tm, w), lambda i, j: (i * nt + j, 0))
    resident = lambda a: pl.BlockSpec(a.shape, lambda i, j: (0,) * a.ndim, pipeline_mode=pl.Buffered(1))
    n_chunks = wdn.shape[0] // FFN_CHUNK
    n_slabs = 2 * FFN_CHUNK // LANES
    return pl.pallas_call(
        _post_kernel,
        grid=(b, nt),
        in_specs=[tile(d), tile(yg.shape[1]), tile(yf.shape[1]), tile(p2d.shape[1])]
                 + [resident(a) for a in (wo, g2, wup, cw, cb, wdn, g3, wpg, bpg, wpe, gpe)],
        out_specs=tile(d),
        out_shape=jax.ShapeDtypeStruct(x2d.shape, F32),
        scratch_shapes=[
            pltpu.VMEM((tm, d), F32),
            pltpu.VMEM((tm, d), BF16),
            pltpu.VMEM((2, n_slabs, CARRY_ROWS + tm, LANES), F32),
            pltpu.VMEM((DOWN_LAG + 1, tm, FFN_CHUNK), BF16),
            pltpu.VMEM((n_chunks, n_slabs, CARRY_ROWS, LANES), F32),
            pltpu.VMEM((tm, d), F32),
        ],
        compiler_params=pltpu.CompilerParams(dimension_semantics=("arbitrary", "arbitrary"),
                                             vmem_limit_bytes=VMEM_LIMIT),
        name="post",
    )(x2d, yg, yf, p2d, wo, g2, wup, cw, cb, wdn, g3, wpg, bpg, wpe, gpe)


def _layer(x2d, p2d, b, s, norm1_g, w_in, lr_w, lr_b, onorm_g, b_f, qn_g, kn_g, w_o, norm2_g, w_up,
           conv_w, conv_b, w_down, norm3_g, w_pe, pe_norm_g, w_pg, b_pg):
    d = x2d.shape[1]
    d_ff = w_down.shape[0]
    gqk, gvw, fw = GLA_HEADS * GLA_DK, GLA_HEADS * GLA_DV, FOX_HEADS * FOX_DH
    sizes = (gqk, gqk, gvw, gvw, GLA_LOWRANK, fw, fw, fw, FOX_HEADS)
    offs = [0]
    for n in sizes:
        offs.append(offs[-1] + n)
    w_gla = w_in[:, offs[0]:offs[4]].astype(BF16)
    w_fox = w_in[:, offs[5]:offs[8]].astype(BF16)
    w_misc = jnp.zeros((d, LANES), BF16)
    w_misc = w_misc.at[:, MISC_F0:MISC_F0 + FOX_HEADS].set(w_in[:, offs[8]:offs[9]].astype(BF16))
    w_misc = w_misc.at[:, MISC_LR0:MISC_LR0 + GLA_LOWRANK].set(w_in[:, offs[4]:offs[5]].astype(BF16))

    row = lambda a: a.reshape(1, -1).astype(F32)
    bf_pad = jnp.zeros((1, LANES), F32).at[0, MISC_F0:MISC_F0 + FOX_HEADS].set(b_f)
    qg2 = row(jnp.tile(qn_g, 2)) * (FOX_DH ** -0.5 * LOG2E)
    gq, gk, gv, gog, fqa, fka, fv, misc = _inproj(x2d, row(norm1_g), w_gla, w_fox, w_misc, bf_pad, qg2,
                                                  row(jnp.tile(kn_g, 2)), s, tm=512)

    lrw_pad = jnp.zeros((LANES, gqk), F32).at[MISC_LR0:MISC_LR0 + GLA_LOWRANK].set(lr_w).astype(BF16)
    y_fox, y_gla = _mixers(fqa, fka, fv, gq, gk, gv, gog, misc, lrw_pad, row(lr_b), row(onorm_g), b, s)

    return _post(
        x2d, y_gla, y_fox, p2d, w_o.astype(BF16), row(norm2_g),
        w_up.astype(BF16), conv_w.astype(F32), row(conv_b), w_down.astype(BF16),
        row(norm3_g), w_pg.astype(BF16), row(b_pg), w_pe.astype(BF16), row(pe_norm_g), b, s, tm=512)


def kernel(x, p, norm1_g, w_in, gla_lr_w, gla_lr_b, gla_onorm_g, fox_b_f, fox_qnorm_g, fox_knorm_g, w_o, norm2_g, w_up, conv_w, conv_b, w_down, norm3_g, w_pe, pe_norm_g, w_pg, b_pg):
    b, s, d = x.shape
    x2d = x.reshape(b * s, d)
    for i in range(p.shape[0]):
        x2d = _layer(x2d, p[i].reshape(b * s, -1), b, s, norm1_g[i], w_in[i], gla_lr_w[i], gla_lr_b[i],
                     gla_onorm_g[i], fox_b_f[i], fox_qnorm_g[i], fox_knorm_g[i], w_o[i], norm2_g[i], w_up[i],
                     conv_w[i], conv_b[i], w_down[i], norm3_g[i], w_pe[i], pe_norm_g[i], w_pg[i], b_pg[i])
    return x2d.reshape(b, s, d)
```

```python
import functools

import jax
import jax.numpy as jnp
from jax import lax
from jax.experimental import pallas as pl
from jax.experimental.pallas import tpu as pltpu

F32 = jnp.float32
BF16 = jnp.bfloat16

EPS = 1e-6
LOG2E = 1.4426950408889634
LANES = 128
BF16_SUBLANES = 16
GLA_HEADS, GLA_DK, GLA_DV = 4, 64, 128
GLA_LOWRANK = 16
GLA_INV_TAU = 1.0 / 16.0
GLA_CHUNK = 64
GLA_SUPER = 256
FOX_HEADS, FOX_DH = 8, 64
FOX_TILE = 256
CONV_W = 3
FFN_CHUNK = 256
CARRY_ROWS = 8
DOWN_LAG = 2
MISC_F0 = 0
MISC_LR0 = 8
VMEM_LIMIT = 56 * 1024 * 1024


def _dot(a, b):
    return jnp.dot(a, b, preferred_element_type=F32)


def _dot_nt(a, b):
    return lax.dot_general(a, b, (((1,), (1,)), ((), ())), preferred_element_type=F32)


def _dot_tn(a, b):
    return lax.dot_general(a, b, (((0,), (0,)), ((), ())), preferred_element_type=F32)


def _log_sigmoid(z):
    return jnp.minimum(z, 0.0) - jnp.log(1.0 + jnp.exp(-jnp.abs(z)))


def _split_bf16(x, pieces):
    out = []
    for _ in range(pieces):
        p = x.astype(BF16)
        out.append(p)
        x = x - p.astype(F32)
    return out


def _rms(x, gain):
    ms = jnp.mean(x * x, axis=-1, keepdims=True)
    return x * lax.rsqrt(ms + EPS) * gain


def _inproj_kernel(x_ref, g_ref, wg_ref, wf_ref, wm_ref, bf_ref, qg_ref, kg_ref, *rest, tiles_per_seq, n_cast):
    cast_in, rest = rest[:n_cast], rest[n_cast:]
    gq_ref, gk_ref, gv_ref, gog_ref, fqa_ref, fka_ref, fv_ref, misc_ref = rest[:8]
    cast_out = rest[8:8 + n_cast]
    carry_scr, h_scr = rest[8 + n_cast:]
    _inproj_body(x_ref, g_ref, wg_ref, wf_ref, wm_ref, bf_ref, qg_ref, kg_ref,
                 gq_ref, gk_ref, gv_ref, gog_ref, fqa_ref, fka_ref, fv_ref, misc_ref, carry_scr, h_scr, tiles_per_seq)
    for w_ref, o_ref in zip(cast_in, cast_out):
        o_ref[...] = w_ref[...].astype(BF16)


def _inproj_body(x_ref, g_ref, wg_ref, wf_ref, wm_ref, bf_ref, qg_ref, kg_ref,
                 gq_ref, gk_ref, gv_ref, gog_ref, fqa_ref, fka_ref, fv_ref, misc_ref, carry_scr, h_scr,
                 tiles_per_seq):
    tm = x_ref.shape[0]
    fw = FOX_HEADS * FOX_DH

    @pl.when(pl.program_id(0) % tiles_per_seq == 0)
    def _():
        carry_scr[...] = jnp.zeros_like(carry_scr)

    h_scr[...] = _rms(x_ref[...], g_ref[...]).astype(BF16)
    h = h_scr[...]
    misc = _dot(h, wm_ref[...])
    misc_ref[...] = misc
    q_all = _dot(h, wf_ref[:, :fw])
    k_all = _dot(h, wf_ref[:, fw:2 * fw])

    blk = FOX_TILE
    trow = lax.broadcasted_iota(jnp.int32, (blk, blk), 0)
    tcol = lax.broadcasted_iota(jnp.int32, (blk, blk), 1)
    tri = (tcol <= trow).astype(BF16)
    carry = carry_scr[...]
    c_blocks = []
    for r0 in range(0, tm, blk):
        cb = carry
        for piece in _split_bf16(_log_sigmoid(misc[r0:r0 + blk] + bf_ref[...]), 3):
            cb = cb + _dot(tri, piece)
        c_blocks.append(cb)
        carry = cb[blk - 1:blk]
    carry_scr[...] = carry
    c_all = jnp.concatenate(c_blocks, axis=0)

    lane = lax.broadcasted_iota(jnp.int32, (tm, LANES), 1)
    lower = lane < FOX_DH

    def pair_norm(x, g):
        sq = x * x
        ms_lo = jnp.sum(jnp.where(lower, sq, 0.0), axis=-1, keepdims=True)
        ms_hi = jnp.sum(jnp.where(lower, 0.0, sq), axis=-1, keepdims=True)
        ms = jnp.where(lower, ms_lo, ms_hi) * (1.0 / FOX_DH)
        return x * lax.rsqrt(ms + EPS) * g

    fv_ref[...] = _dot(h, wf_ref[:, 2 * fw:]).astype(fv_ref.dtype)
    off = 0
    for ref in (gq_ref, gk_ref, gv_ref, gog_ref):
        n = ref.shape[-1]
        ref[...] = _dot(h, wg_ref[:, off:off + n]).astype(ref.dtype)
        off += n

    for pair in range(fw // LANES):
        ps = slice(pair * LANES, (pair + 1) * LANES)
        qn = pair_norm(q_all[:, ps], qg_ref[...])
        kn = pair_norm(k_all[:, ps], kg_ref[...])
        for e in range(2):
            head = 2 * pair + e
            in_head = (lane >= FOX_DH) if e else lower
            piece0 = 0 if e else FOX_DH
            c_col = jnp.sum(jnp.where(lane == MISC_F0 + head, c_all, 0.0), axis=-1, keepdims=True)
            is_piece = (lane >= piece0) & (lane < piece0 + 3)
            hs = slice(head * LANES, (head + 1) * LANES)
            fqa_ref[:, hs] = jnp.where(in_head, qn, jnp.where(is_piece, 1.0, 0.0)).astype(BF16)
            ka = jnp.where(in_head, kn, 0.0).astype(BF16)
            for j, p in enumerate(_split_bf16(c_col * (-LOG2E), 3)):
                ka = jnp.where(lane == piece0 + j, p, ka)
            fka_ref[:, hs] = ka


def _cast_slab(rows, steps):
    for n_slabs in range(min(steps, rows // BF16_SUBLANES), 0, -1):
        if rows % n_slabs == 0 and (rows // n_slabs) % BF16_SUBLANES == 0:
            return rows // n_slabs
    raise ValueError(f"no bf16-tile aligned slab for {rows} rows")


def _inproj(x2d, g, w_gla, w_fox, w_misc, bf_pad, qg2, kg2, later_weights, s, tm):
    t, d = x2d.shape
    steps = t // tm
    widths = (256, 256, 512, 512, FOX_HEADS * LANES, FOX_HEADS * LANES, 512, LANES)
    dtypes = (BF16,) * 7 + (F32,)
    whole = lambda a: pl.BlockSpec(a.shape, lambda i: (0, 0))

    def slab(w):
        rows = _cast_slab(w.shape[0], steps)
        last = w.shape[0] // rows - 1
        return pl.BlockSpec((rows, w.shape[1]), lambda i: (jnp.minimum(i, last), 0))

    outs = pl.pallas_call(
        functools.partial(_inproj_kernel, tiles_per_seq=s // tm, n_cast=len(later_weights)),
        grid=(steps,),
        in_specs=[pl.BlockSpec((tm, d), lambda i: (i, 0))]
                 + [whole(a) for a in (g, w_gla, w_fox, w_misc, bf_pad, qg2, kg2)]
                 + [slab(w) for w in later_weights],
        out_specs=[pl.BlockSpec((tm, n), lambda i: (i, 0)) for n in widths] + [slab(w) for w in later_weights],
        out_shape=[jax.ShapeDtypeStruct((t, n), dt) for n, dt in zip(widths, dtypes)]
                  + [jax.ShapeDtypeStruct(w.shape, BF16) for w in later_weights],
        scratch_shapes=[pltpu.VMEM((1, LANES), F32),
                        pltpu.VMEM((tm, d), BF16)],
        compiler_params=pltpu.CompilerParams(dimension_semantics=("arbitrary",), vmem_limit_bytes=VMEM_LIMIT),
        name="inproj",
    )(x2d, g, w_gla, w_fox, w_misc, bf_pad, qg2, kg2, *later_weights)
    return outs[:len(widths)], outs[len(widths):]


def _gla_stages(q_ref, k_ref, v_ref, og_ref, misc_ref, lrw_ref, lrb_ref, ong_ref, y_ref, st_scr):
    s_len = q_ref.shape[0]
    r = GLA_SUPER
    n_sb = s_len // r
    n_ch = r // GLA_CHUNK
    dkw = GLA_HEADS * GLA_DK
    row = lax.broadcasted_iota(jnp.int32, (r, r), 0)
    col = lax.broadcasted_iota(jnp.int32, (r, r), 1)
    causal_bd = ((row // GLA_CHUNK) == (col // GLA_CHUNK)) & (col <= row)
    tri_bd = causal_bd.astype(BF16)
    head_of_col = lax.broadcasted_iota(jnp.int32, (r, dkw), 1) // GLA_DK
    heads = range(GLA_HEADS)
    sb_rows = lambda n: slice(n * r, (n + 1) * r)
    ch_rows = lambda c: slice(c * GLA_CHUNK, (c + 1) * GLA_CHUNK)
    v_cols = lambda h: slice(h * GLA_DV, (h + 1) * GLA_DV)

    la_pieces = []
    for n in range(n_sb):
        z = _dot(misc_ref[sb_rows(n), :].astype(BF16), lrw_ref[...]) + lrb_ref[...]
        la_pieces.append(_split_bf16(_log_sigmoid(z) * GLA_INV_TAU, 2))
        yield

    qh, ke, kd, decay = [], [], [], []
    for n in range(n_sb):
        b = _dot(tri_bd, la_pieces[n][0]) + _dot(tri_bd, la_pieces[n][1])
        b_last = [b[(c + 1) * GLA_CHUNK - 1:(c + 1) * GLA_CHUNK, :] for c in range(n_ch)]
        b_tot = jnp.concatenate([jnp.broadcast_to(bl, (GLA_CHUNK, dkw)) for bl in b_last], axis=0)
        q = q_ref[sb_rows(n), :].astype(F32)
        k = k_ref[sb_rows(n), :].astype(F32)
        qe = (q * (GLA_DK ** -0.5)) * jnp.exp(b)
        qh.append([jnp.where(head_of_col == h, qe, 0.0).astype(BF16) for h in heads])
        ke.append((k * jnp.exp(-b)).astype(BF16))
        kd_n = k * jnp.exp(b_tot - b)
        kd.append([jnp.where(head_of_col == h, kd_n, 0.0).astype(BF16) for h in heads])
        decay.append([jnp.exp(bl) for bl in b_last])
        yield

    a = []
    for n in range(n_sb):
        a.append([jnp.where(causal_bd, _dot_nt(qh[n][h], ke[n]), 0.0).astype(BF16) for h in heads])
        yield

    o_intra = []
    for n in range(n_sb):
        o_intra.append([_dot(a[n][h], v_ref[sb_rows(n), v_cols(h)]) for h in heads])
        yield

    ds = []
    for n in range(n_sb):
        for c in range(n_ch):
            rows = slice(n * r + c * GLA_CHUNK, n * r + (c + 1) * GLA_CHUNK)
            v_stack = jnp.concatenate([v_ref[rows, v_cols(h)] for h in heads], axis=0)
            kd_stack = jnp.concatenate([kd[n][h][ch_rows(c)] for h in heads], axis=0)
            ds.append(_dot_tn(v_stack, kd_stack))
        yield

    st = st_scr[...]
    o_inter = []
    for n in range(n_sb):
        for c in range(n_ch):
            qe_stack = jnp.concatenate([qh[n][h][ch_rows(c)] for h in heads], axis=0)
            o_inter.append(_dot_nt(qe_stack, st.astype(BF16)))
            st = st * decay[n][c] + ds[n * n_ch + c]
        yield
    st_scr[...] = st

    for n in range(n_sb):
        og = og_ref[sb_rows(n), :].astype(F32)
        gate = og * jax.nn.sigmoid(og)
        for h in heads:
            inter = jnp.concatenate([o_inter[n * n_ch + c][ch_rows(h)] for c in range(n_ch)], axis=0)
            o = o_intra[n][h] + inter
            y_ref[sb_rows(n), v_cols(h)] = (_rms(o, ong_ref[...]) * gate[:, v_cols(h)]).astype(y_ref.dtype)
        yield


def _fox_stages(qa_ref, ka_ref, v_ref, y_ref, vt_scr, s_scr):
    s_len = v_ref.shape[0]
    t = FOX_TILE
    n_tiles = s_len // t
    krow = lax.broadcasted_iota(jnp.int32, (t, t), 0)
    qcol = lax.broadcasted_iota(jnp.int32, (t, t), 1)
    orow = lax.broadcasted_iota(jnp.int32, (LANES, t), 0)
    vt_scr[...] = v_ref[...].astype(F32).T.astype(BF16)

    def scores(i, e):
        hs = slice(e * LANES, (e + 1) * LANES)
        qa = qa_ref[i * t:(i + 1) * t, hs]
        mp = None
        for kb in range(i + 1):
            sc = _dot_nt(ka_ref[kb * t:(kb + 1) * t, hs], qa)
            if kb == i:
                sc = jnp.where(krow <= qcol, sc, -jnp.inf)
            s_scr[e, kb] = sc
            bm = jnp.max(sc.reshape(t // 8, 8, t), axis=0)
            mp = bm if mp is None else jnp.maximum(mp, bm)
        return jnp.max(mp, axis=0, keepdims=True)

    def weighted_values(i, e, m):
        lp = jnp.zeros((8, t), F32)
        acc = jnp.zeros((LANES, t), F32)
        for kb in range(i + 1):
            p = jnp.exp2(s_scr[e, kb] - m)
            lp = lp + jnp.sum(p.reshape(t // 8, 8, t), axis=0)
            acc = acc + _dot(vt_scr[:, kb * t:(kb + 1) * t], p.astype(BF16))
        return acc / jnp.sum(lp, axis=0, keepdims=True)

    units = [(i, e) for i in range(n_tiles) for e in range(2)]
    m_next = scores(*units[0])
    yield
    o_even = None
    for n, (i, e) in enumerate(units):
        m_cur = m_next
        if n + 1 < len(units):
            m_next = scores(*units[n + 1])
            yield
        o_t = weighted_values(i, e, m_cur)
        if e == 0:
            o_even = o_t
        else:
            o_pair = jnp.where(orow < FOX_DH, o_even, o_t)
            y_ref[i * t:(i + 1) * t, :] = o_pair.T.astype(y_ref.dtype)
        yield


def _mixers_kernel(qa_ref, ka_ref, fv_ref, gq_ref, gk_ref, gv_ref, gog_ref, misc_ref, lrw_ref, lrb_ref, ong_ref,
                   yf_ref, yg_ref, vt_scr, s_scr, st_scr):
    @pl.when(pl.program_id(1) == 0)
    def _():
        st_scr[...] = jnp.zeros_like(st_scr)

    fox = _fox_stages(qa_ref, ka_ref, fv_ref, yf_ref, vt_scr, s_scr)
    gla = _gla_stages(gq_ref, gk_ref, gv_ref, gog_ref, misc_ref, lrw_ref, lrb_ref, ong_ref, yg_ref, st_scr)
    rota = (fox, fox, gla)
    finished = set()
    turn = 0
    while len(finished) < 2:
        gen = rota[turn % len(rota)]
        turn += 1
        if gen not in finished:
            try:
                next(gen)
            except StopIteration:
                finished.add(gen)


def _mixers(fqa, fka, fv, gq, gk, gv, gog, misc, lrw_pad, lrb, ong, b, s):
    n_pairs = FOX_HEADS * FOX_DH // LANES
    t = FOX_TILE
    dkw, dvw = GLA_HEADS * GLA_DK, GLA_HEADS * GLA_DV
    sq = s // n_pairs
    heads2 = pl.BlockSpec((s, 2 * LANES), lambda i, j: (i, j))
    pair = pl.BlockSpec((s, LANES), lambda i, j: (i, j))
    quarter = lambda w: pl.BlockSpec((sq, w), lambda i, j: (i * n_pairs + j, 0))
    whole = lambda a: pl.BlockSpec(a.shape, lambda i, j: (0, 0))
    return pl.pallas_call(
        _mixers_kernel,
        grid=(b, n_pairs),
        in_specs=[heads2, heads2, pair, quarter(dkw), quarter(dkw), quarter(dvw), quarter(dvw), quarter(LANES),
                  whole(lrw_pad), whole(lrb), whole(ong)],
        out_specs=[pair, quarter(dvw)],
        out_shape=[jax.ShapeDtypeStruct((b * s, FOX_HEADS * FOX_DH), BF16),
                   jax.ShapeDtypeStruct((b * s, dvw), BF16)],
        scratch_shapes=[
            pltpu.VMEM((LANES, s), BF16),
            pltpu.VMEM((2, s // t, t, t), F32),
            pltpu.VMEM((GLA_DV, dkw), F32),
        ],
        compiler_params=pltpu.CompilerParams(dimension_semantics=("arbitrary", "arbitrary"),
                                             vmem_limit_bytes=VMEM_LIMIT),
        name="mixers",
    )(fqa, fka, fv, gq, gk, gv, gog, misc, lrw_pad, lrb, ong)


def _post_kernel(x_ref, yg_ref, yf_ref, p_ref, wo_ref, g2_ref, wup_ref, cw_ref, cb_ref, wdn_ref,
                 g3_ref, wpg_ref, bpg_ref, wpe_ref, gpe_ref, o_ref,
                 x1_scr, h2_scr, u_scr, act_scr, carry_scr, acc_scr):
    tm = x_ref.shape[0]
    d_ff = wdn_ref.shape[0]
    n_chunks = d_ff // FFN_CHUNK
    n_slabs = 2 * FFN_CHUNK // LANES
    half = n_slabs // 2
    dvw = yg_ref.shape[1]
    first_tile = pl.program_id(1) == 0

    @pl.when(first_tile)
    def _():
        carry_scr[...] = jnp.zeros_like(carry_scr)

    o_ref[...] = _rms(_dot(p_ref[...].astype(BF16), wpe_ref[...]), gpe_ref[...])

    x1 = x_ref[...] + _dot(yg_ref[...], wo_ref[:dvw, :]) + _dot(yf_ref[...], wo_ref[dvw:, :])
    x1_scr[...] = x1
    h2_scr[...] = _rms(x1, g2_ref[...]).astype(BF16)
    acc_scr[...] = jnp.zeros_like(acc_scr)

    for j in range(n_chunks + DOWN_LAG):
        if j < n_chunks:
            ub = u_scr.at[j % 2]
            col0 = [(c // half) * d_ff + j * FFN_CHUNK + (c % half) * LANES for c in range(n_slabs)]
            h2 = h2_scr[...]
            for part in range(2):
                u = _dot(h2, wup_ref[:, col0[part * half]:col0[part * half] + FFN_CHUNK])
                for c in range(part * half, (part + 1) * half):
                    ub[c, :CARRY_ROWS, :] = carry_scr[j, c]
                    ub[c, CARRY_ROWS:, :] = u[:, (c % half) * LANES:(c % half + 1) * LANES]
                    carry_scr[j, c] = ub[c, tm:, :]
        if j >= DOWN_LAG:
            jd = j - DOWN_LAG
            acc_scr[...] += _dot(act_scr[jd % (DOWN_LAG + 1)], wdn_ref[jd * FFN_CHUNK:(jd + 1) * FFN_CHUNK, :])
        if j < n_chunks:
            def conv(c):
                cs = slice(col0[c], col0[c] + LANES)
                return (cb_ref[:, cs]
                        + ub[c, CARRY_ROWS - 2:CARRY_ROWS - 2 + tm, :] * cw_ref[0:1, cs]
                        + ub[c, CARRY_ROWS - 1:CARRY_ROWS - 1 + tm, :] * cw_ref[1:2, cs]
                        + ub[c, CARRY_ROWS:, :] * cw_ref[2:3, cs])
            for c in range(half):
                gate = conv(c)
                act = gate * jax.nn.sigmoid(gate) * conv(c + half)
                act_scr[j % (DOWN_LAG + 1), :, c * LANES:(c + 1) * LANES] = act.astype(BF16)

    x2 = x1_scr[...] + acc_scr[...]
    gate = jax.nn.sigmoid(_dot(_rms(x2, g3_ref[...]).astype(BF16), wpg_ref[...]) + bpg_ref[...])
    o_ref[...] = x2 + gate * o_ref[...]


def _post(x2d, yg, yf, p2d, wo, g2, wup, cw, cb, wdn, g3, wpg, bpg, wpe, gpe, b, s, tm):
    d = x2d.shape[1]
    nt = s // tm
    tile = lambda w: pl.BlockSpec((tm, w), lambda i, j: (i * nt + j, 0))
    resident = lambda a: pl.BlockSpec(a.shape, lambda i, j: (0,) * a.ndim, pipeline_mode=pl.Buffered(1))
    n_chunks = wdn.shape[0] // FFN_CHUNK
    n_slabs = 2 * FFN_CHUNK // LANES
    return pl.pallas_call(
        _post_kernel,
        grid=(b, nt),
        in_specs=[tile(d), tile(yg.shape[1]), tile(yf.shape[1]), tile(p2d.shape[1])]
                 + [resident(a) for a in (wo, g2, wup, cw, cb, wdn, g3, wpg, bpg, wpe, gpe)],
        out_specs=tile(d),
        out_shape=jax.ShapeDtypeStruct(x2d.shape, F32),
        scratch_shapes=[
            pltpu.VMEM((tm, d), F32),
            pltpu.VMEM((tm, d), BF16),
            pltpu.VMEM((2, n_slabs, CARRY_ROWS + tm, LANES), F32),
            pltpu.VMEM((DOWN_LAG + 1, tm, FFN_CHUNK), BF16),
            pltpu.VMEM((n_chunks, n_slabs, CARRY_ROWS, LANES), F32),
            pltpu.VMEM((tm, d), F32),
        ],
        compiler_params=pltpu.CompilerParams(dimension_semantics=("arbitrary", "arbitrary"),
                                             vmem_limit_bytes=VMEM_LIMIT),
        name="post",
    )(x2d, yg, yf, p2d, wo, g2, wup, cw, cb, wdn, g3, wpg, bpg, wpe, gpe)


def _layer(x2d, p2d, b, s, norm1_g, w_in, lr_w, lr_b, onorm_g, b_f, qn_g, kn_g, w_o, norm2_g, w_up,
           conv_w, conv_b, w_down, norm3_g, w_pe, pe_norm_g, w_pg, b_pg):
    d = x2d.shape[1]
    d_ff = w_down.shape[0]
    gqk, gvw, fw = GLA_HEADS * GLA_DK, GLA_HEADS * GLA_DV, FOX_HEADS * FOX_DH
    sizes = (gqk, gqk, gvw, gvw, GLA_LOWRANK, fw, fw, fw, FOX_HEADS)
    offs = [0]
    for n in sizes:
        offs.append(offs[-1] + n)
    w_gla = w_in[:, offs[0]:offs[4]].astype(BF16)
    w_fox = w_in[:, offs[5]:offs[8]].astype(BF16)
    w_misc = jnp.zeros((d, LANES), BF16)
    w_misc = w_misc.at[:, MISC_F0:MISC_F0 + FOX_HEADS].set(w_in[:, offs[8]:offs[9]].astype(BF16))
    w_misc = w_misc.at[:, MISC_LR0:MISC_LR0 + GLA_LOWRANK].set(w_in[:, offs[4]:offs[5]].astype(BF16))

    row = lambda a: a.reshape(1, -1).astype(F32)
    bf_pad = jnp.zeros((1, LANES), F32).at[0, MISC_F0:MISC_F0 + FOX_HEADS].set(b_f)
    qg2 = row(jnp.tile(qn_g, 2)) * (FOX_DH ** -0.5 * LOG2E)
    (gq, gk, gv, gog, fqa, fka, fv, misc), (wo_b, wup_b, wdn_b, wpg_b, wpe_b) = _inproj(
        x2d, row(norm1_g), w_gla, w_fox, w_misc, bf_pad, qg2, row(jnp.tile(kn_g, 2)),
        (w_o, w_up, w_down, w_pg, w_pe), s, tm=512)

    lrw_pad = jnp.zeros((LANES, gqk), F32).at[MISC_LR0:MISC_LR0 + GLA_LOWRANK].set(lr_w).astype(BF16)
    y_fox, y_gla = _mixers(fqa, fka, fv, gq, gk, gv, gog, misc, lrw_pad, row(lr_b), row(onorm_g), b, s)

    return _post(
        x2d, y_gla, y_fox, p2d, wo_b, row(norm2_g), wup_b, conv_w.astype(F32), row(conv_b), wdn_b,
        row(norm3_g), wpg_b, row(b_pg), wpe_b, row(pe_norm_g), b, s, tm=512)


def kernel(x, p, norm1_g, w_in, gla_lr_w, gla_lr_b, gla_onorm_g, fox_b_f, fox_qnorm_g, fox_knorm_g, w_o, norm2_g, w_up, conv_w, conv_b, w_down, norm3_g, w_pe, pe_norm_g, w_pg, b_pg):
    b, s, d = x.shape
    x2d = x.reshape(b * s, d)
    for i in range(p.shape[0]):
        x2d = _layer(x2d, p[i].reshape(b * s, -1), b, s, norm1_g[i], w_in[i], gla_lr_w[i], gla_lr_b[i],
                     gla_onorm_g[i], fox_b_f[i], fox_qnorm_g[i], fox_knorm_g[i], w_o[i], norm2_g[i], w_up[i],
                     conv_w[i], conv_b[i], w_down[i], norm3_g[i], w_pe[i], pe_norm_g[i], w_pg[i], b_pg[i])
    return x2d.reshape(b, s, d)
```

```python
import functools

import jax
import jax.numpy as jnp
from jax import lax
from jax.experimental import pallas as pl
from jax.experimental.pallas import tpu as pltpu

F32 = jnp.float32
BF16 = jnp.bfloat16

EPS = 1e-6
LOG2E = 1.4426950408889634
LANES = 128
BF16_SUBLANES = 16
GLA_HEADS, GLA_DK, GLA_DV = 4, 64, 128
GLA_LOWRANK = 16
GLA_INV_TAU = 1.0 / 16.0
GLA_CHUNK = 64
GLA_SUPER = 256
FOX_HEADS, FOX_DH = 8, 64
FOX_TILE = 256
FOX_SLOTS = 3
CONV_W = 3
FFN_CHUNK = 256
CARRY_ROWS = 8
DOWN_LAG = 2
MISC_F0 = 0
MISC_LR0 = 8
VMEM_LIMIT = 56 * 1024 * 1024


def _dot(a, b):
    return jnp.dot(a, b, preferred_element_type=F32)


def _dot_nt(a, b):
    return lax.dot_general(a, b, (((1,), (1,)), ((), ())), preferred_element_type=F32)


def _dot_tn(a, b):
    return lax.dot_general(a, b, (((0,), (0,)), ((), ())), preferred_element_type=F32)


def _log_sigmoid(z):
    return jnp.minimum(z, 0.0) - jnp.log(1.0 + jnp.exp(-jnp.abs(z)))


def _split_bf16(x, pieces):
    out = []
    for _ in range(pieces):
        p = x.astype(BF16)
        out.append(p)
        x = x - p.astype(F32)
    return out


def _rms(x, gain):
    ms = jnp.mean(x * x, axis=-1, keepdims=True)
    return x * lax.rsqrt(ms + EPS) * gain


def _inproj_kernel(x_ref, g_ref, win_ref, bf_ref, qg_ref, kg_ref, *rest, tiles_per_seq, n_cast):
    cast_in, rest = rest[:n_cast], rest[n_cast:]
    gq_ref, gk_ref, gv_ref, gog_ref, fqa_ref, fka_ref, fv_ref, misc_ref = rest[:8]
    cast_out = rest[8:8 + n_cast]
    carry_scr, h_scr, wg_ref, wf_ref, wm_ref = rest[8 + n_cast:]

    @pl.when(pl.program_id(0) == 0)
    def _():
        gla_w = wg_ref.shape[1]
        fox_w = wf_ref.shape[1]
        lr0 = gla_w
        fox0 = lr0 + GLA_LOWRANK
        f0 = fox0 + fox_w
        wg_ref[...] = win_ref[:, :gla_w].astype(BF16)
        wf_ref[...] = win_ref[:, fox0:f0].astype(BF16)
        wm_ref[...] = jnp.zeros_like(wm_ref)
        wm_ref[:, MISC_F0:MISC_F0 + FOX_HEADS] = win_ref[:, f0:f0 + FOX_HEADS].astype(BF16)
        wm_ref[:, MISC_LR0:MISC_LR0 + GLA_LOWRANK] = win_ref[:, lr0:fox0].astype(BF16)

    _inproj_body(x_ref, g_ref, wg_ref, wf_ref, wm_ref, bf_ref, qg_ref, kg_ref,
                 gq_ref, gk_ref, gv_ref, gog_ref, fqa_ref, fka_ref, fv_ref, misc_ref, carry_scr, h_scr, tiles_per_seq)
    for w_ref, o_ref in zip(cast_in, cast_out):
        o_ref[...] = w_ref[...].astype(BF16)


def _inproj_body(x_ref, g_ref, wg_ref, wf_ref, wm_ref, bf_ref, qg_ref, kg_ref,
                 gq_ref, gk_ref, gv_ref, gog_ref, fqa_ref, fka_ref, fv_ref, misc_ref, carry_scr, h_scr,
                 tiles_per_seq):
    tm = x_ref.shape[0]
    fw = FOX_HEADS * FOX_DH

    @pl.when(pl.program_id(0) % tiles_per_seq == 0)
    def _():
        carry_scr[...] = jnp.zeros_like(carry_scr)

    h_scr[...] = _rms(x_ref[...], g_ref[...]).astype(BF16)
    h = h_scr[...]
    misc = _dot(h, wm_ref[...])
    misc_ref[...] = misc
    q_all = _dot(h, wf_ref[:, :fw])
    k_all = _dot(h, wf_ref[:, fw:2 * fw])

    blk = FOX_TILE
    trow = lax.broadcasted_iota(jnp.int32, (blk, blk), 0)
    tcol = lax.broadcasted_iota(jnp.int32, (blk, blk), 1)
    tri = (tcol <= trow).astype(BF16)
    carry = carry_scr[...]
    c_blocks = []
    for r0 in range(0, tm, blk):
        cb = carry
        for piece in _split_bf16(_log_sigmoid(misc[r0:r0 + blk] + bf_ref[...]), 3):
            cb = cb + _dot(tri, piece)
        c_blocks.append(cb)
        carry = cb[blk - 1:blk]
    carry_scr[...] = carry
    c_all = jnp.concatenate(c_blocks, axis=0)

    lane = lax.broadcasted_iota(jnp.int32, (tm, LANES), 1)
    lower = lane < FOX_DH

    def pair_norm(x, g):
        sq = x * x
        ms_lo = jnp.sum(jnp.where(lower, sq, 0.0), axis=-1, keepdims=True)
        ms_hi = jnp.sum(jnp.where(lower, 0.0, sq), axis=-1, keepdims=True)
        ms = jnp.where(lower, ms_lo, ms_hi) * (1.0 / FOX_DH)
        return x * lax.rsqrt(ms + EPS) * g

    fv_ref[...] = _dot(h, wf_ref[:, 2 * fw:]).astype(fv_ref.dtype)
    off = 0
    for ref in (gq_ref, gk_ref, gv_ref, gog_ref):
        n = ref.shape[-1]
        ref[...] = _dot(h, wg_ref[:, off:off + n]).astype(ref.dtype)
        off += n

    for pair in range(fw // LANES):
        ps = slice(pair * LANES, (pair + 1) * LANES)
        qn = pair_norm(q_all[:, ps], qg_ref[...])
        kn = pair_norm(k_all[:, ps], kg_ref[...])
        for e in range(2):
            head = 2 * pair + e
            in_head = (lane >= FOX_DH) if e else lower
            piece0 = 0 if e else FOX_DH
            c_col = jnp.sum(jnp.where(lane == MISC_F0 + head, c_all, 0.0), axis=-1, keepdims=True)
            is_piece = (lane >= piece0) & (lane < piece0 + 3)
            hs = slice(head * LANES, (head + 1) * LANES)
            fqa_ref[:, hs] = jnp.where(in_head, qn, jnp.where(is_piece, 1.0, 0.0)).astype(BF16)
            ka = jnp.where(in_head, kn, 0.0).astype(BF16)
            for j, p in enumerate(_split_bf16(c_col * (-LOG2E), 3)):
                ka = jnp.where(lane == piece0 + j, p, ka)
            fka_ref[:, hs] = ka


def _cast_slab(rows, steps):
    for n_slabs in range(min(steps, rows // BF16_SUBLANES), 0, -1):
        if rows % n_slabs == 0 and (rows // n_slabs) % BF16_SUBLANES == 0:
            return rows // n_slabs
    raise ValueError(f"no bf16-tile aligned slab for {rows} rows")


def _inproj(x2d, g, w_in, bf_pad, qg2, kg2, later_weights, s, tm):
    t, d = x2d.shape
    steps = t // tm
    gla_w = 2 * GLA_HEADS * (GLA_DK + GLA_DV)
    fox_w = 3 * FOX_HEADS * FOX_DH
    assert w_in.shape == (d, gla_w + GLA_LOWRANK + fox_w + FOX_HEADS)
    widths = (256, 256, 512, 512, FOX_HEADS * LANES, FOX_HEADS * LANES, 512, LANES)
    dtypes = (BF16,) * 7 + (F32,)
    whole = lambda a: pl.BlockSpec(a.shape, lambda i: (0, 0))
    once = lambda a: pl.BlockSpec(a.shape, lambda i: (0, 0), pipeline_mode=pl.Buffered(1))

    def slab(w):
        rows = _cast_slab(w.shape[0], steps)
        last = w.shape[0] // rows - 1
        return pl.BlockSpec((rows, w.shape[1]), lambda i: (jnp.minimum(i, last), 0))

    outs = pl.pallas_call(
        functools.partial(_inproj_kernel, tiles_per_seq=s // tm, n_cast=len(later_weights)),
        grid=(steps,),
        in_specs=[pl.BlockSpec((tm, d), lambda i: (i, 0)), whole(g), once(w_in)]
                 + [whole(a) for a in (bf_pad, qg2, kg2)]
                 + [slab(w) for w in later_weights],
        out_specs=[pl.BlockSpec((tm, n), lambda i: (i, 0)) for n in widths] + [slab(w) for w in later_weights],
        out_shape=[jax.ShapeDtypeStruct((t, n), dt) for n, dt in zip(widths, dtypes)]
                  + [jax.ShapeDtypeStruct(w.shape, BF16) for w in later_weights],
        scratch_shapes=[pltpu.VMEM((1, LANES), F32),
                        pltpu.VMEM((tm, d), BF16),
                        pltpu.VMEM((d, gla_w), BF16),
                        pltpu.VMEM((d, fox_w), BF16),
                        pltpu.VMEM((d, LANES), BF16)],
        compiler_params=pltpu.CompilerParams(dimension_semantics=("arbitrary",), vmem_limit_bytes=VMEM_LIMIT),
        name="inproj",
    )(x2d, g, w_in, bf_pad, qg2, kg2, *later_weights)
    return outs[:len(widths)], outs[len(widths):]


def _gla_stages(q_ref, k_ref, v_ref, og_ref, misc_ref, lrw_ref, lrb_ref, ong_ref, y_ref, st_scr):
    s_len = q_ref.shape[0]
    r = GLA_SUPER
    n_sb = s_len // r
    n_ch = r // GLA_CHUNK
    dkw = GLA_HEADS * GLA_DK
    row = lax.broadcasted_iota(jnp.int32, (r, r), 0)
    col = lax.broadcasted_iota(jnp.int32, (r, r), 1)
    causal_bd = ((row // GLA_CHUNK) == (col // GLA_CHUNK)) & (col <= row)
    tri_bd = causal_bd.astype(BF16)
    head_of_col = lax.broadcasted_iota(jnp.int32, (r, dkw), 1) // GLA_DK
    heads = range(GLA_HEADS)
    sb_rows = lambda n: slice(n * r, (n + 1) * r)
    ch_rows = lambda c: slice(c * GLA_CHUNK, (c + 1) * GLA_CHUNK)
    v_cols = lambda h: slice(h * GLA_DV, (h + 1) * GLA_DV)

    la_pieces = []
    for n in range(n_sb):
        z = _dot(misc_ref[sb_rows(n), :].astype(BF16), lrw_ref[...]) + lrb_ref[...]
        la_pieces.append(_split_bf16(_log_sigmoid(z) * GLA_INV_TAU, 2))
        yield

    qh, ke, kd, decay = [], [], [], []
    for n in range(n_sb):
        b = _dot(tri_bd, la_pieces[n][0]) + _dot(tri_bd, la_pieces[n][1])
        b_last = [b[(c + 1) * GLA_CHUNK - 1:(c + 1) * GLA_CHUNK, :] for c in range(n_ch)]
        b_tot = jnp.concatenate([jnp.broadcast_to(bl, (GLA_CHUNK, dkw)) for bl in b_last], axis=0)
        q = q_ref[sb_rows(n), :].astype(F32)
        k = k_ref[sb_rows(n), :].astype(F32)
        qe = (q * (GLA_DK ** -0.5)) * jnp.exp(b)
        qh.append([jnp.where(head_of_col == h, qe, 0.0).astype(BF16) for h in heads])
        ke.append((k * jnp.exp(-b)).astype(BF16))
        kd_n = k * jnp.exp(b_tot - b)
        kd.append([jnp.where(head_of_col == h, kd_n, 0.0).astype(BF16) for h in heads])
        decay.append([jnp.exp(bl) for bl in b_last])
        yield

    a = []
    for n in range(n_sb):
        a.append([jnp.where(causal_bd, _dot_nt(qh[n][h], ke[n]), 0.0).astype(BF16) for h in heads])
        yield

    o_intra = []
    for n in range(n_sb):
        o_intra.append([_dot(a[n][h], v_ref[sb_rows(n), v_cols(h)]) for h in heads])
        yield

    ds = []
    for n in range(n_sb):
        for c in range(n_ch):
            rows = slice(n * r + c * GLA_CHUNK, n * r + (c + 1) * GLA_CHUNK)
            v_stack = jnp.concatenate([v_ref[rows, v_cols(h)] for h in heads], axis=0)
            kd_stack = jnp.concatenate([kd[n][h][ch_rows(c)] for h in heads], axis=0)
            ds.append(_dot_tn(v_stack, kd_stack))
        yield

    st = st_scr[...]
    o_inter = []
    for n in range(n_sb):
        for c in range(n_ch):
            qe_stack = jnp.concatenate([qh[n][h][ch_rows(c)] for h in heads], axis=0)
            o_inter.append(_dot_nt(qe_stack, st.astype(BF16)))
            st = st * decay[n][c] + ds[n * n_ch + c]
        yield
    st_scr[...] = st

    for n in range(n_sb):
        og = og_ref[sb_rows(n), :].astype(F32)
        gate = og * jax.nn.sigmoid(og)
        for h in heads:
            inter = jnp.concatenate([o_inter[n * n_ch + c][ch_rows(h)] for c in range(n_ch)], axis=0)
            o = o_intra[n][h] + inter
            y_ref[sb_rows(n), v_cols(h)] = (_rms(o, ong_ref[...]) * gate[:, v_cols(h)]).astype(y_ref.dtype)
        yield


def _fox_stages(qa_ref, ka_ref, v_ref, y_ref, vt_scr, s_scr):
    s_len = v_ref.shape[0]
    t = FOX_TILE
    n_tiles = s_len // t
    krow = lax.broadcasted_iota(jnp.int32, (t, t), 0)
    qcol = lax.broadcasted_iota(jnp.int32, (t, t), 1)
    orow = lax.broadcasted_iota(jnp.int32, (LANES, t), 0)
    vt_scr[...] = v_ref[...].astype(F32).T.astype(BF16)

    units = [(i, e) for i in range(n_tiles) for e in range(2)]
    n_slots = s_scr.shape[0]

    def scores(n):
        i, e = units[n]
        hs = slice(e * LANES, (e + 1) * LANES)
        qa = qa_ref[i * t:(i + 1) * t, hs]
        mp = None
        for kb in range(i + 1):
            sc = _dot_nt(ka_ref[kb * t:(kb + 1) * t, hs], qa)
            if kb == i:
                sc = jnp.where(krow <= qcol, sc, -jnp.inf)
            s_scr[n % n_slots, kb] = sc
            bm = jnp.max(sc.reshape(t // 8, 8, t), axis=0)
            mp = bm if mp is None else jnp.maximum(mp, bm)
        return jnp.max(mp, axis=0, keepdims=True)

    def weighted_values(n, m):
        lp = jnp.zeros((8, t), F32)
        acc = jnp.zeros((LANES, t), F32)
        for kb in range(units[n][0] + 1):
            p = jnp.exp2(s_scr[n % n_slots, kb] - m)
            lp = lp + jnp.sum(p.reshape(t // 8, 8, t), axis=0)
            acc = acc + _dot(vt_scr[:, kb * t:(kb + 1) * t], p.astype(BF16))
        return acc / jnp.sum(lp, axis=0, keepdims=True)

    ahead = n_slots - 1
    col_max = {}
    for n in range(min(ahead, len(units))):
        col_max[n] = scores(n)
        yield
    o_even = None
    for n, (i, e) in enumerate(units):
        if n + ahead < len(units):
            col_max[n + ahead] = scores(n + ahead)
            yield
        o_t = weighted_values(n, col_max.pop(n))
        if e == 0:
            o_even = o_t
        else:
            o_pair = jnp.where(orow < FOX_DH, o_even, o_t)
            y_ref[i * t:(i + 1) * t, :] = o_pair.T.astype(y_ref.dtype)
        yield


def _mixers_kernel(qa_ref, ka_ref, fv_ref, gq_ref, gk_ref, gv_ref, gog_ref, misc_ref, lrw_ref, lrb_ref, ong_ref,
                   yf_ref, yg_ref, vt_scr, s_scr, st_scr):
    @pl.when(pl.program_id(1) == 0)
    def _():
        st_scr[...] = jnp.zeros_like(st_scr)

    fox = _fox_stages(qa_ref, ka_ref, fv_ref, yf_ref, vt_scr, s_scr)
    gla = _gla_stages(gq_ref, gk_ref, gv_ref, gog_ref, misc_ref, lrw_ref, lrb_ref, ong_ref, yg_ref, st_scr)
    rota = (fox, fox, gla)
    finished = set()
    turn = 0
    while len(finished) < 2:
        gen = rota[turn % len(rota)]
        turn += 1
        if gen not in finished:
            try:
                next(gen)
            except StopIteration:
                finished.add(gen)


def _mixers(fqa, fka, fv, gq, gk, gv, gog, misc, lrw_pad, lrb, ong, b, s):
    n_pairs = FOX_HEADS * FOX_DH // LANES
    t = FOX_TILE
    dkw, dvw = GLA_HEADS * GLA_DK, GLA_HEADS * GLA_DV
    sq = s // n_pairs
    heads2 = pl.BlockSpec((s, 2 * LANES), lambda i, j: (i, j))
    pair = pl.BlockSpec((s, LANES), lambda i, j: (i, j))
    quarter = lambda w: pl.BlockSpec((sq, w), lambda i, j: (i * n_pairs + j, 0))
    whole = lambda a: pl.BlockSpec(a.shape, lambda i, j: (0, 0))
    return pl.pallas_call(
        _mixers_kernel,
        grid=(b, n_pairs),
        in_specs=[heads2, heads2, pair, quarter(dkw), quarter(dkw), quarter(dvw), quarter(dvw), quarter(LANES),
                  whole(lrw_pad), whole(lrb), whole(ong)],
        out_specs=[pair, quarter(dvw)],
        out_shape=[jax.ShapeDtypeStruct((b * s, FOX_HEADS * FOX_DH), BF16),
                   jax.ShapeDtypeStruct((b * s, dvw), BF16)],
        scratch_shapes=[
            pltpu.VMEM((LANES, s), BF16),
            pltpu.VMEM((FOX_SLOTS, s // t, t, t), F32),
            pltpu.VMEM((GLA_DV, dkw), F32),
        ],
        compiler_params=pltpu.CompilerParams(dimension_semantics=("arbitrary", "arbitrary"),
                                             vmem_limit_bytes=VMEM_LIMIT),
        name="mixers",
    )(fqa, fka, fv, gq, gk, gv, gog, misc, lrw_pad, lrb, ong)


def _post_kernel(x_ref, yg_ref, yf_ref, p_ref, wo_ref, g2_ref, wup_ref, cw_ref, cb_ref, wdn_ref,
                 g3_ref, wpg_ref, bpg_ref, wpe_ref, gpe_ref, o_ref,
                 x1_scr, h2_scr, u_scr, act_scr, carry_scr, acc_scr):
    tm = x_ref.shape[0]
    d_ff = wdn_ref.shape[0]
    n_chunks = d_ff // FFN_CHUNK
    n_slabs = 2 * FFN_CHUNK // LANES
    half = n_slabs // 2
    dvw = yg_ref.shape[1]
    first_tile = pl.program_id(1) == 0

    @pl.when(first_tile)
    def _():
        carry_scr[...] = jnp.zeros_like(carry_scr)

    o_ref[...] = _rms(_dot(p_ref[...].astype(BF16), wpe_ref[...]), gpe_ref[...])

    x1 = x_ref[...] + _dot(yg_ref[...], wo_ref[:dvw, :]) + _dot(yf_ref[...], wo_ref[dvw:, :])
    x1_scr[...] = x1
    h2_scr[...] = _rms(x1, g2_ref[...]).astype(BF16)
    acc_scr[...] = jnp.zeros_like(acc_scr)

    for j in range(n_chunks + DOWN_LAG):
        if j < n_chunks:
            ub = u_scr.at[j % 2]
            col0 = [(c // half) * d_ff + j * FFN_CHUNK + (c % half) * LANES for c in range(n_slabs)]
            h2 = h2_scr[...]
            for part in range(2):
                u = _dot(h2, wup_ref[:, col0[part * half]:col0[part * half] + FFN_CHUNK])
                for c in range(part * half, (part + 1) * half):
                    ub[c, :CARRY_ROWS, :] = carry_scr[j, c]
                    ub[c, CARRY_ROWS:, :] = u[:, (c % half) * LANES:(c % half + 1) * LANES]
                    carry_scr[j, c] = ub[c, tm:, :]
        if j >= DOWN_LAG:
            jd = j - DOWN_LAG
            acc_scr[...] += _dot(act_scr[jd % (DOWN_LAG + 1)], wdn_ref[jd * FFN_CHUNK:(jd + 1) * FFN_CHUNK, :])
        if j < n_chunks:
            def conv(c):
                cs = slice(col0[c], col0[c] + LANES)
                return (cb_ref[:, cs]
                        + ub[c, CARRY_ROWS - 2:CARRY_ROWS - 2 + tm, :] * cw_ref[0:1, cs]
                        + ub[c, CARRY_ROWS - 1:CARRY_ROWS - 1 + tm, :] * cw_ref[1:2, cs]
                        + ub[c, CARRY_ROWS:, :] * cw_ref[2:3, cs])
            for c in range(half):
                gate = conv(c)
                act = gate * jax.nn.sigmoid(gate) * conv(c + half)
                act_scr[j % (DOWN_LAG + 1), :, c * LANES:(c + 1) * LANES] = act.astype(BF16)

    x2 = x1_scr[...] + acc_scr[...]
    gate = jax.nn.sigmoid(_dot(_rms(x2, g3_ref[...]).astype(BF16), wpg_ref[...]) + bpg_ref[...])
    o_ref[...] = x2 + gate * o_ref[...]


def _post(x2d, yg, yf, p2d, wo, g2, wup, cw, cb, wdn, g3, wpg, bpg, wpe, gpe, b, s, tm):
    d = x2d.shape[1]
    nt = s // tm
    tile = lambda w: pl.BlockSpec((tm, w), lambda i, j: (i * nt + j, 0))
    resident = lambda a: pl.BlockSpec(a.shape, lambda i, j: (0,) * a.ndim, pipeline_mode=pl.Buffered(1))
    n_chunks = wdn.shape[0] // FFN_CHUNK
    n_slabs = 2 * FFN_CHUNK // LANES
    return pl.pallas_call(
        _post_kernel,
        grid=(b, nt),
        in_specs=[tile(d), tile(yg.shape[1]), tile(yf.shape[1]), tile(p2d.shape[1])]
                 + [resident(a) for a in (wo, g2, wup, cw, cb, wdn, g3, wpg, bpg, wpe, gpe)],
        out_specs=tile(d),
        out_shape=jax.ShapeDtypeStruct(x2d.shape, F32),
        scratch_shapes=[
            pltpu.VMEM((tm, d), F32),
            pltpu.VMEM((tm, d), BF16),
            pltpu.VMEM((2, n_slabs, CARRY_ROWS + tm, LANES), F32),
            pltpu.VMEM((DOWN_LAG + 1, tm, FFN_CHUNK), BF16),
            pltpu.VMEM((n_chunks, n_slabs, CARRY_ROWS, LANES), F32),
            pltpu.VMEM((tm, d), F32),
        ],
        compiler_params=pltpu.CompilerParams(dimension_semantics=("arbitrary", "arbitrary"),
                                             vmem_limit_bytes=VMEM_LIMIT),
        name="post",
    )(x2d, yg, yf, p2d, wo, g2, wup, cw, cb, wdn, g3, wpg, bpg, wpe, gpe)


def _layer(x2d, p2d, b, s, norm1_g, w_in, lr_w, lr_b, onorm_g, b_f, qn_g, kn_g, w_o, norm2_g, w_up,
           conv_w, conv_b, w_down, norm3_g, w_pe, pe_norm_g, w_pg, b_pg):
    gqk = GLA_HEADS * GLA_DK
    row = lambda a: a.reshape(1, -1).astype(F32)
    bf_pad = jnp.zeros((1, LANES), F32).at[0, MISC_F0:MISC_F0 + FOX_HEADS].set(b_f)
    qg2 = row(jnp.tile(qn_g, 2)) * (FOX_DH ** -0.5 * LOG2E)
    (gq, gk, gv, gog, fqa, fka, fv, misc), (wo_b, wup_b, wdn_b, wpg_b, wpe_b) = _inproj(
        x2d, row(norm1_g), w_in.astype(F32), bf_pad, qg2, row(jnp.tile(kn_g, 2)),
        (w_o, w_up, w_down, w_pg, w_pe), s, tm=512)

    lrw_pad = jnp.zeros((LANES, gqk), F32).at[MISC_LR0:MISC_LR0 + GLA_LOWRANK].set(lr_w).astype(BF16)
    y_fox, y_gla = _mixers(fqa, fka, fv, gq, gk, gv, gog, misc, lrw_pad, row(lr_b), row(onorm_g), b, s)

    return _post(
        x2d, y_gla, y_fox, p2d, wo_b, row(norm2_g), wup_b, conv_w.astype(F32), row(conv_b), wdn_b,
        row(norm3_g), wpg_b, row(b_pg), wpe_b, row(pe_norm_g), b, s, tm=512)


def kernel(x, p, norm1_g, w_in, gla_lr_w, gla_lr_b, gla_onorm_g, fox_b_f, fox_qnorm_g, fox_knorm_g, w_o, norm2_g, w_up, conv_w, conv_b, w_down, norm3_g, w_pe, pe_norm_g, w_pg, b_pg):
    b, s, d = x.shape
    x2d = x.reshape(b * s, d)
    for i in range(p.shape[0]):
        x2d = _layer(x2d, p[i].reshape(b * s, -1), b, s, norm1_g[i], w_in[i], gla_lr_w[i], gla_lr_b[i],
                     gla_onorm_g[i], fox_b_f[i], fox_qnorm_g[i], fox_knorm_g[i], w_o[i], norm2_g[i], w_up[i],
                     conv_w[i], conv_b[i], w_down[i], norm3_g[i], w_pe[i], pe_norm_g[i], w_pg[i], b_pg[i])
    return x2d.reshape(b, s, d)
```

```python
import functools

import jax
import jax.numpy as jnp
from jax import lax
from jax.experimental import pallas as pl
from jax.experimental.pallas import tpu as pltpu

F32 = jnp.float32
BF16 = jnp.bfloat16

EPS = 1e-6
LOG2E = 1.4426950408889634
LANES = 128
BF16_SUBLANES = 16
GLA_HEADS, GLA_DK, GLA_DV = 4, 64, 128
GLA_LOWRANK = 16
GLA_INV_TAU = 1.0 / 16.0
GLA_CHUNK = 64
GLA_SUPER = 256
FOX_HEADS, FOX_DH = 8, 64
FOX_TILE = 256
FOX_SLOTS = 3
CONV_W = 3
FFN_CHUNK = 256
CARRY_ROWS = 8
DOWN_LAG = 2
MISC_F0 = 0
MISC_LR0 = 8
VMEM_LIMIT = 56 * 1024 * 1024


def _dot(a, b):
    return jnp.dot(a, b, preferred_element_type=F32)


def _dot_nt(a, b):
    return lax.dot_general(a, b, (((1,), (1,)), ((), ())), preferred_element_type=F32)


def _dot_tn(a, b):
    return lax.dot_general(a, b, (((0,), (0,)), ((), ())), preferred_element_type=F32)


def _log_sigmoid(z):
    return jnp.minimum(z, 0.0) - jnp.log(1.0 + jnp.exp(-jnp.abs(z)))


def _split_bf16(x, pieces):
    out = []
    for _ in range(pieces):
        p = x.astype(BF16)
        out.append(p)
        x = x - p.astype(F32)
    return out


def _rms(x, gain):
    ms = jnp.mean(x * x, axis=-1, keepdims=True)
    return x * lax.rsqrt(ms + EPS) * gain


def _inproj_kernel(x_ref, g_ref, win_ref, bf_ref, qg_ref, kg_ref, *rest, tiles_per_seq, n_cast):
    cast_in, rest = rest[:n_cast], rest[n_cast:]
    gq_ref, gk_ref, gv_ref, gog_ref, fqa_ref, fka_ref, fv_ref, misc_ref = rest[:8]
    cast_out = rest[8:8 + n_cast]
    carry_scr, h_scr, wg_ref, wf_ref, wm_ref = rest[8 + n_cast:]

    @pl.when(pl.program_id(0) == 0)
    def _():
        gla_w = wg_ref.shape[1]
        fox_w = wf_ref.shape[1]
        lr0 = gla_w
        fox0 = lr0 + GLA_LOWRANK
        f0 = fox0 + fox_w
        wg_ref[...] = win_ref[:gla_w, :].T.astype(BF16)
        wf_ref[...] = win_ref[fox0:f0, :].T.astype(BF16)
        pad = jnp.zeros((LANES - MISC_LR0 - GLA_LOWRANK, win_ref.shape[1]), F32)
        misc_t = jnp.concatenate([win_ref[f0:f0 + FOX_HEADS, :], win_ref[lr0:fox0, :], pad], axis=0)
        wm_ref[...] = misc_t.T.astype(BF16)

    _inproj_body(x_ref, g_ref, wg_ref, wf_ref, wm_ref, bf_ref, qg_ref, kg_ref,
                 gq_ref, gk_ref, gv_ref, gog_ref, fqa_ref, fka_ref, fv_ref, misc_ref, carry_scr, h_scr, tiles_per_seq)
    for w_ref, o_ref in zip(cast_in, cast_out):
        o_ref[...] = w_ref[...].astype(BF16)


def _inproj_body(x_ref, g_ref, wg_ref, wf_ref, wm_ref, bf_ref, qg_ref, kg_ref,
                 gq_ref, gk_ref, gv_ref, gog_ref, fqa_ref, fka_ref, fv_ref, misc_ref, carry_scr, h_scr,
                 tiles_per_seq):
    tm = x_ref.shape[0]
    fw = FOX_HEADS * FOX_DH

    @pl.when(pl.program_id(0) % tiles_per_seq == 0)
    def _():
        carry_scr[...] = jnp.zeros_like(carry_scr)

    h_scr[...] = _rms(x_ref[...], g_ref[...]).astype(BF16)
    h = h_scr[...]
    misc = _dot(h, wm_ref[...])
    misc_ref[...] = misc
    q_all = _dot(h, wf_ref[:, :fw])
    k_all = _dot(h, wf_ref[:, fw:2 * fw])

    blk = FOX_TILE
    trow = lax.broadcasted_iota(jnp.int32, (blk, blk), 0)
    tcol = lax.broadcasted_iota(jnp.int32, (blk, blk), 1)
    tri = (tcol <= trow).astype(BF16)
    carry = carry_scr[...]
    c_blocks = []
    for r0 in range(0, tm, blk):
        cb = carry
        for piece in _split_bf16(_log_sigmoid(misc[r0:r0 + blk] + bf_ref[...]), 3):
            cb = cb + _dot(tri, piece)
        c_blocks.append(cb)
        carry = cb[blk - 1:blk]
    carry_scr[...] = carry
    c_all = jnp.concatenate(c_blocks, axis=0)

    lane = lax.broadcasted_iota(jnp.int32, (tm, LANES), 1)
    lower = lane < FOX_DH

    def pair_norm(x, g):
        sq = x * x
        ms_lo = jnp.sum(jnp.where(lower, sq, 0.0), axis=-1, keepdims=True)
        ms_hi = jnp.sum(jnp.where(lower, 0.0, sq), axis=-1, keepdims=True)
        ms = jnp.where(lower, ms_lo, ms_hi) * (1.0 / FOX_DH)
        return x * lax.rsqrt(ms + EPS) * g

    fv_ref[...] = _dot(h, wf_ref[:, 2 * fw:]).astype(fv_ref.dtype)
    off = 0
    for ref in (gq_ref, gk_ref, gv_ref, gog_ref):
        n = ref.shape[-1]
        ref[...] = _dot(h, wg_ref[:, off:off + n]).astype(ref.dtype)
        off += n

    for pair in range(fw // LANES):
        ps = slice(pair * LANES, (pair + 1) * LANES)
        qn = pair_norm(q_all[:, ps], qg_ref[...])
        kn = pair_norm(k_all[:, ps], kg_ref[...])
        for e in range(2):
            head = 2 * pair + e
            in_head = (lane >= FOX_DH) if e else lower
            piece0 = 0 if e else FOX_DH
            c_col = jnp.sum(jnp.where(lane == MISC_F0 + head, c_all, 0.0), axis=-1, keepdims=True)
            is_piece = (lane >= piece0) & (lane < piece0 + 3)
            hs = slice(head * LANES, (head + 1) * LANES)
            fqa_ref[:, hs] = jnp.where(in_head, qn, jnp.where(is_piece, 1.0, 0.0)).astype(BF16)
            ka = jnp.where(in_head, kn, 0.0).astype(BF16)
            for j, p in enumerate(_split_bf16(c_col * (-LOG2E), 3)):
                ka = jnp.where(lane == piece0 + j, p, ka)
            fka_ref[:, hs] = ka


def _cast_slab(rows, steps):
    for n_slabs in range(min(steps, rows // BF16_SUBLANES), 0, -1):
        if rows % n_slabs == 0 and (rows // n_slabs) % BF16_SUBLANES == 0:
            return rows // n_slabs
    raise ValueError(f"no bf16-tile aligned slab for {rows} rows")


def _inproj(x2d, g, w_in, bf_pad, qg2, kg2, later_weights, s, tm):
    t, d = x2d.shape
    steps = t // tm
    gla_w = 2 * GLA_HEADS * (GLA_DK + GLA_DV)
    fox_w = 3 * FOX_HEADS * FOX_DH
    assert w_in.shape == (1, gla_w + GLA_LOWRANK + fox_w + FOX_HEADS, d)
    assert MISC_F0 == 0 and MISC_LR0 == FOX_HEADS
    widths = (256, 256, 512, 512, FOX_HEADS * LANES, FOX_HEADS * LANES, 512, LANES)
    dtypes = (BF16,) * 7 + (F32,)
    whole = lambda a: pl.BlockSpec(a.shape, lambda i: (0, 0))
    once = lambda a: pl.BlockSpec((None,) + a.shape[1:], lambda i: (0, 0, 0), pipeline_mode=pl.Buffered(1))

    def slab(w):
        rows = _cast_slab(w.shape[0], steps)
        last = w.shape[0] // rows - 1
        return pl.BlockSpec((rows, w.shape[1]), lambda i: (jnp.minimum(i, last), 0))

    outs = pl.pallas_call(
        functools.partial(_inproj_kernel, tiles_per_seq=s // tm, n_cast=len(later_weights)),
        grid=(steps,),
        in_specs=[pl.BlockSpec((tm, d), lambda i: (i, 0)), whole(g), once(w_in)]
                 + [whole(a) for a in (bf_pad, qg2, kg2)]
                 + [slab(w) for w in later_weights],
        out_specs=[pl.BlockSpec((tm, n), lambda i: (i, 0)) for n in widths] + [slab(w) for w in later_weights],
        out_shape=[jax.ShapeDtypeStruct((t, n), dt) for n, dt in zip(widths, dtypes)]
                  + [jax.ShapeDtypeStruct(w.shape, BF16) for w in later_weights],
        scratch_shapes=[pltpu.VMEM((1, LANES), F32),
                        pltpu.VMEM((tm, d), BF16),
                        pltpu.VMEM((d, gla_w), BF16),
                        pltpu.VMEM((d, fox_w), BF16),
                        pltpu.VMEM((d, LANES), BF16)],
        compiler_params=pltpu.CompilerParams(dimension_semantics=("arbitrary",), vmem_limit_bytes=VMEM_LIMIT),
        name="inproj",
    )(x2d, g, w_in, bf_pad, qg2, kg2, *later_weights)
    return outs[:len(widths)], outs[len(widths):]


def _gla_stages(q_ref, k_ref, v_ref, og_ref, misc_ref, lrw_ref, lrb_ref, ong_ref, y_ref, st_scr):
    s_len = q_ref.shape[0]
    r = GLA_SUPER
    n_sb = s_len // r
    n_ch = r // GLA_CHUNK
    dkw = GLA_HEADS * GLA_DK
    row = lax.broadcasted_iota(jnp.int32, (r, r), 0)
    col = lax.broadcasted_iota(jnp.int32, (r, r), 1)
    causal_bd = ((row // GLA_CHUNK) == (col // GLA_CHUNK)) & (col <= row)
    tri_bd = causal_bd.astype(BF16)
    head_of_col = lax.broadcasted_iota(jnp.int32, (r, dkw), 1) // GLA_DK
    heads = range(GLA_HEADS)
    sb_rows = lambda n: slice(n * r, (n + 1) * r)
    ch_rows = lambda c: slice(c * GLA_CHUNK, (c + 1) * GLA_CHUNK)
    v_cols = lambda h: slice(h * GLA_DV, (h + 1) * GLA_DV)

    la_pieces = []
    for n in range(n_sb):
        z = _dot(misc_ref[sb_rows(n), :].astype(BF16), lrw_ref[...]) + lrb_ref[...]
        la_pieces.append(_split_bf16(_log_sigmoid(z) * GLA_INV_TAU, 2))
        yield

    qh, ke, kd, decay = [], [], [], []
    for n in range(n_sb):
        b = _dot(tri_bd, la_pieces[n][0]) + _dot(tri_bd, la_pieces[n][1])
        b_last = [b[(c + 1) * GLA_CHUNK - 1:(c + 1) * GLA_CHUNK, :] for c in range(n_ch)]
        b_tot = jnp.concatenate([jnp.broadcast_to(bl, (GLA_CHUNK, dkw)) for bl in b_last], axis=0)
        q = q_ref[sb_rows(n), :].astype(F32)
        k = k_ref[sb_rows(n), :].astype(F32)
        qe = (q * (GLA_DK ** -0.5)) * jnp.exp(b)
        qh.append([jnp.where(head_of_col == h, qe, 0.0).astype(BF16) for h in heads])
        ke.append((k * jnp.exp(-b)).astype(BF16))
        kd_n = k * jnp.exp(b_tot - b)
        kd.append([jnp.where(head_of_col == h, kd_n, 0.0).astype(BF16) for h in heads])
        decay.append([jnp.exp(bl) for bl in b_last])
        yield

    a = []
    for n in range(n_sb):
        a.append([jnp.where(causal_bd, _dot_nt(qh[n][h], ke[n]), 0.0).astype(BF16) for h in heads])
        yield

    o_intra = []
    for n in range(n_sb):
        o_intra.append([_dot(a[n][h], v_ref[sb_rows(n), v_cols(h)]) for h in heads])
        yield

    ds = []
    for n in range(n_sb):
        for c in range(n_ch):
            rows = slice(n * r + c * GLA_CHUNK, n * r + (c + 1) * GLA_CHUNK)
            v_stack = jnp.concatenate([v_ref[rows, v_cols(h)] for h in heads], axis=0)
            kd_stack = jnp.concatenate([kd[n][h][ch_rows(c)] for h in heads], axis=0)
            ds.append(_dot_tn(v_stack, kd_stack))
        yield

    st = st_scr[...]
    o_inter = []
    for n in range(n_sb):
        for c in range(n_ch):
            qe_stack = jnp.concatenate([qh[n][h][ch_rows(c)] for h in heads], axis=0)
            o_inter.append(_dot_nt(qe_stack, st.astype(BF16)))
            st = st * decay[n][c] + ds[n * n_ch + c]
        yield
    st_scr[...] = st

    for n in range(n_sb):
        og = og_ref[sb_rows(n), :].astype(F32)
        gate = og * jax.nn.sigmoid(og)
        for h in heads:
            inter = jnp.concatenate([o_inter[n * n_ch + c][ch_rows(h)] for c in range(n_ch)], axis=0)
            o = o_intra[n][h] + inter
            y_ref[sb_rows(n), v_cols(h)] = (_rms(o, ong_ref[...]) * gate[:, v_cols(h)]).astype(y_ref.dtype)
        yield


def _fox_stages(qa_ref, ka_ref, v_ref, y_ref, vt_scr, s_scr):
    s_len = v_ref.shape[0]
    t = FOX_TILE
    n_tiles = s_len // t
    krow = lax.broadcasted_iota(jnp.int32, (t, t), 0)
    qcol = lax.broadcasted_iota(jnp.int32, (t, t), 1)
    orow = lax.broadcasted_iota(jnp.int32, (LANES, t), 0)
    vt_scr[...] = v_ref[...].astype(F32).T.astype(BF16)

    units = [(i, e) for i in range(n_tiles) for e in range(2)]
    n_slots = s_scr.shape[0]

    def scores(n):
        i, e = units[n]
        hs = slice(e * LANES, (e + 1) * LANES)
        qa = qa_ref[i * t:(i + 1) * t, hs]
        mp = None
        for kb in range(i + 1):
            sc = _dot_nt(ka_ref[kb * t:(kb + 1) * t, hs], qa)
            if kb == i:
                sc = jnp.where(krow <= qcol, sc, -jnp.inf)
            s_scr[n % n_slots, kb] = sc
            bm = jnp.max(sc.reshape(t // 8, 8, t), axis=0)
            mp = bm if mp is None else jnp.maximum(mp, bm)
        return jnp.max(mp, axis=0, keepdims=True)

    def weighted_values(n, m):
        lp = jnp.zeros((8, t), F32)
        acc = jnp.zeros((LANES, t), F32)
        for kb in range(units[n][0] + 1):
            p = jnp.exp2(s_scr[n % n_slots, kb] - m)
            lp = lp + jnp.sum(p.reshape(t // 8, 8, t), axis=0)
            acc = acc + _dot(vt_scr[:, kb * t:(kb + 1) * t], p.astype(BF16))
        return acc / jnp.sum(lp, axis=0, keepdims=True)

    ahead = n_slots - 1
    col_max = {}
    for n in range(min(ahead, len(units))):
        col_max[n] = scores(n)
        yield
    o_even = None
    for n, (i, e) in enumerate(units):
        if n + ahead < len(units):
            col_max[n + ahead] = scores(n + ahead)
            yield
        o_t = weighted_values(n, col_max.pop(n))
        if e == 0:
            o_even = o_t
        else:
            o_pair = jnp.where(orow < FOX_DH, o_even, o_t)
            y_ref[i * t:(i + 1) * t, :] = o_pair.T.astype(y_ref.dtype)
        yield


def _mixers_kernel(qa_ref, ka_ref, fv_ref, gq_ref, gk_ref, gv_ref, gog_ref, misc_ref, lrw_ref, lrb_ref, ong_ref,
                   yf_ref, yg_ref, vt_scr, s_scr, st_scr):
    @pl.when(pl.program_id(1) == 0)
    def _():
        st_scr[...] = jnp.zeros_like(st_scr)

    fox = _fox_stages(qa_ref, ka_ref, fv_ref, yf_ref, vt_scr, s_scr)
    gla = _gla_stages(gq_ref, gk_ref, gv_ref, gog_ref, misc_ref, lrw_ref, lrb_ref, ong_ref, yg_ref, st_scr)
    rota = (fox, fox, gla)
    finished = set()
    turn = 0
    while len(finished) < 2:
        gen = rota[turn % len(rota)]
        turn += 1
        if gen not in finished:
            try:
                next(gen)
            except StopIteration:
                finished.add(gen)


def _mixers(fqa, fka, fv, gq, gk, gv, gog, misc, lrw_pad, lrb, ong, b, s):
    n_pairs = FOX_HEADS * FOX_DH // LANES
    t = FOX_TILE
    dkw, dvw = GLA_HEADS * GLA_DK, GLA_HEADS * GLA_DV
    sq = s // n_pairs
    heads2 = pl.BlockSpec((s, 2 * LANES), lambda i, j: (i, j))
    pair = pl.BlockSpec((s, LANES), lambda i, j: (i, j))
    quarter = lambda w: pl.BlockSpec((sq, w), lambda i, j: (i * n_pairs + j, 0))
    whole = lambda a: pl.BlockSpec(a.shape, lambda i, j: (0, 0))
    return pl.pallas_call(
        _mixers_kernel,
        grid=(b, n_pairs),
        in_specs=[heads2, heads2, pair, quarter(dkw), quarter(dkw), quarter(dvw), quarter(dvw), quarter(LANES),
                  whole(lrw_pad), whole(lrb), whole(ong)],
        out_specs=[pair, quarter(dvw)],
        out_shape=[jax.ShapeDtypeStruct((b * s, FOX_HEADS * FOX_DH), BF16),
                   jax.ShapeDtypeStruct((b * s, dvw), BF16)],
        scratch_shapes=[
            pltpu.VMEM((LANES, s), BF16),
            pltpu.VMEM((FOX_SLOTS, s // t, t, t), F32),
            pltpu.VMEM((GLA_DV, dkw), F32),
        ],
        compiler_params=pltpu.CompilerParams(dimension_semantics=("arbitrary", "arbitrary"),
                                             vmem_limit_bytes=VMEM_LIMIT),
        name="mixers",
    )(fqa, fka, fv, gq, gk, gv, gog, misc, lrw_pad, lrb, ong)


def _post_kernel(x_ref, yg_ref, yf_ref, p_ref, wo_ref, g2_ref, wup_ref, cw_ref, cb_ref, wdn_ref,
                 g3_ref, wpg_ref, bpg_ref, wpe_ref, gpe_ref, o_ref,
                 x1_scr, h2_scr, u_scr, act_scr, carry_scr, acc_scr):
    tm = x_ref.shape[0]
    d_ff = wdn_ref.shape[0]
    n_chunks = d_ff // FFN_CHUNK
    n_slabs = 2 * FFN_CHUNK // LANES
    half = n_slabs // 2
    dvw = yg_ref.shape[1]
    first_tile = pl.program_id(1) == 0

    @pl.when(first_tile)
    def _():
        carry_scr[...] = jnp.zeros_like(carry_scr)

    o_ref[...] = _rms(_dot(p_ref[...].astype(BF16), wpe_ref[...]), gpe_ref[...])

    x1 = x_ref[...] + _dot(yg_ref[...], wo_ref[:dvw, :]) + _dot(yf_ref[...], wo_ref[dvw:, :])
    x1_scr[...] = x1
    h2_scr[...] = _rms(x1, g2_ref[...]).astype(BF16)
    acc_scr[...] = jnp.zeros_like(acc_scr)

    for j in range(n_chunks + DOWN_LAG):
        if j < n_chunks:
            ub = u_scr.at[j % 2]
            col0 = [(c // half) * d_ff + j * FFN_CHUNK + (c % half) * LANES for c in range(n_slabs)]
            h2 = h2_scr[...]
            for part in range(2):
                u = _dot(h2, wup_ref[:, col0[part * half]:col0[part * half] + FFN_CHUNK])
                for c in range(part * half, (part + 1) * half):
                    ub[c, :CARRY_ROWS, :] = carry_scr[j, c]
                    ub[c, CARRY_ROWS:, :] = u[:, (c % half) * LANES:(c % half + 1) * LANES]
                    carry_scr[j, c] = ub[c, tm:, :]
        if j >= DOWN_LAG:
            jd = j - DOWN_LAG
            acc_scr[...] += _dot(act_scr[jd % (DOWN_LAG + 1)], wdn_ref[jd * FFN_CHUNK:(jd + 1) * FFN_CHUNK, :])
        if j < n_chunks:
            def conv(c):
                cs = slice(col0[c], col0[c] + LANES)
                return (cb_ref[:, cs]
                        + ub[c, CARRY_ROWS - 2:CARRY_ROWS - 2 + tm, :] * cw_ref[0:1, cs]
                        + ub[c, CARRY_ROWS - 1:CARRY_ROWS - 1 + tm, :] * cw_ref[1:2, cs]
                        + ub[c, CARRY_ROWS:, :] * cw_ref[2:3, cs])
            for c in range(half):
                gate = conv(c)
                act = gate * jax.nn.sigmoid(gate) * conv(c + half)
                act_scr[j % (DOWN_LAG + 1), :, c * LANES:(c + 1) * LANES] = act.astype(BF16)

    x2 = x1_scr[...] + acc_scr[...]
    gate = jax.nn.sigmoid(_dot(_rms(x2, g3_ref[...]).astype(BF16), wpg_ref[...]) + bpg_ref[...])
    o_ref[...] = x2 + gate * o_ref[...]


def _post(x2d, yg, yf, p2d, wo, g2, wup, cw, cb, wdn, g3, wpg, bpg, wpe, gpe, b, s, tm):
    d = x2d.shape[1]
    nt = s // tm
    tile = lambda w: pl.BlockSpec((tm, w), lambda i, j: (i * nt + j, 0))
    resident = lambda a: pl.BlockSpec(a.shape, lambda i, j: (0,) * a.ndim, pipeline_mode=pl.Buffered(1))
    n_chunks = wdn.shape[0] // FFN_CHUNK
    n_slabs = 2 * FFN_CHUNK // LANES
    return pl.pallas_call(
        _post_kernel,
        grid=(b, nt),
        in_specs=[tile(d), tile(yg.shape[1]), tile(yf.shape[1]), tile(p2d.shape[1])]
                 + [resident(a) for a in (wo, g2, wup, cw, cb, wdn, g3, wpg, bpg, wpe, gpe)],
        out_specs=tile(d),
        out_shape=jax.ShapeDtypeStruct(x2d.shape, F32),
        scratch_shapes=[
            pltpu.VMEM((tm, d), F32),
            pltpu.VMEM((tm, d), BF16),
            pltpu.VMEM((2, n_slabs, CARRY_ROWS + tm, LANES), F32),
            pltpu.VMEM((DOWN_LAG + 1, tm, FFN_CHUNK), BF16),
            pltpu.VMEM((n_chunks, n_slabs, CARRY_ROWS, LANES), F32),
            pltpu.VMEM((tm, d), F32),
        ],
        compiler_params=pltpu.CompilerParams(dimension_semantics=("arbitrary", "arbitrary"),
                                             vmem_limit_bytes=VMEM_LIMIT),
        name="post",
    )(x2d, yg, yf, p2d, wo, g2, wup, cw, cb, wdn, g3, wpg, bpg, wpe, gpe)


def _layer(x2d, p2d, b, s, norm1_g, w_in, lr_w, lr_b, onorm_g, b_f, qn_g, kn_g, w_o, norm2_g, w_up,
           conv_w, conv_b, w_down, norm3_g, w_pe, pe_norm_g, w_pg, b_pg):
    gqk = GLA_HEADS * GLA_DK
    row = lambda a: a.reshape(1, -1).astype(F32)
    bf_pad = jnp.zeros((1, LANES), F32).at[0, MISC_F0:MISC_F0 + FOX_HEADS].set(b_f)
    qg2 = row(jnp.tile(qn_g, 2)) * (FOX_DH ** -0.5 * LOG2E)
    (gq, gk, gv, gog, fqa, fka, fv, misc), (wo_b, wup_b, wdn_b, wpg_b, wpe_b) = _inproj(
        x2d, row(norm1_g), jnp.swapaxes(w_in, 1, 2).astype(F32), bf_pad, qg2, row(jnp.tile(kn_g, 2)),
        (w_o, w_up, w_down, w_pg, w_pe), s, tm=512)

    lrw_pad = jnp.zeros((LANES, gqk), F32).at[MISC_LR0:MISC_LR0 + GLA_LOWRANK].set(lr_w).astype(BF16)
    y_fox, y_gla = _mixers(fqa, fka, fv, gq, gk, gv, gog, misc, lrw_pad, row(lr_b), row(onorm_g), b, s)

    return _post(
        x2d, y_gla, y_fox, p2d, wo_b, row(norm2_g), wup_b, conv_w.astype(F32), row(conv_b), wdn_b,
        row(norm3_g), wpg_b, row(b_pg), wpe_b, row(pe_norm_g), b, s, tm=512)


def kernel(x, p, norm1_g, w_in, gla_lr_w, gla_lr_b, gla_onorm_g, fox_b_f, fox_qnorm_g, fox_knorm_g, w_o, norm2_g, w_up, conv_w, conv_b, w_down, norm3_g, w_pe, pe_norm_g, w_pg, b_pg):
    b, s, d = x.shape
    x2d = x.reshape(b * s, d)
    for i in range(p.shape[0]):
        x2d = _layer(x2d, p[i].reshape(b * s, -1), b, s, norm1_g[i], w_in[i:i + 1], gla_lr_w[i], gla_lr_b[i],
                     gla_onorm_g[i], fox_b_f[i], fox_qnorm_g[i], fox_knorm_g[i], w_o[i], norm2_g[i], w_up[i],
                     conv_w[i], conv_b[i], w_down[i], norm3_g[i], w_pe[i], pe_norm_g[i], w_pg[i], b_pg[i])
    return x2d.reshape(b, s, d)
```

```python
import functools

import jax
import jax.numpy as jnp
from jax import lax
from jax.experimental import pallas as pl
from jax.experimental.pallas import tpu as pltpu

F32 = jnp.float32
BF16 = jnp.bfloat16

EPS = 1e-6
LOG2E = 1.4426950408889634
LANES = 128
BF16_SUBLANES = 16
GLA_HEADS, GLA_DK, GLA_DV = 4, 64, 128
GLA_LOWRANK = 16
GLA_INV_TAU = 1.0 / 16.0
GLA_CHUNK = 64
GLA_SUPER = 256
FOX_HEADS, FOX_DH = 8, 64
FOX_TILE = 256
FOX_SLOTS = 3
FOX_BLOCKS_AHEAD = 4
FOX_BOUNDED_MAX_LOGIT = 60.0
CONV_W = 3
FFN_CHUNK = 256
CARRY_ROWS = 8
DOWN_LAG = 2
MISC_F0 = 0
MISC_LR0 = 8
VMEM_LIMIT = 56 * 1024 * 1024


def _dot(a, b):
    return jnp.dot(a, b, preferred_element_type=F32)


def _dot_nt(a, b):
    return lax.dot_general(a, b, (((1,), (1,)), ((), ())), preferred_element_type=F32)


def _dot_tn(a, b):
    return lax.dot_general(a, b, (((0,), (0,)), ((), ())), preferred_element_type=F32)


def _log_sigmoid(z):
    return jnp.minimum(z, 0.0) - jnp.log(1.0 + jnp.exp(-jnp.abs(z)))


def _split_bf16(x, pieces):
    out = []
    for _ in range(pieces):
        p = x.astype(BF16)
        out.append(p)
        x = x - p.astype(F32)
    return out


def _rms(x, gain):
    ms = jnp.mean(x * x, axis=-1, keepdims=True)
    return x * lax.rsqrt(ms + EPS) * gain


def _inproj_kernel(x_ref, g_ref, win_ref, bf_ref, qg_ref, kg_ref, *rest, tiles_per_seq, n_cast):
    cast_in, rest = rest[:n_cast], rest[n_cast:]
    gq_ref, gk_ref, gv_ref, gog_ref, fqa_ref, fka_ref, fv_ref, misc_ref, qbias_ref = rest[:9]
    cast_out = rest[9:9 + n_cast]
    carry_scr, h_scr, wg_ref, wf_ref, wm_ref = rest[9 + n_cast:]

    @pl.when(pl.program_id(0) == 0)
    def _():
        gla_w = wg_ref.shape[1]
        fox_w = wf_ref.shape[1]
        lr0 = gla_w
        fox0 = lr0 + GLA_LOWRANK
        f0 = fox0 + fox_w
        wg_ref[...] = win_ref[:gla_w, :].T.astype(BF16)
        wf_ref[...] = win_ref[fox0:f0, :].T.astype(BF16)
        pad = jnp.zeros((LANES - MISC_LR0 - GLA_LOWRANK, win_ref.shape[1]), F32)
        misc_t = jnp.concatenate([win_ref[f0:f0 + FOX_HEADS, :], win_ref[lr0:fox0, :], pad], axis=0)
        wm_ref[...] = misc_t.T.astype(BF16)

    _inproj_body(x_ref, g_ref, wg_ref, wf_ref, wm_ref, bf_ref, qg_ref, kg_ref, gq_ref, gk_ref, gv_ref, gog_ref,
                 fqa_ref, fka_ref, fv_ref, misc_ref, qbias_ref, carry_scr, h_scr, tiles_per_seq)
    for w_ref, o_ref in zip(cast_in, cast_out):
        o_ref[...] = w_ref[...].astype(BF16)


def _inproj_body(x_ref, g_ref, wg_ref, wf_ref, wm_ref, bf_ref, qg_ref, kg_ref, gq_ref, gk_ref, gv_ref, gog_ref,
                 fqa_ref, fka_ref, fv_ref, misc_ref, qbias_ref, carry_scr, h_scr, tiles_per_seq):
    tm = x_ref.shape[0]
    fw = FOX_HEADS * FOX_DH

    @pl.when(pl.program_id(0) % tiles_per_seq == 0)
    def _():
        carry_scr[...] = jnp.zeros_like(carry_scr)

    h_scr[...] = _rms(x_ref[...], g_ref[...]).astype(BF16)
    h = h_scr[...]
    misc = _dot(h, wm_ref[...])
    misc_ref[...] = misc
    q_all = _dot(h, wf_ref[:, :fw])
    k_all = _dot(h, wf_ref[:, fw:2 * fw])

    blk = FOX_TILE
    trow = lax.broadcasted_iota(jnp.int32, (blk, blk), 0)
    tcol = lax.broadcasted_iota(jnp.int32, (blk, blk), 1)
    tri = (tcol <= trow).astype(BF16)
    carry = carry_scr[...]
    c_blocks = []
    for r0 in range(0, tm, blk):
        cb = carry
        for piece in _split_bf16(_log_sigmoid(misc[r0:r0 + blk] + bf_ref[...]), 3):
            cb = cb + _dot(tri, piece)
        c_blocks.append(cb)
        carry = cb[blk - 1:blk]
    carry_scr[...] = carry
    c_all = jnp.concatenate(c_blocks, axis=0)
    qbias_ref[...] = (c_all * (-LOG2E)).T

    lane = lax.broadcasted_iota(jnp.int32, (tm, LANES), 1)
    lower = lane < FOX_DH

    def pair_norm(x, g):
        sq = x * x
        ms_lo = jnp.sum(jnp.where(lower, sq, 0.0), axis=-1, keepdims=True)
        ms_hi = jnp.sum(jnp.where(lower, 0.0, sq), axis=-1, keepdims=True)
        ms = jnp.where(lower, ms_lo, ms_hi) * (1.0 / FOX_DH)
        return x * lax.rsqrt(ms + EPS) * g

    fv_ref[...] = _dot(h, wf_ref[:, 2 * fw:]).astype(fv_ref.dtype)
    off = 0
    for ref in (gq_ref, gk_ref, gv_ref, gog_ref):
        n = ref.shape[-1]
        ref[...] = _dot(h, wg_ref[:, off:off + n]).astype(ref.dtype)
        off += n

    for pair in range(fw // LANES):
        ps = slice(pair * LANES, (pair + 1) * LANES)
        qn = pair_norm(q_all[:, ps], qg_ref[...])
        kn = pair_norm(k_all[:, ps], kg_ref[...])
        for e in range(2):
            head = 2 * pair + e
            in_head = (lane >= FOX_DH) if e else lower
            piece0 = 0 if e else FOX_DH
            c_col = jnp.sum(jnp.where(lane == MISC_F0 + head, c_all, 0.0), axis=-1, keepdims=True)
            is_piece = (lane >= piece0) & (lane < piece0 + 3)
            hs = slice(head * LANES, (head + 1) * LANES)
            fqa_ref[:, hs] = jnp.where(in_head, qn, jnp.where(is_piece, 1.0, 0.0)).astype(BF16)
            ka = jnp.where(in_head, kn, 0.0).astype(BF16)
            for j, p in enumerate(_split_bf16(c_col * (-LOG2E), 3)):
                ka = jnp.where(lane == piece0 + j, p, ka)
            fka_ref[:, hs] = ka


def _cast_slab(rows, steps):
    for n_slabs in range(min(steps, rows // BF16_SUBLANES), 0, -1):
        if rows % n_slabs == 0 and (rows // n_slabs) % BF16_SUBLANES == 0:
            return rows // n_slabs
    raise ValueError(f"no bf16-tile aligned slab for {rows} rows")


def _inproj(x2d, g, w_in, bf_pad, qg2, kg2, later_weights, s, tm):
    t, d = x2d.shape
    steps = t // tm
    gla_w = 2 * GLA_HEADS * (GLA_DK + GLA_DV)
    fox_w = 3 * FOX_HEADS * FOX_DH
    assert w_in.shape == (1, gla_w + GLA_LOWRANK + fox_w + FOX_HEADS, d)
    assert MISC_F0 == 0 and MISC_LR0 == FOX_HEADS
    widths = (256, 256, 512, 512, FOX_HEADS * LANES, FOX_HEADS * LANES, 512, LANES)
    dtypes = (BF16,) * 7 + (F32,)
    whole = lambda a: pl.BlockSpec(a.shape, lambda i: (0, 0))
    once = lambda a: pl.BlockSpec((None,) + a.shape[1:], lambda i: (0, 0, 0), pipeline_mode=pl.Buffered(1))

    def slab(w):
        rows = _cast_slab(w.shape[0], steps)
        last = w.shape[0] // rows - 1
        return pl.BlockSpec((rows, w.shape[1]), lambda i: (jnp.minimum(i, last), 0))

    outs = pl.pallas_call(
        functools.partial(_inproj_kernel, tiles_per_seq=s // tm, n_cast=len(later_weights)),
        grid=(steps,),
        in_specs=[pl.BlockSpec((tm, d), lambda i: (i, 0)), whole(g), once(w_in)]
                 + [whole(a) for a in (bf_pad, qg2, kg2)]
                 + [slab(w) for w in later_weights],
        out_specs=[pl.BlockSpec((tm, n), lambda i: (i, 0)) for n in widths]
                  + [pl.BlockSpec((None, LANES, tm), lambda i: (i, 0, 0))]
                  + [slab(w) for w in later_weights],
        out_shape=[jax.ShapeDtypeStruct((t, n), dt) for n, dt in zip(widths, dtypes)]
                  + [jax.ShapeDtypeStruct((steps, LANES, tm), F32)]
                  + [jax.ShapeDtypeStruct(w.shape, BF16) for w in later_weights],
        scratch_shapes=[pltpu.VMEM((1, LANES), F32),
                        pltpu.VMEM((tm, d), BF16),
                        pltpu.VMEM((d, gla_w), BF16),
                        pltpu.VMEM((d, fox_w), BF16),
                        pltpu.VMEM((d, LANES), BF16)],
        compiler_params=pltpu.CompilerParams(dimension_semantics=("arbitrary",), vmem_limit_bytes=VMEM_LIMIT),
        name="inproj",
    )(x2d, g, w_in, bf_pad, qg2, kg2, *later_weights)
    return outs[:len(widths) + 1], outs[len(widths) + 1:]


def _gla_stages(q_ref, k_ref, v_ref, og_ref, misc_ref, lrw_ref, lrb_ref, ong_ref, y_ref, st_scr):
    s_len = q_ref.shape[0]
    r = GLA_SUPER
    n_sb = s_len // r
    n_ch = r // GLA_CHUNK
    dkw = GLA_HEADS * GLA_DK
    row = lax.broadcasted_iota(jnp.int32, (r, r), 0)
    col = lax.broadcasted_iota(jnp.int32, (r, r), 1)
    causal_bd = ((row // GLA_CHUNK) == (col // GLA_CHUNK)) & (col <= row)
    tri_bd = causal_bd.astype(BF16)
    head_of_col = lax.broadcasted_iota(jnp.int32, (r, dkw), 1) // GLA_DK
    heads = range(GLA_HEADS)
    sb_rows = lambda n: slice(n * r, (n + 1) * r)
    ch_rows = lambda c: slice(c * GLA_CHUNK, (c + 1) * GLA_CHUNK)
    v_cols = lambda h: slice(h * GLA_DV, (h + 1) * GLA_DV)

    la_pieces = []
    for n in range(n_sb):
        z = _dot(misc_ref[sb_rows(n), :].astype(BF16), lrw_ref[...]) + lrb_ref[...]
        la_pieces.append(_split_bf16(_log_sigmoid(z) * GLA_INV_TAU, 2))
        yield

    qh, ke, kd, decay = [], [], [], []
    for n in range(n_sb):
        b = _dot(tri_bd, la_pieces[n][0]) + _dot(tri_bd, la_pieces[n][1])
        b_last = [b[(c + 1) * GLA_CHUNK - 1:(c + 1) * GLA_CHUNK, :] for c in range(n_ch)]
        b_tot = jnp.concatenate([jnp.broadcast_to(bl, (GLA_CHUNK, dkw)) for bl in b_last], axis=0)
        q = q_ref[sb_rows(n), :].astype(F32)
        k = k_ref[sb_rows(n), :].astype(F32)
        qe = (q * (GLA_DK ** -0.5)) * jnp.exp(b)
        qh.append([jnp.where(head_of_col == h, qe, 0.0).astype(BF16) for h in heads])
        ke.append((k * jnp.exp(-b)).astype(BF16))
        kd_n = k * jnp.exp(b_tot - b)
        kd.append([jnp.where(head_of_col == h, kd_n, 0.0).astype(BF16) for h in heads])
        decay.append([jnp.exp(bl) for bl in b_last])
        yield

    a = []
    for n in range(n_sb):
        a.append([jnp.where(causal_bd, _dot_nt(qh[n][h], ke[n]), 0.0).astype(BF16) for h in heads])
        yield

    o_intra = []
    for n in range(n_sb):
        o_intra.append([_dot(a[n][h], v_ref[sb_rows(n), v_cols(h)]) for h in heads])
        yield

    ds = []
    for n in range(n_sb):
        for c in range(n_ch):
            rows = slice(n * r + c * GLA_CHUNK, n * r + (c + 1) * GLA_CHUNK)
            v_stack = jnp.concatenate([v_ref[rows, v_cols(h)] for h in heads], axis=0)
            kd_stack = jnp.concatenate([kd[n][h][ch_rows(c)] for h in heads], axis=0)
            ds.append(_dot_tn(v_stack, kd_stack))
        yield

    st = st_scr[...]
    o_inter = []
    for n in range(n_sb):
        for c in range(n_ch):
            qe_stack = jnp.concatenate([qh[n][h][ch_rows(c)] for h in heads], axis=0)
            o_inter.append(_dot_nt(qe_stack, st.astype(BF16)))
            st = st * decay[n][c] + ds[n * n_ch + c]
        yield
    st_scr[...] = st

    for n in range(n_sb):
        og = og_ref[sb_rows(n), :].astype(F32)
        gate = og * jax.nn.sigmoid(og)
        for h in heads:
            inter = jnp.concatenate([o_inter[n * n_ch + c][ch_rows(h)] for c in range(n_ch)], axis=0)
            o = o_intra[n][h] + inter
            y_ref[sb_rows(n), v_cols(h)] = (_rms(o, ong_ref[...]) * gate[:, v_cols(h)]).astype(y_ref.dtype)
        yield


def _fox_stages(qa_ref, ka_ref, v_ref, y_ref, vt_scr, s_scr):
    s_len = v_ref.shape[0]
    t = FOX_TILE
    n_tiles = s_len // t
    krow = lax.broadcasted_iota(jnp.int32, (t, t), 0)
    qcol = lax.broadcasted_iota(jnp.int32, (t, t), 1)
    orow = lax.broadcasted_iota(jnp.int32, (LANES, t), 0)
    vt_scr[...] = v_ref[...].astype(F32).T.astype(BF16)

    units = [(i, e) for i in range(n_tiles) for e in range(2)]
    n_slots = s_scr.shape[0]

    def scores(n):
        i, e = units[n]
        hs = slice(e * LANES, (e + 1) * LANES)
        qa = qa_ref[i * t:(i + 1) * t, hs]
        mp = None
        for kb in range(i + 1):
            sc = _dot_nt(ka_ref[kb * t:(kb + 1) * t, hs], qa)
            if kb == i:
                sc = jnp.where(krow <= qcol, sc, -jnp.inf)
            s_scr[n % n_slots, kb] = sc
            bm = jnp.max(sc.reshape(t // 8, 8, t), axis=0)
            mp = bm if mp is None else jnp.maximum(mp, bm)
        return jnp.max(mp, axis=0, keepdims=True)

    def weighted_values(n, m):
        lp = jnp.zeros((8, t), F32)
        acc = jnp.zeros((LANES, t), F32)
        for kb in range(units[n][0] + 1):
            p = jnp.exp2(s_scr[n % n_slots, kb] - m)
            lp = lp + jnp.sum(p.reshape(t // 8, 8, t), axis=0)
            acc = acc + _dot(vt_scr[:, kb * t:(kb + 1) * t], p.astype(BF16))
        return acc / jnp.sum(lp, axis=0, keepdims=True)

    ahead = n_slots - 1
    col_max = {}
    for n in range(min(ahead, len(units))):
        col_max[n] = scores(n)
        yield
    o_even = None
    for n, (i, e) in enumerate(units):
        if n + ahead < len(units):
            col_max[n + ahead] = scores(n + ahead)
            yield
        o_t = weighted_values(n, col_max.pop(n))
        if e == 0:
            o_even = o_t
        else:
            o_pair = jnp.where(orow < FOX_DH, o_even, o_t)
            y_ref[i * t:(i + 1) * t, :] = o_pair.T.astype(y_ref.dtype)
        yield


def _fox_stages_bounded(qa_ref, ka_ref, v_ref, qbias_ref, y_ref, vt_scr, *, tile_rows):
    s_len = v_ref.shape[0]
    t = FOX_TILE
    n_tiles = s_len // t
    pair = pl.program_id(1)
    krow = lax.broadcasted_iota(jnp.int32, (t, t), 0)
    qcol = lax.broadcasted_iota(jnp.int32, (t, t), 1)
    orow = lax.broadcasted_iota(jnp.int32, (LANES, t), 0)
    vt_scr[...] = v_ref[...].astype(F32).T.astype(BF16)

    units = [(i, e) for i in range(n_tiles) for e in range(2)]
    blocks = [(n, kb) for n, (i, e) in enumerate(units) for kb in range(i + 1)]
    row_sum, acc = {}, {}
    o_even = [None]

    def values(n, kb, p):
        i, e = units[n]
        d = _dot(vt_scr[:, kb * t:(kb + 1) * t], p)
        acc[n] = d if n not in acc else acc[n] + d
        if kb == i:
            o_t = acc.pop(n) / jnp.sum(row_sum.pop(n), axis=0, keepdims=True)
            if e == 0:
                o_even[0] = o_t
            else:
                o_pair = jnp.where(orow < FOX_DH, o_even[0], o_t)
                y_ref[i * t:(i + 1) * t, :] = o_pair.T.astype(y_ref.dtype)

    pending = []
    for j, (n, kb) in enumerate(blocks):
        i, e = units[n]
        hs = slice(e * LANES, (e + 1) * LANES)
        q0 = i * t
        m = qbias_ref[q0 // tile_rows, pl.ds(2 * pair + e, 1), q0 % tile_rows:q0 % tile_rows + t]
        sc = _dot_nt(ka_ref[kb * t:(kb + 1) * t, hs], qa_ref[q0:q0 + t, hs])
        if kb == i:
            sc = jnp.where(krow <= qcol, sc, -jnp.inf)
        p = jnp.exp2(sc - m)
        ps = jnp.sum(p.reshape(t // 8, 8, t), axis=0)
        row_sum[n] = ps if n not in row_sum else row_sum[n] + ps
        pending.append((n, kb, p.astype(BF16)))
        if len(pending) > FOX_BLOCKS_AHEAD:
            values(*pending.pop(0))
        if j % 2 == 1:
            yield
    for item in pending:
        values(*item)
    yield


def _mixers_kernel(qa_ref, ka_ref, fv_ref, qbias_ref, gq_ref, gk_ref, gv_ref, gog_ref, misc_ref, lrw_ref, lrb_ref,
                   ong_ref, yf_ref, yg_ref, vt_scr, s_scr, st_scr, *, bounded_logits, tile_rows):
    @pl.when(pl.program_id(1) == 0)
    def _():
        st_scr[...] = jnp.zeros_like(st_scr)

    if bounded_logits:
        fox = _fox_stages_bounded(qa_ref, ka_ref, fv_ref, qbias_ref, yf_ref, vt_scr, tile_rows=tile_rows)
    else:
        fox = _fox_stages(qa_ref, ka_ref, fv_ref, yf_ref, vt_scr, s_scr)
    gla = _gla_stages(gq_ref, gk_ref, gv_ref, gog_ref, misc_ref, lrw_ref, lrb_ref, ong_ref, yg_ref, st_scr)
    rota = (fox, fox, gla)
    finished = set()
    turn = 0
    while len(finished) < 2:
        gen = rota[turn % len(rota)]
        turn += 1
        if gen not in finished:
            try:
                next(gen)
            except StopIteration:
                finished.add(gen)


def _mixers(fqa, fka, fv, qbias, gq, gk, gv, gog, misc, lrw_pad, lrb, ong, b, s, bounded_logits):
    n_pairs = FOX_HEADS * FOX_DH // LANES
    t = FOX_TILE
    dkw, dvw = GLA_HEADS * GLA_DK, GLA_HEADS * GLA_DV
    sq = s // n_pairs
    tile_rows = qbias.shape[2]
    heads2 = pl.BlockSpec((s, 2 * LANES), lambda i, j: (i, j))
    pair = pl.BlockSpec((s, LANES), lambda i, j: (i, j))
    seq_bias = pl.BlockSpec((s // tile_rows, FOX_HEADS, tile_rows), lambda i, j: (i, 0, 0))
    quarter = lambda w: pl.BlockSpec((sq, w), lambda i, j: (i * n_pairs + j, 0))
    whole = lambda a: pl.BlockSpec(a.shape, lambda i, j: (0, 0))
    return pl.pallas_call(
        functools.partial(_mixers_kernel, bounded_logits=bounded_logits, tile_rows=tile_rows),
        grid=(b, n_pairs),
        in_specs=[heads2, heads2, pair, seq_bias,
                  quarter(dkw), quarter(dkw), quarter(dvw), quarter(dvw), quarter(LANES),
                  whole(lrw_pad), whole(lrb), whole(ong)],
        out_specs=[pair, quarter(dvw)],
        out_shape=[jax.ShapeDtypeStruct((b * s, FOX_HEADS * FOX_DH), BF16),
                   jax.ShapeDtypeStruct((b * s, dvw), BF16)],
        scratch_shapes=[
            pltpu.VMEM((LANES, s), BF16),
            pltpu.VMEM((FOX_SLOTS, s // t, t, t), F32),
            pltpu.VMEM((GLA_DV, dkw), F32),
        ],
        compiler_params=pltpu.CompilerParams(dimension_semantics=("arbitrary", "arbitrary"),
                                             vmem_limit_bytes=VMEM_LIMIT),
        name="mixers_bounded" if bounded_logits else "mixers",
    )(fqa, fka, fv, qbias, gq, gk, gv, gog, misc, lrw_pad, lrb, ong)


def _post_kernel(x_ref, yg_ref, yf_ref, p_ref, wo_ref, g2_ref, wup_ref, cw_ref, cb_ref, wdn_ref,
                 g3_ref, wpg_ref, bpg_ref, wpe_ref, gpe_ref, o_ref,
                 x1_scr, h2_scr, u_scr, act_scr, carry_scr, acc_scr):
    tm = x_ref.shape[0]
    d_ff = wdn_ref.shape[0]
    n_chunks = d_ff // FFN_CHUNK
    n_slabs = 2 * FFN_CHUNK // LANES
    half = n_slabs // 2
    dvw = yg_ref.shape[1]
    first_tile = pl.program_id(1) == 0

    @pl.when(first_tile)
    def _():
        carry_scr[...] = jnp.zeros_like(carry_scr)

    o_ref[...] = _rms(_dot(p_ref[...].astype(BF16), wpe_ref[...]), gpe_ref[...])

    x1 = x_ref[...] + _dot(yg_ref[...], wo_ref[:dvw, :]) + _dot(yf_ref[...], wo_ref[dvw:, :])
    x1_scr[...] = x1
    h2_scr[...] = _rms(x1, g2_ref[...]).astype(BF16)
    acc_scr[...] = jnp.zeros_like(acc_scr)

    for j in range(n_chunks + DOWN_LAG):
        if j < n_chunks:
            ub = u_scr.at[j % 2]
            col0 = [(c // half) * d_ff + j * FFN_CHUNK + (c % half) * LANES for c in range(n_slabs)]
            h2 = h2_scr[...]
            for part in range(2):
                u = _dot(h2, wup_ref[:, col0[part * half]:col0[part * half] + FFN_CHUNK])
                for c in range(part * half, (part + 1) * half):
                    ub[c, :CARRY_ROWS, :] = carry_scr[j, c]
                    ub[c, CARRY_ROWS:, :] = u[:, (c % half) * LANES:(c % half + 1) * LANES]
                    carry_scr[j, c] = ub[c, tm:, :]
        if j >= DOWN_LAG:
            jd = j - DOWN_LAG
            acc_scr[...] += _dot(act_scr[jd % (DOWN_LAG + 1)], wdn_ref[jd * FFN_CHUNK:(jd + 1) * FFN_CHUNK, :])
        if j < n_chunks:
            def conv(c):
                cs = slice(col0[c], col0[c] + LANES)
                return (cb_ref[:, cs]
                        + ub[c, CARRY_ROWS - 2:CARRY_ROWS - 2 + tm, :] * cw_ref[0:1, cs]
                        + ub[c, CARRY_ROWS - 1:CARRY_ROWS - 1 + tm, :] * cw_ref[1:2, cs]
                        + ub[c, CARRY_ROWS:, :] * cw_ref[2:3, cs])
            for c in range(half):
                gate = conv(c)
                act = gate * jax.nn.sigmoid(gate) * conv(c + half)
                act_scr[j % (DOWN_LAG + 1), :, c * LANES:(c + 1) * LANES] = act.astype(BF16)

    x2 = x1_scr[...] + acc_scr[...]
    gate = jax.nn.sigmoid(_dot(_rms(x2, g3_ref[...]).astype(BF16), wpg_ref[...]) + bpg_ref[...])
    o_ref[...] = x2 + gate * o_ref[...]


def _post(x2d, yg, yf, p2d, wo, g2, wup, cw, cb, wdn, g3, wpg, bpg, wpe, gpe, b, s, tm):
    d = x2d.shape[1]
    nt = s // tm
    tile = lambda w: pl.BlockSpec((tm, w), lambda i, j: (i * nt + j, 0))
    resident = lambda a: pl.BlockSpec(a.shape, lambda i, j: (0,) * a.ndim, pipeline_mode=pl.Buffered(1))
    n_chunks = wdn.shape[0] // FFN_CHUNK
    n_slabs = 2 * FFN_CHUNK // LANES
    return pl.pallas_call(
        _post_kernel,
        grid=(b, nt),
        in_specs=[tile(d), tile(yg.shape[1]), tile(yf.shape[1]), tile(p2d.shape[1])]
                 + [resident(a) for a in (wo, g2, wup, cw, cb, wdn, g3, wpg, bpg, wpe, gpe)],
        out_specs=tile(d),
        out_shape=jax.ShapeDtypeStruct(x2d.shape, F32),
        scratch_shapes=[
            pltpu.VMEM((tm, d), F32),
            pltpu.VMEM((tm, d), BF16),
            pltpu.VMEM((2, n_slabs, CARRY_ROWS + tm, LANES), F32),
            pltpu.VMEM((DOWN_LAG + 1, tm, FFN_CHUNK), BF16),
            pltpu.VMEM((n_chunks, n_slabs, CARRY_ROWS, LANES), F32),
            pltpu.VMEM((tm, d), F32),
        ],
        compiler_params=pltpu.CompilerParams(dimension_semantics=("arbitrary", "arbitrary"),
                                             vmem_limit_bytes=VMEM_LIMIT),
        name="post",
    )(x2d, yg, yf, p2d, wo, g2, wup, cw, cb, wdn, g3, wpg, bpg, wpe, gpe)


def _layer(x2d, p2d, b, s, norm1_g, w_in, lr_w, lr_b, onorm_g, b_f, qn_g, kn_g, w_o, norm2_g, w_up,
           conv_w, conv_b, w_down, norm3_g, w_pe, pe_norm_g, w_pg, b_pg):
    gqk = GLA_HEADS * GLA_DK
    row = lambda a: a.reshape(1, -1).astype(F32)
    bf_pad = jnp.zeros((1, LANES), F32).at[0, MISC_F0:MISC_F0 + FOX_HEADS].set(b_f)
    qg2 = row(jnp.tile(qn_g, 2)) * (FOX_DH ** -0.5 * LOG2E)
    (gq, gk, gv, gog, fqa, fka, fv, misc, qbias), (wo_b, wup_b, wdn_b, wpg_b, wpe_b) = _inproj(
        x2d, row(norm1_g), jnp.swapaxes(w_in, 1, 2).astype(F32), bf_pad, qg2, row(jnp.tile(kn_g, 2)),
        (w_o, w_up, w_down, w_pg, w_pe), s, tm=512)

    lrw_pad = jnp.zeros((LANES, gqk), F32).at[MISC_LR0:MISC_LR0 + GLA_LOWRANK].set(lr_w).astype(BF16)
    logit_bound = FOX_DH ** 0.5 * LOG2E * jnp.max(jnp.abs(qn_g)) * jnp.max(jnp.abs(kn_g))
    mix = lambda bounded: functools.partial(_mixers, b=b, s=s, bounded_logits=bounded)
    y_fox, y_gla = lax.cond(logit_bound <= FOX_BOUNDED_MAX_LOGIT, mix(True), mix(False),
                            fqa, fka, fv, qbias, gq, gk, gv, gog, misc, lrw_pad, row(lr_b), row(onorm_g))

    return _post(
        x2d, y_gla, y_fox, p2d, wo_b, row(norm2_g), wup_b, conv_w.astype(F32), row(conv_b), wdn_b,
        row(norm3_g), wpg_b, row(b_pg), wpe_b, row(pe_norm_g), b, s, tm=512)


def kernel(x, p, norm1_g, w_in, gla_lr_w, gla_lr_b, gla_onorm_g, fox_b_f, fox_qnorm_g, fox_knorm_g, w_o, norm2_g, w_up, conv_w, conv_b, w_down, norm3_g, w_pe, pe_norm_g, w_pg, b_pg):
    b, s, d = x.shape
    x2d = x.reshape(b * s, d)
    for i in range(p.shape[0]):
        x2d = _layer(x2d, p[i].reshape(b * s, -1), b, s, norm1_g[i], w_in[i:i + 1], gla_lr_w[i], gla_lr_b[i],
                     gla_onorm_g[i], fox_b_f[i], fox_qnorm_g[i], fox_knorm_g[i], w_o[i], norm2_g[i], w_up[i],
                     conv_w[i], conv_b[i], w_down[i], norm3_g[i], w_pe[i], pe_norm_g[i], w_pg[i], b_pg[i])
    return x2d.reshape(b, s, d)
```

```python
import functools

import jax
import jax.numpy as jnp
from jax import lax
from jax.experimental import pallas as pl
from jax.experimental.pallas import tpu as pltpu

F32 = jnp.float32
BF16 = jnp.bfloat16

EPS = 1e-6
LOG2E = 1.4426950408889634
LANES = 128
BF16_SUBLANES = 16
GLA_HEADS, GLA_DK, GLA_DV = 4, 64, 128
GLA_LOWRANK = 16
GLA_INV_TAU = 1.0 / 16.0
GLA_CHUNK = 64
GLA_SUPER = 256
FOX_HEADS, FOX_DH = 8, 64
FOX_TILE = 256
FOX_SLOTS = 3
FOX_BLOCKS_AHEAD = 4
FOX_BOUNDED_MAX_LOGIT = 60.0
CONV_W = 3
FFN_CHUNK = 256
CARRY_ROWS = 8
DOWN_LAG = 2
MISC_F0 = 0
MISC_LR0 = 8
VMEM_LIMIT = 56 * 1024 * 1024


def _dot(a, b):
    return jnp.dot(a, b, preferred_element_type=F32)


def _dot_nt(a, b):
    return lax.dot_general(a, b, (((1,), (1,)), ((), ())), preferred_element_type=F32)


def _dot_tn(a, b):
    return lax.dot_general(a, b, (((0,), (0,)), ((), ())), preferred_element_type=F32)


def _log_sigmoid(z):
    return jnp.minimum(z, 0.0) - jnp.log(1.0 + jnp.exp(-jnp.abs(z)))


def _split_bf16(x, pieces):
    out = []
    for _ in range(pieces):
        p = x.astype(BF16)
        out.append(p)
        x = x - p.astype(F32)
    return out


def _rms(x, gain):
    ms = jnp.mean(x * x, axis=-1, keepdims=True)
    return x * lax.rsqrt(ms + EPS) * gain


def _inproj_kernel(x_ref, g_ref, win_ref, bf_ref, qg_ref, kg_ref, *rest, tiles_per_seq, n_cast):
    cast_in, rest = rest[:n_cast], rest[n_cast:]
    gq_ref, gk_ref, gv_ref, gog_ref, fqa_ref, fka_ref, fv_ref, misc_ref, qbias_ref = rest[:9]
    cast_out = rest[9:9 + n_cast]
    carry_scr, h_scr, wg_ref, wf_ref, wm_ref = rest[9 + n_cast:]

    @pl.when(pl.program_id(0) == 0)
    def _():
        gla_w = wg_ref.shape[1]
        fox_w = wf_ref.shape[1]
        lr0 = gla_w
        fox0 = lr0 + GLA_LOWRANK
        f0 = fox0 + fox_w
        wg_ref[...] = win_ref[:gla_w, :].T.astype(BF16)
        wf_ref[...] = win_ref[fox0:f0, :].T.astype(BF16)
        pad = jnp.zeros((LANES - MISC_LR0 - GLA_LOWRANK, win_ref.shape[1]), F32)
        misc_t = jnp.concatenate([win_ref[f0:f0 + FOX_HEADS, :], win_ref[lr0:fox0, :], pad], axis=0)
        wm_ref[...] = misc_t.T.astype(BF16)

    _inproj_body(x_ref, g_ref, wg_ref, wf_ref, wm_ref, bf_ref, qg_ref, kg_ref, gq_ref, gk_ref, gv_ref, gog_ref,
                 fqa_ref, fka_ref, fv_ref, misc_ref, qbias_ref, carry_scr, h_scr, tiles_per_seq)
    for w_ref, o_ref in zip(cast_in, cast_out):
        o_ref[...] = w_ref[...].astype(BF16)


def _inproj_body(x_ref, g_ref, wg_ref, wf_ref, wm_ref, bf_ref, qg_ref, kg_ref, gq_ref, gk_ref, gv_ref, gog_ref,
                 fqa_ref, fka_ref, fv_ref, misc_ref, qbias_ref, carry_scr, h_scr, tiles_per_seq):
    tm = x_ref.shape[0]
    fw = FOX_HEADS * FOX_DH

    @pl.when(pl.program_id(0) % tiles_per_seq == 0)
    def _():
        carry_scr[...] = jnp.zeros_like(carry_scr)

    h_scr[...] = _rms(x_ref[...], g_ref[...]).astype(BF16)
    h = h_scr[...]
    misc = _dot(h, wm_ref[...])
    misc_ref[...] = misc
    q_all = _dot(h, wf_ref[:, :fw])
    k_all = _dot(h, wf_ref[:, fw:2 * fw])

    blk = FOX_TILE
    trow = lax.broadcasted_iota(jnp.int32, (blk, blk), 0)
    tcol = lax.broadcasted_iota(jnp.int32, (blk, blk), 1)
    tri = (tcol <= trow).astype(BF16)
    carry = carry_scr[...]
    c_blocks = []
    for r0 in range(0, tm, blk):
        cb = carry
        for piece in _split_bf16(_log_sigmoid(misc[r0:r0 + blk] + bf_ref[...]), 3):
            cb = cb + _dot(tri, piece)
        c_blocks.append(cb)
        carry = cb[blk - 1:blk]
    carry_scr[...] = carry
    c_all = jnp.concatenate(c_blocks, axis=0)
    qbias_ref[...] = (c_all * (-LOG2E)).T

    lane = lax.broadcasted_iota(jnp.int32, (tm, LANES), 1)
    lower = lane < FOX_DH

    def pair_norm(x, g):
        sq = x * x
        ms_lo = jnp.sum(jnp.where(lower, sq, 0.0), axis=-1, keepdims=True)
        ms_hi = jnp.sum(jnp.where(lower, 0.0, sq), axis=-1, keepdims=True)
        ms = jnp.where(lower, ms_lo, ms_hi) * (1.0 / FOX_DH)
        return x * lax.rsqrt(ms + EPS) * g

    fv_ref[...] = _dot(h, wf_ref[:, 2 * fw:]).astype(fv_ref.dtype)
    off = 0
    for ref in (gq_ref, gk_ref, gv_ref, gog_ref):
        n = ref.shape[-1]
        ref[...] = _dot(h, wg_ref[:, off:off + n]).astype(ref.dtype)
        off += n

    for pair in range(fw // LANES):
        ps = slice(pair * LANES, (pair + 1) * LANES)
        qn = pair_norm(q_all[:, ps], qg_ref[...])
        kn = pair_norm(k_all[:, ps], kg_ref[...])
        for e in range(2):
            head = 2 * pair + e
            in_head = (lane >= FOX_DH) if e else lower
            piece0 = 0 if e else FOX_DH
            c_col = jnp.sum(jnp.where(lane == MISC_F0 + head, c_all, 0.0), axis=-1, keepdims=True)
            is_piece = (lane >= piece0) & (lane < piece0 + 3)
            hs = slice(head * LANES, (head + 1) * LANES)
            fqa_ref[:, hs] = jnp.where(in_head, qn, jnp.where(is_piece, 1.0, 0.0)).astype(BF16)
            ka = jnp.where(in_head, kn, 0.0).astype(BF16)
            for j, p in enumerate(_split_bf16(c_col * (-LOG2E), 3)):
                ka = jnp.where(lane == piece0 + j, p, ka)
            fka_ref[:, hs] = ka


def _cast_slab(rows, steps):
    for n_slabs in range(min(steps, rows // BF16_SUBLANES), 0, -1):
        if rows % n_slabs == 0 and (rows // n_slabs) % BF16_SUBLANES == 0:
            return rows // n_slabs
    raise ValueError(f"no bf16-tile aligned slab for {rows} rows")


def _inproj(x2d, g, w_in, bf_pad, qg2, kg2, later_weights, s, tm):
    t, d = x2d.shape
    steps = t // tm
    gla_w = 2 * GLA_HEADS * (GLA_DK + GLA_DV)
    fox_w = 3 * FOX_HEADS * FOX_DH
    assert w_in.shape == (1, gla_w + GLA_LOWRANK + fox_w + FOX_HEADS, d)
    assert MISC_F0 == 0 and MISC_LR0 == FOX_HEADS
    widths = (256, 256, 512, 512, FOX_HEADS * LANES, FOX_HEADS * LANES, 512, LANES)
    dtypes = (BF16,) * 7 + (F32,)
    whole = lambda a: pl.BlockSpec(a.shape, lambda i: (0, 0))
    once = lambda a: pl.BlockSpec((None,) + a.shape[1:], lambda i: (0, 0, 0), pipeline_mode=pl.Buffered(1))

    def slab(w):
        rows = _cast_slab(w.shape[0], steps)
        last = w.shape[0] // rows - 1
        return pl.BlockSpec((rows, w.shape[1]), lambda i: (jnp.minimum(i, last), 0))

    outs = pl.pallas_call(
        functools.partial(_inproj_kernel, tiles_per_seq=s // tm, n_cast=len(later_weights)),
        grid=(steps,),
        in_specs=[pl.BlockSpec((tm, d), lambda i: (i, 0)), whole(g), once(w_in)]
                 + [whole(a) for a in (bf_pad, qg2, kg2)]
                 + [slab(w) for w in later_weights],
        out_specs=[pl.BlockSpec((tm, n), lambda i: (i, 0)) for n in widths]
                  + [pl.BlockSpec((None, LANES, tm), lambda i: (i, 0, 0))]
                  + [slab(w) for w in later_weights],
        out_shape=[jax.ShapeDtypeStruct((t, n), dt) for n, dt in zip(widths, dtypes)]
                  + [jax.ShapeDtypeStruct((steps, LANES, tm), F32)]
                  + [jax.ShapeDtypeStruct(w.shape, BF16) for w in later_weights],
        scratch_shapes=[pltpu.VMEM((1, LANES), F32),
                        pltpu.VMEM((tm, d), BF16),
                        pltpu.VMEM((d, gla_w), BF16),
                        pltpu.VMEM((d, fox_w), BF16),
                        pltpu.VMEM((d, LANES), BF16)],
        compiler_params=pltpu.CompilerParams(dimension_semantics=("arbitrary",), vmem_limit_bytes=VMEM_LIMIT),
        name="inproj",
    )(x2d, g, w_in, bf_pad, qg2, kg2, *later_weights)
    return outs[:len(widths) + 1], outs[len(widths) + 1:]


def _gla_stages(q_ref, k_ref, v_ref, og_ref, misc_ref, lrw_ref, lrb_ref, ong_ref, y_ref, st_scr):
    s_len = q_ref.shape[0]
    r = GLA_SUPER
    n_sb = s_len // r
    n_ch = r // GLA_CHUNK
    dkw = GLA_HEADS * GLA_DK
    row = lax.broadcasted_iota(jnp.int32, (r, r), 0)
    col = lax.broadcasted_iota(jnp.int32, (r, r), 1)
    causal_bd = ((row // GLA_CHUNK) == (col // GLA_CHUNK)) & (col <= row)
    tri_bd = causal_bd.astype(BF16)
    head_of_col = lax.broadcasted_iota(jnp.int32, (r, dkw), 1) // GLA_DK
    heads = range(GLA_HEADS)
    sb_rows = lambda n: slice(n * r, (n + 1) * r)
    ch_rows = lambda c: slice(c * GLA_CHUNK, (c + 1) * GLA_CHUNK)
    v_cols = lambda h: slice(h * GLA_DV, (h + 1) * GLA_DV)

    la_pieces = []
    for n in range(n_sb):
        z = _dot(misc_ref[sb_rows(n), :].astype(BF16), lrw_ref[...]) + lrb_ref[...]
        la_pieces.append(_split_bf16(_log_sigmoid(z) * GLA_INV_TAU, 2))
        yield

    qh, ke, kd, decay = [], [], [], []
    for n in range(n_sb):
        b = _dot(tri_bd, la_pieces[n][0]) + _dot(tri_bd, la_pieces[n][1])
        b_last = [b[(c + 1) * GLA_CHUNK - 1:(c + 1) * GLA_CHUNK, :] for c in range(n_ch)]
        b_tot = jnp.concatenate([jnp.broadcast_to(bl, (GLA_CHUNK, dkw)) for bl in b_last], axis=0)
        q = q_ref[sb_rows(n), :].astype(F32)
        k = k_ref[sb_rows(n), :].astype(F32)
        qe = (q * (GLA_DK ** -0.5)) * jnp.exp(b)
        qh.append([jnp.where(head_of_col == h, qe, 0.0).astype(BF16) for h in heads])
        ke.append((k * jnp.exp(-b)).astype(BF16))
        kd_n = k * jnp.exp(b_tot - b)
        kd.append([jnp.where(head_of_col == h, kd_n, 0.0).astype(BF16) for h in heads])
        decay.append([jnp.exp(bl) for bl in b_last])
        yield

    a = []
    for n in range(n_sb):
        a.append([jnp.where(causal_bd, _dot_nt(qh[n][h], ke[n]), 0.0).astype(BF16) for h in heads])
        yield

    o_intra = []
    for n in range(n_sb):
        o_intra.append([_dot(a[n][h], v_ref[sb_rows(n), v_cols(h)]) for h in heads])
        yield

    ds = []
    for n in range(n_sb):
        for c in range(n_ch):
            rows = slice(n * r + c * GLA_CHUNK, n * r + (c + 1) * GLA_CHUNK)
            v_stack = jnp.concatenate([v_ref[rows, v_cols(h)] for h in heads], axis=0)
            kd_stack = jnp.concatenate([kd[n][h][ch_rows(c)] for h in heads], axis=0)
            ds.append(_dot_tn(v_stack, kd_stack))
        yield

    st = st_scr[...]
    o_inter = []
    for n in range(n_sb):
        for c in range(n_ch):
            qe_stack = jnp.concatenate([qh[n][h][ch_rows(c)] for h in heads], axis=0)
            o_inter.append(_dot_nt(qe_stack, st.astype(BF16)))
            st = st * decay[n][c] + ds[n * n_ch + c]
        yield
    st_scr[...] = st

    for n in range(n_sb):
        og = og_ref[sb_rows(n), :].astype(F32)
        gate = og * jax.nn.sigmoid(og)
        for h in heads:
            inter = jnp.concatenate([o_inter[n * n_ch + c][ch_rows(h)] for c in range(n_ch)], axis=0)
            o = o_intra[n][h] + inter
            y_ref[sb_rows(n), v_cols(h)] = (_rms(o, ong_ref[...]) * gate[:, v_cols(h)]).astype(y_ref.dtype)
        yield


def _fox_stages(qa_ref, ka_ref, v_ref, y_ref, vt_scr, s_scr):
    s_len = v_ref.shape[0]
    t = FOX_TILE
    n_tiles = s_len // t
    krow = lax.broadcasted_iota(jnp.int32, (t, t), 0)
    qcol = lax.broadcasted_iota(jnp.int32, (t, t), 1)
    orow = lax.broadcasted_iota(jnp.int32, (LANES, t), 0)
    vt_scr[...] = v_ref[...].astype(F32).T.astype(BF16)

    units = [(i, e) for i in range(n_tiles) for e in range(2)]
    n_slots = s_scr.shape[0]

    def scores(n):
        i, e = units[n]
        hs = slice(e * LANES, (e + 1) * LANES)
        qa = qa_ref[i * t:(i + 1) * t, hs]
        mp = None
        for kb in range(i + 1):
            sc = _dot_nt(ka_ref[kb * t:(kb + 1) * t, hs], qa)
            if kb == i:
                sc = jnp.where(krow <= qcol, sc, -jnp.inf)
            s_scr[n % n_slots, kb] = sc
            bm = jnp.max(sc.reshape(t // 8, 8, t), axis=0)
            mp = bm if mp is None else jnp.maximum(mp, bm)
        return jnp.max(mp, axis=0, keepdims=True)

    def weighted_values(n, m):
        lp = jnp.zeros((8, t), F32)
        acc = jnp.zeros((LANES, t), F32)
        for kb in range(units[n][0] + 1):
            p = jnp.exp2(s_scr[n % n_slots, kb] - m)
            lp = lp + jnp.sum(p.reshape(t // 8, 8, t), axis=0)
            acc = acc + _dot(vt_scr[:, kb * t:(kb + 1) * t], p.astype(BF16))
        return acc / jnp.sum(lp, axis=0, keepdims=True)

    ahead = n_slots - 1
    col_max = {}
    for n in range(min(ahead, len(units))):
        col_max[n] = scores(n)
        yield
    o_even = None
    for n, (i, e) in enumerate(units):
        if n + ahead < len(units):
            col_max[n + ahead] = scores(n + ahead)
            yield
        o_t = weighted_values(n, col_max.pop(n))
        if e == 0:
            o_even = o_t
        else:
            o_pair = jnp.where(orow < FOX_DH, o_even, o_t)
            y_ref[i * t:(i + 1) * t, :] = o_pair.T.astype(y_ref.dtype)
        yield


def _fox_stages_bounded(qa_ref, ka_ref, v_ref, qbias_ref, y_ref, vt_scr, *, tile_rows):
    s_len = v_ref.shape[0]
    t = FOX_TILE
    n_tiles = s_len // t
    pair = pl.program_id(1)
    krow = lax.broadcasted_iota(jnp.int32, (t, t), 0)
    qcol = lax.broadcasted_iota(jnp.int32, (t, t), 1)
    orow = lax.broadcasted_iota(jnp.int32, (LANES, t), 0)
    vt_scr[...] = v_ref[...].astype(F32).T.astype(BF16)

    units = [(i, e) for i in range(n_tiles) for e in range(2)]
    blocks = [(n, kb) for n, (i, e) in enumerate(units) for kb in range(i + 1)]
    row_sum, acc = {}, {}
    o_even = [None]

    def values(n, kb, p):
        i, e = units[n]
        d = _dot(vt_scr[:, kb * t:(kb + 1) * t], p)
        acc[n] = d if n not in acc else acc[n] + d
        if kb == i:
            o_t = acc.pop(n) / jnp.sum(row_sum.pop(n), axis=0, keepdims=True)
            if e == 0:
                o_even[0] = o_t
            else:
                o_pair = jnp.where(orow < FOX_DH, o_even[0], o_t)
                y_ref[i * t:(i + 1) * t, :] = o_pair.T.astype(y_ref.dtype)

    pending = []
    for j, (n, kb) in enumerate(blocks):
        i, e = units[n]
        hs = slice(e * LANES, (e + 1) * LANES)
        q0 = i * t
        m = qbias_ref[q0 // tile_rows, pl.ds(2 * pair + e, 1), q0 % tile_rows:q0 % tile_rows + t]
        sc = _dot_nt(ka_ref[kb * t:(kb + 1) * t, hs], qa_ref[q0:q0 + t, hs])
        if kb == i:
            sc = jnp.where(krow <= qcol, sc, -jnp.inf)
        p = jnp.exp2(sc - m)
        ps = jnp.sum(p.reshape(t // 8, 8, t), axis=0)
        row_sum[n] = ps if n not in row_sum else row_sum[n] + ps
        pending.append((n, kb, p.astype(BF16)))
        if len(pending) > FOX_BLOCKS_AHEAD:
            values(*pending.pop(0))
        if j % 2 == 1:
            yield
    for item in pending:
        values(*item)
    yield


def _mixers_kernel(qa_ref, ka_ref, fv_ref, qbias_ref, gq_ref, gk_ref, gv_ref, gog_ref, misc_ref, lrw_ref, lrb_ref,
                   ong_ref, yf_ref, yg_ref, vt_scr, s_scr, st_scr, *, bounded_logits, tile_rows):
    @pl.when(pl.program_id(1) == 0)
    def _():
        st_scr[...] = jnp.zeros_like(st_scr)

    if bounded_logits:
        fox = _fox_stages_bounded(qa_ref, ka_ref, fv_ref, qbias_ref, yf_ref, vt_scr, tile_rows=tile_rows)
    else:
        fox = _fox_stages(qa_ref, ka_ref, fv_ref, yf_ref, vt_scr, s_scr)
    gla = _gla_stages(gq_ref, gk_ref, gv_ref, gog_ref, misc_ref, lrw_ref, lrb_ref, ong_ref, yg_ref, st_scr)
    rota = (fox, fox, fox, gla) if bounded_logits else (fox, fox, gla)
    finished = set()
    turn = 0
    while len(finished) < 2:
        gen = rota[turn % len(rota)]
        turn += 1
        if gen not in finished:
            try:
                next(gen)
            except StopIteration:
                finished.add(gen)


def _mixers(fqa, fka, fv, qbias, gq, gk, gv, gog, misc, lrw_pad, lrb, ong, b, s, bounded_logits):
    n_pairs = FOX_HEADS * FOX_DH // LANES
    t = FOX_TILE
    dkw, dvw = GLA_HEADS * GLA_DK, GLA_HEADS * GLA_DV
    sq = s // n_pairs
    tile_rows = qbias.shape[2]
    heads2 = pl.BlockSpec((s, 2 * LANES), lambda i, j: (i, j))
    pair = pl.BlockSpec((s, LANES), lambda i, j: (i, j))
    seq_bias = pl.BlockSpec((s // tile_rows, FOX_HEADS, tile_rows), lambda i, j: (i, 0, 0))
    quarter = lambda w: pl.BlockSpec((sq, w), lambda i, j: (i * n_pairs + j, 0))
    whole = lambda a: pl.BlockSpec(a.shape, lambda i, j: (0, 0))
    return pl.pallas_call(
        functools.partial(_mixers_kernel, bounded_logits=bounded_logits, tile_rows=tile_rows),
        grid=(b, n_pairs),
        in_specs=[heads2, heads2, pair, seq_bias,
                  quarter(dkw), quarter(dkw), quarter(dvw), quarter(dvw), quarter(LANES),
                  whole(lrw_pad), whole(lrb), whole(ong)],
        out_specs=[pair, quarter(dvw)],
        out_shape=[jax.ShapeDtypeStruct((b * s, FOX_HEADS * FOX_DH), BF16),
                   jax.ShapeDtypeStruct((b * s, dvw), BF16)],
        scratch_shapes=[
            pltpu.VMEM((LANES, s), BF16),
            pltpu.VMEM((FOX_SLOTS, s // t, t, t), F32),
            pltpu.VMEM((GLA_DV, dkw), F32),
        ],
        compiler_params=pltpu.CompilerParams(dimension_semantics=("arbitrary", "arbitrary"),
                                             vmem_limit_bytes=VMEM_LIMIT),
        name="mixers_bounded" if bounded_logits else "mixers",
    )(fqa, fka, fv, qbias, gq, gk, gv, gog, misc, lrw_pad, lrb, ong)


def _post_kernel(x_ref, yg_ref, yf_ref, p_ref, wo_ref, g2_ref, wup_ref, cw_ref, cb_ref, wdn_ref,
                 g3_ref, wpg_ref, bpg_ref, wpe_ref, gpe_ref, o_ref,
                 x1_scr, h2_scr, u_scr, act_scr, carry_scr, acc_scr):
    tm = x_ref.shape[0]
    d_ff = wdn_ref.shape[0]
    n_chunks = d_ff // FFN_CHUNK
    n_slabs = 2 * FFN_CHUNK // LANES
    half = n_slabs // 2
    dvw = yg_ref.shape[1]
    first_tile = pl.program_id(1) == 0

    @pl.when(first_tile)
    def _():
        carry_scr[...] = jnp.zeros_like(carry_scr)

    o_ref[...] = _rms(_dot(p_ref[...].astype(BF16), wpe_ref[...]), gpe_ref[...])

    x1 = x_ref[...] + _dot(yg_ref[...], wo_ref[:dvw, :]) + _dot(yf_ref[...], wo_ref[dvw:, :])
    x1_scr[...] = x1
    h2_scr[...] = _rms(x1, g2_ref[...]).astype(BF16)
    acc_scr[...] = jnp.zeros_like(acc_scr)

    for j in range(n_chunks + DOWN_LAG):
        if j < n_chunks:
            ub = u_scr.at[j % 2]
            col0 = [(c // half) * d_ff + j * FFN_CHUNK + (c % half) * LANES for c in range(n_slabs)]
            h2 = h2_scr[...]
            for part in range(2):
                u = _dot(h2, wup_ref[:, col0[part * half]:col0[part * half] + FFN_CHUNK])
                for c in range(part * half, (part + 1) * half):
                    ub[c, :CARRY_ROWS, :] = carry_scr[j, c]
                    ub[c, CARRY_ROWS:, :] = u[:, (c % half) * LANES:(c % half + 1) * LANES]
                    carry_scr[j, c] = ub[c, tm:, :]
        if j >= DOWN_LAG:
            jd = j - DOWN_LAG
            acc_scr[...] += _dot(act_scr[jd % (DOWN_LAG + 1)], wdn_ref[jd * FFN_CHUNK:(jd + 1) * FFN_CHUNK, :])
        if j < n_chunks:
            def conv(c):
                cs = slice(col0[c], col0[c] + LANES)
                return (cb_ref[:, cs]
                        + ub[c, CARRY_ROWS - 2:CARRY_ROWS - 2 + tm, :] * cw_ref[0:1, cs]
                        + ub[c, CARRY_ROWS - 1:CARRY_ROWS - 1 + tm, :] * cw_ref[1:2, cs]
                        + ub[c, CARRY_ROWS:, :] * cw_ref[2:3, cs])
            for c in range(half):
                gate = conv(c)
                act = gate * jax.nn.sigmoid(gate) * conv(c + half)
                act_scr[j % (DOWN_LAG + 1), :, c * LANES:(c + 1) * LANES] = act.astype(BF16)

    x2 = x1_scr[...] + acc_scr[...]
    gate = jax.nn.sigmoid(_dot(_rms(x2, g3_ref[...]).astype(BF16), wpg_ref[...]) + bpg_ref[...])
    o_ref[...] = x2 + gate * o_ref[...]


def _post(x2d, yg, yf, p2d, wo, g2, wup, cw, cb, wdn, g3, wpg, bpg, wpe, gpe, b, s, tm):
    d = x2d.shape[1]
    nt = s // tm
    tile = lambda w: pl.BlockSpec((tm, w), lambda i, j: (i * nt + j, 0))
    resident = lambda a: pl.BlockSpec(a.shape, lambda i, j: (0,) * a.ndim, pipeline_mode=pl.Buffered(1))
    n_chunks = wdn.shape[0] // FFN_CHUNK
    n_slabs = 2 * FFN_CHUNK // LANES
    return pl.pallas_call(
        _post_kernel,
        grid=(b, nt),
        in_specs=[tile(d), tile(yg.shape[1]), tile(yf.shape[1]), tile(p2d.shape[1])]
                 + [resident(a) for a in (wo, g2, wup, cw, cb, wdn, g3, wpg, bpg, wpe, gpe)],
        out_specs=tile(d),
        out_shape=jax.ShapeDtypeStruct(x2d.shape, F32),
        scratch_shapes=[
            pltpu.VMEM((tm, d), F32),
            pltpu.VMEM((tm, d), BF16),
            pltpu.VMEM((2, n_slabs, CARRY_ROWS + tm, LANES), F32),
            pltpu.VMEM((DOWN_LAG + 1, tm, FFN_CHUNK), BF16),
            pltpu.VMEM((n_chunks, n_slabs, CARRY_ROWS, LANES), F32),
            pltpu.VMEM((tm, d), F32),
        ],
        compiler_params=pltpu.CompilerParams(dimension_semantics=("arbitrary", "arbitrary"),
                                             vmem_limit_bytes=VMEM_LIMIT),
        name="post",
    )(x2d, yg, yf, p2d, wo, g2, wup, cw, cb, wdn, g3, wpg, bpg, wpe, gpe)


def _layer(x2d, p2d, b, s, norm1_g, w_in, lr_w, lr_b, onorm_g, b_f, qn_g, kn_g, w_o, norm2_g, w_up,
           conv_w, conv_b, w_down, norm3_g, w_pe, pe_norm_g, w_pg, b_pg):
    gqk = GLA_HEADS * GLA_DK
    row = lambda a: a.reshape(1, -1).astype(F32)
    bf_pad = jnp.zeros((1, LANES), F32).at[0, MISC_F0:MISC_F0 + FOX_HEADS].set(b_f)
    qg2 = row(jnp.tile(qn_g, 2)) * (FOX_DH ** -0.5 * LOG2E)
    (gq, gk, gv, gog, fqa, fka, fv, misc, qbias), (wo_b, wup_b, wdn_b, wpg_b, wpe_b) = _inproj(
        x2d, row(norm1_g), jnp.swapaxes(w_in, 1, 2).astype(F32), bf_pad, qg2, row(jnp.tile(kn_g, 2)),
        (w_o, w_up, w_down, w_pg, w_pe), s, tm=512)

    lrw_pad = jnp.zeros((LANES, gqk), F32).at[MISC_LR0:MISC_LR0 + GLA_LOWRANK].set(lr_w).astype(BF16)
    logit_bound = FOX_DH ** 0.5 * LOG2E * jnp.max(jnp.abs(qn_g)) * jnp.max(jnp.abs(kn_g))
    mix = lambda bounded: functools.partial(_mixers, b=b, s=s, bounded_logits=bounded)
    y_fox, y_gla = lax.cond(logit_bound <= FOX_BOUNDED_MAX_LOGIT, mix(True), mix(False),
                            fqa, fka, fv, qbias, gq, gk, gv, gog, misc, lrw_pad, row(lr_b), row(onorm_g))

    return _post(
        x2d, y_gla, y_fox, p2d, wo_b, row(norm2_g), wup_b, conv_w.astype(F32), row(conv_b), wdn_b,
        row(norm3_g), wpg_b, row(b_pg), wpe_b, row(pe_norm_g), b, s, tm=512)


def kernel(x, p, norm1_g, w_in, gla_lr_w, gla_lr_b, gla_onorm_g, fox_b_f, fox_qnorm_g, fox_knorm_g, w_o, norm2_g, w_up, conv_w, conv_b, w_down, norm3_g, w_pe, pe_norm_g, w_pg, b_pg):
    b, s, d = x.shape
    x2d = x.reshape(b * s, d)
    for i in range(p.shape[0]):
        x2d = _layer(x2d, p[i].reshape(b * s, -1), b, s, norm1_g[i], w_in[i:i + 1], gla_lr_w[i], gla_lr_b[i],
                     gla_onorm_g[i], fox_b_f[i], fox_qnorm_g[i], fox_knorm_g[i], w_o[i], norm2_g[i], w_up[i],
                     conv_w[i], conv_b[i], w_down[i], norm3_g[i], w_pe[i], pe_norm_g[i], w_pg[i], b_pg[i])
    return x2d.reshape(b, s, d)
```

```python
import functools

import jax
import jax.numpy as jnp
from jax import lax
from jax.experimental import pallas as pl
from jax.experimental.pallas import tpu as pltpu

F32 = jnp.float32
BF16 = jnp.bfloat16

EPS = 1e-6
LOG2E = 1.4426950408889634
LANES = 128
BF16_SUBLANES = 16
GLA_HEADS, GLA_DK, GLA_DV = 4, 64, 128
GLA_LOWRANK = 16
GLA_INV_TAU = 1.0 / 16.0
GLA_CHUNK = 64
GLA_SUPER = 256
FOX_HEADS, FOX_DH = 8, 64
FOX_TILE = 256
FOX_SLOTS = 3
FOX_BLOCKS_AHEAD = 4
BIAS_PIECES = 3
FOX_BOUNDED_MAX_LOGIT = 60.0
FFN_CHUNK = 256
CARRY_ROWS = 8
DOWN_LAG = 2
MISC_F0 = 0
MISC_LR0 = 8
VMEM_LIMIT = 56 * 1024 * 1024


def _dot(a, b):
    return jnp.dot(a, b, preferred_element_type=F32)


def _dot_nt(a, b):
    return lax.dot_general(a, b, (((1,), (1,)), ((), ())), preferred_element_type=F32)


def _dot_tn(a, b):
    return lax.dot_general(a, b, (((0,), (0,)), ((), ())), preferred_element_type=F32)


def _log_sigmoid(z):
    return jnp.minimum(z, 0.0) - jnp.log(1.0 + jnp.exp(-jnp.abs(z)))


def _split_bf16(x, pieces):
    out = []
    for _ in range(pieces):
        p = x.astype(BF16)
        out.append(p)
        x = x - p.astype(F32)
    return out


def _rms(x, gain):
    ms = jnp.mean(x * x, axis=-1, keepdims=True)
    return x * lax.rsqrt(ms + EPS) * gain


def _inproj_kernel(x_ref, g_ref, win_ref, bf_ref, qg_ref, kg_ref, *rest, tiles_per_seq, n_cast):
    cast_in, rest = rest[:n_cast], rest[n_cast:]
    gq_ref, gk_ref, gv_ref, gog_ref, fqa_ref, fka_ref, fv_ref, misc_ref = rest[:8]
    cast_out = rest[8:8 + n_cast]
    carry_scr, h_scr, wg_ref, wf_ref, wm_ref = rest[8 + n_cast:]

    @pl.when(pl.program_id(0) == 0)
    def _():
        gla_w = wg_ref.shape[1]
        fox_w = wf_ref.shape[1]
        lr0 = gla_w
        fox0 = lr0 + GLA_LOWRANK
        f0 = fox0 + fox_w
        wg_ref[...] = win_ref[:gla_w, :].T.astype(BF16)
        wf_ref[...] = win_ref[fox0:f0, :].T.astype(BF16)
        pad = jnp.zeros((LANES - MISC_LR0 - GLA_LOWRANK, win_ref.shape[1]), F32)
        misc_t = jnp.concatenate([win_ref[f0:f0 + FOX_HEADS, :], win_ref[lr0:fox0, :], pad], axis=0)
        wm_ref[...] = misc_t.T.astype(BF16)

    _inproj_body(x_ref, g_ref, wg_ref, wf_ref, wm_ref, bf_ref, qg_ref, kg_ref, gq_ref, gk_ref, gv_ref, gog_ref,
                 fqa_ref, fka_ref, fv_ref, misc_ref, carry_scr, h_scr, tiles_per_seq)
    for w_ref, o_ref in zip(cast_in, cast_out):
        o_ref[...] = w_ref[...].astype(BF16)


def _inproj_body(x_ref, g_ref, wg_ref, wf_ref, wm_ref, bf_ref, qg_ref, kg_ref, gq_ref, gk_ref, gv_ref, gog_ref,
                 fqa_ref, fka_ref, fv_ref, misc_ref, carry_scr, h_scr, tiles_per_seq):
    tm = x_ref.shape[0]
    fw = FOX_HEADS * FOX_DH

    @pl.when(pl.program_id(0) % tiles_per_seq == 0)
    def _():
        carry_scr[...] = jnp.zeros_like(carry_scr)

    h_scr[...] = _rms(x_ref[...], g_ref[...]).astype(BF16)
    h = h_scr[...]
    misc = _dot(h, wm_ref[...])
    misc_ref[...] = misc
    q_all = _dot(h, wf_ref[:, :fw])
    k_all = _dot(h, wf_ref[:, fw:2 * fw])

    blk = FOX_TILE
    trow = lax.broadcasted_iota(jnp.int32, (blk, blk), 0)
    tcol = lax.broadcasted_iota(jnp.int32, (blk, blk), 1)
    tri = (tcol <= trow).astype(BF16)
    carry = carry_scr[...]
    c_blocks = []
    for r0 in range(0, tm, blk):
        cb = carry
        for piece in _split_bf16(_log_sigmoid(misc[r0:r0 + blk] + bf_ref[...]), 3):
            cb = cb + _dot(tri, piece)
        c_blocks.append(cb)
        carry = cb[blk - 1:blk]
    carry_scr[...] = carry
    c_all = jnp.concatenate(c_blocks, axis=0)

    lane = lax.broadcasted_iota(jnp.int32, (tm, LANES), 1)
    lower = lane < FOX_DH

    def pair_norm(x, g):
        sq = x * x
        ms_lo = jnp.sum(jnp.where(lower, sq, 0.0), axis=-1, keepdims=True)
        ms_hi = jnp.sum(jnp.where(lower, 0.0, sq), axis=-1, keepdims=True)
        ms = jnp.where(lower, ms_lo, ms_hi) * (1.0 / FOX_DH)
        return x * lax.rsqrt(ms + EPS) * g

    fv_ref[...] = _dot(h, wf_ref[:, 2 * fw:]).astype(fv_ref.dtype)
    off = 0
    for ref in (gq_ref, gk_ref, gv_ref, gog_ref):
        n = ref.shape[-1]
        ref[...] = _dot(h, wg_ref[:, off:off + n]).astype(ref.dtype)
        off += n

    for pair in range(fw // LANES):
        ps = slice(pair * LANES, (pair + 1) * LANES)
        qn = pair_norm(q_all[:, ps], qg_ref[...])
        kn = pair_norm(k_all[:, ps], kg_ref[...])
        for e in range(2):
            head = 2 * pair + e
            in_head = (lane >= FOX_DH) if e else lower
            piece0 = 0 if e else FOX_DH
            c_col = jnp.sum(jnp.where(lane == MISC_F0 + head, c_all, 0.0), axis=-1, keepdims=True)
            bias = _split_bf16(c_col * (-LOG2E), BIAS_PIECES)
            ones_k = (lane >= piece0 + BIAS_PIECES) & (lane < piece0 + 2 * BIAS_PIECES)
            ones_q = (lane >= piece0) & (lane < piece0 + BIAS_PIECES)
            qa = jnp.where(in_head, qn, jnp.where(ones_q, 1.0, 0.0)).astype(BF16)
            ka = jnp.where(in_head, kn, jnp.where(ones_k, 1.0, 0.0)).astype(BF16)
            for j, p in enumerate(bias):
                ka = jnp.where(lane == piece0 + j, p, ka)
                qa = jnp.where(lane == piece0 + BIAS_PIECES + j, -p, qa)
            hs = slice(head * LANES, (head + 1) * LANES)
            fqa_ref[:, hs] = qa
            fka_ref[:, hs] = ka


def _cast_slab(rows, steps):
    for n_slabs in range(min(steps, rows // BF16_SUBLANES), 0, -1):
        if rows % n_slabs == 0 and (rows // n_slabs) % BF16_SUBLANES == 0:
            return rows // n_slabs
    raise ValueError(f"no bf16-tile aligned slab for {rows} rows")


def _inproj(x2d, g, w_in, bf_pad, qg2, kg2, later_weights, s, tm):
    t, d = x2d.shape
    steps = t // tm
    gla_w = 2 * GLA_HEADS * (GLA_DK + GLA_DV)
    fox_w = 3 * FOX_HEADS * FOX_DH
    assert w_in.shape == (1, gla_w + GLA_LOWRANK + fox_w + FOX_HEADS, d)
    assert MISC_F0 == 0 and MISC_LR0 == FOX_HEADS
    gqk, gvw = GLA_HEADS * GLA_DK, GLA_HEADS * GLA_DV
    widths = (gqk, gqk, gvw, gvw, FOX_HEADS * LANES, FOX_HEADS * LANES, FOX_HEADS * FOX_DH, LANES)
    dtypes = (BF16,) * 7 + (F32,)
    whole = lambda a: pl.BlockSpec(a.shape, lambda i: (0, 0))
    once = lambda a: pl.BlockSpec((None,) + a.shape[1:], lambda i: (0, 0, 0), pipeline_mode=pl.Buffered(1))

    def slab(w):
        rows = _cast_slab(w.shape[0], steps)
        last = w.shape[0] // rows - 1
        return pl.BlockSpec((rows, w.shape[1]), lambda i: (jnp.minimum(i, last), 0))

    outs = pl.pallas_call(
        functools.partial(_inproj_kernel, tiles_per_seq=s // tm, n_cast=len(later_weights)),
        grid=(steps,),
        in_specs=[pl.BlockSpec((tm, d), lambda i: (i, 0)), whole(g), once(w_in)]
                 + [whole(a) for a in (bf_pad, qg2, kg2)]
                 + [slab(w) for w in later_weights],
        out_specs=[pl.BlockSpec((tm, n), lambda i: (i, 0)) for n in widths] + [slab(w) for w in later_weights],
        out_shape=[jax.ShapeDtypeStruct((t, n), dt) for n, dt in zip(widths, dtypes)]
                  + [jax.ShapeDtypeStruct(w.shape, BF16) for w in later_weights],
        scratch_shapes=[pltpu.VMEM((1, LANES), F32),
                        pltpu.VMEM((tm, d), BF16),
                        pltpu.VMEM((d, gla_w), BF16),
                        pltpu.VMEM((d, fox_w), BF16),
                        pltpu.VMEM((d, LANES), BF16)],
        compiler_params=pltpu.CompilerParams(dimension_semantics=("arbitrary",), vmem_limit_bytes=VMEM_LIMIT),
        name="inproj",
    )(x2d, g, w_in, bf_pad, qg2, kg2, *later_weights)
    return outs[:len(widths)], outs[len(widths):]


def _gla_stages(q_ref, k_ref, v_ref, og_ref, misc_ref, lrw_ref, lrb_ref, ong_ref, y_ref, st_scr):
    s_len = q_ref.shape[0]
    r = GLA_SUPER
    n_sb = s_len // r
    n_ch = r // GLA_CHUNK
    dkw = GLA_HEADS * GLA_DK
    row = lax.broadcasted_iota(jnp.int32, (r, r), 0)
    col = lax.broadcasted_iota(jnp.int32, (r, r), 1)
    causal_bd = ((row // GLA_CHUNK) == (col // GLA_CHUNK)) & (col <= row)
    tri_bd = causal_bd.astype(BF16)
    head_of_col = lax.broadcasted_iota(jnp.int32, (r, dkw), 1) // GLA_DK
    heads = range(GLA_HEADS)
    sb_rows = lambda n: slice(n * r, (n + 1) * r)
    ch_rows = lambda c: slice(c * GLA_CHUNK, (c + 1) * GLA_CHUNK)
    v_cols = lambda h: slice(h * GLA_DV, (h + 1) * GLA_DV)

    la_pieces = []
    for n in range(n_sb):
        z = _dot(misc_ref[sb_rows(n), :].astype(BF16), lrw_ref[...]) + lrb_ref[...]
        la_pieces.append(_split_bf16(_log_sigmoid(z) * GLA_INV_TAU, 2))
        yield

    qh, ke, kd, decay = [], [], [], []
    for n in range(n_sb):
        b = _dot(tri_bd, la_pieces[n][0]) + _dot(tri_bd, la_pieces[n][1])
        b_last = [b[(c + 1) * GLA_CHUNK - 1:(c + 1) * GLA_CHUNK, :] for c in range(n_ch)]
        b_tot = jnp.concatenate([jnp.broadcast_to(bl, (GLA_CHUNK, dkw)) for bl in b_last], axis=0)
        q = q_ref[sb_rows(n), :].astype(F32)
        k = k_ref[sb_rows(n), :].astype(F32)
        qe = (q * (GLA_DK ** -0.5)) * jnp.exp(b)
        qh.append([jnp.where(head_of_col == h, qe, 0.0).astype(BF16) for h in heads])
        ke.append((k * jnp.exp(-b)).astype(BF16))
        kd_n = k * jnp.exp(b_tot - b)
        kd.append([jnp.where(head_of_col == h, kd_n, 0.0).astype(BF16) for h in heads])
        decay.append([jnp.exp(bl) for bl in b_last])
        yield

    a = []
    for n in range(n_sb):
        a.append([jnp.where(causal_bd, _dot_nt(qh[n][h], ke[n]), 0.0).astype(BF16) for h in heads])
        yield

    o_intra = []
    for n in range(n_sb):
        o_intra.append([_dot(a[n][h], v_ref[sb_rows(n), v_cols(h)]) for h in heads])
        yield

    ds = []
    for n in range(n_sb):
        for c in range(n_ch):
            rows = slice(n * r + c * GLA_CHUNK, n * r + (c + 1) * GLA_CHUNK)
            v_stack = jnp.concatenate([v_ref[rows, v_cols(h)] for h in heads], axis=0)
            kd_stack = jnp.concatenate([kd[n][h][ch_rows(c)] for h in heads], axis=0)
            ds.append(_dot_tn(v_stack, kd_stack))
        yield

    st = st_scr[...]
    o_inter = []
    for n in range(n_sb):
        for c in range(n_ch):
            qe_stack = jnp.concatenate([qh[n][h][ch_rows(c)] for h in heads], axis=0)
            o_inter.append(_dot_nt(qe_stack, st.astype(BF16)))
            st = st * decay[n][c] + ds[n * n_ch + c]
        yield
    st_scr[...] = st

    for n in range(n_sb):
        og = og_ref[sb_rows(n), :].astype(F32)
        gate = og * jax.nn.sigmoid(og)
        for h in heads:
            inter = jnp.concatenate([o_inter[n * n_ch + c][ch_rows(h)] for c in range(n_ch)], axis=0)
            o = o_intra[n][h] + inter
            y_ref[sb_rows(n), v_cols(h)] = (_rms(o, ong_ref[...]) * gate[:, v_cols(h)]).astype(y_ref.dtype)
        yield


def _fox_stages(qa_ref, ka_ref, v_ref, y_ref, vt_scr, s_scr):
    s_len = v_ref.shape[0]
    t = FOX_TILE
    n_tiles = s_len // t
    krow = lax.broadcasted_iota(jnp.int32, (t, t), 0)
    qcol = lax.broadcasted_iota(jnp.int32, (t, t), 1)
    orow = lax.broadcasted_iota(jnp.int32, (LANES, t), 0)
    vt_scr[...] = v_ref[...].astype(F32).T.astype(BF16)

    units = [(i, e) for i in range(n_tiles) for e in range(2)]
    n_slots = s_scr.shape[0]

    def scores(n):
        i, e = units[n]
        hs = slice(e * LANES, (e + 1) * LANES)
        qa = qa_ref[i * t:(i + 1) * t, hs]
        mp = None
        for kb in range(i + 1):
            sc = _dot_nt(ka_ref[kb * t:(kb + 1) * t, hs], qa)
            if kb == i:
                sc = jnp.where(krow <= qcol, sc, -jnp.inf)
            s_scr[n % n_slots, kb] = sc
            bm = jnp.max(sc.reshape(t // 8, 8, t), axis=0)
            mp = bm if mp is None else jnp.maximum(mp, bm)
        return jnp.max(mp, axis=0, keepdims=True)

    def weighted_values(n, m):
        lp = jnp.zeros((8, t), F32)
        acc = jnp.zeros((LANES, t), F32)
        for kb in range(units[n][0] + 1):
            p = jnp.exp2(s_scr[n % n_slots, kb] - m)
            lp = lp + jnp.sum(p.reshape(t // 8, 8, t), axis=0)
            acc = acc + _dot(vt_scr[:, kb * t:(kb + 1) * t], p.astype(BF16))
        return acc / jnp.sum(lp, axis=0, keepdims=True)

    ahead = n_slots - 1
    col_max = {}
    for n in range(min(ahead, len(units))):
        col_max[n] = scores(n)
        yield
    o_even = None
    for n, (i, e) in enumerate(units):
        if n + ahead < len(units):
            col_max[n + ahead] = scores(n + ahead)
            yield
        o_t = weighted_values(n, col_max.pop(n))
        if e == 0:
            o_even = o_t
        else:
            o_pair = jnp.where(orow < FOX_DH, o_even, o_t)
            y_ref[i * t:(i + 1) * t, :] = o_pair.T.astype(y_ref.dtype)
        yield


def _fox_stages_bounded(qa_ref, ka_ref, v_ref, y_ref, vt_scr):
    s_len = v_ref.shape[0]
    t = FOX_TILE
    n_tiles = s_len // t
    krow = lax.broadcasted_iota(jnp.int32, (t, t), 0)
    qcol = lax.broadcasted_iota(jnp.int32, (t, t), 1)
    orow = lax.broadcasted_iota(jnp.int32, (LANES, t), 0)
    vt_scr[...] = v_ref[...].astype(F32).T.astype(BF16)

    units = [(i, e) for i in range(n_tiles) for e in range(2)]
    blocks = [(n, kb) for n, (i, e) in enumerate(units) for kb in range(i + 1)]
    row_sum, acc = {}, {}
    o_even = [None]

    def values(n, kb, p):
        i, e = units[n]
        d = _dot(vt_scr[:, kb * t:(kb + 1) * t], p)
        acc[n] = d if n not in acc else acc[n] + d
        if kb == i:
            o_t = acc.pop(n) / jnp.sum(row_sum.pop(n), axis=0, keepdims=True)
            if e == 0:
                o_even[0] = o_t
            else:
                o_pair = jnp.where(orow < FOX_DH, o_even[0], o_t)
                y_ref[i * t:(i + 1) * t, :] = o_pair.T.astype(y_ref.dtype)

    pending = []
    for j, (n, kb) in enumerate(blocks):
        i, e = units[n]
        hs = slice(e * LANES, (e + 1) * LANES)
        sc = _dot_nt(ka_ref[kb * t:(kb + 1) * t, hs], qa_ref[i * t:(i + 1) * t, hs])
        if kb == i:
            sc = jnp.where(krow <= qcol, sc, -jnp.inf)
        p = jnp.exp2(sc)
        ps = jnp.sum(p.reshape(t // 8, 8, t), axis=0)
        row_sum[n] = ps if n not in row_sum else row_sum[n] + ps
        pending.append((n, kb, p.astype(BF16)))
        if len(pending) > FOX_BLOCKS_AHEAD:
            values(*pending.pop(0))
        if j % 2 == 1:
            yield
    for item in pending:
        values(*item)
    yield


def _mixers_kernel(qa_ref, ka_ref, fv_ref, gq_ref, gk_ref, gv_ref, gog_ref, misc_ref, lrw_ref, lrb_ref,
                   ong_ref, yf_ref, yg_ref, vt_scr, s_scr, st_scr, *, bounded_logits):
    @pl.when(pl.program_id(1) == 0)
    def _():
        st_scr[...] = jnp.zeros_like(st_scr)

    if bounded_logits:
        fox = _fox_stages_bounded(qa_ref, ka_ref, fv_ref, yf_ref, vt_scr)
    else:
        fox = _fox_stages(qa_ref, ka_ref, fv_ref, yf_ref, vt_scr, s_scr)
    gla = _gla_stages(gq_ref, gk_ref, gv_ref, gog_ref, misc_ref, lrw_ref, lrb_ref, ong_ref, yg_ref, st_scr)
    rota = (fox, fox, fox, gla) if bounded_logits else (fox, fox, gla)
    finished = set()
    turn = 0
    while len(finished) < 2:
        gen = rota[turn % len(rota)]
        turn += 1
        if gen not in finished:
            try:
                next(gen)
            except StopIteration:
                finished.add(gen)


def _mixers(fqa, fka, fv, gq, gk, gv, gog, misc, lrw_pad, lrb, ong, b, s, bounded_logits):
    n_pairs = FOX_HEADS * FOX_DH // LANES
    t = FOX_TILE
    dkw, dvw = GLA_HEADS * GLA_DK, GLA_HEADS * GLA_DV
    sq = s // n_pairs
    heads2 = pl.BlockSpec((s, 2 * LANES), lambda i, j: (i, j))
    pair = pl.BlockSpec((s, LANES), lambda i, j: (i, j))
    quarter = lambda w: pl.BlockSpec((sq, w), lambda i, j: (i * n_pairs + j, 0))
    whole = lambda a: pl.BlockSpec(a.shape, lambda i, j: (0, 0))
    return pl.pallas_call(
        functools.partial(_mixers_kernel, bounded_logits=bounded_logits),
        grid=(b, n_pairs),
        in_specs=[heads2, heads2, pair, quarter(dkw), quarter(dkw), quarter(dvw), quarter(dvw), quarter(LANES),
                  whole(lrw_pad), whole(lrb), whole(ong)],
        out_specs=[pair, quarter(dvw)],
        out_shape=[jax.ShapeDtypeStruct((b * s, FOX_HEADS * FOX_DH), BF16),
                   jax.ShapeDtypeStruct((b * s, dvw), BF16)],
        scratch_shapes=[
            pltpu.VMEM((LANES, s), BF16),
            pltpu.VMEM((FOX_SLOTS, s // t, t, t), F32),
            pltpu.VMEM((GLA_DV, dkw), F32),
        ],
        compiler_params=pltpu.CompilerParams(dimension_semantics=("arbitrary", "arbitrary"),
                                             vmem_limit_bytes=VMEM_LIMIT),
        name="mixers_bounded" if bounded_logits else "mixers",
    )(fqa, fka, fv, gq, gk, gv, gog, misc, lrw_pad, lrb, ong)


def _post_kernel(x_ref, yg_ref, yf_ref, p_ref, wo_ref, g2_ref, wup_ref, cw_ref, cb_ref, wdn_ref,
                 g3_ref, wpg_ref, bpg_ref, wpe_ref, gpe_ref, o_ref,
                 x1_scr, h2_scr, u_scr, act_scr, carry_scr, acc_scr):
    tm = x_ref.shape[0]
    d_ff = wdn_ref.shape[0]
    n_chunks = d_ff // FFN_CHUNK
    n_slabs = 2 * FFN_CHUNK // LANES
    half = n_slabs // 2
    dvw = yg_ref.shape[1]
    first_tile = pl.program_id(1) == 0

    @pl.when(first_tile)
    def _():
        carry_scr[...] = jnp.zeros_like(carry_scr)

    o_ref[...] = _rms(_dot(p_ref[...].astype(BF16), wpe_ref[...]), gpe_ref[...])

    x1 = x_ref[...] + _dot(yg_ref[...], wo_ref[:dvw, :]) + _dot(yf_ref[...], wo_ref[dvw:, :])
    x1_scr[...] = x1
    h2_scr[...] = _rms(x1, g2_ref[...]).astype(BF16)
    acc_scr[...] = jnp.zeros_like(acc_scr)

    for j in range(n_chunks + DOWN_LAG):
        if j < n_chunks:
            ub = u_scr.at[j % 2]
            col0 = [(c // half) * d_ff + j * FFN_CHUNK + (c % half) * LANES for c in range(n_slabs)]
            h2 = h2_scr[...]
            for part in range(2):
                u = _dot(h2, wup_ref[:, col0[part * half]:col0[part * half] + FFN_CHUNK])
                for c in range(part * half, (part + 1) * half):
                    ub[c, :CARRY_ROWS, :] = carry_scr[j, c]
                    ub[c, CARRY_ROWS:, :] = u[:, (c % half) * LANES:(c % half + 1) * LANES]
                    carry_scr[j, c] = ub[c, tm:, :]
        if j >= DOWN_LAG:
            jd = j - DOWN_LAG
            acc_scr[...] += _dot(act_scr[jd % (DOWN_LAG + 1)], wdn_ref[jd * FFN_CHUNK:(jd + 1) * FFN_CHUNK, :])
        if j < n_chunks:
            def conv(c):
                cs = slice(col0[c], col0[c] + LANES)
                return (cb_ref[:, cs]
                        + ub[c, CARRY_ROWS - 2:CARRY_ROWS - 2 + tm, :] * cw_ref[0:1, cs]
                        + ub[c, CARRY_ROWS - 1:CARRY_ROWS - 1 + tm, :] * cw_ref[1:2, cs]
                        + ub[c, CARRY_ROWS:, :] * cw_ref[2:3, cs])
            for c in range(half):
                gate = conv(c)
                act = gate * jax.nn.sigmoid(gate) * conv(c + half)
                act_scr[j % (DOWN_LAG + 1), :, c * LANES:(c + 1) * LANES] = act.astype(BF16)

    x2 = x1_scr[...] + acc_scr[...]
    gate = jax.nn.sigmoid(_dot(_rms(x2, g3_ref[...]).astype(BF16), wpg_ref[...]) + bpg_ref[...])
    o_ref[...] = x2 + gate * o_ref[...]


def _post(x2d, yg, yf, p2d, wo, g2, wup, cw, cb, wdn, g3, wpg, bpg, wpe, gpe, b, s, tm):
    d = x2d.shape[1]
    nt = s // tm
    tile = lambda w: pl.BlockSpec((tm, w), lambda i, j: (i * nt + j, 0))
    resident = lambda a: pl.BlockSpec(a.shape, lambda i, j: (0,) * a.ndim, pipeline_mode=pl.Buffered(1))
    n_chunks = wdn.shape[0] // FFN_CHUNK
    n_slabs = 2 * FFN_CHUNK // LANES
    return pl.pallas_call(
        _post_kernel,
        grid=(b, nt),
        in_specs=[tile(d), tile(yg.shape[1]), tile(yf.shape[1]), tile(p2d.shape[1])]
                 + [resident(a) for a in (wo, g2, wup, cw, cb, wdn, g3, wpg, bpg, wpe, gpe)],
        out_specs=tile(d),
        out_shape=jax.ShapeDtypeStruct(x2d.shape, F32),
        scratch_shapes=[
            pltpu.VMEM((tm, d), F32),
            pltpu.VMEM((tm, d), BF16),
            pltpu.VMEM((2, n_slabs, CARRY_ROWS + tm, LANES), F32),
            pltpu.VMEM((DOWN_LAG + 1, tm, FFN_CHUNK), BF16),
            pltpu.VMEM((n_chunks, n_slabs, CARRY_ROWS, LANES), F32),
            pltpu.VMEM((tm, d), F32),
        ],
        compiler_params=pltpu.CompilerParams(dimension_semantics=("arbitrary", "arbitrary"),
                                             vmem_limit_bytes=VMEM_LIMIT),
        name="post",
    )(x2d, yg, yf, p2d, wo, g2, wup, cw, cb, wdn, g3, wpg, bpg, wpe, gpe)


def _layer(x2d, p2d, b, s, norm1_g, w_in, lr_w, lr_b, onorm_g, b_f, qn_g, kn_g, w_o, norm2_g, w_up,
           conv_w, conv_b, w_down, norm3_g, w_pe, pe_norm_g, w_pg, b_pg):
    gqk = GLA_HEADS * GLA_DK
    row = lambda a: a.reshape(1, -1).astype(F32)
    bf_pad = jnp.zeros((1, LANES), F32).at[0, MISC_F0:MISC_F0 + FOX_HEADS].set(b_f)
    qg2 = row(jnp.tile(qn_g, 2)) * (FOX_DH ** -0.5 * LOG2E)
    (gq, gk, gv, gog, fqa, fka, fv, misc), (wo_b, wup_b, wdn_b, wpg_b, wpe_b) = _inproj(
        x2d, row(norm1_g), jnp.swapaxes(w_in, 1, 2).astype(F32), bf_pad, qg2, row(jnp.tile(kn_g, 2)),
        (w_o, w_up, w_down, w_pg, w_pe), s, tm=512)

    lrw_pad = jnp.zeros((LANES, gqk), F32).at[MISC_LR0:MISC_LR0 + GLA_LOWRANK].set(lr_w).astype(BF16)
    logit_bound = FOX_DH ** 0.5 * LOG2E * jnp.max(jnp.abs(qn_g)) * jnp.max(jnp.abs(kn_g))
    mix = lambda bounded: functools.partial(_mixers, b=b, s=s, bounded_logits=bounded)
    y_fox, y_gla = lax.cond(logit_bound <= FOX_BOUNDED_MAX_LOGIT, mix(True), mix(False),
                            fqa, fka, fv, gq, gk, gv, gog, misc, lrw_pad, row(lr_b), row(onorm_g))

    return _post(
        x2d, y_gla, y_fox, p2d, wo_b, row(norm2_g), wup_b, conv_w.astype(F32), row(conv_b), wdn_b,
        row(norm3_g), wpg_b, row(b_pg), wpe_b, row(pe_norm_g), b, s, tm=512)


def kernel(x, p, norm1_g, w_in, gla_lr_w, gla_lr_b, gla_onorm_g, fox_b_f, fox_qnorm_g, fox_knorm_g, w_o, norm2_g, w_up, conv_w, conv_b, w_down, norm3_g, w_pe, pe_norm_g, w_pg, b_pg):
    b, s, d = x.shape
    x2d = x.reshape(b * s, d)
    for i in range(p.shape[0]):
        x2d = _layer(x2d, p[i].reshape(b * s, -1), b, s, norm1_g[i], w_in[i:i + 1], gla_lr_w[i], gla_lr_b[i],
                     gla_onorm_g[i], fox_b_f[i], fox_qnorm_g[i], fox_knorm_g[i], w_o[i], norm2_g[i], w_up[i],
                     conv_w[i], conv_b[i], w_down[i], norm3_g[i], w_pe[i], pe_norm_g[i], w_pg[i], b_pg[i])
    return x2d.reshape(b, s, d)
```

```python
import functools

import jax
import jax.numpy as jnp
from jax import lax
from jax.experimental import pallas as pl
from jax.experimental.pallas import tpu as pltpu

F32 = jnp.float32
BF16 = jnp.bfloat16

EPS = 1e-6
LOG2E = 1.4426950408889634
LANES = 128
BF16_SUBLANES = 16
GLA_HEADS, GLA_DK, GLA_DV = 4, 64, 128
GLA_LOWRANK = 16
GLA_INV_TAU = 1.0 / 16.0
GLA_CHUNK = 64
GLA_SUPER = 256
FOX_HEADS, FOX_DH = 8, 64
FOX_TILE = 256
FOX_SLOTS = 3
FOX_BLOCKS_AHEAD = 4
FOX_BOUNDED_MAX_LOGIT = 60.0
FFN_CHUNK = 256
CARRY_ROWS = 8
DOWN_LAG = 2
MISC_F0 = 0
MISC_LR0 = 8
VMEM_LIMIT = 56 * 1024 * 1024


def _dot(a, b):
    return jnp.dot(a, b, preferred_element_type=F32)


def _dot_nt(a, b):
    return lax.dot_general(a, b, (((1,), (1,)), ((), ())), preferred_element_type=F32)


def _dot_tn(a, b):
    return lax.dot_general(a, b, (((0,), (0,)), ((), ())), preferred_element_type=F32)


def _log_sigmoid(z):
    return jnp.minimum(z, 0.0) - jnp.log(1.0 + jnp.exp(-jnp.abs(z)))


def _split_bf16(x, pieces):
    out = []
    for _ in range(pieces):
        p = x.astype(BF16)
        out.append(p)
        x = x - p.astype(F32)
    return out


def _rms(x, gain):
    ms = jnp.mean(x * x, axis=-1, keepdims=True)
    return x * lax.rsqrt(ms + EPS) * gain


def _inproj_kernel(x_ref, g_ref, win_ref, bf_ref, qg_ref, kg_ref, *rest, tiles_per_seq, n_cast):
    cast_in, rest = rest[:n_cast], rest[n_cast:]
    gq_ref, gk_ref, gv_ref, gog_ref, fqa_ref, fka_ref, fv_ref, misc_ref, qbias_ref = rest[:9]
    cast_out = rest[9:9 + n_cast]
    carry_scr, h_scr, wg_ref, wf_ref, wm_ref = rest[9 + n_cast:]

    @pl.when(pl.program_id(0) == 0)
    def _():
        gla_w = wg_ref.shape[1]
        fox_w = wf_ref.shape[1]
        lr0 = gla_w
        fox0 = lr0 + GLA_LOWRANK
        f0 = fox0 + fox_w
        wg_ref[...] = win_ref[:gla_w, :].T.astype(BF16)
        wf_ref[...] = win_ref[fox0:f0, :].T.astype(BF16)
        pad = jnp.zeros((LANES - MISC_LR0 - GLA_LOWRANK, win_ref.shape[1]), F32)
        misc_t = jnp.concatenate([win_ref[f0:f0 + FOX_HEADS, :], win_ref[lr0:fox0, :], pad], axis=0)
        wm_ref[...] = misc_t.T.astype(BF16)

    _inproj_body(x_ref, g_ref, wg_ref, wf_ref, wm_ref, bf_ref, qg_ref, kg_ref, gq_ref, gk_ref, gv_ref, gog_ref,
                 fqa_ref, fka_ref, fv_ref, misc_ref, qbias_ref, carry_scr, h_scr, tiles_per_seq)
    for w_ref, o_ref in zip(cast_in, cast_out):
        o_ref[...] = w_ref[...].astype(BF16)


def _inproj_body(x_ref, g_ref, wg_ref, wf_ref, wm_ref, bf_ref, qg_ref, kg_ref, gq_ref, gk_ref, gv_ref, gog_ref,
                 fqa_ref, fka_ref, fv_ref, misc_ref, qbias_ref, carry_scr, h_scr, tiles_per_seq):
    tm = x_ref.shape[0]
    fw = FOX_HEADS * FOX_DH

    @pl.when(pl.program_id(0) % tiles_per_seq == 0)
    def _():
        carry_scr[...] = jnp.zeros_like(carry_scr)

    h_scr[...] = _rms(x_ref[...], g_ref[...]).astype(BF16)
    h = h_scr[...]
    misc = _dot(h, wm_ref[...])
    misc_ref[...] = misc
    q_all = _dot(h, wf_ref[:, :fw])
    k_all = _dot(h, wf_ref[:, fw:2 * fw])

    blk = FOX_TILE
    trow = lax.broadcasted_iota(jnp.int32, (blk, blk), 0)
    tcol = lax.broadcasted_iota(jnp.int32, (blk, blk), 1)
    tri = (tcol <= trow).astype(BF16)
    carry = carry_scr[...]
    c_blocks = []
    for r0 in range(0, tm, blk):
        cb = carry
        for piece in _split_bf16(_log_sigmoid(misc[r0:r0 + blk] + bf_ref[...]), 3):
            cb = cb + _dot(tri, piece)
        c_blocks.append(cb)
        carry = cb[blk - 1:blk]
    carry_scr[...] = carry
    c_all = jnp.concatenate(c_blocks, axis=0)
    qbias_ref[...] = (c_all * (-LOG2E)).T

    lane = lax.broadcasted_iota(jnp.int32, (tm, LANES), 1)
    lower = lane < FOX_DH

    def pair_norm(x, g):
        sq = x * x
        ms_lo = jnp.sum(jnp.where(lower, sq, 0.0), axis=-1, keepdims=True)
        ms_hi = jnp.sum(jnp.where(lower, 0.0, sq), axis=-1, keepdims=True)
        ms = jnp.where(lower, ms_lo, ms_hi) * (1.0 / FOX_DH)
        return x * lax.rsqrt(ms + EPS) * g

    fv_ref[...] = _dot(h, wf_ref[:, 2 * fw:]).astype(fv_ref.dtype)
    off = 0
    for ref in (gq_ref, gk_ref, gv_ref, gog_ref):
        n = ref.shape[-1]
        ref[...] = _dot(h, wg_ref[:, off:off + n]).astype(ref.dtype)
        off += n

    for pair in range(fw // LANES):
        ps = slice(pair * LANES, (pair + 1) * LANES)
        qn = pair_norm(q_all[:, ps], qg_ref[...])
        kn = pair_norm(k_all[:, ps], kg_ref[...])
        for e in range(2):
            head = 2 * pair + e
            in_head = (lane >= FOX_DH) if e else lower
            piece0 = 0 if e else FOX_DH
            c_col = jnp.sum(jnp.where(lane == MISC_F0 + head, c_all, 0.0), axis=-1, keepdims=True)
            is_piece = (lane >= piece0) & (lane < piece0 + 3)
            hs = slice(head * LANES, (head + 1) * LANES)
            fqa_ref[:, hs] = jnp.where(in_head, qn, jnp.where(is_piece, 1.0, 0.0)).astype(BF16)
            ka = jnp.where(in_head, kn, 0.0).astype(BF16)
            for j, p in enumerate(_split_bf16(c_col * (-LOG2E), 3)):
                ka = jnp.where(lane == piece0 + j, p, ka)
            fka_ref[:, hs] = ka


def _cast_slab(rows, steps):
    for n_slabs in range(min(steps, rows // BF16_SUBLANES), 0, -1):
        if rows % n_slabs == 0 and (rows // n_slabs) % BF16_SUBLANES == 0:
            return rows // n_slabs
    raise ValueError(f"no bf16-tile aligned slab for {rows} rows")


def _inproj(x2d, g, w_in, bf_pad, qg2, kg2, later_weights, s, tm):
    t, d = x2d.shape
    steps = t // tm
    gla_w = 2 * GLA_HEADS * (GLA_DK + GLA_DV)
    fox_w = 3 * FOX_HEADS * FOX_DH
    assert w_in.shape == (1, gla_w + GLA_LOWRANK + fox_w + FOX_HEADS, d)
    assert MISC_F0 == 0 and MISC_LR0 == FOX_HEADS
    gqk, gvw = GLA_HEADS * GLA_DK, GLA_HEADS * GLA_DV
    widths = (gqk, gqk, gvw, gvw, FOX_HEADS * LANES, FOX_HEADS * LANES, FOX_HEADS * FOX_DH, LANES)
    dtypes = (BF16,) * 7 + (F32,)
    whole = lambda a: pl.BlockSpec(a.shape, lambda i: (0, 0))
    once = lambda a: pl.BlockSpec((None,) + a.shape[1:], lambda i: (0, 0, 0), pipeline_mode=pl.Buffered(1))

    def slab(w):
        rows = _cast_slab(w.shape[0], steps)
        last = w.shape[0] // rows - 1
        return pl.BlockSpec((rows, w.shape[1]), lambda i: (jnp.minimum(i, last), 0))

    outs = pl.pallas_call(
        functools.partial(_inproj_kernel, tiles_per_seq=s // tm, n_cast=len(later_weights)),
        grid=(steps,),
        in_specs=[pl.BlockSpec((tm, d), lambda i: (i, 0)), whole(g), once(w_in)]
                 + [whole(a) for a in (bf_pad, qg2, kg2)]
                 + [slab(w) for w in later_weights],
        out_specs=[pl.BlockSpec((tm, n), lambda i: (i, 0)) for n in widths]
                  + [pl.BlockSpec((None, LANES, tm), lambda i: (i, 0, 0))]
                  + [slab(w) for w in later_weights],
        out_shape=[jax.ShapeDtypeStruct((t, n), dt) for n, dt in zip(widths, dtypes)]
                  + [jax.ShapeDtypeStruct((steps, LANES, tm), F32)]
                  + [jax.ShapeDtypeStruct(w.shape, BF16) for w in later_weights],
        scratch_shapes=[pltpu.VMEM((1, LANES), F32),
                        pltpu.VMEM((tm, d), BF16),
                        pltpu.VMEM((d, gla_w), BF16),
                        pltpu.VMEM((d, fox_w), BF16),
                        pltpu.VMEM((d, LANES), BF16)],
        compiler_params=pltpu.CompilerParams(dimension_semantics=("arbitrary",), vmem_limit_bytes=VMEM_LIMIT),
        name="inproj",
    )(x2d, g, w_in, bf_pad, qg2, kg2, *later_weights)
    return outs[:len(widths) + 1], outs[len(widths) + 1:]


def _gla_stages(q_ref, k_ref, v_ref, og_ref, misc_ref, lrw_ref, lrb_ref, ong_ref, y_ref, st_scr):
    s_len = q_ref.shape[0]
    r = GLA_SUPER
    n_sb = s_len // r
    n_ch = r // GLA_CHUNK
    dkw = GLA_HEADS * GLA_DK
    row = lax.broadcasted_iota(jnp.int32, (r, r), 0)
    col = lax.broadcasted_iota(jnp.int32, (r, r), 1)
    causal_bd = ((row // GLA_CHUNK) == (col // GLA_CHUNK)) & (col <= row)
    tri_bd = causal_bd.astype(BF16)
    head_of_col = lax.broadcasted_iota(jnp.int32, (r, dkw), 1) // GLA_DK
    heads = range(GLA_HEADS)
    sb_rows = lambda n: slice(n * r, (n + 1) * r)
    ch_rows = lambda c: slice(c * GLA_CHUNK, (c + 1) * GLA_CHUNK)
    v_cols = lambda h: slice(h * GLA_DV, (h + 1) * GLA_DV)

    la_pieces = []
    for n in range(n_sb):
        z = _dot(misc_ref[sb_rows(n), :].astype(BF16), lrw_ref[...]) + lrb_ref[...]
        la_pieces.append(_split_bf16(_log_sigmoid(z) * GLA_INV_TAU, 2))
        yield

    qh, ke, kd, decay = [], [], [], []
    for n in range(n_sb):
        b = _dot(tri_bd, la_pieces[n][0]) + _dot(tri_bd, la_pieces[n][1])
        b_last = [b[(c + 1) * GLA_CHUNK - 1:(c + 1) * GLA_CHUNK, :] for c in range(n_ch)]
        b_tot = jnp.concatenate([jnp.broadcast_to(bl, (GLA_CHUNK, dkw)) for bl in b_last], axis=0)
        q = q_ref[sb_rows(n), :].astype(F32)
        k = k_ref[sb_rows(n), :].astype(F32)
        qe = (q * (GLA_DK ** -0.5)) * jnp.exp(b)
        qh.append([jnp.where(head_of_col == h, qe, 0.0).astype(BF16) for h in heads])
        ke.append((k * jnp.exp(-b)).astype(BF16))
        kd_n = k * jnp.exp(b_tot - b)
        kd.append([jnp.where(head_of_col == h, kd_n, 0.0).astype(BF16) for h in heads])
        decay.append([jnp.exp(bl) for bl in b_last])
        yield

    a = []
    for n in range(n_sb):
        a.append([jnp.where(causal_bd, _dot_nt(qh[n][h], ke[n]), 0.0).astype(BF16) for h in heads])
        yield

    o_intra = []
    for n in range(n_sb):
        o_intra.append([_dot(a[n][h], v_ref[sb_rows(n), v_cols(h)]) for h in heads])
        yield

    ds = []
    for n in range(n_sb):
        for c in range(n_ch):
            rows = slice(n * r + c * GLA_CHUNK, n * r + (c + 1) * GLA_CHUNK)
            v_stack = jnp.concatenate([v_ref[rows, v_cols(h)] for h in heads], axis=0)
            kd_stack = jnp.concatenate([kd[n][h][ch_rows(c)] for h in heads], axis=0)
            ds.append(_dot_tn(v_stack, kd_stack))
        yield

    st = st_scr[...]
    o_inter = []
    for n in range(n_sb):
        for c in range(n_ch):
            qe_stack = jnp.concatenate([qh[n][h][ch_rows(c)] for h in heads], axis=0)
            o_inter.append(_dot_nt(qe_stack, st.astype(BF16)))
            st = st * decay[n][c] + ds[n * n_ch + c]
        yield
    st_scr[...] = st

    for n in range(n_sb):
        og = og_ref[sb_rows(n), :].astype(F32)
        gate = og * jax.nn.sigmoid(og)
        for h in heads:
            inter = jnp.concatenate([o_inter[n * n_ch + c][ch_rows(h)] for c in range(n_ch)], axis=0)
            o = o_intra[n][h] + inter
            y_ref[sb_rows(n), v_cols(h)] = (_rms(o, ong_ref[...]) * gate[:, v_cols(h)]).astype(y_ref.dtype)
        yield


def _fox_stages(qa_ref, ka_ref, v_ref, y_ref, vt_scr, s_scr):
    s_len = v_ref.shape[0]
    t = FOX_TILE
    n_tiles = s_len // t
    krow = lax.broadcasted_iota(jnp.int32, (t, t), 0)
    qcol = lax.broadcasted_iota(jnp.int32, (t, t), 1)
    orow = lax.broadcasted_iota(jnp.int32, (LANES, t), 0)
    vt_scr[...] = v_ref[...].astype(F32).T.astype(BF16)

    units = [(i, e) for i in range(n_tiles) for e in range(2)]
    n_slots = s_scr.shape[0]

    def scores(n):
        i, e = units[n]
        hs = slice(e * LANES, (e + 1) * LANES)
        qa = qa_ref[i * t:(i + 1) * t, hs]
        mp = None
        for kb in range(i + 1):
            sc = _dot_nt(ka_ref[kb * t:(kb + 1) * t, hs], qa)
            if kb == i:
                sc = jnp.where(krow <= qcol, sc, -jnp.inf)
            s_scr[n % n_slots, kb] = sc
            bm = jnp.max(sc.reshape(t // 8, 8, t), axis=0)
            mp = bm if mp is None else jnp.maximum(mp, bm)
        return jnp.max(mp, axis=0, keepdims=True)

    def weighted_values(n, m):
        lp = jnp.zeros((8, t), F32)
        acc = jnp.zeros((LANES, t), F32)
        for kb in range(units[n][0] + 1):
            p = jnp.exp2(s_scr[n % n_slots, kb] - m)
            lp = lp + jnp.sum(p.reshape(t // 8, 8, t), axis=0)
            acc = acc + _dot(vt_scr[:, kb * t:(kb + 1) * t], p.astype(BF16))
        return acc / jnp.sum(lp, axis=0, keepdims=True)

    ahead = n_slots - 1
    col_max = {}
    for n in range(min(ahead, len(units))):
        col_max[n] = scores(n)
        yield
    o_even = None
    for n, (i, e) in enumerate(units):
        if n + ahead < len(units):
            col_max[n + ahead] = scores(n + ahead)
            yield
        o_t = weighted_values(n, col_max.pop(n))
        if e == 0:
            o_even = o_t
        else:
            o_pair = jnp.where(orow < FOX_DH, o_even, o_t)
            y_ref[i * t:(i + 1) * t, :] = o_pair.T.astype(y_ref.dtype)
        yield


def _fox_stages_bounded(qa_ref, ka_ref, v_ref, qbias_ref, y_ref, vt_scr, *, tile_rows):
    s_len = v_ref.shape[0]
    t = FOX_TILE
    n_tiles = s_len // t
    pair = pl.program_id(1)
    krow = lax.broadcasted_iota(jnp.int32, (t, t), 0)
    qcol = lax.broadcasted_iota(jnp.int32, (t, t), 1)
    orow = lax.broadcasted_iota(jnp.int32, (LANES, t), 0)
    vt_scr[...] = v_ref[...].astype(F32).T.astype(BF16)

    units = [(i, e) for i in range(n_tiles) for e in range(2)]
    blocks = [(n, kb) for n, (i, e) in enumerate(units) for kb in range(i + 1)]
    row_sum, acc = {}, {}
    o_even = [None]

    def values(n, kb, p):
        i, e = units[n]
        d = _dot(vt_scr[:, kb * t:(kb + 1) * t], p)
        acc[n] = d if n not in acc else acc[n] + d
        if kb == i:
            o_t = acc.pop(n) / jnp.sum(row_sum.pop(n), axis=0, keepdims=True)
            if e == 0:
                o_even[0] = o_t
            else:
                o_pair = jnp.where(orow < FOX_DH, o_even[0], o_t)
                y_ref[i * t:(i + 1) * t, :] = o_pair.T.astype(y_ref.dtype)

    pending = []
    for j, (n, kb) in enumerate(blocks):
        i, e = units[n]
        hs = slice(e * LANES, (e + 1) * LANES)
        q0 = i * t
        m = qbias_ref[q0 // tile_rows, pl.ds(2 * pair + e, 1), q0 % tile_rows:q0 % tile_rows + t]
        sc = _dot_nt(ka_ref[kb * t:(kb + 1) * t, hs], qa_ref[q0:q0 + t, hs])
        if kb == i:
            sc = jnp.where(krow <= qcol, sc, -jnp.inf)
        p = jnp.exp2(sc - m)
        ps = jnp.sum(p.reshape(t // 8, 8, t), axis=0)
        row_sum[n] = ps if n not in row_sum else row_sum[n] + ps
        pending.append((n, kb, p.astype(BF16)))
        if len(pending) > FOX_BLOCKS_AHEAD:
            values(*pending.pop(0))
        if j % 2 == 1:
            yield
    for item in pending:
        values(*item)
    yield


def _mixers_kernel(qa_ref, ka_ref, fv_ref, qbias_ref, gq_ref, gk_ref, gv_ref, gog_ref, misc_ref, lrw_ref, lrb_ref,
                   ong_ref, yf_ref, yg_ref, vt_scr, s_scr, st_scr, *, bounded_logits, tile_rows):
    @pl.when(pl.program_id(1) == 0)
    def _():
        st_scr[...] = jnp.zeros_like(st_scr)

    if bounded_logits:
        fox = _fox_stages_bounded(qa_ref, ka_ref, fv_ref, qbias_ref, yf_ref, vt_scr, tile_rows=tile_rows)
    else:
        fox = _fox_stages(qa_ref, ka_ref, fv_ref, yf_ref, vt_scr, s_scr)
    gla = _gla_stages(gq_ref, gk_ref, gv_ref, gog_ref, misc_ref, lrw_ref, lrb_ref, ong_ref, yg_ref, st_scr)
    rota = (fox, fox, fox, gla) if bounded_logits else (fox, fox, gla)
    finished = set()
    turn = 0
    while len(finished) < 2:
        gen = rota[turn % len(rota)]
        turn += 1
        if gen not in finished:
            try:
                next(gen)
            except StopIteration:
                finished.add(gen)


def _mixers(fqa, fka, fv, qbias, gq, gk, gv, gog, misc, lrw_pad, lrb, ong, b, s, bounded_logits):
    n_pairs = FOX_HEADS * FOX_DH // LANES
    t = FOX_TILE
    dkw, dvw = GLA_HEADS * GLA_DK, GLA_HEADS * GLA_DV
    sq = s // n_pairs
    assert s % t == 0 and sq % GLA_SUPER == 0, "sequence must split into FoX tiles and GLA super-blocks"
    tile_rows = qbias.shape[2]
    heads2 = pl.BlockSpec((s, 2 * LANES), lambda i, j: (i, j))
    pair = pl.BlockSpec((s, LANES), lambda i, j: (i, j))
    seq_bias = pl.BlockSpec((s // tile_rows, FOX_HEADS, tile_rows), lambda i, j: (i, 0, 0))
    quarter = lambda w: pl.BlockSpec((sq, w), lambda i, j: (i * n_pairs + j, 0))
    whole = lambda a: pl.BlockSpec(a.shape, lambda i, j: (0, 0))
    return pl.pallas_call(
        functools.partial(_mixers_kernel, bounded_logits=bounded_logits, tile_rows=tile_rows),
        grid=(b, n_pairs),
        in_specs=[heads2, heads2, pair, seq_bias,
                  quarter(dkw), quarter(dkw), quarter(dvw), quarter(dvw), quarter(LANES),
                  whole(lrw_pad), whole(lrb), whole(ong)],
        out_specs=[pair, quarter(dvw)],
        out_shape=[jax.ShapeDtypeStruct((b * s, FOX_HEADS * FOX_DH), BF16),
                   jax.ShapeDtypeStruct((b * s, dvw), BF16)],
        scratch_shapes=[
            pltpu.VMEM((LANES, s), BF16),
            pltpu.VMEM((FOX_SLOTS, s // t, t, t), F32),
            pltpu.VMEM((GLA_DV, dkw), F32),
        ],
        compiler_params=pltpu.CompilerParams(dimension_semantics=("arbitrary", "arbitrary"),
                                             vmem_limit_bytes=VMEM_LIMIT),
        name="mixers_bounded" if bounded_logits else "mixers",
    )(fqa, fka, fv, qbias, gq, gk, gv, gog, misc, lrw_pad, lrb, ong)


def _post_kernel(x_ref, yg_ref, yf_ref, p_ref, wo_ref, g2_ref, wup_ref, cw_ref, cb_ref, wdn_ref,
                 g3_ref, wpg_ref, bpg_ref, wpe_ref, gpe_ref, o_ref,
                 x1_scr, h2_scr, u_scr, act_scr, carry_scr, acc_scr):
    tm = x_ref.shape[0]
    d_ff = wdn_ref.shape[0]
    n_chunks = d_ff // FFN_CHUNK
    n_slabs = 2 * FFN_CHUNK // LANES
    half = n_slabs // 2
    dvw = yg_ref.shape[1]
    first_tile = pl.program_id(1) == 0

    @pl.when(first_tile)
    def _():
        carry_scr[...] = jnp.zeros_like(carry_scr)

    o_ref[...] = _rms(_dot(p_ref[...].astype(BF16), wpe_ref[...]), gpe_ref[...])

    x1 = x_ref[...] + _dot(yg_ref[...], wo_ref[:dvw, :]) + _dot(yf_ref[...], wo_ref[dvw:, :])
    x1_scr[...] = x1
    h2_scr[...] = _rms(x1, g2_ref[...]).astype(BF16)
    acc_scr[...] = jnp.zeros_like(acc_scr)

    for j in range(n_chunks + DOWN_LAG):
        if j < n_chunks:
            ub = u_scr.at[j % 2]
            col0 = [(c // half) * d_ff + j * FFN_CHUNK + (c % half) * LANES for c in range(n_slabs)]
            h2 = h2_scr[...]
            for part in range(2):
                u = _dot(h2, wup_ref[:, col0[part * half]:col0[part * half] + FFN_CHUNK])
                for c in range(part * half, (part + 1) * half):
                    ub[c, :CARRY_ROWS, :] = carry_scr[j, c]
                    ub[c, CARRY_ROWS:, :] = u[:, (c % half) * LANES:(c % half + 1) * LANES]
                    carry_scr[j, c] = ub[c, tm:, :]
        if j >= DOWN_LAG:
            jd = j - DOWN_LAG
            acc_scr[...] += _dot(act_scr[jd % (DOWN_LAG + 1)], wdn_ref[jd * FFN_CHUNK:(jd + 1) * FFN_CHUNK, :])
        if j < n_chunks:
            def conv(c):
                cs = slice(col0[c], col0[c] + LANES)
                return (cb_ref[:, cs]
                        + ub[c, CARRY_ROWS - 2:CARRY_ROWS - 2 + tm, :] * cw_ref[0:1, cs]
                        + ub[c, CARRY_ROWS - 1:CARRY_ROWS - 1 + tm, :] * cw_ref[1:2, cs]
                        + ub[c, CARRY_ROWS:, :] * cw_ref[2:3, cs])
            for c in range(half):
                gate = conv(c)
                act = gate * jax.nn.sigmoid(gate) * conv(c + half)
                act_scr[j % (DOWN_LAG + 1), :, c * LANES:(c + 1) * LANES] = act.astype(BF16)

    x2 = x1_scr[...] + acc_scr[...]
    gate = jax.nn.sigmoid(_dot(_rms(x2, g3_ref[...]).astype(BF16), wpg_ref[...]) + bpg_ref[...])
    o_ref[...] = x2 + gate * o_ref[...]


def _post(x2d, yg, yf, p2d, wo, g2, wup, cw, cb, wdn, g3, wpg, bpg, wpe, gpe, b, s, tm):
    d = x2d.shape[1]
    nt = s // tm
    tile = lambda w: pl.BlockSpec((tm, w), lambda i, j: (i * nt + j, 0))
    resident = lambda a: pl.BlockSpec(a.shape, lambda i, j: (0,) * a.ndim, pipeline_mode=pl.Buffered(1))
    n_chunks = wdn.shape[0] // FFN_CHUNK
    n_slabs = 2 * FFN_CHUNK // LANES
    return pl.pallas_call(
        _post_kernel,
        grid=(b, nt),
        in_specs=[tile(d), tile(yg.shape[1]), tile(yf.shape[1]), tile(p2d.shape[1])]
                 + [resident(a) for a in (wo, g2, wup, cw, cb, wdn, g3, wpg, bpg, wpe, gpe)],
        out_specs=tile(d),
        out_shape=jax.ShapeDtypeStruct(x2d.shape, F32),
        scratch_shapes=[
            pltpu.VMEM((tm, d), F32),
            pltpu.VMEM((tm, d), BF16),
            pltpu.VMEM((2, n_slabs, CARRY_ROWS + tm, LANES), F32),
            pltpu.VMEM((DOWN_LAG + 1, tm, FFN_CHUNK), BF16),
            pltpu.VMEM((n_chunks, n_slabs, CARRY_ROWS, LANES), F32),
            pltpu.VMEM((tm, d), F32),
        ],
        compiler_params=pltpu.CompilerParams(dimension_semantics=("arbitrary", "arbitrary"),
                                             vmem_limit_bytes=VMEM_LIMIT),
        name="post",
    )(x2d, yg, yf, p2d, wo, g2, wup, cw, cb, wdn, g3, wpg, bpg, wpe, gpe)


def _layer(x2d, p2d, b, s, norm1_g, w_in, lr_w, lr_b, onorm_g, b_f, qn_g, kn_g, w_o, norm2_g, w_up,
           conv_w, conv_b, w_down, norm3_g, w_pe, pe_norm_g, w_pg, b_pg):
    gqk = GLA_HEADS * GLA_DK
    row = lambda a: a.reshape(1, -1).astype(F32)
    bf_pad = jnp.zeros((1, LANES), F32).at[0, MISC_F0:MISC_F0 + FOX_HEADS].set(b_f)
    qg2 = row(jnp.tile(qn_g, 2)) * (FOX_DH ** -0.5 * LOG2E)
    (gq, gk, gv, gog, fqa, fka, fv, misc, qbias), (wo_b, wup_b, wdn_b, wpg_b, wpe_b) = _inproj(
        x2d, row(norm1_g), jnp.swapaxes(w_in, 1, 2).astype(F32), bf_pad, qg2, row(jnp.tile(kn_g, 2)),
        (w_o, w_up, w_down, w_pg, w_pe), s, tm=512)

    lrw_pad = jnp.zeros((LANES, gqk), F32).at[MISC_LR0:MISC_LR0 + GLA_LOWRANK].set(lr_w).astype(BF16)
    logit_bound = FOX_DH ** 0.5 * LOG2E * jnp.max(jnp.abs(qn_g)) * jnp.max(jnp.abs(kn_g))
    mix = lambda bounded: functools.partial(_mixers, b=b, s=s, bounded_logits=bounded)
    y_fox, y_gla = lax.cond(logit_bound <= FOX_BOUNDED_MAX_LOGIT, mix(True), mix(False),
                            fqa, fka, fv, qbias, gq, gk, gv, gog, misc, lrw_pad, row(lr_b), row(onorm_g))

    return _post(
        x2d, y_gla, y_fox, p2d, wo_b, row(norm2_g), wup_b, conv_w.astype(F32), row(conv_b), wdn_b,
        row(norm3_g), wpg_b, row(b_pg), wpe_b, row(pe_norm_g), b, s, tm=512)


def kernel(x, p, norm1_g, w_in, gla_lr_w, gla_lr_b, gla_onorm_g, fox_b_f, fox_qnorm_g, fox_knorm_g, w_o, norm2_g, w_up, conv_w, conv_b, w_down, norm3_g, w_pe, pe_norm_g, w_pg, b_pg):
    b, s, d = x.shape
    x2d = x.reshape(b * s, d)
    for i in range(p.shape[0]):
        x2d = _layer(x2d, p[i].reshape(b * s, -1), b, s, norm1_g[i], w_in[i:i + 1], gla_lr_w[i], gla_lr_b[i],
                     gla_onorm_g[i], fox_b_f[i], fox_qnorm_g[i], fox_knorm_g[i], w_o[i], norm2_g[i], w_up[i],
                     conv_w[i], conv_b[i], w_down[i], norm3_g[i], w_pe[i], pe_norm_g[i], w_pg[i], b_pg[i])
    return x2d.reshape(b, s, d)
```

```python
import functools

import jax
import jax.numpy as jnp
from jax import lax
from jax.experimental import pallas as pl
from jax.experimental.pallas import tpu as pltpu

F32 = jnp.float32
BF16 = jnp.bfloat16

EPS = 1e-6
LOG2E = 1.4426950408889634
LANES = 128
BF16_SUBLANES = 16
GLA_HEADS, GLA_DK, GLA_DV = 4, 64, 128
GLA_LOWRANK = 16
GLA_INV_TAU = 1.0 / 16.0
GLA_CHUNK = 64
GLA_SUPER = 256
FOX_HEADS, FOX_DH = 8, 64
FOX_TILE = 256
FOX_SLOTS = 3
FOX_BLOCKS_AHEAD = 4
FOX_BOUNDED_MAX_LOGIT = 60.0
FFN_CHUNK = 256
CARRY_ROWS = 8
DOWN_LAG = 2
MISC_F0 = 0
MISC_LR0 = 8
VMEM_LIMIT = 56 * 1024 * 1024


def _dot(a, b):
    return jnp.dot(a, b, preferred_element_type=F32)


def _dot_nt(a, b):
    return lax.dot_general(a, b, (((1,), (1,)), ((), ())), preferred_element_type=F32)


def _dot_tn(a, b):
    return lax.dot_general(a, b, (((0,), (0,)), ((), ())), preferred_element_type=F32)


def _log_sigmoid(z):
    return jnp.minimum(z, 0.0) - jnp.log(1.0 + jnp.exp(-jnp.abs(z)))


def _split_bf16(x, pieces):
    out = []
    for _ in range(pieces):
        p = x.astype(BF16)
        out.append(p)
        x = x - p.astype(F32)
    return out


def _rms(x, gain):
    ms = jnp.mean(x * x, axis=-1, keepdims=True)
    return x * lax.rsqrt(ms + EPS) * gain


def _inproj_kernel(x_ref, g_ref, win_ref, bf_ref, qg_ref, kg_ref, *rest, tiles_per_seq, n_cast):
    cast_in, rest = rest[:n_cast], rest[n_cast:]
    gq_ref, gk_ref, gv_ref, gog_ref, fqa_ref, fka_ref, fv_ref, misc_ref, qbias_ref = rest[:9]
    cast_out = rest[9:9 + n_cast]
    carry_scr, h_scr, wg_ref, wf_ref, wm_ref = rest[9 + n_cast:]

    @pl.when(pl.program_id(0) == 0)
    def _():
        gla_w = wg_ref.shape[1]
        fox_w = wf_ref.shape[1]
        lr0 = gla_w
        fox0 = lr0 + GLA_LOWRANK
        f0 = fox0 + fox_w
        wg_ref[...] = win_ref[:gla_w, :].T.astype(BF16)
        wf_ref[...] = win_ref[fox0:f0, :].T.astype(BF16)
        pad = jnp.zeros((LANES - MISC_LR0 - GLA_LOWRANK, win_ref.shape[1]), F32)
        misc_t = jnp.concatenate([win_ref[f0:f0 + FOX_HEADS, :], win_ref[lr0:fox0, :], pad], axis=0)
        wm_ref[...] = misc_t.T.astype(BF16)

    _inproj_body(x_ref, g_ref, wg_ref, wf_ref, wm_ref, bf_ref, qg_ref, kg_ref, gq_ref, gk_ref, gv_ref, gog_ref,
                 fqa_ref, fka_ref, fv_ref, misc_ref, qbias_ref, carry_scr, h_scr, tiles_per_seq)
    for w_ref, o_ref in zip(cast_in, cast_out):
        o_ref[...] = w_ref[...].astype(BF16)


def _inproj_body(x_ref, g_ref, wg_ref, wf_ref, wm_ref, bf_ref, qg_ref, kg_ref, gq_ref, gk_ref, gv_ref, gog_ref,
                 fqa_ref, fka_ref, fv_ref, misc_ref, qbias_ref, carry_scr, h_scr, tiles_per_seq):
    tm = x_ref.shape[0]
    fw = FOX_HEADS * FOX_DH

    @pl.when(pl.program_id(0) % tiles_per_seq == 0)
    def _():
        carry_scr[...] = jnp.zeros_like(carry_scr)

    h_scr[...] = _rms(x_ref[...], g_ref[...]).astype(BF16)
    h = h_scr[...]
    misc = _dot(h, wm_ref[...])
    misc_ref[...] = misc
    q_all = _dot(h, wf_ref[:, :fw])
    k_all = _dot(h, wf_ref[:, fw:2 * fw])

    blk = FOX_TILE
    trow = lax.broadcasted_iota(jnp.int32, (blk, blk), 0)
    tcol = lax.broadcasted_iota(jnp.int32, (blk, blk), 1)
    tri = (tcol <= trow).astype(BF16)
    n_blk = tm // blk
    logf = jnp.concatenate([_log_sigmoid(misc[r0:r0 + blk] + bf_ref[...]) for r0 in range(0, tm, blk)], axis=1)
    within = sum(_dot(tri, piece) for piece in _split_bf16(logf, 3))
    carry = carry_scr[...]
    c_blocks = []
    for i in range(n_blk):
        cb = within[:, i * LANES:(i + 1) * LANES] + carry
        c_blocks.append(cb)
        carry = cb[blk - 1:blk]
    carry_scr[...] = carry
    c_all = jnp.concatenate(c_blocks, axis=0)
    qbias_ref[...] = (c_all * (-LOG2E)).T

    lane = lax.broadcasted_iota(jnp.int32, (tm, LANES), 1)
    lower = lane < FOX_DH

    def pair_norm(x, g):
        sq = x * x
        ms_lo = jnp.sum(jnp.where(lower, sq, 0.0), axis=-1, keepdims=True)
        ms_hi = jnp.sum(jnp.where(lower, 0.0, sq), axis=-1, keepdims=True)
        ms = jnp.where(lower, ms_lo, ms_hi) * (1.0 / FOX_DH)
        return x * lax.rsqrt(ms + EPS) * g

    fv_ref[...] = _dot(h, wf_ref[:, 2 * fw:]).astype(fv_ref.dtype)
    off = 0
    for ref in (gq_ref, gk_ref, gv_ref, gog_ref):
        n = ref.shape[-1]
        ref[...] = _dot(h, wg_ref[:, off:off + n]).astype(ref.dtype)
        off += n

    for pair in range(fw // LANES):
        ps = slice(pair * LANES, (pair + 1) * LANES)
        qn = pair_norm(q_all[:, ps], qg_ref[...])
        kn = pair_norm(k_all[:, ps], kg_ref[...])
        for e in range(2):
            head = 2 * pair + e
            in_head = (lane >= FOX_DH) if e else lower
            piece0 = 0 if e else FOX_DH
            c_col = jnp.sum(jnp.where(lane == MISC_F0 + head, c_all, 0.0), axis=-1, keepdims=True)
            is_piece = (lane >= piece0) & (lane < piece0 + 3)
            hs = slice(head * LANES, (head + 1) * LANES)
            fqa_ref[:, hs] = jnp.where(in_head, qn, jnp.where(is_piece, 1.0, 0.0)).astype(BF16)
            ka = jnp.where(in_head, kn, 0.0).astype(BF16)
            for j, p in enumerate(_split_bf16(c_col * (-LOG2E), 3)):
                ka = jnp.where(lane == piece0 + j, p, ka)
            fka_ref[:, hs] = ka


def _cast_slab(rows, steps):
    for n_slabs in range(min(steps, rows // BF16_SUBLANES), 0, -1):
        if rows % n_slabs == 0 and (rows // n_slabs) % BF16_SUBLANES == 0:
            return rows // n_slabs
    raise ValueError(f"no bf16-tile aligned slab for {rows} rows")


def _inproj(x2d, g, w_in, bf_pad, qg2, kg2, later_weights, s, tm):
    t, d = x2d.shape
    steps = t // tm
    gla_w = 2 * GLA_HEADS * (GLA_DK + GLA_DV)
    fox_w = 3 * FOX_HEADS * FOX_DH
    assert w_in.shape == (1, gla_w + GLA_LOWRANK + fox_w + FOX_HEADS, d)
    assert MISC_F0 == 0 and MISC_LR0 == FOX_HEADS
    gqk, gvw = GLA_HEADS * GLA_DK, GLA_HEADS * GLA_DV
    widths = (gqk, gqk, gvw, gvw, FOX_HEADS * LANES, FOX_HEADS * LANES, FOX_HEADS * FOX_DH, LANES)
    dtypes = (BF16,) * 7 + (F32,)
    whole = lambda a: pl.BlockSpec(a.shape, lambda i: (0, 0))
    once = lambda a: pl.BlockSpec((None,) + a.shape[1:], lambda i: (0, 0, 0), pipeline_mode=pl.Buffered(1))

    def slab(w):
        rows = _cast_slab(w.shape[0], steps)
        last = w.shape[0] // rows - 1
        return pl.BlockSpec((rows, w.shape[1]), lambda i: (jnp.minimum(i, last), 0))

    outs = pl.pallas_call(
        functools.partial(_inproj_kernel, tiles_per_seq=s // tm, n_cast=len(later_weights)),
        grid=(steps,),
        in_specs=[pl.BlockSpec((tm, d), lambda i: (i, 0)), whole(g), once(w_in)]
                 + [whole(a) for a in (bf_pad, qg2, kg2)]
                 + [slab(w) for w in later_weights],
        out_specs=[pl.BlockSpec((tm, n), lambda i: (i, 0)) for n in widths]
                  + [pl.BlockSpec((None, LANES, tm), lambda i: (i, 0, 0))]
                  + [slab(w) for w in later_weights],
        out_shape=[jax.ShapeDtypeStruct((t, n), dt) for n, dt in zip(widths, dtypes)]
                  + [jax.ShapeDtypeStruct((steps, LANES, tm), F32)]
                  + [jax.ShapeDtypeStruct(w.shape, BF16) for w in later_weights],
        scratch_shapes=[pltpu.VMEM((1, LANES), F32),
                        pltpu.VMEM((tm, d), BF16),
                        pltpu.VMEM((d, gla_w), BF16),
                        pltpu.VMEM((d, fox_w), BF16),
                        pltpu.VMEM((d, LANES), BF16)],
        compiler_params=pltpu.CompilerParams(dimension_semantics=("arbitrary",), vmem_limit_bytes=VMEM_LIMIT),
        name="inproj",
    )(x2d, g, w_in, bf_pad, qg2, kg2, *later_weights)
    return outs[:len(widths) + 1], outs[len(widths) + 1:]


def _gla_stages(q_ref, k_ref, v_ref, og_ref, misc_ref, lrw_ref, lrb_ref, ong_ref, y_ref, st_scr):
    s_len = q_ref.shape[0]
    r = GLA_SUPER
    n_sb = s_len // r
    n_ch = r // GLA_CHUNK
    dkw = GLA_HEADS * GLA_DK
    row = lax.broadcasted_iota(jnp.int32, (r, r), 0)
    col = lax.broadcasted_iota(jnp.int32, (r, r), 1)
    causal_bd = ((row // GLA_CHUNK) == (col // GLA_CHUNK)) & (col <= row)
    tri_bd = causal_bd.astype(BF16)
    head_of_col = lax.broadcasted_iota(jnp.int32, (r, dkw), 1) // GLA_DK
    heads = range(GLA_HEADS)
    sb_rows = lambda n: slice(n * r, (n + 1) * r)
    ch_rows = lambda c: slice(c * GLA_CHUNK, (c + 1) * GLA_CHUNK)
    v_cols = lambda h: slice(h * GLA_DV, (h + 1) * GLA_DV)

    la_pieces = []
    for n in range(n_sb):
        z = _dot(misc_ref[sb_rows(n), :].astype(BF16), lrw_ref[...]) + lrb_ref[...]
        la_pieces.append(_split_bf16(_log_sigmoid(z) * GLA_INV_TAU, 2))
        yield

    qh, ke, kd, decay = [], [], [], []
    for n in range(n_sb):
        b = _dot(tri_bd, la_pieces[n][0]) + _dot(tri_bd, la_pieces[n][1])
        b_last = [b[(c + 1) * GLA_CHUNK - 1:(c + 1) * GLA_CHUNK, :] for c in range(n_ch)]
        b_tot = jnp.concatenate([jnp.broadcast_to(bl, (GLA_CHUNK, dkw)) for bl in b_last], axis=0)
        q = q_ref[sb_rows(n), :].astype(F32)
        k = k_ref[sb_rows(n), :].astype(F32)
        qe = (q * (GLA_DK ** -0.5)) * jnp.exp(b)
        qh.append([jnp.where(head_of_col == h, qe, 0.0).astype(BF16) for h in heads])
        ke.append((k * jnp.exp(-b)).astype(BF16))
        kd_n = k * jnp.exp(b_tot - b)
        kd.append([jnp.where(head_of_col == h, kd_n, 0.0).astype(BF16) for h in heads])
        decay.append([jnp.exp(bl) for bl in b_last])
        yield

    a = []
    for n in range(n_sb):
        a.append([jnp.where(causal_bd, _dot_nt(qh[n][h], ke[n]), 0.0).astype(BF16) for h in heads])
        yield

    o_intra = []
    for n in range(n_sb):
        o_intra.append([_dot(a[n][h], v_ref[sb_rows(n), v_cols(h)]) for h in heads])
        yield

    ds = []
    for n in range(n_sb):
        for c in range(n_ch):
            rows = slice(n * r + c * GLA_CHUNK, n * r + (c + 1) * GLA_CHUNK)
            v_stack = jnp.concatenate([v_ref[rows, v_cols(h)] for h in heads], axis=0)
            kd_stack = jnp.concatenate([kd[n][h][ch_rows(c)] for h in heads], axis=0)
            ds.append(_dot_tn(v_stack, kd_stack))
        yield

    st = st_scr[...]
    o_inter = []
    for n in range(n_sb):
        for c in range(n_ch):
            qe_stack = jnp.concatenate([qh[n][h][ch_rows(c)] for h in heads], axis=0)
            o_inter.append(_dot_nt(qe_stack, st.astype(BF16)))
            st = st * decay[n][c] + ds[n * n_ch + c]
        yield
    st_scr[...] = st

    for n in range(n_sb):
        og = og_ref[sb_rows(n), :].astype(F32)
        gate = og * jax.nn.sigmoid(og)
        for h in heads:
            inter = jnp.concatenate([o_inter[n * n_ch + c][ch_rows(h)] for c in range(n_ch)], axis=0)
            o = o_intra[n][h] + inter
            y_ref[sb_rows(n), v_cols(h)] = (_rms(o, ong_ref[...]) * gate[:, v_cols(h)]).astype(y_ref.dtype)
        yield


def _fox_stages(qa_ref, ka_ref, v_ref, y_ref, vt_scr, s_scr):
    s_len = v_ref.shape[0]
    t = FOX_TILE
    n_tiles = s_len // t
    krow = lax.broadcasted_iota(jnp.int32, (t, t), 0)
    qcol = lax.broadcasted_iota(jnp.int32, (t, t), 1)
    orow = lax.broadcasted_iota(jnp.int32, (LANES, t), 0)
    vt_scr[...] = v_ref[...].astype(F32).T.astype(BF16)

    units = [(i, e) for i in range(n_tiles) for e in range(2)]
    n_slots = s_scr.shape[0]

    def scores(n):
        i, e = units[n]
        hs = slice(e * LANES, (e + 1) * LANES)
        qa = qa_ref[i * t:(i + 1) * t, hs]
        mp = None
        for kb in range(i + 1):
            sc = _dot_nt(ka_ref[kb * t:(kb + 1) * t, hs], qa)
            if kb == i:
                sc = jnp.where(krow <= qcol, sc, -jnp.inf)
            s_scr[n % n_slots, kb] = sc
            bm = jnp.max(sc.reshape(t // 8, 8, t), axis=0)
            mp = bm if mp is None else jnp.maximum(mp, bm)
        return jnp.max(mp, axis=0, keepdims=True)

    def weighted_values(n, m):
        lp = jnp.zeros((8, t), F32)
        acc = jnp.zeros((LANES, t), F32)
        for kb in range(units[n][0] + 1):
            p = jnp.exp2(s_scr[n % n_slots, kb] - m)
            lp = lp + jnp.sum(p.reshape(t // 8, 8, t), axis=0)
            acc = acc + _dot(vt_scr[:, kb * t:(kb + 1) * t], p.astype(BF16))
        return acc / jnp.sum(lp, axis=0, keepdims=True)

    ahead = n_slots - 1
    col_max = {}
    for n in range(min(ahead, len(units))):
        col_max[n] = scores(n)
        yield
    o_even = None
    for n, (i, e) in enumerate(units):
        if n + ahead < len(units):
            col_max[n + ahead] = scores(n + ahead)
            yield
        o_t = weighted_values(n, col_max.pop(n))
        if e == 0:
            o_even = o_t
        else:
            o_pair = jnp.where(orow < FOX_DH, o_even, o_t)
            y_ref[i * t:(i + 1) * t, :] = o_pair.T.astype(y_ref.dtype)
        yield


def _fox_stages_bounded(qa_ref, ka_ref, v_ref, qbias_ref, y_ref, vt_scr, *, tile_rows):
    s_len = v_ref.shape[0]
    t = FOX_TILE
    n_tiles = s_len // t
    pair = pl.program_id(1)
    krow = lax.broadcasted_iota(jnp.int32, (t, t), 0)
    qcol = lax.broadcasted_iota(jnp.int32, (t, t), 1)
    orow = lax.broadcasted_iota(jnp.int32, (LANES, t), 0)
    vt_scr[...] = v_ref[...].astype(F32).T.astype(BF16)

    units = [(i, e) for i in range(n_tiles) for e in range(2)]
    blocks = [(n, kb) for n, (i, e) in enumerate(units) for kb in range(i + 1)]
    row_sum, acc = {}, {}
    o_even = [None]

    def values(n, kb, p):
        i, e = units[n]
        d = _dot(vt_scr[:, kb * t:(kb + 1) * t], p)
        acc[n] = d if n not in acc else acc[n] + d
        if kb == i:
            o_t = acc.pop(n) / jnp.sum(row_sum.pop(n), axis=0, keepdims=True)
            if e == 0:
                o_even[0] = o_t
            else:
                o_pair = jnp.where(orow < FOX_DH, o_even[0], o_t)
                y_ref[i * t:(i + 1) * t, :] = o_pair.T.astype(y_ref.dtype)

    pending = []
    for j, (n, kb) in enumerate(blocks):
        i, e = units[n]
        hs = slice(e * LANES, (e + 1) * LANES)
        q0 = i * t
        m = qbias_ref[q0 // tile_rows, pl.ds(2 * pair + e, 1), q0 % tile_rows:q0 % tile_rows + t]
        sc = _dot_nt(ka_ref[kb * t:(kb + 1) * t, hs], qa_ref[q0:q0 + t, hs])
        if kb == i:
            sc = jnp.where(krow <= qcol, sc, -jnp.inf)
        p = jnp.exp2(sc - m)
        ps = jnp.sum(p.reshape(t // 8, 8, t), axis=0)
        row_sum[n] = ps if n not in row_sum else row_sum[n] + ps
        pending.append((n, kb, p.astype(BF16)))
        if len(pending) > FOX_BLOCKS_AHEAD:
            values(*pending.pop(0))
        if j % 2 == 1:
            yield
    for item in pending:
        values(*item)
    yield


def _mixers_kernel(qa_ref, ka_ref, fv_ref, qbias_ref, gq_ref, gk_ref, gv_ref, gog_ref, misc_ref, lrw_ref, lrb_ref,
                   ong_ref, yf_ref, yg_ref, vt_scr, s_scr, st_scr, *, bounded_logits, tile_rows):
    @pl.when(pl.program_id(1) == 0)
    def _():
        st_scr[...] = jnp.zeros_like(st_scr)

    if bounded_logits:
        fox = _fox_stages_bounded(qa_ref, ka_ref, fv_ref, qbias_ref, yf_ref, vt_scr, tile_rows=tile_rows)
    else:
        fox = _fox_stages(qa_ref, ka_ref, fv_ref, yf_ref, vt_scr, s_scr)
    gla = _gla_stages(gq_ref, gk_ref, gv_ref, gog_ref, misc_ref, lrw_ref, lrb_ref, ong_ref, yg_ref, st_scr)
    rota = (fox, fox, fox, gla) if bounded_logits else (fox, fox, gla)
    finished = set()
    turn = 0
    while len(finished) < 2:
        gen = rota[turn % len(rota)]
        turn += 1
        if gen not in finished:
            try:
                next(gen)
            except StopIteration:
                finished.add(gen)


def _mixers(fqa, fka, fv, qbias, gq, gk, gv, gog, misc, lrw_pad, lrb, ong, b, s, bounded_logits):
    n_pairs = FOX_HEADS * FOX_DH // LANES
    t = FOX_TILE
    dkw, dvw = GLA_HEADS * GLA_DK, GLA_HEADS * GLA_DV
    sq = s // n_pairs
    assert s % t == 0 and sq % GLA_SUPER == 0, "sequence must split into FoX tiles and GLA super-blocks"
    tile_rows = qbias.shape[2]
    heads2 = pl.BlockSpec((s, 2 * LANES), lambda i, j: (i, j))
    pair = pl.BlockSpec((s, LANES), lambda i, j: (i, j))
    seq_bias = pl.BlockSpec((s // tile_rows, FOX_HEADS, tile_rows), lambda i, j: (i, 0, 0))
    quarter = lambda w: pl.BlockSpec((sq, w), lambda i, j: (i * n_pairs + j, 0))
    whole = lambda a: pl.BlockSpec(a.shape, lambda i, j: (0, 0))
    return pl.pallas_call(
        functools.partial(_mixers_kernel, bounded_logits=bounded_logits, tile_rows=tile_rows),
        grid=(b, n_pairs),
        in_specs=[heads2, heads2, pair, seq_bias,
                  quarter(dkw), quarter(dkw), quarter(dvw), quarter(dvw), quarter(LANES),
                  whole(lrw_pad), whole(lrb), whole(ong)],
        out_specs=[pair, quarter(dvw)],
        out_shape=[jax.ShapeDtypeStruct((b * s, FOX_HEADS * FOX_DH), BF16),
                   jax.ShapeDtypeStruct((b * s, dvw), BF16)],
        scratch_shapes=[
            pltpu.VMEM((LANES, s), BF16),
            pltpu.VMEM((FOX_SLOTS, s // t, t, t), F32),
            pltpu.VMEM((GLA_DV, dkw), F32),
        ],
        compiler_params=pltpu.CompilerParams(dimension_semantics=("arbitrary", "arbitrary"),
                                             vmem_limit_bytes=VMEM_LIMIT),
        name="mixers_bounded" if bounded_logits else "mixers",
    )(fqa, fka, fv, qbias, gq, gk, gv, gog, misc, lrw_pad, lrb, ong)


def _post_kernel(x_ref, yg_ref, yf_ref, p_ref, wo_ref, g2_ref, wup_ref, cw_ref, cb_ref, wdn_ref,
                 g3_ref, wpg_ref, bpg_ref, wpe_ref, gpe_ref, o_ref,
                 x1_scr, h2_scr, u_scr, act_scr, carry_scr, acc_scr):
    tm = x_ref.shape[0]
    d_ff = wdn_ref.shape[0]
    n_chunks = d_ff // FFN_CHUNK
    n_slabs = 2 * FFN_CHUNK // LANES
    half = n_slabs // 2
    dvw = yg_ref.shape[1]
    first_tile = pl.program_id(1) == 0

    @pl.when(first_tile)
    def _():
        carry_scr[...] = jnp.zeros_like(carry_scr)

    o_ref[...] = _rms(_dot(p_ref[...].astype(BF16), wpe_ref[...]), gpe_ref[...])

    x1 = x_ref[...] + _dot(yg_ref[...], wo_ref[:dvw, :]) + _dot(yf_ref[...], wo_ref[dvw:, :])
    x1_scr[...] = x1
    h2_scr[...] = _rms(x1, g2_ref[...]).astype(BF16)
    acc_scr[...] = jnp.zeros_like(acc_scr)

    for j in range(n_chunks + DOWN_LAG):
        if j < n_chunks:
            ub = u_scr.at[j % 2]
            col0 = [(c // half) * d_ff + j * FFN_CHUNK + (c % half) * LANES for c in range(n_slabs)]
            h2 = h2_scr[...]
            for part in range(2):
                u = _dot(h2, wup_ref[:, col0[part * half]:col0[part * half] + FFN_CHUNK])
                for c in range(part * half, (part + 1) * half):
                    ub[c, :CARRY_ROWS, :] = carry_scr[j, c]
                    ub[c, CARRY_ROWS:, :] = u[:, (c % half) * LANES:(c % half + 1) * LANES]
                    carry_scr[j, c] = ub[c, tm:, :]
        if j >= DOWN_LAG:
            jd = j - DOWN_LAG
            acc_scr[...] += _dot(act_scr[jd % (DOWN_LAG + 1)], wdn_ref[jd * FFN_CHUNK:(jd + 1) * FFN_CHUNK, :])
        if j < n_chunks:
            def conv(c):
                cs = slice(col0[c], col0[c] + LANES)
                return (cb_ref[:, cs]
                        + ub[c, CARRY_ROWS - 2:CARRY_ROWS - 2 + tm, :] * cw_ref[0:1, cs]
                        + ub[c, CARRY_ROWS - 1:CARRY_ROWS - 1 + tm, :] * cw_ref[1:2, cs]
                        + ub[c, CARRY_ROWS:, :] * cw_ref[2:3, cs])
            for c in range(half):
                gate = conv(c)
                act = gate * jax.nn.sigmoid(gate) * conv(c + half)
                act_scr[j % (DOWN_LAG + 1), :, c * LANES:(c + 1) * LANES] = act.astype(BF16)

    x2 = x1_scr[...] + acc_scr[...]
    gate = jax.nn.sigmoid(_dot(_rms(x2, g3_ref[...]).astype(BF16), wpg_ref[...]) + bpg_ref[...])
    o_ref[...] = x2 + gate * o_ref[...]


def _post(x2d, yg, yf, p2d, wo, g2, wup, cw, cb, wdn, g3, wpg, bpg, wpe, gpe, b, s, tm):
    d = x2d.shape[1]
    nt = s // tm
    tile = lambda w: pl.BlockSpec((tm, w), lambda i, j: (i * nt + j, 0))
    resident = lambda a: pl.BlockSpec(a.shape, lambda i, j: (0,) * a.ndim, pipeline_mode=pl.Buffered(1))
    n_chunks = wdn.shape[0] // FFN_CHUNK
    n_slabs = 2 * FFN_CHUNK // LANES
    return pl.pallas_call(
        _post_kernel,
        grid=(b, nt),
        in_specs=[tile(d), tile(yg.shape[1]), tile(yf.shape[1]), tile(p2d.shape[1])]
                 + [resident(a) for a in (wo, g2, wup, cw, cb, wdn, g3, wpg, bpg, wpe, gpe)],
        out_specs=tile(d),
        out_shape=jax.ShapeDtypeStruct(x2d.shape, F32),
        scratch_shapes=[
            pltpu.VMEM((tm, d), F32),
            pltpu.VMEM((tm, d), BF16),
            pltpu.VMEM((2, n_slabs, CARRY_ROWS + tm, LANES), F32),
            pltpu.VMEM((DOWN_LAG + 1, tm, FFN_CHUNK), BF16),
            pltpu.VMEM((n_chunks, n_slabs, CARRY_ROWS, LANES), F32),
            pltpu.VMEM((tm, d), F32),
        ],
        compiler_params=pltpu.CompilerParams(dimension_semantics=("arbitrary", "arbitrary"),
                                             vmem_limit_bytes=VMEM_LIMIT),
        name="post",
    )(x2d, yg, yf, p2d, wo, g2, wup, cw, cb, wdn, g3, wpg, bpg, wpe, gpe)


def _layer(x2d, p2d, b, s, norm1_g, w_in, lr_w, lr_b, onorm_g, b_f, qn_g, kn_g, w_o, norm2_g, w_up,
           conv_w, conv_b, w_down, norm3_g, w_pe, pe_norm_g, w_pg, b_pg):
    gqk = GLA_HEADS * GLA_DK
    row = lambda a: a.reshape(1, -1).astype(F32)
    bf_pad = jnp.zeros((1, LANES), F32).at[0, MISC_F0:MISC_F0 + FOX_HEADS].set(b_f)
    qg2 = row(jnp.tile(qn_g, 2)) * (FOX_DH ** -0.5 * LOG2E)
    (gq, gk, gv, gog, fqa, fka, fv, misc, qbias), (wo_b, wup_b, wdn_b, wpg_b, wpe_b) = _inproj(
        x2d, row(norm1_g), jnp.swapaxes(w_in, 1, 2).astype(F32), bf_pad, qg2, row(jnp.tile(kn_g, 2)),
        (w_o, w_up, w_down, w_pg, w_pe), s, tm=512)

    lrw_pad = jnp.zeros((LANES, gqk), F32).at[MISC_LR0:MISC_LR0 + GLA_LOWRANK].set(lr_w).astype(BF16)
    logit_bound = FOX_DH ** 0.5 * LOG2E * jnp.max(jnp.abs(qn_g)) * jnp.max(jnp.abs(kn_g))
    mix = lambda bounded: functools.partial(_mixers, b=b, s=s, bounded_logits=bounded)
    y_fox, y_gla = lax.cond(logit_bound <= FOX_BOUNDED_MAX_LOGIT, mix(True), mix(False),
                            fqa, fka, fv, qbias, gq, gk, gv, gog, misc, lrw_pad, row(lr_b), row(onorm_g))

    return _post(
        x2d, y_gla, y_fox, p2d, wo_b, row(norm2_g), wup_b, conv_w.astype(F32), row(conv_b), wdn_b,
        row(norm3_g), wpg_b, row(b_pg), wpe_b, row(pe_norm_g), b, s, tm=512)


def kernel(x, p, norm1_g, w_in, gla_lr_w, gla_lr_b, gla_onorm_g, fox_b_f, fox_qnorm_g, fox_knorm_g, w_o, norm2_g, w_up, conv_w, conv_b, w_down, norm3_g, w_pe, pe_norm_g, w_pg, b_pg):
    b, s, d = x.shape
    x2d = x.reshape(b * s, d)
    for i in range(p.shape[0]):
        x2d = _layer(x2d, p[i].reshape(b * s, -1), b, s, norm1_g[i], w_in[i:i + 1], gla_lr_w[i], gla_lr_b[i],
                     gla_onorm_g[i], fox_b_f[i], fox_qnorm_g[i], fox_knorm_g[i], w_o[i], norm2_g[i], w_up[i],
                     conv_w[i], conv_b[i], w_down[i], norm3_g[i], w_pe[i], pe_norm_g[i], w_pg[i], b_pg[i])
    return x2d.reshape(b, s, d)
```

```python
import functools

import jax
import jax.numpy as jnp
from jax import lax
from jax.experimental import pallas as pl
from jax.experimental.pallas import tpu as pltpu

F32 = jnp.float32
BF16 = jnp.bfloat16

EPS = 1e-6
LOG2E = 1.4426950408889634
LANES = 128
BF16_SUBLANES = 16
GLA_HEADS, GLA_DK, GLA_DV = 4, 64, 128
GLA_LOWRANK = 16
GLA_INV_TAU = 1.0 / 16.0
GLA_CHUNK = 64
GLA_SUPER = 256
FOX_HEADS, FOX_DH = 8, 64
FOX_TILE = 256
FOX_SLOTS = 3
FOX_BLOCKS_AHEAD = 4
FOX_BOUNDED_MAX_LOGIT = 60.0
FFN_CHUNK = 256
CARRY_ROWS = 8
DOWN_LAG = 2
MISC_F0 = 0
MISC_LR0 = 8
VMEM_LIMIT = 56 * 1024 * 1024


def _dot(a, b):
    return jnp.dot(a, b, preferred_element_type=F32)


def _dot_nt(a, b):
    return lax.dot_general(a, b, (((1,), (1,)), ((), ())), preferred_element_type=F32)


def _dot_tn(a, b):
    return lax.dot_general(a, b, (((0,), (0,)), ((), ())), preferred_element_type=F32)


def _log_sigmoid(z):
    return jnp.minimum(z, 0.0) - jnp.log(1.0 + jnp.exp(-jnp.abs(z)))


def _split_bf16(x, pieces):
    out = []
    for _ in range(pieces):
        p = x.astype(BF16)
        out.append(p)
        x = x - p.astype(F32)
    return out


def _rms(x, gain):
    ms = jnp.mean(x * x, axis=-1, keepdims=True)
    return x * lax.rsqrt(ms + EPS) * gain


def _inproj_kernel(x_ref, g_ref, win_ref, bf_ref, qg_ref, kg_ref, *rest, tiles_per_seq, n_cast):
    cast_in, rest = rest[:n_cast], rest[n_cast:]
    gq_ref, gk_ref, gv_ref, gog_ref, fqa_ref, fka_ref, fv_ref, misc_ref, qbias_ref = rest[:9]
    cast_out = rest[9:9 + n_cast]
    carry_scr, h_scr, wg_ref, wf_ref, wm_ref = rest[9 + n_cast:]

    @pl.when(pl.program_id(0) == 0)
    def _():
        gla_w = wg_ref.shape[1]
        fox_w = wf_ref.shape[1]
        lr0 = gla_w
        fox0 = lr0 + GLA_LOWRANK
        f0 = fox0 + fox_w
        wg_ref[...] = win_ref[:gla_w, :].T.astype(BF16)
        wf_ref[...] = win_ref[fox0:f0, :].T.astype(BF16)
        pad = jnp.zeros((LANES - MISC_LR0 - GLA_LOWRANK, win_ref.shape[1]), F32)
        misc_t = jnp.concatenate([win_ref[f0:f0 + FOX_HEADS, :], win_ref[lr0:fox0, :], pad], axis=0)
        wm_ref[...] = misc_t.T.astype(BF16)

    _inproj_body(x_ref, g_ref, wg_ref, wf_ref, wm_ref, bf_ref, qg_ref, kg_ref, gq_ref, gk_ref, gv_ref, gog_ref,
                 fqa_ref, fka_ref, fv_ref, misc_ref, qbias_ref, carry_scr, h_scr, tiles_per_seq)
    for w_ref, o_ref in zip(cast_in, cast_out):
        o_ref[...] = w_ref[...].astype(BF16)


def _inproj_body(x_ref, g_ref, wg_ref, wf_ref, wm_ref, bf_ref, qg_ref, kg_ref, gq_ref, gk_ref, gv_ref, gog_ref,
                 fqa_ref, fka_ref, fv_ref, misc_ref, qbias_ref, carry_scr, h_scr, tiles_per_seq):
    tm = x_ref.shape[0]
    fw = FOX_HEADS * FOX_DH

    @pl.when(pl.program_id(0) % tiles_per_seq == 0)
    def _():
        carry_scr[...] = jnp.zeros_like(carry_scr)

    h_scr[...] = _rms(x_ref[...], g_ref[...]).astype(BF16)
    h = h_scr[...]
    misc = _dot(h, wm_ref[...])
    misc_ref[...] = misc
    q_all = _dot(h, wf_ref[:, :fw])
    k_all = _dot(h, wf_ref[:, fw:2 * fw])

    blk = FOX_TILE
    trow = lax.broadcasted_iota(jnp.int32, (blk, blk), 0)
    tcol = lax.broadcasted_iota(jnp.int32, (blk, blk), 1)
    tri = (tcol <= trow).astype(BF16)
    n_blk = tm // blk
    logf = jnp.concatenate([_log_sigmoid(misc[r0:r0 + blk] + bf_ref[...]) for r0 in range(0, tm, blk)], axis=1)
    within = sum(_dot(tri, piece) for piece in _split_bf16(logf, 3))
    carry = carry_scr[...]
    c_blocks = []
    for i in range(n_blk):
        cb = within[:, i * LANES:(i + 1) * LANES] + carry
        c_blocks.append(cb)
        carry = cb[blk - 1:blk]
    carry_scr[...] = carry
    c_all = jnp.concatenate(c_blocks, axis=0)
    qbias_ref[...] = (c_all * (-LOG2E)).T

    lane = lax.broadcasted_iota(jnp.int32, (tm, LANES), 1)
    lower = lane < FOX_DH

    def pair_norm(x, g):
        sq = x * x
        ms_lo = jnp.sum(jnp.where(lower, sq, 0.0), axis=-1, keepdims=True)
        ms_hi = jnp.sum(jnp.where(lower, 0.0, sq), axis=-1, keepdims=True)
        ms = jnp.where(lower, ms_lo, ms_hi) * (1.0 / FOX_DH)
        return x * lax.rsqrt(ms + EPS) * g

    fv_ref[...] = _dot(h, wf_ref[:, 2 * fw:]).astype(fv_ref.dtype)
    off = 0
    for ref in (gq_ref, gk_ref, gv_ref, gog_ref):
        n = ref.shape[-1]
        ref[...] = _dot(h, wg_ref[:, off:off + n]).astype(ref.dtype)
        off += n

    for pair in range(fw // LANES):
        ps = slice(pair * LANES, (pair + 1) * LANES)
        qn = pair_norm(q_all[:, ps], qg_ref[...])
        kn = pair_norm(k_all[:, ps], kg_ref[...])
        for e in range(2):
            head = 2 * pair + e
            in_head = (lane >= FOX_DH) if e else lower
            piece0 = 0 if e else FOX_DH
            c_col = jnp.sum(jnp.where(lane == MISC_F0 + head, c_all, 0.0), axis=-1, keepdims=True)
            is_piece = (lane >= piece0) & (lane < piece0 + 3)
            hs = slice(head * LANES, (head + 1) * LANES)
            fqa_ref[:, hs] = jnp.where(in_head, qn, jnp.where(is_piece, 1.0, 0.0)).astype(BF16)
            ka = jnp.where(in_head, kn, 0.0).astype(BF16)
            for j, p in enumerate(_split_bf16(c_col * (-LOG2E), 3)):
                ka = jnp.where(lane == piece0 + j, p, ka)
            fka_ref[:, hs] = ka


def _cast_slab(rows, steps):
    for n_slabs in range(min(steps, rows // BF16_SUBLANES), 0, -1):
        if rows % n_slabs == 0 and (rows // n_slabs) % BF16_SUBLANES == 0:
            return rows // n_slabs
    raise ValueError(f"no bf16-tile aligned slab for {rows} rows")


def _inproj(x2d, g, w_in, bf_pad, qg2, kg2, later_weights, s, tm):
    t, d = x2d.shape
    steps = t // tm
    gla_w = 2 * GLA_HEADS * (GLA_DK + GLA_DV)
    fox_w = 3 * FOX_HEADS * FOX_DH
    assert w_in.shape == (1, gla_w + GLA_LOWRANK + fox_w + FOX_HEADS, d)
    assert MISC_F0 == 0 and MISC_LR0 == FOX_HEADS
    gqk, gvw = GLA_HEADS * GLA_DK, GLA_HEADS * GLA_DV
    widths = (gqk, gqk, gvw, gvw, FOX_HEADS * LANES, FOX_HEADS * LANES, FOX_HEADS * FOX_DH, LANES)
    dtypes = (BF16,) * 7 + (F32,)
    whole = lambda a: pl.BlockSpec(a.shape, lambda i: (0, 0))
    once = lambda a: pl.BlockSpec((None,) + a.shape[1:], lambda i: (0, 0, 0), pipeline_mode=pl.Buffered(1))

    def slab(w):
        rows = _cast_slab(w.shape[0], steps)
        last = w.shape[0] // rows - 1
        return pl.BlockSpec((rows, w.shape[1]), lambda i: (jnp.minimum(i, last), 0))

    outs = pl.pallas_call(
        functools.partial(_inproj_kernel, tiles_per_seq=s // tm, n_cast=len(later_weights)),
        grid=(steps,),
        in_specs=[pl.BlockSpec((tm, d), lambda i: (i, 0)), whole(g), once(w_in)]
                 + [whole(a) for a in (bf_pad, qg2, kg2)]
                 + [slab(w) for w in later_weights],
        out_specs=[pl.BlockSpec((tm, n), lambda i: (i, 0)) for n in widths]
                  + [pl.BlockSpec((None, LANES, tm), lambda i: (i, 0, 0))]
                  + [slab(w) for w in later_weights],
        out_shape=[jax.ShapeDtypeStruct((t, n), dt) for n, dt in zip(widths, dtypes)]
                  + [jax.ShapeDtypeStruct((steps, LANES, tm), F32)]
                  + [jax.ShapeDtypeStruct(w.shape, BF16) for w in later_weights],
        scratch_shapes=[pltpu.VMEM((1, LANES), F32),
                        pltpu.VMEM((tm, d), BF16),
                        pltpu.VMEM((d, gla_w), BF16),
                        pltpu.VMEM((d, fox_w), BF16),
                        pltpu.VMEM((d, LANES), BF16)],
        compiler_params=pltpu.CompilerParams(dimension_semantics=("arbitrary",), vmem_limit_bytes=VMEM_LIMIT),
        name="inproj",
    )(x2d, g, w_in, bf_pad, qg2, kg2, *later_weights)
    return outs[:len(widths) + 1], outs[len(widths) + 1:]


def _gla_stages(q_ref, k_ref, v_ref, og_ref, misc_ref, lrw_ref, lrb_ref, ong_ref, y_ref, st_scr):
    s_len = q_ref.shape[0]
    r = GLA_SUPER
    n_sb = s_len // r
    n_ch = r // GLA_CHUNK
    dkw = GLA_HEADS * GLA_DK
    row = lax.broadcasted_iota(jnp.int32, (r, r), 0)
    col = lax.broadcasted_iota(jnp.int32, (r, r), 1)
    causal_bd = ((row // GLA_CHUNK) == (col // GLA_CHUNK)) & (col <= row)
    tri_bd = causal_bd.astype(BF16)
    head_of_col = lax.broadcasted_iota(jnp.int32, (r, dkw), 1) // GLA_DK
    heads = range(GLA_HEADS)
    sb_rows = lambda n: slice(n * r, (n + 1) * r)
    ch_rows = lambda c: slice(c * GLA_CHUNK, (c + 1) * GLA_CHUNK)
    v_cols = lambda h: slice(h * GLA_DV, (h + 1) * GLA_DV)

    la_pieces = []
    for n in range(n_sb):
        z = _dot(misc_ref[sb_rows(n), :].astype(BF16), lrw_ref[...]) + lrb_ref[...]
        la_pieces.append(_split_bf16(_log_sigmoid(z) * GLA_INV_TAU, 2))
        yield

    qh, ke, kd, decay = [], [], [], []
    for n in range(n_sb):
        b = _dot(tri_bd, la_pieces[n][0]) + _dot(tri_bd, la_pieces[n][1])
        b_last = [b[(c + 1) * GLA_CHUNK - 1:(c + 1) * GLA_CHUNK, :] for c in range(n_ch)]
        b_tot = jnp.concatenate([jnp.broadcast_to(bl, (GLA_CHUNK, dkw)) for bl in b_last], axis=0)
        q = q_ref[sb_rows(n), :].astype(F32)
        k = k_ref[sb_rows(n), :].astype(F32)
        qe = (q * (GLA_DK ** -0.5)) * jnp.exp(b)
        qh.append([jnp.where(head_of_col == h, qe, 0.0).astype(BF16) for h in heads])
        ke.append((k * jnp.exp(-b)).astype(BF16))
        kd_n = k * jnp.exp(b_tot - b)
        kd.append([jnp.where(head_of_col == h, kd_n, 0.0).astype(BF16) for h in heads])
        decay.append([jnp.exp(bl) for bl in b_last])
        yield

    a = []
    for n in range(n_sb):
        a.append([jnp.where(causal_bd, _dot_nt(qh[n][h], ke[n]), 0.0).astype(BF16) for h in heads])
        yield

    o_intra = []
    for n in range(n_sb):
        o_intra.append([_dot(a[n][h], v_ref[sb_rows(n), v_cols(h)]) for h in heads])
        yield

    ds = []
    for n in range(n_sb):
        for c in range(n_ch):
            rows = slice(n * r + c * GLA_CHUNK, n * r + (c + 1) * GLA_CHUNK)
            v_stack = jnp.concatenate([v_ref[rows, v_cols(h)] for h in heads], axis=0)
            kd_stack = jnp.concatenate([kd[n][h][ch_rows(c)] for h in heads], axis=0)
            ds.append(_dot_tn(v_stack, kd_stack))
        yield

    st = st_scr[...]
    o_inter = []
    for n in range(n_sb):
        for c in range(n_ch):
            qe_stack = jnp.concatenate([qh[n][h][ch_rows(c)] for h in heads], axis=0)
            o_inter.append(_dot_nt(qe_stack, st.astype(BF16)))
            st = st * decay[n][c] + ds[n * n_ch + c]
        yield
    st_scr[...] = st

    for n in range(n_sb):
        og = og_ref[sb_rows(n), :].astype(F32)
        gate = og * jax.nn.sigmoid(og)
        for h in heads:
            inter = jnp.concatenate([o_inter[n * n_ch + c][ch_rows(h)] for c in range(n_ch)], axis=0)
            o = o_intra[n][h] + inter
            y_ref[sb_rows(n), v_cols(h)] = (_rms(o, ong_ref[...]) * gate[:, v_cols(h)]).astype(y_ref.dtype)
        yield


def _fox_stages(qa_ref, ka_ref, v_ref, y_ref, vt_scr, s_scr):
    s_len = v_ref.shape[0]
    t = FOX_TILE
    n_tiles = s_len // t
    krow = lax.broadcasted_iota(jnp.int32, (t, t), 0)
    qcol = lax.broadcasted_iota(jnp.int32, (t, t), 1)
    orow = lax.broadcasted_iota(jnp.int32, (LANES, t), 0)
    vt_scr[...] = v_ref[...].astype(F32).T.astype(BF16)

    units = [(i, e) for i in range(n_tiles) for e in range(2)]
    n_slots = s_scr.shape[0]

    def scores(n):
        i, e = units[n]
        hs = slice(e * LANES, (e + 1) * LANES)
        qa = qa_ref[i * t:(i + 1) * t, hs]
        mp = None
        for kb in range(i + 1):
            sc = _dot_nt(ka_ref[kb * t:(kb + 1) * t, hs], qa)
            if kb == i:
                sc = jnp.where(krow <= qcol, sc, -jnp.inf)
            s_scr[n % n_slots, kb] = sc
            bm = jnp.max(sc.reshape(t // 8, 8, t), axis=0)
            mp = bm if mp is None else jnp.maximum(mp, bm)
        return jnp.max(mp, axis=0, keepdims=True)

    def weighted_values(n, m):
        lp = jnp.zeros((8, t), F32)
        acc = jnp.zeros((LANES, t), F32)
        for kb in range(units[n][0] + 1):
            p = jnp.exp2(s_scr[n % n_slots, kb] - m)
            lp = lp + jnp.sum(p.reshape(t // 8, 8, t), axis=0)
            acc = acc + _dot(vt_scr[:, kb * t:(kb + 1) * t], p.astype(BF16))
        return acc / jnp.sum(lp, axis=0, keepdims=True)

    ahead = n_slots - 1
    col_max = {}
    for n in range(min(ahead, len(units))):
        col_max[n] = scores(n)
        yield
    o_even = None
    for n, (i, e) in enumerate(units):
        if n + ahead < len(units):
            col_max[n + ahead] = scores(n + ahead)
            yield
        o_t = weighted_values(n, col_max.pop(n))
        if e == 0:
            o_even = o_t
        else:
            o_pair = jnp.where(orow < FOX_DH, o_even, o_t)
            y_ref[i * t:(i + 1) * t, :] = o_pair.T.astype(y_ref.dtype)
        yield


def _fox_stages_bounded(qa_ref, ka_ref, v_ref, qbias_ref, y_ref, vt_scr, *, tile_rows):
    s_len = v_ref.shape[0]
    t = FOX_TILE
    n_tiles = s_len // t
    pair = pl.program_id(1)
    krow = lax.broadcasted_iota(jnp.int32, (t, t), 0)
    qcol = lax.broadcasted_iota(jnp.int32, (t, t), 1)
    orow = lax.broadcasted_iota(jnp.int32, (LANES, t), 0)
    vt_scr[...] = v_ref[...].astype(F32).T.astype(BF16)

    units = [(i, e) for i in range(n_tiles) for e in range(2)]
    blocks = [(n, kb) for n, (i, e) in enumerate(units) for kb in range(i + 1)]
    row_sum, acc = {}, {}
    o_even = [None]

    def values(n, kb, p):
        i, e = units[n]
        d = _dot(vt_scr[:, kb * t:(kb + 1) * t], p)
        acc[n] = d if n not in acc else acc[n] + d
        if kb == i:
            o_t = acc.pop(n) / jnp.sum(row_sum.pop(n), axis=0, keepdims=True)
            if e == 0:
                o_even[0] = o_t
            else:
                o_pair = jnp.where(orow < FOX_DH, o_even[0], o_t)
                y_ref[i * t:(i + 1) * t, :] = o_pair.T.astype(y_ref.dtype)

    pending = []
    for j, (n, kb) in enumerate(blocks):
        i, e = units[n]
        hs = slice(e * LANES, (e + 1) * LANES)
        q0 = i * t
        m = qbias_ref[q0 // tile_rows, pl.ds(2 * pair + e, 1), q0 % tile_rows:q0 % tile_rows + t]
        sc = _dot_nt(ka_ref[kb * t:(kb + 1) * t, hs], qa_ref[q0:q0 + t, hs])
        if kb == i:
            sc = jnp.where(krow <= qcol, sc, -jnp.inf)
        p = jnp.exp2(sc - m)
        ps = jnp.sum(p.reshape(t // 8, 8, t), axis=0)
        row_sum[n] = ps if n not in row_sum else row_sum[n] + ps
        pending.append((n, kb, p.astype(BF16)))
        if len(pending) > FOX_BLOCKS_AHEAD:
            values(*pending.pop(0))
        if j % 2 == 1:
            yield
    for item in pending:
        values(*item)
    yield


def _mixers_kernel(qa_ref, ka_ref, fv_ref, qbias_ref, gq_ref, gk_ref, gv_ref, gog_ref, misc_ref, lrw_ref, lrb_ref,
                   ong_ref, yf_ref, yg_ref, vt_scr, s_scr, st_scr, *, bounded_logits, tile_rows):
    @pl.when(pl.program_id(1) == 0)
    def _():
        st_scr[...] = jnp.zeros_like(st_scr)

    if bounded_logits:
        fox = _fox_stages_bounded(qa_ref, ka_ref, fv_ref, qbias_ref, yf_ref, vt_scr, tile_rows=tile_rows)
    else:
        fox = _fox_stages(qa_ref, ka_ref, fv_ref, yf_ref, vt_scr, s_scr)
    gla = _gla_stages(gq_ref, gk_ref, gv_ref, gog_ref, misc_ref, lrw_ref, lrb_ref, ong_ref, yg_ref, st_scr)
    rota = (fox, fox, fox, gla) if bounded_logits else (fox, fox, gla)
    finished = set()
    turn = 0
    while len(finished) < 2:
        gen = rota[turn % len(rota)]
        turn += 1
        if gen not in finished:
            try:
                next(gen)
            except StopIteration:
                finished.add(gen)


def _mixers(fqa, fka, fv, qbias, gq, gk, gv, gog, misc, lrw_pad, lrb, ong, b, s, bounded_logits):
    n_pairs = FOX_HEADS * FOX_DH // LANES
    t = FOX_TILE
    dkw, dvw = GLA_HEADS * GLA_DK, GLA_HEADS * GLA_DV
    sq = s // n_pairs
    assert s % t == 0 and sq % GLA_SUPER == 0, "sequence must split into FoX tiles and GLA super-blocks"
    tile_rows = qbias.shape[2]
    heads2 = pl.BlockSpec((s, 2 * LANES), lambda i, j: (i, j))
    pair = pl.BlockSpec((s, LANES), lambda i, j: (i, j))
    seq_bias = pl.BlockSpec((s // tile_rows, FOX_HEADS, tile_rows), lambda i, j: (i, 0, 0))
    quarter = lambda w: pl.BlockSpec((sq, w), lambda i, j: (i * n_pairs + j, 0))
    whole = lambda a: pl.BlockSpec(a.shape, lambda i, j: (0, 0))
    return pl.pallas_call(
        functools.partial(_mixers_kernel, bounded_logits=bounded_logits, tile_rows=tile_rows),
        grid=(b, n_pairs),
        in_specs=[heads2, heads2, pair, seq_bias,
                  quarter(dkw), quarter(dkw), quarter(dvw), quarter(dvw), quarter(LANES),
                  whole(lrw_pad), whole(lrb), whole(ong)],
        out_specs=[pair, quarter(dvw)],
        out_shape=[jax.ShapeDtypeStruct((b * s, FOX_HEADS * FOX_DH), BF16),
                   jax.ShapeDtypeStruct((b * s, dvw), BF16)],
        scratch_shapes=[
            pltpu.VMEM((LANES, s), BF16),
            pltpu.VMEM((FOX_SLOTS, s // t, t, t), F32),
            pltpu.VMEM((GLA_DV, dkw), F32),
        ],
        compiler_params=pltpu.CompilerParams(dimension_semantics=("arbitrary", "arbitrary"),
                                             vmem_limit_bytes=VMEM_LIMIT),
        name="mixers_bounded" if bounded_logits else "mixers",
    )(fqa, fka, fv, qbias, gq, gk, gv, gog, misc, lrw_pad, lrb, ong)


def _post_kernel(x_ref, yg_ref, yf_ref, p_ref, wo_ref, g2_ref, wup_ref, cw_ref, cb_ref, wdn_ref,
                 g3_ref, wpg_ref, bpg_ref, wpe_ref, gpe_ref, o_ref,
                 x1_scr, h2_scr, u_scr, act_scr, carry_scr, acc_scr):
    tm = x_ref.shape[0]
    d_ff = wdn_ref.shape[0]
    n_chunks = d_ff // FFN_CHUNK
    n_slabs = 2 * FFN_CHUNK // LANES
    half = n_slabs // 2
    dvw = yg_ref.shape[1]
    first_tile = pl.program_id(1) == 0

    @pl.when(first_tile)
    def _():
        carry_scr[...] = jnp.zeros_like(carry_scr)

    x1 = x_ref[...] + _dot(yg_ref[...], wo_ref[:dvw, :]) + _dot(yf_ref[...], wo_ref[dvw:, :])
    x1_scr[...] = x1
    h2_scr[...] = _rms(x1, g2_ref[...]).astype(BF16)
    acc_scr[...] = jnp.zeros_like(acc_scr)

    o_ref[...] = _rms(_dot(p_ref[...].astype(BF16), wpe_ref[...]), gpe_ref[...])

    for j in range(n_chunks + DOWN_LAG):
        if j < n_chunks:
            ub = u_scr.at[j % 2]
            col0 = [(c // half) * d_ff + j * FFN_CHUNK + (c % half) * LANES for c in range(n_slabs)]
            h2 = h2_scr[...]
            for part in range(2):
                u = _dot(h2, wup_ref[:, col0[part * half]:col0[part * half] + FFN_CHUNK])
                for c in range(part * half, (part + 1) * half):
                    ub[c, :CARRY_ROWS, :] = carry_scr[j, c]
                    ub[c, CARRY_ROWS:, :] = u[:, (c % half) * LANES:(c % half + 1) * LANES]
                    carry_scr[j, c] = ub[c, tm:, :]
        if j >= DOWN_LAG:
            jd = j - DOWN_LAG
            acc_scr[...] += _dot(act_scr[jd % (DOWN_LAG + 1)], wdn_ref[jd * FFN_CHUNK:(jd + 1) * FFN_CHUNK, :])
        if j < n_chunks:
            def conv(c):
                cs = slice(col0[c], col0[c] + LANES)
                return (cb_ref[:, cs]
                        + ub[c, CARRY_ROWS - 2:CARRY_ROWS - 2 + tm, :] * cw_ref[0:1, cs]
                        + ub[c, CARRY_ROWS - 1:CARRY_ROWS - 1 + tm, :] * cw_ref[1:2, cs]
                        + ub[c, CARRY_ROWS:, :] * cw_ref[2:3, cs])
            for c in range(half):
                gate = conv(c)
                act = gate * jax.nn.sigmoid(gate) * conv(c + half)
                act_scr[j % (DOWN_LAG + 1), :, c * LANES:(c + 1) * LANES] = act.astype(BF16)

    x2 = x1_scr[...] + acc_scr[...]
    gate = jax.nn.sigmoid(_dot(_rms(x2, g3_ref[...]).astype(BF16), wpg_ref[...]) + bpg_ref[...])
    o_ref[...] = x2 + gate * o_ref[...]


def _post(x2d, yg, yf, p2d, wo, g2, wup, cw, cb, wdn, g3, wpg, bpg, wpe, gpe, b, s, tm):
    d = x2d.shape[1]
    nt = s // tm
    tile = lambda w: pl.BlockSpec((tm, w), lambda i, j: (i * nt + j, 0))
    resident = lambda a: pl.BlockSpec(a.shape, lambda i, j: (0,) * a.ndim, pipeline_mode=pl.Buffered(1))
    n_chunks = wdn.shape[0] // FFN_CHUNK
    n_slabs = 2 * FFN_CHUNK // LANES
    return pl.pallas_call(
        _post_kernel,
        grid=(b, nt),
        in_specs=[tile(d), tile(yg.shape[1]), tile(yf.shape[1]), tile(p2d.shape[1])]
                 + [resident(a) for a in (wo, g2, wup, cw, cb, wdn, g3, wpg, bpg, wpe, gpe)],
        out_specs=tile(d),
        out_shape=jax.ShapeDtypeStruct(x2d.shape, F32),
        scratch_shapes=[
            pltpu.VMEM((tm, d), F32),
            pltpu.VMEM((tm, d), BF16),
            pltpu.VMEM((2, n_slabs, CARRY_ROWS + tm, LANES), F32),
            pltpu.VMEM((DOWN_LAG + 1, tm, FFN_CHUNK), BF16),
            pltpu.VMEM((n_chunks, n_slabs, CARRY_ROWS, LANES), F32),
            pltpu.VMEM((tm, d), F32),
        ],
        compiler_params=pltpu.CompilerParams(dimension_semantics=("arbitrary", "arbitrary"),
                                             vmem_limit_bytes=VMEM_LIMIT),
        name="post",
    )(x2d, yg, yf, p2d, wo, g2, wup, cw, cb, wdn, g3, wpg, bpg, wpe, gpe)


def _layer(x2d, p2d, b, s, norm1_g, w_in, lr_w, lr_b, onorm_g, b_f, qn_g, kn_g, w_o, norm2_g, w_up,
           conv_w, conv_b, w_down, norm3_g, w_pe, pe_norm_g, w_pg, b_pg):
    gqk = GLA_HEADS * GLA_DK
    row = lambda a: a.reshape(1, -1).astype(F32)
    bf_pad = jnp.zeros((1, LANES), F32).at[0, MISC_F0:MISC_F0 + FOX_HEADS].set(b_f)
    qg2 = row(jnp.tile(qn_g, 2)) * (FOX_DH ** -0.5 * LOG2E)
    (gq, gk, gv, gog, fqa, fka, fv, misc, qbias), (wo_b, wup_b, wdn_b, wpg_b, wpe_b) = _inproj(
        x2d, row(norm1_g), jnp.swapaxes(w_in, 1, 2).astype(F32), bf_pad, qg2, row(jnp.tile(kn_g, 2)),
        (w_o, w_up, w_down, w_pg, w_pe), s, tm=512)

    lrw_pad = jnp.zeros((LANES, gqk), F32).at[MISC_LR0:MISC_LR0 + GLA_LOWRANK].set(lr_w).astype(BF16)
    logit_bound = FOX_DH ** 0.5 * LOG2E * jnp.max(jnp.abs(qn_g)) * jnp.max(jnp.abs(kn_g))
    mix = lambda bounded: functools.partial(_mixers, b=b, s=s, bounded_logits=bounded)
    y_fox, y_gla = lax.cond(logit_bound <= FOX_BOUNDED_MAX_LOGIT, mix(True), mix(False),
                            fqa, fka, fv, qbias, gq, gk, gv, gog, misc, lrw_pad, row(lr_b), row(onorm_g))

    return _post(
        x2d, y_gla, y_fox, p2d, wo_b, row(norm2_g), wup_b, conv_w.astype(F32), row(conv_b), wdn_b,
        row(norm3_g), wpg_b, row(b_pg), wpe_b, row(pe_norm_g), b, s, tm=512)


def kernel(x, p, norm1_g, w_in, gla_lr_w, gla_lr_b, gla_onorm_g, fox_b_f, fox_qnorm_g, fox_knorm_g, w_o, norm2_g, w_up, conv_w, conv_b, w_down, norm3_g, w_pe, pe_norm_g, w_pg, b_pg):
    b, s, d = x.shape
    x2d = x.reshape(b * s, d)
    for i in range(p.shape[0]):
        x2d = _layer(x2d, p[i].reshape(b * s, -1), b, s, norm1_g[i], w_in[i:i + 1], gla_lr_w[i], gla_lr_b[i],
                     gla_onorm_g[i], fox_b_f[i], fox_qnorm_g[i], fox_knorm_g[i], w_o[i], norm2_g[i], w_up[i],
                     conv_w[i], conv_b[i], w_down[i], norm3_g[i], w_pe[i], pe_norm_g[i], w_pg[i], b_pg[i])
    return x2d.reshape(b, s, d)
```

```python
import functools

import jax
import jax.numpy as jnp
from jax import lax
from jax.experimental import pallas as pl
from jax.experimental.pallas import tpu as pltpu

F32 = jnp.float32
BF16 = jnp.bfloat16

EPS = 1e-6
LOG2E = 1.4426950408889634
LANES = 128
BF16_SUBLANES = 16
GLA_HEADS, GLA_DK, GLA_DV = 4, 64, 128
GLA_LOWRANK = 16
GLA_INV_TAU = 1.0 / 16.0
GLA_CHUNK = 64
GLA_SUPER = 256
FOX_HEADS, FOX_DH = 8, 64
FOX_TILE = 256
FOX_SLOTS = 3
FOX_BLOCKS_AHEAD = 4
FOX_BOUNDED_MAX_LOGIT = 60.0
FFN_CHUNK = 256
CARRY_ROWS = 8
DOWN_LAG = 2
MISC_F0 = 0
MISC_LR0 = 8
VMEM_LIMIT = 56 * 1024 * 1024


def _dot(a, b):
    return jnp.dot(a, b, preferred_element_type=F32)


def _dot_nt(a, b):
    return lax.dot_general(a, b, (((1,), (1,)), ((), ())), preferred_element_type=F32)


def _dot_tn(a, b):
    return lax.dot_general(a, b, (((0,), (0,)), ((), ())), preferred_element_type=F32)


def _log_sigmoid(z):
    return jnp.minimum(z, 0.0) - jnp.log(1.0 + jnp.exp(-jnp.abs(z)))


def _split_bf16(x, pieces):
    out = []
    for _ in range(pieces):
        p = x.astype(BF16)
        out.append(p)
        x = x - p.astype(F32)
    return out


def _rms(x, gain):
    ms = jnp.mean(x * x, axis=-1, keepdims=True)
    return x * lax.rsqrt(ms + EPS) * gain


def _inproj_kernel(x_ref, g_ref, win_ref, bf_ref, qg_ref, kg_ref, *rest, tiles_per_seq, n_cast):
    cast_in, rest = rest[:n_cast], rest[n_cast:]
    gq_ref, gk_ref, gv_ref, gog_ref, fqa_ref, fka_ref, fv_ref, misc_ref, qbias_ref = rest[:9]
    cast_out = rest[9:9 + n_cast]
    carry_scr, h_scr, wg_ref, wf_ref, wm_ref = rest[9 + n_cast:]

    @pl.when(pl.program_id(0) == 0)
    def _():
        gla_w = wg_ref.shape[1]
        fox_w = wf_ref.shape[1]
        lr0 = gla_w
        fox0 = lr0 + GLA_LOWRANK
        f0 = fox0 + fox_w
        wg_ref[...] = win_ref[:gla_w, :].T.astype(BF16)
        wf_ref[...] = win_ref[fox0:f0, :].T.astype(BF16)
        pad = jnp.zeros((LANES - MISC_LR0 - GLA_LOWRANK, win_ref.shape[1]), F32)
        misc_t = jnp.concatenate([win_ref[f0:f0 + FOX_HEADS, :], win_ref[lr0:fox0, :], pad], axis=0)
        wm_ref[...] = misc_t.T.astype(BF16)

    _inproj_body(x_ref, g_ref, wg_ref, wf_ref, wm_ref, bf_ref, qg_ref, kg_ref, gq_ref, gk_ref, gv_ref, gog_ref,
                 fqa_ref, fka_ref, fv_ref, misc_ref, qbias_ref, carry_scr, h_scr, tiles_per_seq)
    for w_ref, o_ref in zip(cast_in, cast_out):
        o_ref[...] = w_ref[...].astype(BF16)


def _inproj_body(x_ref, g_ref, wg_ref, wf_ref, wm_ref, bf_ref, qg_ref, kg_ref, gq_ref, gk_ref, gv_ref, gog_ref,
                 fqa_ref, fka_ref, fv_ref, misc_ref, qbias_ref, carry_scr, h_scr, tiles_per_seq):
    tm = x_ref.shape[0]
    fw = FOX_HEADS * FOX_DH

    @pl.when(pl.program_id(0) % tiles_per_seq == 0)
    def _():
        carry_scr[...] = jnp.zeros_like(carry_scr)

    h_scr[...] = _rms(x_ref[...], g_ref[...]).astype(BF16)
    h = h_scr[...]
    misc = _dot(h, wm_ref[...])
    misc_ref[...] = misc
    q_all = _dot(h, wf_ref[:, :fw])
    k_all = _dot(h, wf_ref[:, fw:2 * fw])

    blk = FOX_TILE
    trow = lax.broadcasted_iota(jnp.int32, (blk, blk), 0)
    tcol = lax.broadcasted_iota(jnp.int32, (blk, blk), 1)
    tri = (tcol <= trow).astype(BF16)
    n_blk = tm // blk
    logf = jnp.concatenate([_log_sigmoid(misc[r0:r0 + blk] + bf_ref[...]) for r0 in range(0, tm, blk)], axis=1)
    within = sum(_dot(tri, piece) for piece in _split_bf16(logf, 3))
    carry = carry_scr[...]
    c_blocks = []
    for i in range(n_blk):
        cb = within[:, i * LANES:(i + 1) * LANES] + carry
        c_blocks.append(cb)
        carry = cb[blk - 1:blk]
    carry_scr[...] = carry
    c_all = jnp.concatenate(c_blocks, axis=0)
    qbias_ref[...] = (c_all * (-LOG2E)).T

    lane = lax.broadcasted_iota(jnp.int32, (tm, LANES), 1)
    lower = lane < FOX_DH

    def pair_norm(x, g):
        sq = x * x
        ms_lo = jnp.sum(jnp.where(lower, sq, 0.0), axis=-1, keepdims=True)
        ms_hi = jnp.sum(jnp.where(lower, 0.0, sq), axis=-1, keepdims=True)
        ms = jnp.where(lower, ms_lo, ms_hi) * (1.0 / FOX_DH)
        return x * lax.rsqrt(ms + EPS) * g

    fv_ref[...] = _dot(h, wf_ref[:, 2 * fw:]).astype(fv_ref.dtype)
    off = 0
    for ref in (gq_ref, gk_ref, gv_ref, gog_ref):
        n = ref.shape[-1]
        ref[...] = _dot(h, wg_ref[:, off:off + n]).astype(ref.dtype)
        off += n

    for pair in range(fw // LANES):
        ps = slice(pair * LANES, (pair + 1) * LANES)
        qn = pair_norm(q_all[:, ps], qg_ref[...])
        kn = pair_norm(k_all[:, ps], kg_ref[...])
        for e in range(2):
            head = 2 * pair + e
            in_head = (lane >= FOX_DH) if e else lower
            piece0 = 0 if e else FOX_DH
            c_col = jnp.sum(jnp.where(lane == MISC_F0 + head, c_all, 0.0), axis=-1, keepdims=True)
            is_piece = (lane >= piece0) & (lane < piece0 + 3)
            hs = slice(head * LANES, (head + 1) * LANES)
            fqa_ref[:, hs] = jnp.where(in_head, qn, jnp.where(is_piece, 1.0, 0.0)).astype(BF16)
            ka = jnp.where(in_head, kn, 0.0).astype(BF16)
            for j, p in enumerate(_split_bf16(c_col * (-LOG2E), 3)):
                ka = jnp.where(lane == piece0 + j, p, ka)
            fka_ref[:, hs] = ka


def _cast_slab(rows, steps):
    for n_slabs in range(min(steps, rows // BF16_SUBLANES), 0, -1):
        if rows % n_slabs == 0 and (rows // n_slabs) % BF16_SUBLANES == 0:
            return rows // n_slabs
    raise ValueError(f"no bf16-tile aligned slab for {rows} rows")


def _inproj(x2d, g, w_in, bf_pad, qg2, kg2, later_weights, s, tm):
    t, d = x2d.shape
    steps = t // tm
    gla_w = 2 * GLA_HEADS * (GLA_DK + GLA_DV)
    fox_w = 3 * FOX_HEADS * FOX_DH
    assert w_in.shape == (1, gla_w + GLA_LOWRANK + fox_w + FOX_HEADS, d)
    assert MISC_F0 == 0 and MISC_LR0 == FOX_HEADS
    gqk, gvw = GLA_HEADS * GLA_DK, GLA_HEADS * GLA_DV
    widths = (gqk, gqk, gvw, gvw, FOX_HEADS * LANES, FOX_HEADS * LANES, FOX_HEADS * FOX_DH, LANES)
    dtypes = (BF16,) * 7 + (F32,)
    whole = lambda a: pl.BlockSpec(a.shape, lambda i: (0, 0))
    once = lambda a: pl.BlockSpec((None,) + a.shape[1:], lambda i: (0, 0, 0), pipeline_mode=pl.Buffered(1))

    def slab(w):
        rows = _cast_slab(w.shape[0], steps)
        last = w.shape[0] // rows - 1
        return pl.BlockSpec((rows, w.shape[1]), lambda i: (jnp.minimum(i, last), 0))

    outs = pl.pallas_call(
        functools.partial(_inproj_kernel, tiles_per_seq=s // tm, n_cast=len(later_weights)),
        grid=(steps,),
        in_specs=[pl.BlockSpec((tm, d), lambda i: (i, 0)), whole(g), once(w_in)]
                 + [whole(a) for a in (bf_pad, qg2, kg2)]
                 + [slab(w) for w in later_weights],
        out_specs=[pl.BlockSpec((tm, n), lambda i: (i, 0)) for n in widths]
                  + [pl.BlockSpec((None, LANES, tm), lambda i: (i, 0, 0))]
                  + [slab(w) for w in later_weights],
        out_shape=[jax.ShapeDtypeStruct((t, n), dt) for n, dt in zip(widths, dtypes)]
                  + [jax.ShapeDtypeStruct((steps, LANES, tm), F32)]
                  + [jax.ShapeDtypeStruct(w.shape, BF16) for w in later_weights],
        scratch_shapes=[pltpu.VMEM((1, LANES), F32),
                        pltpu.VMEM((tm, d), BF16),
                        pltpu.VMEM((d, gla_w), BF16),
                        pltpu.VMEM((d, fox_w), BF16),
                        pltpu.VMEM((d, LANES), BF16)],
        compiler_params=pltpu.CompilerParams(dimension_semantics=("arbitrary",), vmem_limit_bytes=VMEM_LIMIT),
        name="inproj",
    )(x2d, g, w_in, bf_pad, qg2, kg2, *later_weights)
    return outs[:len(widths) + 1], outs[len(widths) + 1:]


def _gla_stages(q_ref, k_ref, v_ref, og_ref, misc_ref, lrw_ref, lrb_ref, ong_ref, y_ref, st_scr):
    s_len = q_ref.shape[0]
    r = GLA_SUPER
    n_sb = s_len // r
    n_ch = r // GLA_CHUNK
    dkw = GLA_HEADS * GLA_DK
    row = lax.broadcasted_iota(jnp.int32, (r, r), 0)
    col = lax.broadcasted_iota(jnp.int32, (r, r), 1)
    causal_bd = ((row // GLA_CHUNK) == (col // GLA_CHUNK)) & (col <= row)
    tri_bd = causal_bd.astype(BF16)
    head_of_col = lax.broadcasted_iota(jnp.int32, (r, dkw), 1) // GLA_DK
    heads = range(GLA_HEADS)
    sb_rows = lambda n: slice(n * r, (n + 1) * r)
    ch_rows = lambda c: slice(c * GLA_CHUNK, (c + 1) * GLA_CHUNK)
    v_cols = lambda h: slice(h * GLA_DV, (h + 1) * GLA_DV)

    la_pieces = []
    for n in range(n_sb):
        z = _dot(misc_ref[sb_rows(n), :].astype(BF16), lrw_ref[...]) + lrb_ref[...]
        la_pieces.append(_split_bf16(_log_sigmoid(z) * GLA_INV_TAU, 2))
        yield

    qh, ke, kd, decay = [], [], [], []
    for n in range(n_sb):
        b = _dot(tri_bd, la_pieces[n][0]) + _dot(tri_bd, la_pieces[n][1])
        b_last = [b[(c + 1) * GLA_CHUNK - 1:(c + 1) * GLA_CHUNK, :] for c in range(n_ch)]
        b_tot = jnp.concatenate([jnp.broadcast_to(bl, (GLA_CHUNK, dkw)) for bl in b_last], axis=0)
        q = q_ref[sb_rows(n), :].astype(F32)
        k = k_ref[sb_rows(n), :].astype(F32)
        qe = (q * (GLA_DK ** -0.5)) * jnp.exp(b)
        qh.append([jnp.where(head_of_col == h, qe, 0.0).astype(BF16) for h in heads])
        ke.append((k * jnp.exp(-b)).astype(BF16))
        kd_n = k * jnp.exp(b_tot - b)
        kd.append([jnp.where(head_of_col == h, kd_n, 0.0).astype(BF16) for h in heads])
        decay.append([jnp.exp(bl) for bl in b_last])
        yield

    a = []
    for n in range(n_sb):
        a.append([jnp.where(causal_bd, _dot_nt(qh[n][h], ke[n]), 0.0).astype(BF16) for h in heads])
        yield

    o_intra = []
    for n in range(n_sb):
        o_intra.append([_dot(a[n][h], v_ref[sb_rows(n), v_cols(h)]) for h in heads])
        yield

    ds = []
    for n in range(n_sb):
        for c in range(n_ch):
            rows = slice(n * r + c * GLA_CHUNK, n * r + (c + 1) * GLA_CHUNK)
            v_stack = jnp.concatenate([v_ref[rows, v_cols(h)] for h in heads], axis=0)
            kd_stack = jnp.concatenate([kd[n][h][ch_rows(c)] for h in heads], axis=0)
            ds.append(_dot_tn(v_stack, kd_stack))
        yield

    st = st_scr[...]
    o_inter = []
    for n in range(n_sb):
        for c in range(n_ch):
            qe_stack = jnp.concatenate([qh[n][h][ch_rows(c)] for h in heads], axis=0)
            o_inter.append(_dot_nt(qe_stack, st.astype(BF16)))
            st = st * decay[n][c] + ds[n * n_ch + c]
        yield
    st_scr[...] = st

    for n in range(n_sb):
        og = og_ref[sb_rows(n), :].astype(F32)
        gate = og * jax.nn.sigmoid(og)
        for h in heads:
            inter = jnp.concatenate([o_inter[n * n_ch + c][ch_rows(h)] for c in range(n_ch)], axis=0)
            o = o_intra[n][h] + inter
            y_ref[sb_rows(n), v_cols(h)] = (_rms(o, ong_ref[...]) * gate[:, v_cols(h)]).astype(y_ref.dtype)
        yield


def _fox_stages(qa_ref, ka_ref, v_ref, y_ref, vt_scr, s_scr):
    s_len = v_ref.shape[0]
    t = FOX_TILE
    n_tiles = s_len // t
    krow = lax.broadcasted_iota(jnp.int32, (t, t), 0)
    qcol = lax.broadcasted_iota(jnp.int32, (t, t), 1)
    orow = lax.broadcasted_iota(jnp.int32, (LANES, t), 0)
    vt_scr[...] = v_ref[...].astype(F32).T.astype(BF16)

    units = [(i, e) for i in range(n_tiles) for e in range(2)]
    n_slots = s_scr.shape[0]

    def scores(n):
        i, e = units[n]
        hs = slice(e * LANES, (e + 1) * LANES)
        qa = qa_ref[i * t:(i + 1) * t, hs]
        mp = None
        for kb in range(i + 1):
            sc = _dot_nt(ka_ref[kb * t:(kb + 1) * t, hs], qa)
            if kb == i:
                sc = jnp.where(krow <= qcol, sc, -jnp.inf)
            s_scr[n % n_slots, kb] = sc
            bm = jnp.max(sc.reshape(t // 8, 8, t), axis=0)
            mp = bm if mp is None else jnp.maximum(mp, bm)
        return jnp.max(mp, axis=0, keepdims=True)

    def weighted_values(n, m):
        lp = jnp.zeros((8, t), F32)
        acc = jnp.zeros((LANES, t), F32)
        for kb in range(units[n][0] + 1):
            p = jnp.exp2(s_scr[n % n_slots, kb] - m)
            lp = lp + jnp.sum(p.reshape(t // 8, 8, t), axis=0)
            acc = acc + _dot(vt_scr[:, kb * t:(kb + 1) * t], p.astype(BF16))
        return acc / jnp.sum(lp, axis=0, keepdims=True)

    ahead = n_slots - 1
    col_max = {}
    for n in range(min(ahead, len(units))):
        col_max[n] = scores(n)
        yield
    o_even = None
    for n, (i, e) in enumerate(units):
        if n + ahead < len(units):
            col_max[n + ahead] = scores(n + ahead)
            yield
        o_t = weighted_values(n, col_max.pop(n))
        if e == 0:
            o_even = o_t
        else:
            o_pair = jnp.where(orow < FOX_DH, o_even, o_t)
            y_ref[i * t:(i + 1) * t, :] = o_pair.T.astype(y_ref.dtype)
        yield


def _fox_stages_bounded(qa_ref, ka_ref, v_ref, qbias_ref, y_ref, vt_scr, *, tile_rows):
    s_len = v_ref.shape[0]
    t = FOX_TILE
    n_tiles = s_len // t
    pair = pl.program_id(1)
    krow = lax.broadcasted_iota(jnp.int32, (t, t), 0)
    qcol = lax.broadcasted_iota(jnp.int32, (t, t), 1)
    orow = lax.broadcasted_iota(jnp.int32, (LANES, t), 0)
    vt = v_ref[...].astype(F32).T
    vrow = lax.broadcasted_iota(jnp.int32, vt.shape, 0)
    den_row = (FOX_DH, 0)
    vt_scr[0] = jnp.where(vrow < FOX_DH, vt, jnp.where(vrow == den_row[0], 1.0, 0.0)).astype(BF16)
    vt_scr[1] = jnp.where(vrow >= FOX_DH, vt, jnp.where(vrow == den_row[1], 1.0, 0.0)).astype(BF16)

    units = [(i, e) for i in range(n_tiles) for e in range(2)]
    blocks = [(n, kb) for n, (i, e) in enumerate(units) for kb in range(i + 1)]
    acc = {}
    o_even = [None]

    def values(n, kb, p):
        i, e = units[n]
        d = _dot(vt_scr[e, :, kb * t:(kb + 1) * t], p)
        acc[n] = d if n not in acc else acc[n] + d
        if kb == i:
            a = acc.pop(n)
            o_t = a / a[den_row[e]:den_row[e] + 1, :]
            if e == 0:
                o_even[0] = o_t
            else:
                o_pair = jnp.where(orow < FOX_DH, o_even[0], o_t)
                y_ref[i * t:(i + 1) * t, :] = o_pair.T.astype(y_ref.dtype)

    pending = []
    for j, (n, kb) in enumerate(blocks):
        i, e = units[n]
        hs = slice(e * LANES, (e + 1) * LANES)
        q0 = i * t
        m = qbias_ref[q0 // tile_rows, pl.ds(2 * pair + e, 1), q0 % tile_rows:q0 % tile_rows + t]
        sc = _dot_nt(ka_ref[kb * t:(kb + 1) * t, hs], qa_ref[q0:q0 + t, hs])
        if kb == i:
            sc = jnp.where(krow <= qcol, sc, -jnp.inf)
        pending.append((n, kb, jnp.exp2(sc - m).astype(BF16)))
        if len(pending) > FOX_BLOCKS_AHEAD:
            values(*pending.pop(0))
        if j % 2 == 1:
            yield
    for item in pending:
        values(*item)
    yield


def _mixers_kernel(qa_ref, ka_ref, fv_ref, qbias_ref, gq_ref, gk_ref, gv_ref, gog_ref, misc_ref, lrw_ref, lrb_ref,
                   ong_ref, yf_ref, yg_ref, vt_scr, s_scr, st_scr, *, bounded_logits, tile_rows):
    @pl.when(pl.program_id(1) == 0)
    def _():
        st_scr[...] = jnp.zeros_like(st_scr)

    if bounded_logits:
        fox = _fox_stages_bounded(qa_ref, ka_ref, fv_ref, qbias_ref, yf_ref, vt_scr, tile_rows=tile_rows)
    else:
        fox = _fox_stages(qa_ref, ka_ref, fv_ref, yf_ref, vt_scr.at[0], s_scr)
    gla = _gla_stages(gq_ref, gk_ref, gv_ref, gog_ref, misc_ref, lrw_ref, lrb_ref, ong_ref, yg_ref, st_scr)
    rota = (fox, fox, fox, gla) if bounded_logits else (fox, fox, gla)
    finished = set()
    turn = 0
    while len(finished) < 2:
        gen = rota[turn % len(rota)]
        turn += 1
        if gen not in finished:
            try:
                next(gen)
            except StopIteration:
                finished.add(gen)


def _mixers(fqa, fka, fv, qbias, gq, gk, gv, gog, misc, lrw_pad, lrb, ong, b, s, bounded_logits):
    n_pairs = FOX_HEADS * FOX_DH // LANES
    t = FOX_TILE
    dkw, dvw = GLA_HEADS * GLA_DK, GLA_HEADS * GLA_DV
    sq = s // n_pairs
    assert s % t == 0 and sq % GLA_SUPER == 0, "sequence must split into FoX tiles and GLA super-blocks"
    tile_rows = qbias.shape[2]
    heads2 = pl.BlockSpec((s, 2 * LANES), lambda i, j: (i, j))
    pair = pl.BlockSpec((s, LANES), lambda i, j: (i, j))
    seq_bias = pl.BlockSpec((s // tile_rows, FOX_HEADS, tile_rows), lambda i, j: (i, 0, 0))
    quarter = lambda w: pl.BlockSpec((sq, w), lambda i, j: (i * n_pairs + j, 0))
    whole = lambda a: pl.BlockSpec(a.shape, lambda i, j: (0, 0))
    return pl.pallas_call(
        functools.partial(_mixers_kernel, bounded_logits=bounded_logits, tile_rows=tile_rows),
        grid=(b, n_pairs),
        in_specs=[heads2, heads2, pair, seq_bias,
                  quarter(dkw), quarter(dkw), quarter(dvw), quarter(dvw), quarter(LANES),
                  whole(lrw_pad), whole(lrb), whole(ong)],
        out_specs=[pair, quarter(dvw)],
        out_shape=[jax.ShapeDtypeStruct((b * s, FOX_HEADS * FOX_DH), BF16),
                   jax.ShapeDtypeStruct((b * s, dvw), BF16)],
        scratch_shapes=[
            pltpu.VMEM((2, LANES, s), BF16),
            pltpu.VMEM((FOX_SLOTS, s // t, t, t), F32),
            pltpu.VMEM((GLA_DV, dkw), F32),
        ],
        compiler_params=pltpu.CompilerParams(dimension_semantics=("arbitrary", "arbitrary"),
                                             vmem_limit_bytes=VMEM_LIMIT),
        name="mixers_bounded" if bounded_logits else "mixers",
    )(fqa, fka, fv, qbias, gq, gk, gv, gog, misc, lrw_pad, lrb, ong)


def _post_kernel(x_ref, yg_ref, yf_ref, p_ref, wo_ref, g2_ref, wup_ref, cw_ref, cb_ref, wdn_ref,
                 g3_ref, wpg_ref, bpg_ref, wpe_ref, gpe_ref, o_ref,
                 x1_scr, h2_scr, u_scr, act_scr, carry_scr, acc_scr):
    tm = x_ref.shape[0]
    d_ff = wdn_ref.shape[0]
    n_chunks = d_ff // FFN_CHUNK
    n_slabs = 2 * FFN_CHUNK // LANES
    half = n_slabs // 2
    dvw = yg_ref.shape[1]
    first_tile = pl.program_id(1) == 0

    @pl.when(first_tile)
    def _():
        carry_scr[...] = jnp.zeros_like(carry_scr)

    x1 = x_ref[...] + _dot(yg_ref[...], wo_ref[:dvw, :]) + _dot(yf_ref[...], wo_ref[dvw:, :])
    x1_scr[...] = x1
    h2_scr[...] = _rms(x1, g2_ref[...]).astype(BF16)
    acc_scr[...] = jnp.zeros_like(acc_scr)

    o_ref[...] = _rms(_dot(p_ref[...].astype(BF16), wpe_ref[...]), gpe_ref[...])

    for j in range(n_chunks + DOWN_LAG):
        if j < n_chunks:
            ub = u_scr.at[j % 2]
            col0 = [(c // half) * d_ff + j * FFN_CHUNK + (c % half) * LANES for c in range(n_slabs)]
            h2 = h2_scr[...]
            for part in range(2):
                u = _dot(h2, wup_ref[:, col0[part * half]:col0[part * half] + FFN_CHUNK])
                for c in range(part * half, (part + 1) * half):
                    ub[c, :CARRY_ROWS, :] = carry_scr[j, c]
                    ub[c, CARRY_ROWS:, :] = u[:, (c % half) * LANES:(c % half + 1) * LANES]
                    carry_scr[j, c] = ub[c, tm:, :]
        if j >= DOWN_LAG:
            jd = j - DOWN_LAG
            acc_scr[...] += _dot(act_scr[jd % (DOWN_LAG + 1)], wdn_ref[jd * FFN_CHUNK:(jd + 1) * FFN_CHUNK, :])
        if j < n_chunks:
            def conv(c):
                cs = slice(col0[c], col0[c] + LANES)
                return (cb_ref[:, cs]
                        + ub[c, CARRY_ROWS - 2:CARRY_ROWS - 2 + tm, :] * cw_ref[0:1, cs]
                        + ub[c, CARRY_ROWS - 1:CARRY_ROWS - 1 + tm, :] * cw_ref[1:2, cs]
                        + ub[c, CARRY_ROWS:, :] * cw_ref[2:3, cs])
            for c in range(half):
                gate = conv(c)
                act = gate * jax.nn.sigmoid(gate) * conv(c + half)
                act_scr[j % (DOWN_LAG + 1), :, c * LANES:(c + 1) * LANES] = act.astype(BF16)

    x2 = x1_scr[...] + acc_scr[...]
    gate = jax.nn.sigmoid(_dot(_rms(x2, g3_ref[...]).astype(BF16), wpg_ref[...]) + bpg_ref[...])
    o_ref[...] = x2 + gate * o_ref[...]


def _post(x2d, yg, yf, p2d, wo, g2, wup, cw, cb, wdn, g3, wpg, bpg, wpe, gpe, b, s, tm):
    d = x2d.shape[1]
    nt = s // tm
    tile = lambda w: pl.BlockSpec((tm, w), lambda i, j: (i * nt + j, 0))
    resident = lambda a: pl.BlockSpec(a.shape, lambda i, j: (0,) * a.ndim, pipeline_mode=pl.Buffered(1))
    n_chunks = wdn.shape[0] // FFN_CHUNK
    n_slabs = 2 * FFN_CHUNK // LANES
    return pl.pallas_call(
        _post_kernel,
        grid=(b, nt),
        in_specs=[tile(d), tile(yg.shape[1]), tile(yf.shape[1]), tile(p2d.shape[1])]
                 + [resident(a) for a in (wo, g2, wup, cw, cb, wdn, g3, wpg, bpg, wpe, gpe)],
        out_specs=tile(d),
        out_shape=jax.ShapeDtypeStruct(x2d.shape, F32),
        scratch_shapes=[
            pltpu.VMEM((tm, d), F32),
            pltpu.VMEM((tm, d), BF16),
            pltpu.VMEM((2, n_slabs, CARRY_ROWS + tm, LANES), F32),
            pltpu.VMEM((DOWN_LAG + 1, tm, FFN_CHUNK), BF16),
            pltpu.VMEM((n_chunks, n_slabs, CARRY_ROWS, LANES), F32),
            pltpu.VMEM((tm, d), F32),
        ],
        compiler_params=pltpu.CompilerParams(dimension_semantics=("arbitrary", "arbitrary"),
                                             vmem_limit_bytes=VMEM_LIMIT),
        name="post",
    )(x2d, yg, yf, p2d, wo, g2, wup, cw, cb, wdn, g3, wpg, bpg, wpe, gpe)


def _layer(x2d, p2d, b, s, norm1_g, w_in, lr_w, lr_b, onorm_g, b_f, qn_g, kn_g, w_o, norm2_g, w_up,
           conv_w, conv_b, w_down, norm3_g, w_pe, pe_norm_g, w_pg, b_pg):
    gqk = GLA_HEADS * GLA_DK
    row = lambda a: a.reshape(1, -1).astype(F32)
    bf_pad = jnp.zeros((1, LANES), F32).at[0, MISC_F0:MISC_F0 + FOX_HEADS].set(b_f)
    qg2 = row(jnp.tile(qn_g, 2)) * (FOX_DH ** -0.5 * LOG2E)
    (gq, gk, gv, gog, fqa, fka, fv, misc, qbias), (wo_b, wup_b, wdn_b, wpg_b, wpe_b) = _inproj(
        x2d, row(norm1_g), jnp.swapaxes(w_in, 1, 2).astype(F32), bf_pad, qg2, row(jnp.tile(kn_g, 2)),
        (w_o, w_up, w_down, w_pg, w_pe), s, tm=512)

    lrw_pad = jnp.zeros((LANES, gqk), F32).at[MISC_LR0:MISC_LR0 + GLA_LOWRANK].set(lr_w).astype(BF16)
    logit_bound = FOX_DH ** 0.5 * LOG2E * jnp.max(jnp.abs(qn_g)) * jnp.max(jnp.abs(kn_g))
    mix = lambda bounded: functools.partial(_mixers, b=b, s=s, bounded_logits=bounded)
    y_fox, y_gla = lax.cond(logit_bound <= FOX_BOUNDED_MAX_LOGIT, mix(True), mix(False),
                            fqa, fka, fv, qbias, gq, gk, gv, gog, misc, lrw_pad, row(lr_b), row(onorm_g))

    return _post(
        x2d, y_gla, y_fox, p2d, wo_b, row(norm2_g), wup_b, conv_w.astype(F32), row(conv_b), wdn_b,
        row(norm3_g), wpg_b, row(b_pg), wpe_b, row(pe_norm_g), b, s, tm=512)


def kernel(x, p, norm1_g, w_in, gla_lr_w, gla_lr_b, gla_onorm_g, fox_b_f, fox_qnorm_g, fox_knorm_g, w_o, norm2_g, w_up, conv_w, conv_b, w_down, norm3_g, w_pe, pe_norm_g, w_pg, b_pg):
    b, s, d = x.shape
    x2d = x.reshape(b * s, d)
    for i in range(p.shape[0]):
        x2d = _layer(x2d, p[i].reshape(b * s, -1), b, s, norm1_g[i], w_in[i:i + 1], gla_lr_w[i], gla_lr_b[i],
                     gla_onorm_g[i], fox_b_f[i], fox_qnorm_g[i], fox_knorm_g[i], w_o[i], norm2_g[i], w_up[i],
                     conv_w[i], conv_b[i], w_down[i], norm3_g[i], w_pe[i], pe_norm_g[i], w_pg[i], b_pg[i])
    return x2d.reshape(b, s, d)
```

```python
import functools

import jax
import jax.numpy as jnp
from jax import lax
from jax.experimental import pallas as pl
from jax.experimental.pallas import tpu as pltpu

F32 = jnp.float32
BF16 = jnp.bfloat16

EPS = 1e-6
LOG2E = 1.4426950408889634
LANES = 128
BF16_SUBLANES = 16
GLA_HEADS, GLA_DK, GLA_DV = 4, 64, 128
GLA_LOWRANK = 16
GLA_INV_TAU = 1.0 / 16.0
GLA_CHUNK = 64
GLA_SUPER = 256
FOX_HEADS, FOX_DH = 8, 64
FOX_TILE = 256
FOX_SLOTS = 3
FOX_BLOCKS_AHEAD = 4
FOX_BOUNDED_MAX_LOGIT = 60.0
FFN_CHUNK = 256
CARRY_ROWS = 8
DOWN_LAG = 2
UP_SLOTS = 3
MISC_F0 = 0
MISC_LR0 = 8
VMEM_LIMIT = 56 * 1024 * 1024


def _dot(a, b):
    return jnp.dot(a, b, preferred_element_type=F32)


def _dot_nt(a, b):
    return lax.dot_general(a, b, (((1,), (1,)), ((), ())), preferred_element_type=F32)


def _dot_tn(a, b):
    return lax.dot_general(a, b, (((0,), (0,)), ((), ())), preferred_element_type=F32)


def _log_sigmoid(z):
    return jnp.minimum(z, 0.0) - jnp.log(1.0 + jnp.exp(-jnp.abs(z)))


def _split_bf16(x, pieces):
    out = []
    for _ in range(pieces):
        p = x.astype(BF16)
        out.append(p)
        x = x - p.astype(F32)
    return out


def _rms(x, gain):
    ms = jnp.mean(x * x, axis=-1, keepdims=True)
    return x * lax.rsqrt(ms + EPS) * gain


def _inproj_kernel(x_ref, g_ref, win_ref, bf_ref, qg_ref, kg_ref, *rest, tiles_per_seq, n_cast):
    cast_in, rest = rest[:n_cast], rest[n_cast:]
    gq_ref, gk_ref, gv_ref, gog_ref, fqa_ref, fka_ref, fv_ref, misc_ref, qbias_ref = rest[:9]
    cast_out = rest[9:9 + n_cast]
    carry_scr, h_scr, wg_ref, wf_ref, wm_ref = rest[9 + n_cast:]

    @pl.when(pl.program_id(0) == 0)
    def _():
        gla_w = wg_ref.shape[1]
        fox_w = wf_ref.shape[1]
        lr0 = gla_w
        fox0 = lr0 + GLA_LOWRANK
        f0 = fox0 + fox_w
        wg_ref[...] = win_ref[:gla_w, :].T.astype(BF16)
        wf_ref[...] = win_ref[fox0:f0, :].T.astype(BF16)
        pad = jnp.zeros((LANES - MISC_LR0 - GLA_LOWRANK, win_ref.shape[1]), F32)
        misc_t = jnp.concatenate([win_ref[f0:f0 + FOX_HEADS, :], win_ref[lr0:fox0, :], pad], axis=0)
        wm_ref[...] = misc_t.T.astype(BF16)

    _inproj_body(x_ref, g_ref, wg_ref, wf_ref, wm_ref, bf_ref, qg_ref, kg_ref, gq_ref, gk_ref, gv_ref, gog_ref,
                 fqa_ref, fka_ref, fv_ref, misc_ref, qbias_ref, carry_scr, h_scr, tiles_per_seq)
    for w_ref, o_ref in zip(cast_in, cast_out):
        o_ref[...] = w_ref[...].astype(BF16)


def _inproj_body(x_ref, g_ref, wg_ref, wf_ref, wm_ref, bf_ref, qg_ref, kg_ref, gq_ref, gk_ref, gv_ref, gog_ref,
                 fqa_ref, fka_ref, fv_ref, misc_ref, qbias_ref, carry_scr, h_scr, tiles_per_seq):
    tm = x_ref.shape[0]
    fw = FOX_HEADS * FOX_DH

    @pl.when(pl.program_id(0) % tiles_per_seq == 0)
    def _():
        carry_scr[...] = jnp.zeros_like(carry_scr)

    h_scr[...] = _rms(x_ref[...], g_ref[...]).astype(BF16)
    h = h_scr[...]
    misc = _dot(h, wm_ref[...])
    misc_ref[...] = misc
    q_all = _dot(h, wf_ref[:, :fw])
    k_all = _dot(h, wf_ref[:, fw:2 * fw])

    blk = FOX_TILE
    trow = lax.broadcasted_iota(jnp.int32, (blk, blk), 0)
    tcol = lax.broadcasted_iota(jnp.int32, (blk, blk), 1)
    tri = (tcol <= trow).astype(BF16)
    n_blk = tm // blk
    logf = jnp.concatenate([_log_sigmoid(misc[r0:r0 + blk] + bf_ref[...]) for r0 in range(0, tm, blk)], axis=1)
    within = sum(_dot(tri, piece) for piece in _split_bf16(logf, 3))
    carry = carry_scr[...]
    c_blocks = []
    for i in range(n_blk):
        cb = within[:, i * LANES:(i + 1) * LANES] + carry
        c_blocks.append(cb)
        carry = cb[blk - 1:blk]
    carry_scr[...] = carry
    c_all = jnp.concatenate(c_blocks, axis=0)
    qbias_ref[...] = (c_all * (-LOG2E)).T

    lane = lax.broadcasted_iota(jnp.int32, (tm, LANES), 1)
    lower = lane < FOX_DH

    def pair_norm(x, g):
        sq = x * x
        ms_lo = jnp.sum(jnp.where(lower, sq, 0.0), axis=-1, keepdims=True)
        ms_hi = jnp.sum(jnp.where(lower, 0.0, sq), axis=-1, keepdims=True)
        ms = jnp.where(lower, ms_lo, ms_hi) * (1.0 / FOX_DH)
        return x * lax.rsqrt(ms + EPS) * g

    fv_ref[...] = _dot(h, wf_ref[:, 2 * fw:]).astype(fv_ref.dtype)
    off = 0
    for ref in (gq_ref, gk_ref, gv_ref, gog_ref):
        n = ref.shape[-1]
        ref[...] = _dot(h, wg_ref[:, off:off + n]).astype(ref.dtype)
        off += n

    for pair in range(fw // LANES):
        ps = slice(pair * LANES, (pair + 1) * LANES)
        qn = pair_norm(q_all[:, ps], qg_ref[...])
        kn = pair_norm(k_all[:, ps], kg_ref[...])
        for e in range(2):
            head = 2 * pair + e
            in_head = (lane >= FOX_DH) if e else lower
            piece0 = 0 if e else FOX_DH
            c_col = jnp.sum(jnp.where(lane == MISC_F0 + head, c_all, 0.0), axis=-1, keepdims=True)
            is_piece = (lane >= piece0) & (lane < piece0 + 3)
            hs = slice(head * LANES, (head + 1) * LANES)
            fqa_ref[:, hs] = jnp.where(in_head, qn, jnp.where(is_piece, 1.0, 0.0)).astype(BF16)
            ka = jnp.where(in_head, kn, 0.0).astype(BF16)
            for j, p in enumerate(_split_bf16(c_col * (-LOG2E), 3)):
                ka = jnp.where(lane == piece0 + j, p, ka)
            fka_ref[:, hs] = ka


def _cast_slab(rows, steps):
    for n_slabs in range(min(steps, rows // BF16_SUBLANES), 0, -1):
        if rows % n_slabs == 0 and (rows // n_slabs) % BF16_SUBLANES == 0:
            return rows // n_slabs
    raise ValueError(f"no bf16-tile aligned slab for {rows} rows")


def _inproj(x2d, g, w_in, bf_pad, qg2, kg2, later_weights, s, tm):
    t, d = x2d.shape
    steps = t // tm
    gla_w = 2 * GLA_HEADS * (GLA_DK + GLA_DV)
    fox_w = 3 * FOX_HEADS * FOX_DH
    assert w_in.shape == (1, gla_w + GLA_LOWRANK + fox_w + FOX_HEADS, d)
    assert MISC_F0 == 0 and MISC_LR0 == FOX_HEADS
    gqk, gvw = GLA_HEADS * GLA_DK, GLA_HEADS * GLA_DV
    widths = (gqk, gqk, gvw, gvw, FOX_HEADS * LANES, FOX_HEADS * LANES, FOX_HEADS * FOX_DH, LANES)
    dtypes = (BF16,) * 7 + (F32,)
    whole = lambda a: pl.BlockSpec(a.shape, lambda i: (0, 0))
    once = lambda a: pl.BlockSpec((None,) + a.shape[1:], lambda i: (0, 0, 0), pipeline_mode=pl.Buffered(1))

    def slab(w):
        rows = _cast_slab(w.shape[0], steps)
        last = w.shape[0] // rows - 1
        return pl.BlockSpec((rows, w.shape[1]), lambda i: (jnp.minimum(i, last), 0))

    outs = pl.pallas_call(
        functools.partial(_inproj_kernel, tiles_per_seq=s // tm, n_cast=len(later_weights)),
        grid=(steps,),
        in_specs=[pl.BlockSpec((tm, d), lambda i: (i, 0)), whole(g), once(w_in)]
                 + [whole(a) for a in (bf_pad, qg2, kg2)]
                 + [slab(w) for w in later_weights],
        out_specs=[pl.BlockSpec((tm, n), lambda i: (i, 0)) for n in widths]
                  + [pl.BlockSpec((None, LANES, tm), lambda i: (i, 0, 0))]
                  + [slab(w) for w in later_weights],
        out_shape=[jax.ShapeDtypeStruct((t, n), dt) for n, dt in zip(widths, dtypes)]
                  + [jax.ShapeDtypeStruct((steps, LANES, tm), F32)]
                  + [jax.ShapeDtypeStruct(w.shape, BF16) for w in later_weights],
        scratch_shapes=[pltpu.VMEM((1, LANES), F32),
                        pltpu.VMEM((tm, d), BF16),
                        pltpu.VMEM((d, gla_w), BF16),
                        pltpu.VMEM((d, fox_w), BF16),
                        pltpu.VMEM((d, LANES), BF16)],
        compiler_params=pltpu.CompilerParams(dimension_semantics=("arbitrary",), vmem_limit_bytes=VMEM_LIMIT),
        name="inproj",
    )(x2d, g, w_in, bf_pad, qg2, kg2, *later_weights)
    return outs[:len(widths) + 1], outs[len(widths) + 1:]


def _gla_stages(q_ref, k_ref, v_ref, og_ref, misc_ref, lrw_ref, lrb_ref, ong_ref, y_ref, st_scr):
    s_len = q_ref.shape[0]
    r = GLA_SUPER
    n_sb = s_len // r
    n_ch = r // GLA_CHUNK
    dkw = GLA_HEADS * GLA_DK
    row = lax.broadcasted_iota(jnp.int32, (r, r), 0)
    col = lax.broadcasted_iota(jnp.int32, (r, r), 1)
    causal_bd = ((row // GLA_CHUNK) == (col // GLA_CHUNK)) & (col <= row)
    tri_bd = causal_bd.astype(BF16)
    head_of_col = lax.broadcasted_iota(jnp.int32, (r, dkw), 1) // GLA_DK
    heads = range(GLA_HEADS)
    sb_rows = lambda n: slice(n * r, (n + 1) * r)
    ch_rows = lambda c: slice(c * GLA_CHUNK, (c + 1) * GLA_CHUNK)
    v_cols = lambda h: slice(h * GLA_DV, (h + 1) * GLA_DV)

    la_pieces = []
    for n in range(n_sb):
        z = _dot(misc_ref[sb_rows(n), :].astype(BF16), lrw_ref[...]) + lrb_ref[...]
        la_pieces.append(_split_bf16(_log_sigmoid(z) * GLA_INV_TAU, 2))
        yield

    qh, ke, kd, decay = [], [], [], []
    for n in range(n_sb):
        b = _dot(tri_bd, la_pieces[n][0]) + _dot(tri_bd, la_pieces[n][1])
        b_last = [b[(c + 1) * GLA_CHUNK - 1:(c + 1) * GLA_CHUNK, :] for c in range(n_ch)]
        b_tot = jnp.concatenate([jnp.broadcast_to(bl, (GLA_CHUNK, dkw)) for bl in b_last], axis=0)
        q = q_ref[sb_rows(n), :].astype(F32)
        k = k_ref[sb_rows(n), :].astype(F32)
        qe = (q * (GLA_DK ** -0.5)) * jnp.exp(b)
        qh.append([jnp.where(head_of_col == h, qe, 0.0).astype(BF16) for h in heads])
        ke.append((k * jnp.exp(-b)).astype(BF16))
        kd_n = k * jnp.exp(b_tot - b)
        kd.append([jnp.where(head_of_col == h, kd_n, 0.0).astype(BF16) for h in heads])
        decay.append([jnp.exp(bl) for bl in b_last])
        yield

    a = []
    for n in range(n_sb):
        a.append([jnp.where(causal_bd, _dot_nt(qh[n][h], ke[n]), 0.0).astype(BF16) for h in heads])
        yield

    o_intra = []
    for n in range(n_sb):
        o_intra.append([_dot(a[n][h], v_ref[sb_rows(n), v_cols(h)]) for h in heads])
        yield

    ds = []
    for n in range(n_sb):
        for c in range(n_ch):
            rows = slice(n * r + c * GLA_CHUNK, n * r + (c + 1) * GLA_CHUNK)
            v_stack = jnp.concatenate([v_ref[rows, v_cols(h)] for h in heads], axis=0)
            kd_stack = jnp.concatenate([kd[n][h][ch_rows(c)] for h in heads], axis=0)
            ds.append(_dot_tn(v_stack, kd_stack))
        yield

    st = st_scr[...]
    o_inter = []
    for n in range(n_sb):
        for c in range(n_ch):
            qe_stack = jnp.concatenate([qh[n][h][ch_rows(c)] for h in heads], axis=0)
            o_inter.append(_dot_nt(qe_stack, st.astype(BF16)))
            st = st * decay[n][c] + ds[n * n_ch + c]
        yield
    st_scr[...] = st

    for n in range(n_sb):
        og = og_ref[sb_rows(n), :].astype(F32)
        gate = og * jax.nn.sigmoid(og)
        for h in heads:
            inter = jnp.concatenate([o_inter[n * n_ch + c][ch_rows(h)] for c in range(n_ch)], axis=0)
            o = o_intra[n][h] + inter
            y_ref[sb_rows(n), v_cols(h)] = (_rms(o, ong_ref[...]) * gate[:, v_cols(h)]).astype(y_ref.dtype)
        yield


def _fox_stages(qa_ref, ka_ref, v_ref, y_ref, vt_scr, s_scr):
    s_len = v_ref.shape[0]
    t = FOX_TILE
    n_tiles = s_len // t
    krow = lax.broadcasted_iota(jnp.int32, (t, t), 0)
    qcol = lax.broadcasted_iota(jnp.int32, (t, t), 1)
    orow = lax.broadcasted_iota(jnp.int32, (LANES, t), 0)
    vt_scr[...] = v_ref[...].astype(F32).T.astype(BF16)

    units = [(i, e) for i in range(n_tiles) for e in range(2)]
    n_slots = s_scr.shape[0]

    def scores(n):
        i, e = units[n]
        hs = slice(e * LANES, (e + 1) * LANES)
        qa = qa_ref[i * t:(i + 1) * t, hs]
        mp = None
        for kb in range(i + 1):
            sc = _dot_nt(ka_ref[kb * t:(kb + 1) * t, hs], qa)
            if kb == i:
                sc = jnp.where(krow <= qcol, sc, -jnp.inf)
            s_scr[n % n_slots, kb] = sc
            bm = jnp.max(sc.reshape(t // 8, 8, t), axis=0)
            mp = bm if mp is None else jnp.maximum(mp, bm)
        return jnp.max(mp, axis=0, keepdims=True)

    def weighted_values(n, m):
        lp = jnp.zeros((8, t), F32)
        acc = jnp.zeros((LANES, t), F32)
        for kb in range(units[n][0] + 1):
            p = jnp.exp2(s_scr[n % n_slots, kb] - m)
            lp = lp + jnp.sum(p.reshape(t // 8, 8, t), axis=0)
            acc = acc + _dot(vt_scr[:, kb * t:(kb + 1) * t], p.astype(BF16))
        return acc / jnp.sum(lp, axis=0, keepdims=True)

    ahead = n_slots - 1
    col_max = {}
    for n in range(min(ahead, len(units))):
        col_max[n] = scores(n)
        yield
    o_even = None
    for n, (i, e) in enumerate(units):
        if n + ahead < len(units):
            col_max[n + ahead] = scores(n + ahead)
            yield
        o_t = weighted_values(n, col_max.pop(n))
        if e == 0:
            o_even = o_t
        else:
            o_pair = jnp.where(orow < FOX_DH, o_even, o_t)
            y_ref[i * t:(i + 1) * t, :] = o_pair.T.astype(y_ref.dtype)
        yield


def _fox_stages_bounded(qa_ref, ka_ref, v_ref, qbias_ref, y_ref, vt_scr, *, tile_rows):
    s_len = v_ref.shape[0]
    t = FOX_TILE
    n_tiles = s_len // t
    pair = pl.program_id(1)
    krow = lax.broadcasted_iota(jnp.int32, (t, t), 0)
    qcol = lax.broadcasted_iota(jnp.int32, (t, t), 1)
    orow = lax.broadcasted_iota(jnp.int32, (LANES, t), 0)
    vt = v_ref[...].astype(F32).T
    vrow = lax.broadcasted_iota(jnp.int32, vt.shape, 0)
    den_row = (FOX_DH, 0)
    vt_scr[0] = jnp.where(vrow < FOX_DH, vt, jnp.where(vrow == den_row[0], 1.0, 0.0)).astype(BF16)
    vt_scr[1] = jnp.where(vrow >= FOX_DH, vt, jnp.where(vrow == den_row[1], 1.0, 0.0)).astype(BF16)

    units = [(i, e) for i in range(n_tiles) for e in range(2)]
    blocks = [(n, kb) for n, (i, e) in enumerate(units) for kb in range(i + 1)]
    acc = {}
    o_even = [None]

    def values(n, kb, p):
        i, e = units[n]
        d = _dot(vt_scr[e, :, kb * t:(kb + 1) * t], p)
        acc[n] = d if n not in acc else acc[n] + d
        if kb == i:
            a = acc.pop(n)
            o_t = a / a[den_row[e]:den_row[e] + 1, :]
            if e == 0:
                o_even[0] = o_t
            else:
                o_pair = jnp.where(orow < FOX_DH, o_even[0], o_t)
                y_ref[i * t:(i + 1) * t, :] = o_pair.T.astype(y_ref.dtype)

    pending = []
    for j, (n, kb) in enumerate(blocks):
        i, e = units[n]
        hs = slice(e * LANES, (e + 1) * LANES)
        q0 = i * t
        m = qbias_ref[q0 // tile_rows, pl.ds(2 * pair + e, 1), q0 % tile_rows:q0 % tile_rows + t]
        sc = _dot_nt(ka_ref[kb * t:(kb + 1) * t, hs], qa_ref[q0:q0 + t, hs])
        if kb == i:
            sc = jnp.where(krow <= qcol, sc, -jnp.inf)
        pending.append((n, kb, jnp.exp2(sc - m).astype(BF16)))
        if len(pending) > FOX_BLOCKS_AHEAD:
            values(*pending.pop(0))
        if j % 2 == 1:
            yield
    for item in pending:
        values(*item)
    yield


def _mixers_kernel(qa_ref, ka_ref, fv_ref, qbias_ref, gq_ref, gk_ref, gv_ref, gog_ref, misc_ref, lrw_ref, lrb_ref,
                   ong_ref, yf_ref, yg_ref, vt_scr, s_scr, st_scr, *, bounded_logits, tile_rows):
    @pl.when(pl.program_id(1) == 0)
    def _():
        st_scr[...] = jnp.zeros_like(st_scr)

    if bounded_logits:
        fox = _fox_stages_bounded(qa_ref, ka_ref, fv_ref, qbias_ref, yf_ref, vt_scr, tile_rows=tile_rows)
    else:
        fox = _fox_stages(qa_ref, ka_ref, fv_ref, yf_ref, vt_scr.at[0], s_scr)
    gla = _gla_stages(gq_ref, gk_ref, gv_ref, gog_ref, misc_ref, lrw_ref, lrb_ref, ong_ref, yg_ref, st_scr)
    rota = (fox, fox, fox, gla) if bounded_logits else (fox, fox, gla)
    finished = set()
    turn = 0
    while len(finished) < 2:
        gen = rota[turn % len(rota)]
        turn += 1
        if gen not in finished:
            try:
                next(gen)
            except StopIteration:
                finished.add(gen)


def _mixers(fqa, fka, fv, qbias, gq, gk, gv, gog, misc, lrw_pad, lrb, ong, b, s, bounded_logits):
    n_pairs = FOX_HEADS * FOX_DH // LANES
    t = FOX_TILE
    dkw, dvw = GLA_HEADS * GLA_DK, GLA_HEADS * GLA_DV
    sq = s // n_pairs
    assert s % t == 0 and sq % GLA_SUPER == 0, "sequence must split into FoX tiles and GLA super-blocks"
    tile_rows = qbias.shape[2]
    heads2 = pl.BlockSpec((s, 2 * LANES), lambda i, j: (i, j))
    pair = pl.BlockSpec((s, LANES), lambda i, j: (i, j))
    seq_bias = pl.BlockSpec((s // tile_rows, FOX_HEADS, tile_rows), lambda i, j: (i, 0, 0))
    quarter = lambda w: pl.BlockSpec((sq, w), lambda i, j: (i * n_pairs + j, 0))
    whole = lambda a: pl.BlockSpec(a.shape, lambda i, j: (0, 0))
    return pl.pallas_call(
        functools.partial(_mixers_kernel, bounded_logits=bounded_logits, tile_rows=tile_rows),
        grid=(b, n_pairs),
        in_specs=[heads2, heads2, pair, seq_bias,
                  quarter(dkw), quarter(dkw), quarter(dvw), quarter(dvw), quarter(LANES),
                  whole(lrw_pad), whole(lrb), whole(ong)],
        out_specs=[pair, quarter(dvw)],
        out_shape=[jax.ShapeDtypeStruct((b * s, FOX_HEADS * FOX_DH), BF16),
                   jax.ShapeDtypeStruct((b * s, dvw), BF16)],
        scratch_shapes=[
            pltpu.VMEM((2, LANES, s), BF16),
            pltpu.VMEM((FOX_SLOTS, s // t, t, t), F32),
            pltpu.VMEM((GLA_DV, dkw), F32),
        ],
        compiler_params=pltpu.CompilerParams(dimension_semantics=("arbitrary", "arbitrary"),
                                             vmem_limit_bytes=VMEM_LIMIT),
        name="mixers_bounded" if bounded_logits else "mixers",
    )(fqa, fka, fv, qbias, gq, gk, gv, gog, misc, lrw_pad, lrb, ong)


def _post_kernel(x_ref, yg_ref, yf_ref, p_ref, wo_ref, g2_ref, wup_ref, cw_ref, cb_ref, wdn_ref,
                 g3_ref, wpg_ref, bpg_ref, wpe_ref, gpe_ref, o_ref,
                 x1_scr, h2_scr, u_scr, act_scr, carry_scr, acc_scr):
    tm = x_ref.shape[0]
    d_ff = wdn_ref.shape[0]
    n_chunks = d_ff // FFN_CHUNK
    n_slabs = 2 * FFN_CHUNK // LANES
    half = n_slabs // 2
    dvw = yg_ref.shape[1]
    first_tile = pl.program_id(1) == 0

    @pl.when(first_tile)
    def _():
        carry_scr[...] = jnp.zeros_like(carry_scr)

    x1 = x_ref[...] + _dot(yg_ref[...], wo_ref[:dvw, :]) + _dot(yf_ref[...], wo_ref[dvw:, :])
    x1_scr[...] = x1
    h2_scr[...] = _rms(x1, g2_ref[...]).astype(BF16)
    acc_scr[...] = jnp.zeros_like(acc_scr)

    o_ref[...] = _rms(_dot(p_ref[...].astype(BF16), wpe_ref[...]), gpe_ref[...])

    for j in range(n_chunks + DOWN_LAG):
        if j < n_chunks:
            ub = u_scr.at[j % UP_SLOTS]
            col0 = [(c // half) * d_ff + j * FFN_CHUNK + (c % half) * LANES for c in range(n_slabs)]
            h2 = h2_scr[...]
            for part in range(2):
                u = _dot(h2, wup_ref[:, col0[part * half]:col0[part * half] + FFN_CHUNK])
                for c in range(part * half, (part + 1) * half):
                    ub[c, :CARRY_ROWS, :] = carry_scr[j, c]
                    ub[c, CARRY_ROWS:, :] = u[:, (c % half) * LANES:(c % half + 1) * LANES]
                    carry_scr[j, c] = ub[c, tm:, :]
        if j >= DOWN_LAG:
            jd = j - DOWN_LAG
            acc_scr[...] += _dot(act_scr[jd % (DOWN_LAG + 1)], wdn_ref[jd * FFN_CHUNK:(jd + 1) * FFN_CHUNK, :])
        if j < n_chunks:
            def conv(c):
                cs = slice(col0[c], col0[c] + LANES)
                return (cb_ref[:, cs]
                        + ub[c, CARRY_ROWS - 2:CARRY_ROWS - 2 + tm, :] * cw_ref[0:1, cs]
                        + ub[c, CARRY_ROWS - 1:CARRY_ROWS - 1 + tm, :] * cw_ref[1:2, cs]
                        + ub[c, CARRY_ROWS:, :] * cw_ref[2:3, cs])
            for c in range(half):
                gate = conv(c)
                act = gate * jax.nn.sigmoid(gate) * conv(c + half)
                act_scr[j % (DOWN_LAG + 1), :, c * LANES:(c + 1) * LANES] = act.astype(BF16)

    x2 = x1_scr[...] + acc_scr[...]
    gate = jax.nn.sigmoid(_dot(_rms(x2, g3_ref[...]).astype(BF16), wpg_ref[...]) + bpg_ref[...])
    o_ref[...] = x2 + gate * o_ref[...]


def _post(x2d, yg, yf, p2d, wo, g2, wup, cw, cb, wdn, g3, wpg, bpg, wpe, gpe, b, s, tm):
    d = x2d.shape[1]
    nt = s // tm
    tile = lambda w: pl.BlockSpec((tm, w), lambda i, j: (i * nt + j, 0))
    resident = lambda a: pl.BlockSpec(a.shape, lambda i, j: (0,) * a.ndim, pipeline_mode=pl.Buffered(1))
    n_chunks = wdn.shape[0] // FFN_CHUNK
    n_slabs = 2 * FFN_CHUNK // LANES
    return pl.pallas_call(
        _post_kernel,
        grid=(b, nt),
        in_specs=[tile(d), tile(yg.shape[1]), tile(yf.shape[1]), tile(p2d.shape[1])]
                 + [resident(a) for a in (wo, g2, wup, cw, cb, wdn, g3, wpg, bpg, wpe, gpe)],
        out_specs=tile(d),
        out_shape=jax.ShapeDtypeStruct(x2d.shape, F32),
        scratch_shapes=[
            pltpu.VMEM((tm, d), F32),
            pltpu.VMEM((tm, d), BF16),
            pltpu.VMEM((UP_SLOTS, n_slabs, CARRY_ROWS + tm, LANES), F32),
            pltpu.VMEM((DOWN_LAG + 1, tm, FFN_CHUNK), BF16),
            pltpu.VMEM((n_chunks, n_slabs, CARRY_ROWS, LANES), F32),
            pltpu.VMEM((tm, d), F32),
        ],
        compiler_params=pltpu.CompilerParams(dimension_semantics=("arbitrary", "arbitrary"),
                                             vmem_limit_bytes=VMEM_LIMIT),
        name="post",
    )(x2d, yg, yf, p2d, wo, g2, wup, cw, cb, wdn, g3, wpg, bpg, wpe, gpe)


def _layer(x2d, p2d, b, s, norm1_g, w_in, lr_w, lr_b, onorm_g, b_f, qn_g, kn_g, w_o, norm2_g, w_up,
           conv_w, conv_b, w_down, norm3_g, w_pe, pe_norm_g, w_pg, b_pg):
    gqk = GLA_HEADS * GLA_DK
    row = lambda a: a.reshape(1, -1).astype(F32)
    bf_pad = jnp.zeros((1, LANES), F32).at[0, MISC_F0:MISC_F0 + FOX_HEADS].set(b_f)
    qg2 = row(jnp.tile(qn_g, 2)) * (FOX_DH ** -0.5 * LOG2E)
    (gq, gk, gv, gog, fqa, fka, fv, misc, qbias), (wo_b, wup_b, wdn_b, wpg_b, wpe_b) = _inproj(
        x2d, row(norm1_g), jnp.swapaxes(w_in, 1, 2).astype(F32), bf_pad, qg2, row(jnp.tile(kn_g, 2)),
        (w_o, w_up, w_down, w_pg, w_pe), s, tm=512)

    lrw_pad = jnp.zeros((LANES, gqk), F32).at[MISC_LR0:MISC_LR0 + GLA_LOWRANK].set(lr_w).astype(BF16)
    logit_bound = FOX_DH ** 0.5 * LOG2E * jnp.max(jnp.abs(qn_g)) * jnp.max(jnp.abs(kn_g))
    mix = lambda bounded: functools.partial(_mixers, b=b, s=s, bounded_logits=bounded)
    y_fox, y_gla = lax.cond(logit_bound <= FOX_BOUNDED_MAX_LOGIT, mix(True), mix(False),
                            fqa, fka, fv, qbias, gq, gk, gv, gog, misc, lrw_pad, row(lr_b), row(onorm_g))

    return _post(
        x2d, y_gla, y_fox, p2d, wo_b, row(norm2_g), wup_b, conv_w.astype(F32), row(conv_b), wdn_b,
        row(norm3_g), wpg_b, row(b_pg), wpe_b, row(pe_norm_g), b, s, tm=512)


def kernel(x, p, norm1_g, w_in, gla_lr_w, gla_lr_b, gla_onorm_g, fox_b_f, fox_qnorm_g, fox_knorm_g, w_o, norm2_g, w_up, conv_w, conv_b, w_down, norm3_g, w_pe, pe_norm_g, w_pg, b_pg):
    b, s, d = x.shape
    x2d = x.reshape(b * s, d)
    for i in range(p.shape[0]):
        x2d = _layer(x2d, p[i].reshape(b * s, -1), b, s, norm1_g[i], w_in[i:i + 1], gla_lr_w[i], gla_lr_b[i],
                     gla_onorm_g[i], fox_b_f[i], fox_qnorm_g[i], fox_knorm_g[i], w_o[i], norm2_g[i], w_up[i],
                     conv_w[i], conv_b[i], w_down[i], norm3_g[i], w_pe[i], pe_norm_g[i], w_pg[i], b_pg[i])
    return x2d.reshape(b, s, d)
```

```python
import functools

import jax
import jax.numpy as jnp
from jax import lax
from jax.experimental import pallas as pl
from jax.experimental.pallas import tpu as pltpu

F32 = jnp.float32
BF16 = jnp.bfloat16

EPS = 1e-6
LOG2E = 1.4426950408889634
LANES = 128
BF16_SUBLANES = 16
GLA_HEADS, GLA_DK, GLA_DV = 4, 64, 128
GLA_LOWRANK = 16
GLA_INV_TAU = 1.0 / 16.0
GLA_CHUNK = 64
GLA_SUPER = 256
FOX_HEADS, FOX_DH = 8, 64
FOX_TILE = 256
FOX_SLOTS = 3
FOX_BLOCKS_AHEAD = 4
FOX_BOUNDED_MAX_LOGIT = 60.0
FFN_CHUNK = 256
CARRY_ROWS = 8
DOWN_LAG = 2
MISC_F0 = 0
MISC_LR0 = 8
VMEM_LIMIT = 56 * 1024 * 1024


def _dot(a, b):
    return jnp.dot(a, b, preferred_element_type=F32)


def _dot_nt(a, b):
    return lax.dot_general(a, b, (((1,), (1,)), ((), ())), preferred_element_type=F32)


def _dot_tn(a, b):
    return lax.dot_general(a, b, (((0,), (0,)), ((), ())), preferred_element_type=F32)


def _log_sigmoid(z):
    return jnp.minimum(z, 0.0) - jnp.log(1.0 + jnp.exp(-jnp.abs(z)))


def _split_bf16(x, pieces):
    out = []
    for _ in range(pieces):
        p = x.astype(BF16)
        out.append(p)
        x = x - p.astype(F32)
    return out


def _rms(x, gain):
    ms = jnp.mean(x * x, axis=-1, keepdims=True)
    return x * lax.rsqrt(ms + EPS) * gain


def _inproj_kernel(x_ref, g_ref, win_ref, bf_ref, qg_ref, kg_ref, *rest, tiles_per_seq, n_cast):
    cast_in, rest = rest[:n_cast], rest[n_cast:]
    gq_ref, gk_ref, gv_ref, gog_ref, fqa_ref, fka_ref, fv_ref, misc_ref, qbias_ref = rest[:9]
    cast_out = rest[9:9 + n_cast]
    carry_scr, h_scr, wg_ref, wf_ref, wm_ref = rest[9 + n_cast:]

    @pl.when(pl.program_id(0) == 0)
    def _():
        gla_w = wg_ref.shape[1]
        fox_w = wf_ref.shape[1]
        lr0 = gla_w
        fox0 = lr0 + GLA_LOWRANK
        f0 = fox0 + fox_w
        wg_ref[...] = win_ref[:gla_w, :].T.astype(BF16)
        wf_ref[...] = win_ref[fox0:f0, :].T.astype(BF16)
        pad = jnp.zeros((LANES - MISC_LR0 - GLA_LOWRANK, win_ref.shape[1]), F32)
        misc_t = jnp.concatenate([win_ref[f0:f0 + FOX_HEADS, :], win_ref[lr0:fox0, :], pad], axis=0)
        wm_ref[...] = misc_t.T.astype(BF16)

    _inproj_body(x_ref, g_ref, wg_ref, wf_ref, wm_ref, bf_ref, qg_ref, kg_ref, gq_ref, gk_ref, gv_ref, gog_ref,
                 fqa_ref, fka_ref, fv_ref, misc_ref, qbias_ref, carry_scr, h_scr, tiles_per_seq)
    for w_ref, o_ref in zip(cast_in, cast_out):
        o_ref[...] = w_ref[...].astype(BF16)


def _inproj_body(x_ref, g_ref, wg_ref, wf_ref, wm_ref, bf_ref, qg_ref, kg_ref, gq_ref, gk_ref, gv_ref, gog_ref,
                 fqa_ref, fka_ref, fv_ref, misc_ref, qbias_ref, carry_scr, h_scr, tiles_per_seq):
    tm = x_ref.shape[0]
    fw = FOX_HEADS * FOX_DH

    @pl.when(pl.program_id(0) % tiles_per_seq == 0)
    def _():
        carry_scr[...] = jnp.zeros_like(carry_scr)

    h_scr[...] = _rms(x_ref[...], g_ref[...]).astype(BF16)
    h = h_scr[...]
    misc = _dot(h, wm_ref[...])
    misc_ref[...] = misc
    q_all = _dot(h, wf_ref[:, :fw])
    k_all = _dot(h, wf_ref[:, fw:2 * fw])

    blk = FOX_TILE
    trow = lax.broadcasted_iota(jnp.int32, (blk, blk), 0)
    tcol = lax.broadcasted_iota(jnp.int32, (blk, blk), 1)
    tri = (tcol <= trow).astype(BF16)
    n_blk = tm // blk
    logf = jnp.concatenate([_log_sigmoid(misc[r0:r0 + blk] + bf_ref[...]) for r0 in range(0, tm, blk)], axis=1)
    within = sum(_dot(tri, piece) for piece in _split_bf16(logf, 3))
    carry = carry_scr[...]
    c_blocks = []
    for i in range(n_blk):
        cb = within[:, i * LANES:(i + 1) * LANES] + carry
        c_blocks.append(cb)
        carry = cb[blk - 1:blk]
    carry_scr[...] = carry
    c_all = jnp.concatenate(c_blocks, axis=0)
    qbias_ref[...] = (c_all * (-LOG2E)).T

    lane = lax.broadcasted_iota(jnp.int32, (tm, LANES), 1)
    lower = lane < FOX_DH

    def pair_norm(x, g):
        sq = x * x
        ms_lo = jnp.sum(jnp.where(lower, sq, 0.0), axis=-1, keepdims=True)
        ms_hi = jnp.sum(jnp.where(lower, 0.0, sq), axis=-1, keepdims=True)
        ms = jnp.where(lower, ms_lo, ms_hi) * (1.0 / FOX_DH)
        return x * lax.rsqrt(ms + EPS) * g

    fv_ref[...] = _dot(h, wf_ref[:, 2 * fw:]).astype(fv_ref.dtype)
    off = 0
    for ref in (gq_ref, gk_ref, gv_ref, gog_ref):
        n = ref.shape[-1]
        ref[...] = _dot(h, wg_ref[:, off:off + n]).astype(ref.dtype)
        off += n

    for pair in range(fw // LANES):
        ps = slice(pair * LANES, (pair + 1) * LANES)
        qn = pair_norm(q_all[:, ps], qg_ref[...])
        kn = pair_norm(k_all[:, ps], kg_ref[...])
        for e in range(2):
            head = 2 * pair + e
            in_head = (lane >= FOX_DH) if e else lower
            piece0 = 0 if e else FOX_DH
            c_col = jnp.sum(jnp.where(lane == MISC_F0 + head, c_all, 0.0), axis=-1, keepdims=True)
            is_piece = (lane >= piece0) & (lane < piece0 + 3)
            hs = slice(head * LANES, (head + 1) * LANES)
            fqa_ref[:, hs] = jnp.where(in_head, qn, jnp.where(is_piece, 1.0, 0.0)).astype(BF16)
            ka = jnp.where(in_head, kn, 0.0).astype(BF16)
            for j, p in enumerate(_split_bf16(c_col * (-LOG2E), 3)):
                ka = jnp.where(lane == piece0 + j, p, ka)
            fka_ref[:, hs] = ka


def _cast_slab(rows, steps):
    for n_slabs in range(min(steps, rows // BF16_SUBLANES), 0, -1):
        if rows % n_slabs == 0 and (rows // n_slabs) % BF16_SUBLANES == 0:
            return rows // n_slabs
    raise ValueError(f"no bf16-tile aligned slab for {rows} rows")


def _inproj(x2d, g, w_in, bf_pad, qg2, kg2, later_weights, s, tm):
    t, d = x2d.shape
    steps = t // tm
    gla_w = 2 * GLA_HEADS * (GLA_DK + GLA_DV)
    fox_w = 3 * FOX_HEADS * FOX_DH
    assert w_in.shape == (1, gla_w + GLA_LOWRANK + fox_w + FOX_HEADS, d)
    assert MISC_F0 == 0 and MISC_LR0 == FOX_HEADS
    gqk, gvw = GLA_HEADS * GLA_DK, GLA_HEADS * GLA_DV
    widths = (gqk, gqk, gvw, gvw, FOX_HEADS * LANES, FOX_HEADS * LANES, FOX_HEADS * FOX_DH, LANES)
    dtypes = (BF16,) * 7 + (F32,)
    whole = lambda a: pl.BlockSpec(a.shape, lambda i: (0, 0))
    once = lambda a: pl.BlockSpec((None,) + a.shape[1:], lambda i: (0, 0, 0), pipeline_mode=pl.Buffered(1))

    def slab(w):
        rows = _cast_slab(w.shape[0], steps)
        last = w.shape[0] // rows - 1
        return pl.BlockSpec((rows, w.shape[1]), lambda i: (jnp.minimum(i, last), 0))

    outs = pl.pallas_call(
        functools.partial(_inproj_kernel, tiles_per_seq=s // tm, n_cast=len(later_weights)),
        grid=(steps,),
        in_specs=[pl.BlockSpec((tm, d), lambda i: (i, 0)), whole(g), once(w_in)]
                 + [whole(a) for a in (bf_pad, qg2, kg2)]
                 + [slab(w) for w in later_weights],
        out_specs=[pl.BlockSpec((tm, n), lambda i: (i, 0)) for n in widths]
                  + [pl.BlockSpec((None, LANES, tm), lambda i: (i, 0, 0))]
                  + [slab(w) for w in later_weights],
        out_shape=[jax.ShapeDtypeStruct((t, n), dt) for n, dt in zip(widths, dtypes)]
                  + [jax.ShapeDtypeStruct((steps, LANES, tm), F32)]
                  + [jax.ShapeDtypeStruct(w.shape, BF16) for w in later_weights],
        scratch_shapes=[pltpu.VMEM((1, LANES), F32),
                        pltpu.VMEM((tm, d), BF16),
                        pltpu.VMEM((d, gla_w), BF16),
                        pltpu.VMEM((d, fox_w), BF16),
                        pltpu.VMEM((d, LANES), BF16)],
        compiler_params=pltpu.CompilerParams(dimension_semantics=("arbitrary",), vmem_limit_bytes=VMEM_LIMIT),
        name="inproj",
    )(x2d, g, w_in, bf_pad, qg2, kg2, *later_weights)
    return outs[:len(widths) + 1], outs[len(widths) + 1:]


def _gla_stages(q_ref, k_ref, v_ref, og_ref, misc_ref, lrw_ref, lrb_ref, ong_ref, y_ref, st_scr):
    s_len = q_ref.shape[0]
    r = GLA_SUPER
    n_sb = s_len // r
    n_ch = r // GLA_CHUNK
    dkw = GLA_HEADS * GLA_DK
    row = lax.broadcasted_iota(jnp.int32, (r, r), 0)
    col = lax.broadcasted_iota(jnp.int32, (r, r), 1)
    causal_bd = ((row // GLA_CHUNK) == (col // GLA_CHUNK)) & (col <= row)
    tri_bd = causal_bd.astype(BF16)
    head_of_col = lax.broadcasted_iota(jnp.int32, (r, dkw), 1) // GLA_DK
    heads = range(GLA_HEADS)
    sb_rows = lambda n: slice(n * r, (n + 1) * r)
    ch_rows = lambda c: slice(c * GLA_CHUNK, (c + 1) * GLA_CHUNK)
    v_cols = lambda h: slice(h * GLA_DV, (h + 1) * GLA_DV)

    la_pieces = []
    for n in range(n_sb):
        z = _dot(misc_ref[sb_rows(n), :].astype(BF16), lrw_ref[...]) + lrb_ref[...]
        la_pieces.append(_split_bf16(_log_sigmoid(z) * GLA_INV_TAU, 2))
        yield

    qh, ke, kd, decay = [], [], [], []
    for n in range(n_sb):
        b = _dot(tri_bd, la_pieces[n][0]) + _dot(tri_bd, la_pieces[n][1])
        b_last = [b[(c + 1) * GLA_CHUNK - 1:(c + 1) * GLA_CHUNK, :] for c in range(n_ch)]
        b_tot = jnp.concatenate([jnp.broadcast_to(bl, (GLA_CHUNK, dkw)) for bl in b_last], axis=0)
        q = q_ref[sb_rows(n), :].astype(F32)
        k = k_ref[sb_rows(n), :].astype(F32)
        qe = (q * (GLA_DK ** -0.5)) * jnp.exp(b)
        qh.append([jnp.where(head_of_col == h, qe, 0.0).astype(BF16) for h in heads])
        ke.append((k * jnp.exp(-b)).astype(BF16))
        kd_n = k * jnp.exp(b_tot - b)
        kd.append([jnp.where(head_of_col == h, kd_n, 0.0).astype(BF16) for h in heads])
        decay.append([jnp.exp(bl) for bl in b_last])
        yield

    a = []
    for n in range(n_sb):
        a.append([jnp.where(causal_bd, _dot_nt(qh[n][h], ke[n]), 0.0).astype(BF16) for h in heads])
        yield

    o_intra = []
    for n in range(n_sb):
        o_intra.append([_dot(a[n][h], v_ref[sb_rows(n), v_cols(h)]) for h in heads])
        yield

    ds = []
    for n in range(n_sb):
        for c in range(n_ch):
            rows = slice(n * r + c * GLA_CHUNK, n * r + (c + 1) * GLA_CHUNK)
            v_stack = jnp.concatenate([v_ref[rows, v_cols(h)] for h in heads], axis=0)
            kd_stack = jnp.concatenate([kd[n][h][ch_rows(c)] for h in heads], axis=0)
            ds.append(_dot_tn(v_stack, kd_stack))
        yield

    st = st_scr[...]
    o_inter = []
    for n in range(n_sb):
        for c in range(n_ch):
            qe_stack = jnp.concatenate([qh[n][h][ch_rows(c)] for h in heads], axis=0)
            o_inter.append(_dot_nt(qe_stack, st.astype(BF16)))
            st = st * decay[n][c] + ds[n * n_ch + c]
        yield
    st_scr[...] = st

    for n in range(n_sb):
        og = og_ref[sb_rows(n), :].astype(F32)
        gate = og * jax.nn.sigmoid(og)
        for h in heads:
            inter = jnp.concatenate([o_inter[n * n_ch + c][ch_rows(h)] for c in range(n_ch)], axis=0)
            o = o_intra[n][h] + inter
            y_ref[sb_rows(n), v_cols(h)] = (_rms(o, ong_ref[...]) * gate[:, v_cols(h)]).astype(y_ref.dtype)
        yield


def _fox_stages(qa_ref, ka_ref, v_ref, y_ref, vt_scr, s_scr):
    s_len = v_ref.shape[0]
    t = FOX_TILE
    n_tiles = s_len // t
    krow = lax.broadcasted_iota(jnp.int32, (t, t), 0)
    qcol = lax.broadcasted_iota(jnp.int32, (t, t), 1)
    orow = lax.broadcasted_iota(jnp.int32, (LANES, t), 0)
    vt_scr[...] = v_ref[...].astype(F32).T.astype(BF16)

    units = [(i, e) for i in range(n_tiles) for e in range(2)]
    n_slots = s_scr.shape[0]

    def scores(n):
        i, e = units[n]
        hs = slice(e * LANES, (e + 1) * LANES)
        qa = qa_ref[i * t:(i + 1) * t, hs]
        mp = None
        for kb in range(i + 1):
            sc = _dot_nt(ka_ref[kb * t:(kb + 1) * t, hs], qa)
            if kb == i:
                sc = jnp.where(krow <= qcol, sc, -jnp.inf)
            s_scr[n % n_slots, kb] = sc
            bm = jnp.max(sc.reshape(t // 8, 8, t), axis=0)
            mp = bm if mp is None else jnp.maximum(mp, bm)
        return jnp.max(mp, axis=0, keepdims=True)

    def weighted_values(n, m):
        lp = jnp.zeros((8, t), F32)
        acc = jnp.zeros((LANES, t), F32)
        for kb in range(units[n][0] + 1):
            p = jnp.exp2(s_scr[n % n_slots, kb] - m)
            lp = lp + jnp.sum(p.reshape(t // 8, 8, t), axis=0)
            acc = acc + _dot(vt_scr[:, kb * t:(kb + 1) * t], p.astype(BF16))
        return acc / jnp.sum(lp, axis=0, keepdims=True)

    ahead = n_slots - 1
    col_max = {}
    for n in range(min(ahead, len(units))):
        col_max[n] = scores(n)
        yield
    o_even = None
    for n, (i, e) in enumerate(units):
        if n + ahead < len(units):
            col_max[n + ahead] = scores(n + ahead)
            yield
        o_t = weighted_values(n, col_max.pop(n))
        if e == 0:
            o_even = o_t
        else:
            o_pair = jnp.where(orow < FOX_DH, o_even, o_t)
            y_ref[i * t:(i + 1) * t, :] = o_pair.T.astype(y_ref.dtype)
        yield


def _fox_stages_bounded(qa_ref, ka_ref, v_ref, qbias_ref, y_ref, vt_scr, *, tile_rows):
    s_len = v_ref.shape[0]
    t = FOX_TILE
    n_tiles = s_len // t
    pair = pl.program_id(1)
    krow = lax.broadcasted_iota(jnp.int32, (t, t), 0)
    qcol = lax.broadcasted_iota(jnp.int32, (t, t), 1)
    orow = lax.broadcasted_iota(jnp.int32, (LANES, t), 0)
    vt_scr[...] = v_ref[...].astype(F32).T.astype(BF16)

    units = [(i, e) for i in range(n_tiles) for e in range(2)]
    blocks = [(n, kb) for n, (i, e) in enumerate(units) for kb in range(i + 1)]
    row_sum, acc = {}, {}
    o_even = [None]

    def values(n, kb, p):
        i, e = units[n]
        d = _dot(vt_scr[:, kb * t:(kb + 1) * t], p)
        acc[n] = d if n not in acc else acc[n] + d
        if kb == i:
            o_t = acc.pop(n) / jnp.sum(row_sum.pop(n), axis=0, keepdims=True)
            if e == 0:
                o_even[0] = o_t
            else:
                o_pair = jnp.where(orow < FOX_DH, o_even[0], o_t)
                y_ref[i * t:(i + 1) * t, :] = o_pair.T.astype(y_ref.dtype)

    pending = []
    for j, (n, kb) in enumerate(blocks):
        i, e = units[n]
        hs = slice(e * LANES, (e + 1) * LANES)
        q0 = i * t
        m = qbias_ref[q0 // tile_rows, pl.ds(2 * pair + e, 1), q0 % tile_rows:q0 % tile_rows + t]
        sc = _dot_nt(ka_ref[kb * t:(kb + 1) * t, hs], qa_ref[q0:q0 + t, hs])
        if kb == i:
            sc = jnp.where(krow <= qcol, sc, -jnp.inf)
        p = jnp.exp2(sc - m)
        ps = jnp.sum(p.reshape(t // 8, 8, t), axis=0)
        row_sum[n] = ps if n not in row_sum else row_sum[n] + ps
        pending.append((n, kb, p.astype(BF16)))
        if len(pending) > FOX_BLOCKS_AHEAD:
            values(*pending.pop(0))
        if j % 2 == 1:
            yield
    for item in pending:
        values(*item)
    yield


def _mixers_kernel(qa_ref, ka_ref, fv_ref, qbias_ref, gq_ref, gk_ref, gv_ref, gog_ref, misc_ref, lrw_ref, lrb_ref,
                   ong_ref, yf_ref, yg_ref, vt_scr, s_scr, st_scr, *, bounded_logits, tile_rows):
    @pl.when(pl.program_id(1) == 0)
    def _():
        st_scr[...] = jnp.zeros_like(st_scr)

    if bounded_logits:
        fox = _fox_stages_bounded(qa_ref, ka_ref, fv_ref, qbias_ref, yf_ref, vt_scr, tile_rows=tile_rows)
    else:
        fox = _fox_stages(qa_ref, ka_ref, fv_ref, yf_ref, vt_scr, s_scr)
    gla = _gla_stages(gq_ref, gk_ref, gv_ref, gog_ref, misc_ref, lrw_ref, lrb_ref, ong_ref, yg_ref, st_scr)
    rota = (fox, fox, fox, gla) if bounded_logits else (fox, fox, gla)
    finished = set()
    turn = 0
    while len(finished) < 2:
        gen = rota[turn % len(rota)]
        turn += 1
        if gen not in finished:
            try:
                next(gen)
            except StopIteration:
                finished.add(gen)


def _mixers(fqa, fka, fv, qbias, gq, gk, gv, gog, misc, lrw_pad, lrb, ong, b, s, bounded_logits):
    n_pairs = FOX_HEADS * FOX_DH // LANES
    t = FOX_TILE
    dkw, dvw = GLA_HEADS * GLA_DK, GLA_HEADS * GLA_DV
    sq = s // n_pairs
    assert s % t == 0 and sq % GLA_SUPER == 0, "sequence must split into FoX tiles and GLA super-blocks"
    tile_rows = qbias.shape[2]
    heads2 = pl.BlockSpec((s, 2 * LANES), lambda i, j: (i, j))
    pair = pl.BlockSpec((s, LANES), lambda i, j: (i, j))
    seq_bias = pl.BlockSpec((s // tile_rows, FOX_HEADS, tile_rows), lambda i, j: (i, 0, 0))
    quarter = lambda w: pl.BlockSpec((sq, w), lambda i, j: (i * n_pairs + j, 0))
    whole = lambda a: pl.BlockSpec(a.shape, lambda i, j: (0, 0))
    return pl.pallas_call(
        functools.partial(_mixers_kernel, bounded_logits=bounded_logits, tile_rows=tile_rows),
        grid=(b, n_pairs),
        in_specs=[heads2, heads2, pair, seq_bias,
                  quarter(dkw), quarter(dkw), quarter(dvw), quarter(dvw), quarter(LANES),
                  whole(lrw_pad), whole(lrb), whole(ong)],
        out_specs=[pair, quarter(dvw)],
        out_shape=[jax.ShapeDtypeStruct((b * s, FOX_HEADS * FOX_DH), BF16),
                   jax.ShapeDtypeStruct((b * s, dvw), BF16)],
        scratch_shapes=[
            pltpu.VMEM((LANES, s), BF16),
            pltpu.VMEM((FOX_SLOTS, s // t, t, t), F32),
            pltpu.VMEM((GLA_DV, dkw), F32),
        ],
        compiler_params=pltpu.CompilerParams(dimension_semantics=("arbitrary", "arbitrary"),
                                             vmem_limit_bytes=VMEM_LIMIT),
        name="mixers_bounded" if bounded_logits else "mixers",
    )(fqa, fka, fv, qbias, gq, gk, gv, gog, misc, lrw_pad, lrb, ong)


def _post_kernel(x_ref, yg_ref, yf_ref, p_ref, wo_ref, g2_ref, wup_ref, cw_ref, cb_ref, wdn_ref,
                 g3_ref, wpg_ref, bpg_ref, wpe_ref, gpe_ref, o_ref,
                 x1_scr, h2_scr, u_scr, act_scr, carry_scr, acc_scr):
    tm = x_ref.shape[0]
    d_ff = wdn_ref.shape[0]
    n_chunks = d_ff // FFN_CHUNK
    n_slabs = 2 * FFN_CHUNK // LANES
    half = n_slabs // 2
    dvw = yg_ref.shape[1]
    first_tile = pl.program_id(1) == 0

    @pl.when(first_tile)
    def _():
        carry_scr[...] = jnp.zeros_like(carry_scr)

    x1 = x_ref[...] + _dot(yg_ref[...], wo_ref[:dvw, :]) + _dot(yf_ref[...], wo_ref[dvw:, :])
    x1_scr[...] = x1
    h2_scr[...] = _rms(x1, g2_ref[...]).astype(BF16)
    acc_scr[...] = jnp.zeros_like(acc_scr)

    o_ref[...] = _rms(_dot(p_ref[...].astype(BF16), wpe_ref[...]), gpe_ref[...])

    for j in range(n_chunks + DOWN_LAG):
        if j < n_chunks:
            ub = u_scr.at[j % 2]
            col0 = [(c // half) * d_ff + j * FFN_CHUNK + (c % half) * LANES for c in range(n_slabs)]
            h2 = h2_scr[...]
            for part in range(2):
                u = _dot(h2, wup_ref[:, col0[part * half]:col0[part * half] + FFN_CHUNK])
                for c in range(part * half, (part + 1) * half):
                    ub[c, :CARRY_ROWS, :] = carry_scr[j, c]
                    ub[c, CARRY_ROWS:, :] = u[:, (c % half) * LANES:(c % half + 1) * LANES]
                    carry_scr[j, c] = ub[c, tm:, :]
        if j >= DOWN_LAG:
            jd = j - DOWN_LAG
            acc_scr[...] += _dot(act_scr[jd % (DOWN_LAG + 1)], wdn_ref[jd * FFN_CHUNK:(jd + 1) * FFN_CHUNK, :])
        if j < n_chunks:
            def conv(c):
                cs = slice(col0[c], col0[c] + LANES)
                return (cb_ref[:, cs]
                        + ub[c, CARRY_ROWS - 2:CARRY_ROWS - 2 + tm, :] * cw_ref[0:1, cs]
                        + ub[c, CARRY_ROWS - 1:CARRY_ROWS - 1 + tm, :] * cw_ref[1:2, cs]
                        + ub[c, CARRY_ROWS:, :] * cw_ref[2:3, cs])
            for c in range(half):
                gate = conv(c)
                act = gate * jax.nn.sigmoid(gate) * conv(c + half)
                act_scr[j % (DOWN_LAG + 1), :, c * LANES:(c + 1) * LANES] = act.astype(BF16)

    x2 = x1_scr[...] + acc_scr[...]
    gate = jax.nn.sigmoid(_dot(_rms(x2, g3_ref[...]).astype(BF16), wpg_ref[...]) + bpg_ref[...])
    o_ref[...] = x2 + gate * o_ref[...]


def _post(x2d, yg, yf, p2d, wo, g2, wup, cw, cb, wdn, g3, wpg, bpg, wpe, gpe, b, s, tm):
    d = x2d.shape[1]
    nt = s // tm
    tile = lambda w: pl.BlockSpec((tm, w), lambda i, j: (i * nt + j, 0))
    resident = lambda a: pl.BlockSpec(a.shape, lambda i, j: (0,) * a.ndim, pipeline_mode=pl.Buffered(1))
    n_chunks = wdn.shape[0] // FFN_CHUNK
    n_slabs = 2 * FFN_CHUNK // LANES
    return pl.pallas_call(
        _post_kernel,
        grid=(b, nt),
        in_specs=[tile(d), tile(yg.shape[1]), tile(yf.shape[1]), tile(p2d.shape[1])]
                 + [resident(a) for a in (wo, g2, wup, cw, cb, wdn, g3, wpg, bpg, wpe, gpe)],
        out_specs=tile(d),
        out_shape=jax.ShapeDtypeStruct(x2d.shape, F32),
        scratch_shapes=[
            pltpu.VMEM((tm, d), F32),
            pltpu.VMEM((tm, d), BF16),
            pltpu.VMEM((2, n_slabs, CARRY_ROWS + tm, LANES), F32),
            pltpu.VMEM((DOWN_LAG + 1, tm, FFN_CHUNK), BF16),
            pltpu.VMEM((n_chunks, n_slabs, CARRY_ROWS, LANES), F32),
            pltpu.VMEM((tm, d), F32),
        ],
        compiler_params=pltpu.CompilerParams(dimension_semantics=("arbitrary", "arbitrary"),
                                             vmem_limit_bytes=VMEM_LIMIT),
        name="post",
    )(x2d, yg, yf, p2d, wo, g2, wup, cw, cb, wdn, g3, wpg, bpg, wpe, gpe)


def _layer(x2d, p2d, b, s, norm1_g, w_in, lr_w, lr_b, onorm_g, b_f, qn_g, kn_g, w_o, norm2_g, w_up,
           conv_w, conv_b, w_down, norm3_g, w_pe, pe_norm_g, w_pg, b_pg):
    gqk = GLA_HEADS * GLA_DK
    row = lambda a: a.reshape(1, -1).astype(F32)
    bf_pad = jnp.zeros((1, LANES), F32).at[0, MISC_F0:MISC_F0 + FOX_HEADS].set(b_f)
    qg2 = row(jnp.tile(qn_g, 2)) * (FOX_DH ** -0.5 * LOG2E)
    (gq, gk, gv, gog, fqa, fka, fv, misc, qbias), (wo_b, wup_b, wdn_b, wpg_b, wpe_b) = _inproj(
        x2d, row(norm1_g), jnp.swapaxes(w_in, 1, 2).astype(F32), bf_pad, qg2, row(jnp.tile(kn_g, 2)),
        (w_o, w_up, w_down, w_pg, w_pe), s, tm=512)

    lrw_pad = jnp.zeros((LANES, gqk), F32).at[MISC_LR0:MISC_LR0 + GLA_LOWRANK].set(lr_w).astype(BF16)
    logit_bound = FOX_DH ** 0.5 * LOG2E * jnp.max(jnp.abs(qn_g)) * jnp.max(jnp.abs(kn_g))
    mix = lambda bounded: functools.partial(_mixers, b=b, s=s, bounded_logits=bounded)
    y_fox, y_gla = lax.cond(logit_bound <= FOX_BOUNDED_MAX_LOGIT, mix(True), mix(False),
                            fqa, fka, fv, qbias, gq, gk, gv, gog, misc, lrw_pad, row(lr_b), row(onorm_g))

    return _post(
        x2d, y_gla, y_fox, p2d, wo_b, row(norm2_g), wup_b, conv_w.astype(F32), row(conv_b), wdn_b,
        row(norm3_g), wpg_b, row(b_pg), wpe_b, row(pe_norm_g), b, s, tm=256)


def kernel(x, p, norm1_g, w_in, gla_lr_w, gla_lr_b, gla_onorm_g, fox_b_f, fox_qnorm_g, fox_knorm_g, w_o, norm2_g, w_up, conv_w, conv_b, w_down, norm3_g, w_pe, pe_norm_g, w_pg, b_pg):
    b, s, d = x.shape
    x2d = x.reshape(b * s, d)
    for i in range(p.shape[0]):
        x2d = _layer(x2d, p[i].reshape(b * s, -1), b, s, norm1_g[i], w_in[i:i + 1], gla_lr_w[i], gla_lr_b[i],
                     gla_onorm_g[i], fox_b_f[i], fox_qnorm_g[i], fox_knorm_g[i], w_o[i], norm2_g[i], w_up[i],
                     conv_w[i], conv_b[i], w_down[i], norm3_g[i], w_pe[i], pe_norm_g[i], w_pg[i], b_pg[i])
    return x2d.reshape(b, s, d)
```

```python
import functools

import jax
import jax.numpy as jnp
from jax import lax
from jax.experimental import pallas as pl
from jax.experimental.pallas import tpu as pltpu

F32 = jnp.float32
BF16 = jnp.bfloat16

EPS = 1e-6
LOG2E = 1.4426950408889634
LANES = 128
BF16_SUBLANES = 16
GLA_HEADS, GLA_DK, GLA_DV = 4, 64, 128
GLA_LOWRANK = 16
GLA_INV_TAU = 1.0 / 16.0
GLA_CHUNK = 64
GLA_SUPER = 256
FOX_HEADS, FOX_DH = 8, 64
FOX_TILE = 256
FOX_SLOTS = 3
FOX_BLOCKS_AHEAD = 4
FOX_BOUNDED_MAX_LOGIT = 60.0
FFN_CHUNK = 256
CARRY_ROWS = 8
DOWN_LAG = 2
MISC_F0 = 0
MISC_LR0 = 8
VMEM_LIMIT = 56 * 1024 * 1024


def _dot(a, b):
    return jnp.dot(a, b, preferred_element_type=F32)


def _dot_nt(a, b):
    return lax.dot_general(a, b, (((1,), (1,)), ((), ())), preferred_element_type=F32)


def _dot_tn(a, b):
    return lax.dot_general(a, b, (((0,), (0,)), ((), ())), preferred_element_type=F32)


def _log_sigmoid(z):
    return jnp.minimum(z, 0.0) - jnp.log(1.0 + jnp.exp(-jnp.abs(z)))


def _split_bf16(x, pieces):
    out = []
    for _ in range(pieces):
        p = x.astype(BF16)
        out.append(p)
        x = x - p.astype(F32)
    return out


def _rms(x, gain):
    ms = jnp.mean(x * x, axis=-1, keepdims=True)
    return x * lax.rsqrt(ms + EPS) * gain


def _inproj_kernel(x_ref, g_ref, win_ref, bf_ref, qg_ref, kg_ref, *rest, tiles_per_seq, n_cast):
    cast_in, rest = rest[:n_cast], rest[n_cast:]
    gq_ref, gk_ref, gv_ref, gog_ref, fqa_ref, fka_ref, fv_ref, misc_ref, qbias_ref = rest[:9]
    cast_out = rest[9:9 + n_cast]
    carry_scr, h_scr, wg_ref, wf_ref, wm_ref = rest[9 + n_cast:]

    @pl.when(pl.program_id(0) == 0)
    def _():
        gla_w = wg_ref.shape[1]
        fox_w = wf_ref.shape[1]
        lr0 = gla_w
        fox0 = lr0 + GLA_LOWRANK
        f0 = fox0 + fox_w
        wg_ref[...] = win_ref[:gla_w, :].T.astype(BF16)
        wf_ref[...] = win_ref[fox0:f0, :].T.astype(BF16)
        pad = jnp.zeros((LANES - MISC_LR0 - GLA_LOWRANK, win_ref.shape[1]), F32)
        misc_t = jnp.concatenate([win_ref[f0:f0 + FOX_HEADS, :], win_ref[lr0:fox0, :], pad], axis=0)
        wm_ref[...] = misc_t.T.astype(BF16)

    _inproj_body(x_ref, g_ref, wg_ref, wf_ref, wm_ref, bf_ref, qg_ref, kg_ref, gq_ref, gk_ref, gv_ref, gog_ref,
                 fqa_ref, fka_ref, fv_ref, misc_ref, qbias_ref, carry_scr, h_scr, tiles_per_seq)
    for w_ref, o_ref in zip(cast_in, cast_out):
        o_ref[...] = w_ref[...].astype(BF16)


def _inproj_body(x_ref, g_ref, wg_ref, wf_ref, wm_ref, bf_ref, qg_ref, kg_ref, gq_ref, gk_ref, gv_ref, gog_ref,
                 fqa_ref, fka_ref, fv_ref, misc_ref, qbias_ref, carry_scr, h_scr, tiles_per_seq):
    tm = x_ref.shape[0]
    fw = FOX_HEADS * FOX_DH

    @pl.when(pl.program_id(0) % tiles_per_seq == 0)
    def _():
        carry_scr[...] = jnp.zeros_like(carry_scr)

    h_scr[...] = _rms(x_ref[...], g_ref[...]).astype(BF16)
    h = h_scr[...]
    misc = _dot(h, wm_ref[...])
    misc_ref[...] = misc
    q_all = _dot(h, wf_ref[:, :fw])
    k_all = _dot(h, wf_ref[:, fw:2 * fw])

    blk = FOX_TILE
    trow = lax.broadcasted_iota(jnp.int32, (blk, blk), 0)
    tcol = lax.broadcasted_iota(jnp.int32, (blk, blk), 1)
    tri = (tcol <= trow).astype(BF16)
    n_blk = tm // blk
    logf = jnp.concatenate([_log_sigmoid(misc[r0:r0 + blk] + bf_ref[...]) for r0 in range(0, tm, blk)], axis=1)
    within = sum(_dot(tri, piece) for piece in _split_bf16(logf, 3))
    carry = carry_scr[...]
    c_blocks = []
    for i in range(n_blk):
        cb = within[:, i * LANES:(i + 1) * LANES] + carry
        c_blocks.append(cb)
        carry = cb[blk - 1:blk]
    carry_scr[...] = carry
    c_all = jnp.concatenate(c_blocks, axis=0)
    qbias_ref[...] = (c_all * (-LOG2E)).T

    lane = lax.broadcasted_iota(jnp.int32, (tm, LANES), 1)
    lower = lane < FOX_DH

    def pair_norm(x, g):
        sq = x * x
        ms_lo = jnp.sum(jnp.where(lower, sq, 0.0), axis=-1, keepdims=True)
        ms_hi = jnp.sum(jnp.where(lower, 0.0, sq), axis=-1, keepdims=True)
        ms = jnp.where(lower, ms_lo, ms_hi) * (1.0 / FOX_DH)
        return x * lax.rsqrt(ms + EPS) * g

    fv_ref[...] = _dot(h, wf_ref[:, 2 * fw:]).astype(fv_ref.dtype)
    off = 0
    for ref in (gq_ref, gk_ref, gv_ref, gog_ref):
        n = ref.shape[-1]
        ref[...] = _dot(h, wg_ref[:, off:off + n]).astype(ref.dtype)
        off += n

    for pair in range(fw // LANES):
        ps = slice(pair * LANES, (pair + 1) * LANES)
        qn = pair_norm(q_all[:, ps], qg_ref[...])
        kn = pair_norm(k_all[:, ps], kg_ref[...])
        for e in range(2):
            head = 2 * pair + e
            in_head = (lane >= FOX_DH) if e else lower
            piece0 = 0 if e else FOX_DH
            c_col = jnp.sum(jnp.where(lane == MISC_F0 + head, c_all, 0.0), axis=-1, keepdims=True)
            is_piece = (lane >= piece0) & (lane < piece0 + 3)
            hs = slice(head * LANES, (head + 1) * LANES)
            fqa_ref[:, hs] = jnp.where(in_head, qn, jnp.where(is_piece, 1.0, 0.0)).astype(BF16)
            ka = jnp.where(in_head, kn, 0.0).astype(BF16)
            for j, p in enumerate(_split_bf16(c_col * (-LOG2E), 3)):
                ka = jnp.where(lane == piece0 + j, p, ka)
            fka_ref[:, hs] = ka


def _cast_slab(rows, steps):
    for n_slabs in range(min(steps, rows // BF16_SUBLANES), 0, -1):
        if rows % n_slabs == 0 and (rows // n_slabs) % BF16_SUBLANES == 0:
            return rows // n_slabs
    raise ValueError(f"no bf16-tile aligned slab for {rows} rows")


def _inproj(x2d, g, w_in, bf_pad, qg2, kg2, later_weights, s, tm):
    t, d = x2d.shape
    steps = t // tm
    gla_w = 2 * GLA_HEADS * (GLA_DK + GLA_DV)
    fox_w = 3 * FOX_HEADS * FOX_DH
    assert w_in.shape == (1, gla_w + GLA_LOWRANK + fox_w + FOX_HEADS, d)
    assert MISC_F0 == 0 and MISC_LR0 == FOX_HEADS
    gqk, gvw = GLA_HEADS * GLA_DK, GLA_HEADS * GLA_DV
    widths = (gqk, gqk, gvw, gvw, FOX_HEADS * LANES, FOX_HEADS * LANES, FOX_HEADS * FOX_DH, LANES)
    dtypes = (BF16,) * 7 + (F32,)
    whole = lambda a: pl.BlockSpec(a.shape, lambda i: (0, 0))
    once = lambda a: pl.BlockSpec((None,) + a.shape[1:], lambda i: (0, 0, 0), pipeline_mode=pl.Buffered(1))

    def slab(w):
        rows = _cast_slab(w.shape[0], steps)
        last = w.shape[0] // rows - 1
        return pl.BlockSpec((rows, w.shape[1]), lambda i: (jnp.minimum(i, last), 0))

    outs = pl.pallas_call(
        functools.partial(_inproj_kernel, tiles_per_seq=s // tm, n_cast=len(later_weights)),
        grid=(steps,),
        in_specs=[pl.BlockSpec((tm, d), lambda i: (i, 0)), whole(g), once(w_in)]
                 + [whole(a) for a in (bf_pad, qg2, kg2)]
                 + [slab(w) for w in later_weights],
        out_specs=[pl.BlockSpec((tm, n), lambda i: (i, 0)) for n in widths]
                  + [pl.BlockSpec((None, LANES, tm), lambda i: (i, 0, 0))]
                  + [slab(w) for w in later_weights],
        out_shape=[jax.ShapeDtypeStruct((t, n), dt) for n, dt in zip(widths, dtypes)]
                  + [jax.ShapeDtypeStruct((steps, LANES, tm), F32)]
                  + [jax.ShapeDtypeStruct(w.shape, BF16) for w in later_weights],
        scratch_shapes=[pltpu.VMEM((1, LANES), F32),
                        pltpu.VMEM((tm, d), BF16),
                        pltpu.VMEM((d, gla_w), BF16),
                        pltpu.VMEM((d, fox_w), BF16),
                        pltpu.VMEM((d, LANES), BF16)],
        compiler_params=pltpu.CompilerParams(dimension_semantics=("arbitrary",), vmem_limit_bytes=VMEM_LIMIT),
        name="inproj",
    )(x2d, g, w_in, bf_pad, qg2, kg2, *later_weights)
    return outs[:len(widths) + 1], outs[len(widths) + 1:]


def _gla_stages(q_ref, k_ref, v_ref, og_ref, misc_ref, lrw_ref, lrb_ref, ong_ref, y_ref, st_scr):
    s_len = q_ref.shape[0]
    r = GLA_SUPER
    n_sb = s_len // r
    n_ch = r // GLA_CHUNK
    dkw = GLA_HEADS * GLA_DK
    row = lax.broadcasted_iota(jnp.int32, (r, r), 0)
    col = lax.broadcasted_iota(jnp.int32, (r, r), 1)
    causal_bd = ((row // GLA_CHUNK) == (col // GLA_CHUNK)) & (col <= row)
    tri_bd = causal_bd.astype(BF16)
    head_of_col = lax.broadcasted_iota(jnp.int32, (r, dkw), 1) // GLA_DK
    heads = range(GLA_HEADS)
    sb_rows = lambda n: slice(n * r, (n + 1) * r)
    ch_rows = lambda c: slice(c * GLA_CHUNK, (c + 1) * GLA_CHUNK)
    v_cols = lambda h: slice(h * GLA_DV, (h + 1) * GLA_DV)

    la_pieces = []
    for n in range(n_sb):
        z = _dot(misc_ref[sb_rows(n), :].astype(BF16), lrw_ref[...]) + lrb_ref[...]
        la_pieces.append(_split_bf16(_log_sigmoid(z) * GLA_INV_TAU, 2))
        yield

    qh, ke, kd, decay = [], [], [], []
    for n in range(n_sb):
        b = _dot(tri_bd, la_pieces[n][0]) + _dot(tri_bd, la_pieces[n][1])
        b_last = [b[(c + 1) * GLA_CHUNK - 1:(c + 1) * GLA_CHUNK, :] for c in range(n_ch)]
        b_tot = jnp.concatenate([jnp.broadcast_to(bl, (GLA_CHUNK, dkw)) for bl in b_last], axis=0)
        q = q_ref[sb_rows(n), :].astype(F32)
        k = k_ref[sb_rows(n), :].astype(F32)
        qe = (q * (GLA_DK ** -0.5)) * jnp.exp(b)
        qh.append([jnp.where(head_of_col == h, qe, 0.0).astype(BF16) for h in heads])
        ke.append((k * jnp.exp(-b)).astype(BF16))
        kd_n = k * jnp.exp(b_tot - b)
        kd.append([jnp.where(head_of_col == h, kd_n, 0.0).astype(BF16) for h in heads])
        decay.append([jnp.exp(bl) for bl in b_last])
        yield

    a = []
    for n in range(n_sb):
        a.append([jnp.where(causal_bd, _dot_nt(qh[n][h], ke[n]), 0.0).astype(BF16) for h in heads])
        yield

    o_intra = []
    for n in range(n_sb):
        o_intra.append([_dot(a[n][h], v_ref[sb_rows(n), v_cols(h)]) for h in heads])
        yield

    ds = []
    for n in range(n_sb):
        for c in range(n_ch):
            rows = slice(n * r + c * GLA_CHUNK, n * r + (c + 1) * GLA_CHUNK)
            v_stack = jnp.concatenate([v_ref[rows, v_cols(h)] for h in heads], axis=0)
            kd_stack = jnp.concatenate([kd[n][h][ch_rows(c)] for h in heads], axis=0)
            ds.append(_dot_tn(v_stack, kd_stack))
        yield

    st = st_scr[...]
    o_inter = []
    for n in range(n_sb):
        for c in range(n_ch):
            qe_stack = jnp.concatenate([qh[n][h][ch_rows(c)] for h in heads], axis=0)
            o_inter.append(_dot_nt(qe_stack, st.astype(BF16)))
            st = st * decay[n][c] + ds[n * n_ch + c]
        yield
    st_scr[...] = st

    for n in range(n_sb):
        og = og_ref[sb_rows(n), :].astype(F32)
        gate = og * jax.nn.sigmoid(og)
        for h in heads:
            inter = jnp.concatenate([o_inter[n * n_ch + c][ch_rows(h)] for c in range(n_ch)], axis=0)
            o = o_intra[n][h] + inter
            y_ref[sb_rows(n), v_cols(h)] = (_rms(o, ong_ref[...]) * gate[:, v_cols(h)]).astype(y_ref.dtype)
        yield


def _fox_stages(qa_ref, ka_ref, v_ref, y_ref, vt_scr, s_scr):
    s_len = v_ref.shape[0]
    t = FOX_TILE
    n_tiles = s_len // t
    krow = lax.broadcasted_iota(jnp.int32, (t, t), 0)
    qcol = lax.broadcasted_iota(jnp.int32, (t, t), 1)
    orow = lax.broadcasted_iota(jnp.int32, (LANES, t), 0)
    vt_scr[...] = v_ref[...].astype(F32).T.astype(BF16)

    units = [(i, e) for i in range(n_tiles) for e in range(2)]
    n_slots = s_scr.shape[0]

    def scores(n):
        i, e = units[n]
        hs = slice(e * LANES, (e + 1) * LANES)
        qa = qa_ref[i * t:(i + 1) * t, hs]
        mp = None
        for kb in range(i + 1):
            sc = _dot_nt(ka_ref[kb * t:(kb + 1) * t, hs], qa)
            if kb == i:
                sc = jnp.where(krow <= qcol, sc, -jnp.inf)
            s_scr[n % n_slots, kb] = sc
            bm = jnp.max(sc.reshape(t // 8, 8, t), axis=0)
            mp = bm if mp is None else jnp.maximum(mp, bm)
        return jnp.max(mp, axis=0, keepdims=True)

    def weighted_values(n, m):
        lp = jnp.zeros((8, t), F32)
        acc = jnp.zeros((LANES, t), F32)
        for kb in range(units[n][0] + 1):
            p = jnp.exp2(s_scr[n % n_slots, kb] - m)
            lp = lp + jnp.sum(p.reshape(t // 8, 8, t), axis=0)
            acc = acc + _dot(vt_scr[:, kb * t:(kb + 1) * t], p.astype(BF16))
        return acc / jnp.sum(lp, axis=0, keepdims=True)

    ahead = n_slots - 1
    col_max = {}
    for n in range(min(ahead, len(units))):
        col_max[n] = scores(n)
        yield
    o_even = None
    for n, (i, e) in enumerate(units):
        if n + ahead < len(units):
            col_max[n + ahead] = scores(n + ahead)
            yield
        o_t = weighted_values(n, col_max.pop(n))
        if e == 0:
            o_even = o_t
        else:
            o_pair = jnp.where(orow < FOX_DH, o_even, o_t)
            y_ref[i * t:(i + 1) * t, :] = o_pair.T.astype(y_ref.dtype)
        yield


def _fox_stages_bounded(qa_ref, ka_ref, v_ref, qbias_ref, y_ref, vt_scr, *, tile_rows):
    s_len = v_ref.shape[0]
    t = FOX_TILE
    n_tiles = s_len // t
    pair = pl.program_id(1)
    krow = lax.broadcasted_iota(jnp.int32, (t, t), 0)
    qcol = lax.broadcasted_iota(jnp.int32, (t, t), 1)
    orow = lax.broadcasted_iota(jnp.int32, (LANES, t), 0)
    vt_scr[...] = v_ref[...].astype(F32).T.astype(BF16)

    units = [(i, e) for i in range(n_tiles) for e in range(2)]
    blocks = [(n, kb) for n, (i, e) in enumerate(units) for kb in range(i + 1)]
    row_sum, acc = {}, {}
    o_even = [None]

    def values(n, kb, p):
        i, e = units[n]
        d = _dot(vt_scr[:, kb * t:(kb + 1) * t], p)
        acc[n] = d if n not in acc else acc[n] + d
        if kb == i:
            o_t = acc.pop(n) / jnp.sum(row_sum.pop(n), axis=0, keepdims=True)
            if e == 0:
                o_even[0] = o_t
            else:
                o_pair = jnp.where(orow < FOX_DH, o_even[0], o_t)
                y_ref[i * t:(i + 1) * t, :] = o_pair.T.astype(y_ref.dtype)

    pending = []
    for j, (n, kb) in enumerate(blocks):
        i, e = units[n]
        hs = slice(e * LANES, (e + 1) * LANES)
        q0 = i * t
        m = qbias_ref[q0 // tile_rows, pl.ds(2 * pair + e, 1), q0 % tile_rows:q0 % tile_rows + t]
        sc = _dot_nt(ka_ref[kb * t:(kb + 1) * t, hs], qa_ref[q0:q0 + t, hs])
        if kb == i:
            sc = jnp.where(krow <= qcol, sc, -jnp.inf)
        p = jnp.exp2(sc - m)
        ps = jnp.sum(p.reshape(t // 8, 8, t), axis=0)
        row_sum[n] = ps if n not in row_sum else row_sum[n] + ps
        pending.append((n, kb, p.astype(BF16)))
        if len(pending) > FOX_BLOCKS_AHEAD:
            values(*pending.pop(0))
        if j % 2 == 1:
            yield
    for item in pending:
        values(*item)
    yield


def _mixers_kernel(qa_ref, ka_ref, fv_ref, qbias_ref, gq_ref, gk_ref, gv_ref, gog_ref, misc_ref, lrw_ref, lrb_ref,
                   ong_ref, yf_ref, yg_ref, vt_scr, s_scr, st_scr, *, bounded_logits, tile_rows):
    @pl.when(pl.program_id(1) == 0)
    def _():
        st_scr[...] = jnp.zeros_like(st_scr)

    if bounded_logits:
        fox = _fox_stages_bounded(qa_ref, ka_ref, fv_ref, qbias_ref, yf_ref, vt_scr, tile_rows=tile_rows)
    else:
        fox = _fox_stages(qa_ref, ka_ref, fv_ref, yf_ref, vt_scr, s_scr)
    gla = _gla_stages(gq_ref, gk_ref, gv_ref, gog_ref, misc_ref, lrw_ref, lrb_ref, ong_ref, yg_ref, st_scr)
    rota = (fox, fox, fox, gla) if bounded_logits else (fox, fox, gla)
    finished = set()
    turn = 0
    while len(finished) < 2:
        gen = rota[turn % len(rota)]
        turn += 1
        if gen not in finished:
            try:
                next(gen)
            except StopIteration:
                finished.add(gen)


def _mixers(fqa, fka, fv, qbias, gq, gk, gv, gog, misc, lrw_pad, lrb, ong, b, s, bounded_logits):
    n_pairs = FOX_HEADS * FOX_DH // LANES
    t = FOX_TILE
    dkw, dvw = GLA_HEADS * GLA_DK, GLA_HEADS * GLA_DV
    sq = s // n_pairs
    assert s % t == 0 and sq % GLA_SUPER == 0, "sequence must split into FoX tiles and GLA super-blocks"
    tile_rows = qbias.shape[2]
    heads2 = pl.BlockSpec((s, 2 * LANES), lambda i, j: (i, j))
    pair = pl.BlockSpec((s, LANES), lambda i, j: (i, j))
    seq_bias = pl.BlockSpec((s // tile_rows, FOX_HEADS, tile_rows), lambda i, j: (i, 0, 0))
    quarter = lambda w: pl.BlockSpec((sq, w), lambda i, j: (i * n_pairs + j, 0))
    whole = lambda a: pl.BlockSpec(a.shape, lambda i, j: (0, 0))
    return pl.pallas_call(
        functools.partial(_mixers_kernel, bounded_logits=bounded_logits, tile_rows=tile_rows),
        grid=(b, n_pairs),
        in_specs=[heads2, heads2, pair, seq_bias,
                  quarter(dkw), quarter(dkw), quarter(dvw), quarter(dvw), quarter(LANES),
                  whole(lrw_pad), whole(lrb), whole(ong)],
        out_specs=[pair, quarter(dvw)],
        out_shape=[jax.ShapeDtypeStruct((b * s, FOX_HEADS * FOX_DH), BF16),
                   jax.ShapeDtypeStruct((b * s, dvw), BF16)],
        scratch_shapes=[
            pltpu.VMEM((LANES, s), BF16),
            pltpu.VMEM((FOX_SLOTS, s // t, t, t), F32),
            pltpu.VMEM((GLA_DV, dkw), F32),
        ],
        compiler_params=pltpu.CompilerParams(dimension_semantics=("arbitrary", "arbitrary"),
                                             vmem_limit_bytes=VMEM_LIMIT),
        name="mixers_bounded" if bounded_logits else "mixers",
    )(fqa, fka, fv, qbias, gq, gk, gv, gog, misc, lrw_pad, lrb, ong)


def _post_kernel(x_ref, yg_ref, yf_ref, p_ref, wo_ref, g2_ref, wup_ref, cw_ref, cb_ref, wdn_ref,
                 g3_ref, wpg_ref, bpg_ref, wpe_ref, gpe_ref, o_ref,
                 x1_scr, h2_scr, u_scr, act_scr, carry_scr, acc_scr):
    tm = x_ref.shape[0]
    d_ff = wdn_ref.shape[0]
    n_chunks = d_ff // FFN_CHUNK
    n_slabs = 2 * FFN_CHUNK // LANES
    half = n_slabs // 2
    dvw = yg_ref.shape[1]
    first_tile = pl.program_id(1) == 0

    @pl.when(first_tile)
    def _():
        carry_scr[...] = jnp.zeros_like(carry_scr)

    x1 = x_ref[...] + _dot(yg_ref[...], wo_ref[:dvw, :]) + _dot(yf_ref[...], wo_ref[dvw:, :])
    x1_scr[...] = x1
    h2_scr[...] = _rms(x1, g2_ref[...]).astype(BF16)
    acc_scr[...] = jnp.zeros_like(acc_scr)

    o_ref[...] = _rms(_dot(p_ref[...].astype(BF16), wpe_ref[...]), gpe_ref[...])

    for j in range(n_chunks + DOWN_LAG):
        if j < n_chunks:
            ub = u_scr.at[j % 2]
            col0 = [(c // half) * d_ff + j * FFN_CHUNK + (c % half) * LANES for c in range(n_slabs)]
            h2 = h2_scr[...]
            for part in range(2):
                u = _dot(h2, wup_ref[:, col0[part * half]:col0[part * half] + FFN_CHUNK])
                for c in range(part * half, (part + 1) * half):
                    ub[c, :CARRY_ROWS, :] = carry_scr[j, c]
                    ub[c, CARRY_ROWS:, :] = u[:, (c % half) * LANES:(c % half + 1) * LANES]
                    carry_scr[j, c] = ub[c, tm:, :]
        if j >= DOWN_LAG:
            jd = j - DOWN_LAG
            acc_scr[...] += _dot(act_scr[jd % (DOWN_LAG + 1)], wdn_ref[jd * FFN_CHUNK:(jd + 1) * FFN_CHUNK, :])
        if j < n_chunks:
            def conv(c):
                cs = slice(col0[c], col0[c] + LANES)
                return (cb_ref[:, cs]
                        + ub[c, CARRY_ROWS - 2:CARRY_ROWS - 2 + tm, :] * cw_ref[0:1, cs]
                        + ub[c, CARRY_ROWS - 1:CARRY_ROWS - 1 + tm, :] * cw_ref[1:2, cs]
                        + ub[c, CARRY_ROWS:, :] * cw_ref[2:3, cs])
            for c in range(half):
                gate = conv(c)
                act = gate * jax.nn.sigmoid(gate) * conv(c + half)
                act_scr[j % (DOWN_LAG + 1), :, c * LANES:(c + 1) * LANES] = act.astype(BF16)

    x2 = x1_scr[...] + acc_scr[...]
    z = _dot(_rms(x2, g3_ref[...]).astype(BF16), wpg_ref[...]) + bpg_ref[...]
    gate = 0.5 * jnp.tanh(0.5 * z) + 0.5
    o_ref[...] = x2 + gate * o_ref[...]


def _post(x2d, yg, yf, p2d, wo, g2, wup, cw, cb, wdn, g3, wpg, bpg, wpe, gpe, b, s, tm):
    d = x2d.shape[1]
    nt = s // tm
    tile = lambda w: pl.BlockSpec((tm, w), lambda i, j: (i * nt + j, 0))
    resident = lambda a: pl.BlockSpec(a.shape, lambda i, j: (0,) * a.ndim, pipeline_mode=pl.Buffered(1))
    n_chunks = wdn.shape[0] // FFN_CHUNK
    n_slabs = 2 * FFN_CHUNK // LANES
    return pl.pallas_call(
        _post_kernel,
        grid=(b, nt),
        in_specs=[tile(d), tile(yg.shape[1]), tile(yf.shape[1]), tile(p2d.shape[1])]
                 + [resident(a) for a in (wo, g2, wup, cw, cb, wdn, g3, wpg, bpg, wpe, gpe)],
        out_specs=tile(d),
        out_shape=jax.ShapeDtypeStruct(x2d.shape, F32),
        scratch_shapes=[
            pltpu.VMEM((tm, d), F32),
            pltpu.VMEM((tm, d), BF16),
            pltpu.VMEM((2, n_slabs, CARRY_ROWS + tm, LANES), F32),
            pltpu.VMEM((DOWN_LAG + 1, tm, FFN_CHUNK), BF16),
            pltpu.VMEM((n_chunks, n_slabs, CARRY_ROWS, LANES), F32),
            pltpu.VMEM((tm, d), F32),
        ],
        compiler_params=pltpu.CompilerParams(dimension_semantics=("arbitrary", "arbitrary"),
                                             vmem_limit_bytes=VMEM_LIMIT),
        name="post",
    )(x2d, yg, yf, p2d, wo, g2, wup, cw, cb, wdn, g3, wpg, bpg, wpe, gpe)


def _layer(x2d, p2d, b, s, norm1_g, w_in, lr_w, lr_b, onorm_g, b_f, qn_g, kn_g, w_o, norm2_g, w_up,
           conv_w, conv_b, w_down, norm3_g, w_pe, pe_norm_g, w_pg, b_pg):
    gqk = GLA_HEADS * GLA_DK
    row = lambda a: a.reshape(1, -1).astype(F32)
    bf_pad = jnp.zeros((1, LANES), F32).at[0, MISC_F0:MISC_F0 + FOX_HEADS].set(b_f)
    qg2 = row(jnp.tile(qn_g, 2)) * (FOX_DH ** -0.5 * LOG2E)
    (gq, gk, gv, gog, fqa, fka, fv, misc, qbias), (wo_b, wup_b, wdn_b, wpg_b, wpe_b) = _inproj(
        x2d, row(norm1_g), jnp.swapaxes(w_in, 1, 2).astype(F32), bf_pad, qg2, row(jnp.tile(kn_g, 2)),
        (w_o, w_up, w_down, w_pg, w_pe), s, tm=512)

    lrw_pad = jnp.zeros((LANES, gqk), F32).at[MISC_LR0:MISC_LR0 + GLA_LOWRANK].set(lr_w).astype(BF16)
    logit_bound = FOX_DH ** 0.5 * LOG2E * jnp.max(jnp.abs(qn_g)) * jnp.max(jnp.abs(kn_g))
    mix = lambda bounded: functools.partial(_mixers, b=b, s=s, bounded_logits=bounded)
    y_fox, y_gla = lax.cond(logit_bound <= FOX_BOUNDED_MAX_LOGIT, mix(True), mix(False),
                            fqa, fka, fv, qbias, gq, gk, gv, gog, misc, lrw_pad, row(lr_b), row(onorm_g))

    return _post(
        x2d, y_gla, y_fox, p2d, wo_b, row(norm2_g), wup_b, conv_w.astype(F32), row(conv_b), wdn_b,
        row(norm3_g), wpg_b, row(b_pg), wpe_b, row(pe_norm_g), b, s, tm=512)


def kernel(x, p, norm1_g, w_in, gla_lr_w, gla_lr_b, gla_onorm_g, fox_b_f, fox_qnorm_g, fox_knorm_g, w_o, norm2_g, w_up, conv_w, conv_b, w_down, norm3_g, w_pe, pe_norm_g, w_pg, b_pg):
    b, s, d = x.shape
    x2d = x.reshape(b * s, d)
    for i in range(p.shape[0]):
        x2d = _layer(x2d, p[i].reshape(b * s, -1), b, s, norm1_g[i], w_in[i:i + 1], gla_lr_w[i], gla_lr_b[i],
                     gla_onorm_g[i], fox_b_f[i], fox_qnorm_g[i], fox_knorm_g[i], w_o[i], norm2_g[i], w_up[i],
                     conv_w[i], conv_b[i], w_down[i], norm3_g[i], w_pe[i], pe_norm_g[i], w_pg[i], b_pg[i])
    return x2d.reshape(b, s, d)
```

```python
import functools

import jax
import jax.numpy as jnp
from jax import lax
from jax.experimental import pallas as pl
from jax.experimental.pallas import tpu as pltpu

F32 = jnp.float32
BF16 = jnp.bfloat16

EPS = 1e-6
LOG2E = 1.4426950408889634
LANES = 128
BF16_SUBLANES = 16
GLA_HEADS, GLA_DK, GLA_DV = 4, 64, 128
GLA_LOWRANK = 16
GLA_INV_TAU = 1.0 / 16.0
GLA_CHUNK = 64
GLA_SUPER = 256
FOX_HEADS, FOX_DH = 8, 64
FOX_TILE = 256
FOX_SLOTS = 3
FOX_BLOCKS_AHEAD = 4
FOX_BOUNDED_MAX_LOGIT = 60.0
FFN_CHUNK = 256
CARRY_ROWS = 8
DOWN_LAG = 2
MISC_F0 = 0
MISC_LR0 = 8
VMEM_LIMIT = 56 * 1024 * 1024


def _dot(a, b):
    return jnp.dot(a, b, preferred_element_type=F32)


def _dot_nt(a, b):
    return lax.dot_general(a, b, (((1,), (1,)), ((), ())), preferred_element_type=F32)


def _dot_tn(a, b):
    return lax.dot_general(a, b, (((0,), (0,)), ((), ())), preferred_element_type=F32)


def _log_sigmoid(z):
    return jnp.minimum(z, 0.0) - jnp.log(1.0 + jnp.exp(-jnp.abs(z)))


def _split_bf16(x, pieces):
    out = []
    for _ in range(pieces):
        p = x.astype(BF16)
        out.append(p)
        x = x - p.astype(F32)
    return out


def _rms(x, gain):
    ms = jnp.mean(x * x, axis=-1, keepdims=True)
    return x * lax.rsqrt(ms + EPS) * gain


def _inproj_kernel(x_ref, g_ref, win_ref, bf_ref, qg_ref, kg_ref, *rest, tiles_per_seq, n_cast):
    cast_in, rest = rest[:n_cast], rest[n_cast:]
    gq_ref, gk_ref, gv_ref, gog_ref, fqa_ref, fka_ref, fv_ref, misc_ref, qbias_ref = rest[:9]
    cast_out = rest[9:9 + n_cast]
    carry_scr, h_scr, wg_ref, wf_ref, wm_ref = rest[9 + n_cast:]

    @pl.when(pl.program_id(0) == 0)
    def _():
        gla_w = wg_ref.shape[1]
        fox_w = wf_ref.shape[1]
        lr0 = gla_w
        fox0 = lr0 + GLA_LOWRANK
        f0 = fox0 + fox_w
        wg_ref[...] = win_ref[:gla_w, :].T.astype(BF16)
        wf_ref[...] = win_ref[fox0:f0, :].T.astype(BF16)
        pad = jnp.zeros((LANES - MISC_LR0 - GLA_LOWRANK, win_ref.shape[1]), F32)
        misc_t = jnp.concatenate([win_ref[f0:f0 + FOX_HEADS, :], win_ref[lr0:fox0, :], pad], axis=0)
        wm_ref[...] = misc_t.T.astype(BF16)

    _inproj_body(x_ref, g_ref, wg_ref, wf_ref, wm_ref, bf_ref, qg_ref, kg_ref, gq_ref, gk_ref, gv_ref, gog_ref,
                 fqa_ref, fka_ref, fv_ref, misc_ref, qbias_ref, carry_scr, h_scr, tiles_per_seq)
    for w_ref, o_ref in zip(cast_in, cast_out):
        o_ref[...] = w_ref[...].astype(BF16)


def _inproj_body(x_ref, g_ref, wg_ref, wf_ref, wm_ref, bf_ref, qg_ref, kg_ref, gq_ref, gk_ref, gv_ref, gog_ref,
                 fqa_ref, fka_ref, fv_ref, misc_ref, qbias_ref, carry_scr, h_scr, tiles_per_seq):
    tm = x_ref.shape[0]
    fw = FOX_HEADS * FOX_DH

    @pl.when(pl.program_id(0) % tiles_per_seq == 0)
    def _():
        carry_scr[...] = jnp.zeros_like(carry_scr)

    h_scr[...] = _rms(x_ref[...], g_ref[...]).astype(BF16)
    h = h_scr[...]
    misc = _dot(h, wm_ref[...])
    misc_ref[...] = misc
    q_all = _dot(h, wf_ref[:, :fw])
    k_all = _dot(h, wf_ref[:, fw:2 * fw])

    blk = FOX_TILE
    trow = lax.broadcasted_iota(jnp.int32, (blk, blk), 0)
    tcol = lax.broadcasted_iota(jnp.int32, (blk, blk), 1)
    tri = (tcol <= trow).astype(BF16)
    n_blk = tm // blk
    logf = jnp.concatenate([_log_sigmoid(misc[r0:r0 + blk] + bf_ref[...]) for r0 in range(0, tm, blk)], axis=1)
    within = sum(_dot(tri, piece) for piece in _split_bf16(logf, 3))
    carry = carry_scr[...]
    c_blocks = []
    for i in range(n_blk):
        cb = within[:, i * LANES:(i + 1) * LANES] + carry
        c_blocks.append(cb)
        carry = cb[blk - 1:blk]
    carry_scr[...] = carry
    c_all = jnp.concatenate(c_blocks, axis=0)
    qbias_ref[...] = (c_all * (-LOG2E)).T

    lane = lax.broadcasted_iota(jnp.int32, (tm, LANES), 1)
    lower = lane < FOX_DH

    def pair_norm(x, g):
        sq = x * x
        ms_lo = jnp.sum(jnp.where(lower, sq, 0.0), axis=-1, keepdims=True)
        ms_hi = jnp.sum(jnp.where(lower, 0.0, sq), axis=-1, keepdims=True)
        ms = jnp.where(lower, ms_lo, ms_hi) * (1.0 / FOX_DH)
        return x * lax.rsqrt(ms + EPS) * g

    fv_ref[...] = _dot(h, wf_ref[:, 2 * fw:]).astype(fv_ref.dtype)
    off = 0
    for ref in (gq_ref, gk_ref, gv_ref, gog_ref):
        n = ref.shape[-1]
        ref[...] = _dot(h, wg_ref[:, off:off + n]).astype(ref.dtype)
        off += n

    for pair in range(fw // LANES):
        ps = slice(pair * LANES, (pair + 1) * LANES)
        qn = pair_norm(q_all[:, ps], qg_ref[...])
        kn = pair_norm(k_all[:, ps], kg_ref[...])
        for e in range(2):
            head = 2 * pair + e
            in_head = (lane >= FOX_DH) if e else lower
            piece0 = 0 if e else FOX_DH
            c_col = jnp.sum(jnp.where(lane == MISC_F0 + head, c_all, 0.0), axis=-1, keepdims=True)
            is_piece = (lane >= piece0) & (lane < piece0 + 3)
            hs = slice(head * LANES, (head + 1) * LANES)
            fqa_ref[:, hs] = jnp.where(in_head, qn, jnp.where(is_piece, 1.0, 0.0)).astype(BF16)
            ka = jnp.where(in_head, kn, 0.0).astype(BF16)
            for j, p in enumerate(_split_bf16(c_col * (-LOG2E), 3)):
                ka = jnp.where(lane == piece0 + j, p, ka)
            fka_ref[:, hs] = ka


def _cast_slab(rows, steps):
    for n_slabs in range(min(steps, rows // BF16_SUBLANES), 0, -1):
        if rows % n_slabs == 0 and (rows // n_slabs) % BF16_SUBLANES == 0:
            return rows // n_slabs
    raise ValueError(f"no bf16-tile aligned slab for {rows} rows")


def _inproj(x2d, g, w_in, bf_pad, qg2, kg2, later_weights, s, tm):
    t, d = x2d.shape
    steps = t // tm
    gla_w = 2 * GLA_HEADS * (GLA_DK + GLA_DV)
    fox_w = 3 * FOX_HEADS * FOX_DH
    assert w_in.shape == (1, gla_w + GLA_LOWRANK + fox_w + FOX_HEADS, d)
    assert MISC_F0 == 0 and MISC_LR0 == FOX_HEADS
    gqk, gvw = GLA_HEADS * GLA_DK, GLA_HEADS * GLA_DV
    widths = (gqk, gqk, gvw, gvw, FOX_HEADS * LANES, FOX_HEADS * LANES, FOX_HEADS * FOX_DH, LANES)
    dtypes = (BF16,) * 7 + (F32,)
    whole = lambda a: pl.BlockSpec(a.shape, lambda i: (0, 0))
    once = lambda a: pl.BlockSpec((None,) + a.shape[1:], lambda i: (0, 0, 0), pipeline_mode=pl.Buffered(1))

    def slab(w):
        rows = _cast_slab(w.shape[0], steps)
        last = w.shape[0] // rows - 1
        return pl.BlockSpec((rows, w.shape[1]), lambda i: (jnp.minimum(i, last), 0))

    outs = pl.pallas_call(
        functools.partial(_inproj_kernel, tiles_per_seq=s // tm, n_cast=len(later_weights)),
        grid=(steps,),
        in_specs=[pl.BlockSpec((tm, d), lambda i: (i, 0)), whole(g), once(w_in)]
                 + [whole(a) for a in (bf_pad, qg2, kg2)]
                 + [slab(w) for w in later_weights],
        out_specs=[pl.BlockSpec((tm, n), lambda i: (i, 0)) for n in widths]
                  + [pl.BlockSpec((None, LANES, tm), lambda i: (i, 0, 0))]
                  + [slab(w) for w in later_weights],
        out_shape=[jax.ShapeDtypeStruct((t, n), dt) for n, dt in zip(widths, dtypes)]
                  + [jax.ShapeDtypeStruct((steps, LANES, tm), F32)]
                  + [jax.ShapeDtypeStruct(w.shape, BF16) for w in later_weights],
        scratch_shapes=[pltpu.VMEM((1, LANES), F32),
                        pltpu.VMEM((tm, d), BF16),
                        pltpu.VMEM((d, gla_w), BF16),
                        pltpu.VMEM((d, fox_w), BF16),
                        pltpu.VMEM((d, LANES), BF16)],
        compiler_params=pltpu.CompilerParams(dimension_semantics=("arbitrary",), vmem_limit_bytes=VMEM_LIMIT),
        name="inproj",
    )(x2d, g, w_in, bf_pad, qg2, kg2, *later_weights)
    return outs[:len(widths) + 1], outs[len(widths) + 1:]


def _gla_stages(q_ref, k_ref, v_ref, og_ref, misc_ref, lrw_ref, lrb_ref, ong_ref, y_ref, st_scr):
    s_len = q_ref.shape[0]
    r = GLA_SUPER
    n_sb = s_len // r
    n_ch = r // GLA_CHUNK
    dkw = GLA_HEADS * GLA_DK
    row = lax.broadcasted_iota(jnp.int32, (r, r), 0)
    col = lax.broadcasted_iota(jnp.int32, (r, r), 1)
    causal_bd = ((row // GLA_CHUNK) == (col // GLA_CHUNK)) & (col <= row)
    tri_bd = causal_bd.astype(BF16)
    head_of_col = lax.broadcasted_iota(jnp.int32, (r, dkw), 1) // GLA_DK
    heads = range(GLA_HEADS)
    sb_rows = lambda n: slice(n * r, (n + 1) * r)
    ch_rows = lambda c: slice(c * GLA_CHUNK, (c + 1) * GLA_CHUNK)
    v_cols = lambda h: slice(h * GLA_DV, (h + 1) * GLA_DV)

    la_pieces = []
    for n in range(n_sb):
        z = _dot(misc_ref[sb_rows(n), :].astype(BF16), lrw_ref[...]) + lrb_ref[...]
        la_pieces.append(_split_bf16(_log_sigmoid(z) * GLA_INV_TAU, 2))
        yield

    qh, ke, kd, decay = [], [], [], []
    for n in range(n_sb):
        b = _dot(tri_bd, la_pieces[n][0]) + _dot(tri_bd, la_pieces[n][1])
        b_last = [b[(c + 1) * GLA_CHUNK - 1:(c + 1) * GLA_CHUNK, :] for c in range(n_ch)]
        b_tot = jnp.concatenate([jnp.broadcast_to(bl, (GLA_CHUNK, dkw)) for bl in b_last], axis=0)
        q = q_ref[sb_rows(n), :].astype(F32)
        k = k_ref[sb_rows(n), :].astype(F32)
        qe = (q * (GLA_DK ** -0.5)) * jnp.exp(b)
        qh.append([jnp.where(head_of_col == h, qe, 0.0).astype(BF16) for h in heads])
        ke.append((k * jnp.exp(-b)).astype(BF16))
        kd_n = k * jnp.exp(b_tot - b)
        kd.append([jnp.where(head_of_col == h, kd_n, 0.0).astype(BF16) for h in heads])
        decay.append([jnp.exp(bl) for bl in b_last])
        yield

    a = []
    for n in range(n_sb):
        a.append([jnp.where(causal_bd, _dot_nt(qh[n][h], ke[n]), 0.0).astype(BF16) for h in heads])
        yield

    o_intra = []
    for n in range(n_sb):
        o_intra.append([_dot(a[n][h], v_ref[sb_rows(n), v_cols(h)]) for h in heads])
        yield

    ds = []
    for n in range(n_sb):
        for c in range(n_ch):
            rows = slice(n * r + c * GLA_CHUNK, n * r + (c + 1) * GLA_CHUNK)
            v_stack = jnp.concatenate([v_ref[rows, v_cols(h)] for h in heads], axis=0)
            kd_stack = jnp.concatenate([kd[n][h][ch_rows(c)] for h in heads], axis=0)
            ds.append(_dot_tn(v_stack, kd_stack))
        yield

    st = st_scr[...]
    o_inter = []
    for n in range(n_sb):
        for c in range(n_ch):
            qe_stack = jnp.concatenate([qh[n][h][ch_rows(c)] for h in heads], axis=0)
            o_inter.append(_dot_nt(qe_stack, st.astype(BF16)))
            st = st * decay[n][c] + ds[n * n_ch + c]
        yield
    st_scr[...] = st

    for n in range(n_sb):
        og = og_ref[sb_rows(n), :].astype(F32)
        gate = og * jax.nn.sigmoid(og)
        for h in heads:
            inter = jnp.concatenate([o_inter[n * n_ch + c][ch_rows(h)] for c in range(n_ch)], axis=0)
            o = o_intra[n][h] + inter
            y_ref[sb_rows(n), v_cols(h)] = (_rms(o, ong_ref[...]) * gate[:, v_cols(h)]).astype(y_ref.dtype)
        yield


def _fox_stages(qa_ref, ka_ref, v_ref, y_ref, vt_scr, s_scr):
    s_len = v_ref.shape[0]
    t = FOX_TILE
    n_tiles = s_len // t
    krow = lax.broadcasted_iota(jnp.int32, (t, t), 0)
    qcol = lax.broadcasted_iota(jnp.int32, (t, t), 1)
    orow = lax.broadcasted_iota(jnp.int32, (LANES, t), 0)
    vt_scr[...] = v_ref[...].astype(F32).T.astype(BF16)

    units = [(i, e) for i in range(n_tiles) for e in range(2)]
    n_slots = s_scr.shape[0]

    def scores(n):
        i, e = units[n]
        hs = slice(e * LANES, (e + 1) * LANES)
        qa = qa_ref[i * t:(i + 1) * t, hs]
        mp = None
        for kb in range(i + 1):
            sc = _dot_nt(ka_ref[kb * t:(kb + 1) * t, hs], qa)
            if kb == i:
                sc = jnp.where(krow <= qcol, sc, -jnp.inf)
            s_scr[n % n_slots, kb] = sc
            bm = jnp.max(sc.reshape(t // 8, 8, t), axis=0)
            mp = bm if mp is None else jnp.maximum(mp, bm)
        return jnp.max(mp, axis=0, keepdims=True)

    def weighted_values(n, m):
        lp = jnp.zeros((8, t), F32)
        acc = jnp.zeros((LANES, t), F32)
        for kb in range(units[n][0] + 1):
            p = jnp.exp2(s_scr[n % n_slots, kb] - m)
            lp = lp + jnp.sum(p.reshape(t // 8, 8, t), axis=0)
            acc = acc + _dot(vt_scr[:, kb * t:(kb + 1) * t], p.astype(BF16))
        return acc / jnp.sum(lp, axis=0, keepdims=True)

    ahead = n_slots - 1
    col_max = {}
    for n in range(min(ahead, len(units))):
        col_max[n] = scores(n)
        yield
    o_even = None
    for n, (i, e) in enumerate(units):
        if n + ahead < len(units):
            col_max[n + ahead] = scores(n + ahead)
            yield
        o_t = weighted_values(n, col_max.pop(n))
        if e == 0:
            o_even = o_t
        else:
            o_pair = jnp.where(orow < FOX_DH, o_even, o_t)
            y_ref[i * t:(i + 1) * t, :] = o_pair.T.astype(y_ref.dtype)
        yield


def _fox_stages_bounded(qa_ref, ka_ref, v_ref, qbias_ref, y_ref, vt_scr, *, tile_rows):
    s_len = v_ref.shape[0]
    t = FOX_TILE
    n_tiles = s_len // t
    pair = pl.program_id(1)
    krow = lax.broadcasted_iota(jnp.int32, (t, t), 0)
    qcol = lax.broadcasted_iota(jnp.int32, (t, t), 1)
    orow = lax.broadcasted_iota(jnp.int32, (LANES, t), 0)
    vt_scr[...] = v_ref[...].astype(F32).T.astype(BF16)

    units = [(i, e) for i in range(n_tiles) for e in range(2)]
    blocks = [(n, kb) for n, (i, e) in enumerate(units) for kb in range(i + 1)]
    row_sum, acc = {}, {}
    o_even = [None]

    def values(n, kb, p):
        i, e = units[n]
        d = _dot(vt_scr[:, kb * t:(kb + 1) * t], p)
        acc[n] = d if n not in acc else acc[n] + d
        if kb == i:
            o_t = acc.pop(n) / jnp.sum(row_sum.pop(n), axis=0, keepdims=True)
            if e == 0:
                o_even[0] = o_t
            else:
                o_pair = jnp.where(orow < FOX_DH, o_even[0], o_t)
                y_ref[i * t:(i + 1) * t, :] = o_pair.T.astype(y_ref.dtype)

    pending = []
    for j, (n, kb) in enumerate(blocks):
        i, e = units[n]
        hs = slice(e * LANES, (e + 1) * LANES)
        q0 = i * t
        m = qbias_ref[q0 // tile_rows, pl.ds(2 * pair + e, 1), q0 % tile_rows:q0 % tile_rows + t]
        sc = _dot_nt(ka_ref[kb * t:(kb + 1) * t, hs], qa_ref[q0:q0 + t, hs])
        if kb == i:
            sc = jnp.where(krow <= qcol, sc, -jnp.inf)
        p = jnp.exp2(sc - m)
        ps = jnp.sum(p.reshape(t // 8, 8, t), axis=0)
        row_sum[n] = ps if n not in row_sum else row_sum[n] + ps
        pending.append((n, kb, p.astype(BF16)))
        if len(pending) > FOX_BLOCKS_AHEAD:
            values(*pending.pop(0))
        if j % 2 == 1:
            yield
    for item in pending:
        values(*item)
    yield


def _mixers_kernel(qa_ref, ka_ref, fv_ref, qbias_ref, gq_ref, gk_ref, gv_ref, gog_ref, misc_ref, lrw_ref, lrb_ref,
                   ong_ref, yf_ref, yg_ref, vt_scr, s_scr, st_scr, *, bounded_logits, tile_rows):
    @pl.when(pl.program_id(1) == 0)
    def _():
        st_scr[...] = jnp.zeros_like(st_scr)

    if bounded_logits:
        fox = _fox_stages_bounded(qa_ref, ka_ref, fv_ref, qbias_ref, yf_ref, vt_scr, tile_rows=tile_rows)
    else:
        fox = _fox_stages(qa_ref, ka_ref, fv_ref, yf_ref, vt_scr, s_scr)
    gla = _gla_stages(gq_ref, gk_ref, gv_ref, gog_ref, misc_ref, lrw_ref, lrb_ref, ong_ref, yg_ref, st_scr)
    rota = (fox, fox, fox, gla) if bounded_logits else (fox, fox, gla)
    finished = set()
    turn = 0
    while len(finished) < 2:
        gen = rota[turn % len(rota)]
        turn += 1
        if gen not in finished:
            try:
                next(gen)
            except StopIteration:
                finished.add(gen)


def _mixers(fqa, fka, fv, qbias, gq, gk, gv, gog, misc, lrw_pad, lrb, ong, b, s, bounded_logits):
    n_pairs = FOX_HEADS * FOX_DH // LANES
    t = FOX_TILE
    dkw, dvw = GLA_HEADS * GLA_DK, GLA_HEADS * GLA_DV
    sq = s // n_pairs
    assert s % t == 0 and sq % GLA_SUPER == 0, "sequence must split into FoX tiles and GLA super-blocks"
    tile_rows = qbias.shape[2]
    heads2 = pl.BlockSpec((s, 2 * LANES), lambda i, j: (i, j))
    pair = pl.BlockSpec((s, LANES), lambda i, j: (i, j))
    seq_bias = pl.BlockSpec((s // tile_rows, FOX_HEADS, tile_rows), lambda i, j: (i, 0, 0))
    quarter = lambda w: pl.BlockSpec((sq, w), lambda i, j: (i * n_pairs + j, 0))
    whole = lambda a: pl.BlockSpec(a.shape, lambda i, j: (0, 0))
    return pl.pallas_call(
        functools.partial(_mixers_kernel, bounded_logits=bounded_logits, tile_rows=tile_rows),
        grid=(b, n_pairs),
        in_specs=[heads2, heads2, pair, seq_bias,
                  quarter(dkw), quarter(dkw), quarter(dvw), quarter(dvw), quarter(LANES),
                  whole(lrw_pad), whole(lrb), whole(ong)],
        out_specs=[pair, quarter(dvw)],
        out_shape=[jax.ShapeDtypeStruct((b * s, FOX_HEADS * FOX_DH), BF16),
                   jax.ShapeDtypeStruct((b * s, dvw), BF16)],
        scratch_shapes=[
            pltpu.VMEM((LANES, s), BF16),
            pltpu.VMEM((FOX_SLOTS, s // t, t, t), F32),
            pltpu.VMEM((GLA_DV, dkw), F32),
        ],
        compiler_params=pltpu.CompilerParams(dimension_semantics=("arbitrary", "arbitrary"),
                                             vmem_limit_bytes=VMEM_LIMIT),
        name="mixers_bounded" if bounded_logits else "mixers",
    )(fqa, fka, fv, qbias, gq, gk, gv, gog, misc, lrw_pad, lrb, ong)


def _post_kernel(x_ref, yg_ref, yf_ref, p_ref, wo_ref, g2_ref, wup_ref, cw_ref, cb_ref, wdn_ref,
                 g3_ref, wpg_ref, bpg_ref, wpe_ref, gpe_ref, o_ref,
                 x1_scr, h2_scr, u_scr, act_scr, carry_scr, acc_scr):
    tm = x_ref.shape[0]
    d_ff = wdn_ref.shape[0]
    n_chunks = d_ff // FFN_CHUNK
    n_slabs = 2 * FFN_CHUNK // LANES
    half = n_slabs // 2
    dvw = yg_ref.shape[1]
    first_tile = pl.program_id(1) == 0

    @pl.when(first_tile)
    def _():
        carry_scr[...] = jnp.zeros_like(carry_scr)

    x1 = x_ref[...] + _dot(yg_ref[...], wo_ref[:dvw, :]) + _dot(yf_ref[...], wo_ref[dvw:, :])
    x1_scr[...] = x1
    h2_scr[...] = _rms(x1, g2_ref[...]).astype(BF16)

    o_ref[...] = _rms(_dot(p_ref[...].astype(BF16), wpe_ref[...]), gpe_ref[...])
    acc_scr[...] = jnp.zeros_like(acc_scr)

    for j in range(n_chunks + DOWN_LAG):
        if j < n_chunks:
            ub = u_scr.at[j % 2]
            col0 = [(c // half) * d_ff + j * FFN_CHUNK + (c % half) * LANES for c in range(n_slabs)]
            h2 = h2_scr[...]
            for part in range(2):
                u = _dot(h2, wup_ref[:, col0[part * half]:col0[part * half] + FFN_CHUNK])
                for c in range(part * half, (part + 1) * half):
                    ub[c, :CARRY_ROWS, :] = carry_scr[j, c]
                    ub[c, CARRY_ROWS:, :] = u[:, (c % half) * LANES:(c % half + 1) * LANES]
                    carry_scr[j, c] = ub[c, tm:, :]
        if j >= DOWN_LAG:
            jd = j - DOWN_LAG
            down = _dot(act_scr[jd % (DOWN_LAG + 1)], wdn_ref[jd * FFN_CHUNK:(jd + 1) * FFN_CHUNK, :])
            if jd == n_chunks // 2:
                bits = lax.bitcast_convert_type(o_ref[...], jnp.uint32)
                down = down + lax.bitcast_convert_type((bits >> 16) >> 16, F32)
            acc_scr[...] += down
        if j < n_chunks:
            def conv(c):
                cs = slice(col0[c], col0[c] + LANES)
                return (cb_ref[:, cs]
                        + ub[c, CARRY_ROWS - 2:CARRY_ROWS - 2 + tm, :] * cw_ref[0:1, cs]
                        + ub[c, CARRY_ROWS - 1:CARRY_ROWS - 1 + tm, :] * cw_ref[1:2, cs]
                        + ub[c, CARRY_ROWS:, :] * cw_ref[2:3, cs])
            for c in range(half):
                gate = conv(c)
                act = gate * jax.nn.sigmoid(gate) * conv(c + half)
                act_scr[j % (DOWN_LAG + 1), :, c * LANES:(c + 1) * LANES] = act.astype(BF16)

    x2 = x1_scr[...] + acc_scr[...]
    gate = jax.nn.sigmoid(_dot(_rms(x2, g3_ref[...]).astype(BF16), wpg_ref[...]) + bpg_ref[...])
    o_ref[...] = x2 + gate * o_ref[...]


def _post(x2d, yg, yf, p2d, wo, g2, wup, cw, cb, wdn, g3, wpg, bpg, wpe, gpe, b, s, tm):
    d = x2d.shape[1]
    nt = s // tm
    tile = lambda w: pl.BlockSpec((tm, w), lambda i, j: (i * nt + j, 0))
    resident = lambda a: pl.BlockSpec(a.shape, lambda i, j: (0,) * a.ndim, pipeline_mode=pl.Buffered(1))
    n_chunks = wdn.shape[0] // FFN_CHUNK
    n_slabs = 2 * FFN_CHUNK // LANES
    return pl.pallas_call(
        _post_kernel,
        grid=(b, nt),
        in_specs=[tile(d), tile(yg.shape[1]), tile(yf.shape[1]), tile(p2d.shape[1])]
                 + [resident(a) for a in (wo, g2, wup, cw, cb, wdn, g3, wpg, bpg, wpe, gpe)],
        out_specs=tile(d),
        out_shape=jax.ShapeDtypeStruct(x2d.shape, F32),
        scratch_shapes=[
            pltpu.VMEM((tm, d), F32),
            pltpu.VMEM((tm, d), BF16),
            pltpu.VMEM((2, n_slabs, CARRY_ROWS + tm, LANES), F32),
            pltpu.VMEM((DOWN_LAG + 1, tm, FFN_CHUNK), BF16),
            pltpu.VMEM((n_chunks, n_slabs, CARRY_ROWS, LANES), F32),
            pltpu.VMEM((tm, d), F32),
        ],
        compiler_params=pltpu.CompilerParams(dimension_semantics=("arbitrary", "arbitrary"),
                                             vmem_limit_bytes=VMEM_LIMIT),
        name="post",
    )(x2d, yg, yf, p2d, wo, g2, wup, cw, cb, wdn, g3, wpg, bpg, wpe, gpe)


def _layer(x2d, p2d, b, s, norm1_g, w_in, lr_w, lr_b, onorm_g, b_f, qn_g, kn_g, w_o, norm2_g, w_up,
           conv_w, conv_b, w_down, norm3_g, w_pe, pe_norm_g, w_pg, b_pg):
    gqk = GLA_HEADS * GLA_DK
    row = lambda a: a.reshape(1, -1).astype(F32)
    bf_pad = jnp.zeros((1, LANES), F32).at[0, MISC_F0:MISC_F0 + FOX_HEADS].set(b_f)
    qg2 = row(jnp.tile(qn_g, 2)) * (FOX_DH ** -0.5 * LOG2E)
    (gq, gk, gv, gog, fqa, fka, fv, misc, qbias), (wo_b, wup_b, wdn_b, wpg_b, wpe_b) = _inproj(
        x2d, row(norm1_g), jnp.swapaxes(w_in, 1, 2).astype(F32), bf_pad, qg2, row(jnp.tile(kn_g, 2)),
        (w_o, w_up, w_down, w_pg, w_pe), s, tm=512)

    lrw_pad = jnp.zeros((LANES, gqk), F32).at[MISC_LR0:MISC_LR0 + GLA_LOWRANK].set(lr_w).astype(BF16)
    logit_bound = FOX_DH ** 0.5 * LOG2E * jnp.max(jnp.abs(qn_g)) * jnp.max(jnp.abs(kn_g))
    mix = lambda bounded: functools.partial(_mixers, b=b, s=s, bounded_logits=bounded)
    y_fox, y_gla = lax.cond(logit_bound <= FOX_BOUNDED_MAX_LOGIT, mix(True), mix(False),
                            fqa, fka, fv, qbias, gq, gk, gv, gog, misc, lrw_pad, row(lr_b), row(onorm_g))

    return _post(
        x2d, y_gla, y_fox, p2d, wo_b, row(norm2_g), wup_b, conv_w.astype(F32), row(conv_b), wdn_b,
        row(norm3_g), wpg_b, row(b_pg), wpe_b, row(pe_norm_g), b, s, tm=512)


def kernel(x, p, norm1_g, w_in, gla_lr_w, gla_lr_b, gla_onorm_g, fox_b_f, fox_qnorm_g, fox_knorm_g, w_o, norm2_g, w_up, conv_w, conv_b, w_down, norm3_g, w_pe, pe_norm_g, w_pg, b_pg):
    b, s, d = x.shape
    x2d = x.reshape(b * s, d)
    for i in range(p.shape[0]):
        x2d = _layer(x2d, p[i].reshape(b * s, -1), b, s, norm1_g[i], w_in[i:i + 1], gla_lr_w[i], gla_lr_b[i],
                     gla_onorm_g[i], fox_b_f[i], fox_qnorm_g[i], fox_knorm_g[i], w_o[i], norm2_g[i], w_up[i],
                     conv_w[i], conv_b[i], w_down[i], norm3_g[i], w_pe[i], pe_norm_g[i], w_pg[i], b_pg[i])
    return x2d.reshape(b, s, d)
```
